```python
import jax, jax.numpy as jnp
from jax import lax
import numpy as np

D_MODEL = 1024
BATCH = 2
SEQ = 8192
DEPTH = 2

CTX_LEN = 256
GRID_W = 64
HEAD_DIM = 64
ROPE_BASE = 10000.0
ATT_HEADS = 8
ATT_KV_HEADS = 2
ATT_GROUP = ATT_HEADS // ATT_KV_HEADS
ATT_WIDTH = ATT_HEADS * HEAD_DIM
KV_WIDTH = ATT_KV_HEADS * HEAD_DIM
WINDOW = 128
BLOCK = 128
ATT_SCALE = HEAD_DIM ** -0.5
NEG_INF = -1e30
FNET_GROUPS = 4
FNET_WIDTH = FNET_GROUPS * HEAD_DIM
GLA_HEADS = 4
GLA_WIDTH = GLA_HEADS * HEAD_DIM
GLA_GATE_RANK = 16
GLA_TAU = 16.0
GLA_CHUNK = 64
GLA_SCALE = HEAD_DIM ** -0.5
MIX_WIDTH = ATT_WIDTH + FNET_WIDTH + GLA_WIDTH
_O1 = ATT_WIDTH
_O2 = _O1 + KV_WIDTH
_O3 = _O2 + KV_WIDTH
_O4 = _O3 + FNET_WIDTH
_O5 = _O4 + GLA_WIDTH
_O6 = _O5 + GLA_WIDTH
_O7 = _O6 + GLA_WIDTH
_O8 = _O7 + GLA_WIDTH
IN_WIDTH = _O8 + 2 * GLA_GATE_RANK
IN_SPLITS = (_O1, _O2, _O3, _O4, _O5, _O6, _O7, _O8)
FFN_HIDDEN = ((8 * D_MODEL // 3 + 255) // 256) * 256

kernel_name = "hybrid_prefix_dit_gqa_fnet_gla"


def rms_norm(x, g, eps=1e-6):
    xf = x.astype(jnp.float32)
    y = xf * lax.rsqrt(jnp.mean(xf * xf, axis=-1, keepdims=True) + eps)
    return (y * g.astype(jnp.float32)).astype(x.dtype)


def split_heads(t, nh):
    return t.reshape(t.shape[0], t.shape[1], nh, HEAD_DIM)


def axial_rope_tables(n):
    rows = n // GRID_W
    row = jnp.repeat(jnp.arange(rows, dtype=jnp.float32), GRID_W)
    col = jnp.tile(jnp.arange(GRID_W, dtype=jnp.float32), rows)
    axis_dim = HEAD_DIM // 2
    inv_freq = ROPE_BASE ** (-jnp.arange(0, axis_dim, 2, dtype=jnp.float32) / axis_dim)
    ang_r = row[:, None] * inv_freq[None, :]
    ang_c = col[:, None] * inv_freq[None, :]
    return (jnp.cos(ang_r), jnp.sin(ang_r), jnp.cos(ang_c), jnp.sin(ang_c))


def _rotate(x, cos, sin):
    m = x.shape[-1] // 2
    x1, x2 = x[..., :m], x[..., m:]
    c = cos[:, None, :]
    s = sin[:, None, :]
    return jnp.concatenate([x1 * c - x2 * s, x2 * c + x1 * s], axis=-1)


def apply_axial_rope(x, rope):
    cos_r, sin_r, cos_c, sin_c = rope
    h = HEAD_DIM // 2
    xf = x.astype(jnp.float32)
    y = jnp.concatenate([_rotate(xf[..., :h], cos_r, sin_r), _rotate(xf[..., h:], cos_c, sin_c)], axis=-1)
    return y.astype(x.dtype)


def context_attention(qc, kc, vc, sink):
    B, L = qc.shape[0], qc.shape[1]
    q = qc.reshape(B, L, ATT_KV_HEADS, ATT_GROUP, HEAD_DIM)
    s = jnp.einsum('bqkgd,bskd->bkgqs', q, kc).astype(jnp.float32) * ATT_SCALE
    sk = sink.astype(jnp.float32).reshape(1, ATT_KV_HEADS, ATT_GROUP, 1, 1)
    m = jnp.maximum(s.max(-1, keepdims=True), sk)
    p = jnp.exp(s - m)
    p = p / (p.sum(-1, keepdims=True) + jnp.exp(sk - m))
    o = jnp.einsum('bkgqs,bskd->bqkgd', p.astype(vc.dtype), vc)
    return o.reshape(B, L, ATT_WIDTH)


def window_attention(q, k, v, kc, vc, sink):
    B, n = q.shape[0], q.shape[1]
    nb = n // BLOCK
    nbr = WINDOW // BLOCK
    span = BLOCK + 2 * WINDOW
    qb = q.reshape(B, nb, BLOCK, ATT_KV_HEADS, ATT_GROUP, HEAD_DIM)
    pad = ((0, 0), (WINDOW, WINDOW), (0, 0), (0, 0))
    kp = jnp.pad(k, pad).reshape(B, nb + 2 * nbr, BLOCK, ATT_KV_HEADS, HEAD_DIM)
    vp = jnp.pad(v, pad).reshape(B, nb + 2 * nbr, BLOCK, ATT_KV_HEADS, HEAD_DIM)
    kb = jnp.concatenate([kp[:, i:i + nb] for i in range(2 * nbr + 1)], axis=2)
    vb = jnp.concatenate([vp[:, i:i + nb] for i in range(2 * nbr + 1)], axis=2)
    qi = jnp.arange(BLOCK)[:, None]
    kj = jnp.arange(span)[None, :]
    band = jnp.abs(qi + WINDOW - kj) <= WINDOW
    kpos = jnp.arange(nb)[:, None] * BLOCK - WINDOW + jnp.arange(span)[None, :]
    valid = band[None] & ((kpos >= 0) & (kpos < n))[:, None, :]
    s_loc = jnp.einsum('bnqkgd,bnskd->bnkgqs', qb, kb).astype(jnp.float32) * ATT_SCALE
    s_loc = jnp.where(valid[None, :, None, None], s_loc, NEG_INF)
    s_ctx = jnp.einsum('bnqkgd,bskd->bnkgqs', qb, kc).astype(jnp.float32) * ATT_SCALE
    sk = sink.astype(jnp.float32).reshape(1, 1, ATT_KV_HEADS, ATT_GROUP, 1, 1)
    m = jnp.maximum(jnp.maximum(s_loc.max(-1, keepdims=True), s_ctx.max(-1, keepdims=True)), sk)
    p_loc = jnp.exp(s_loc - m)
    p_ctx = jnp.exp(s_ctx - m)
    inv = 1.0 / (p_loc.sum(-1, keepdims=True) + p_ctx.sum(-1, keepdims=True) + jnp.exp(sk - m))
    o = (jnp.einsum('bnkgqs,bnskd->bnqkgd', (p_loc * inv).astype(v.dtype), vb)
         + jnp.einsum('bnkgqs,bskd->bnqkgd', (p_ctx * inv).astype(vc.dtype), vc))
    return o.reshape(B, n, ATT_WIDTH)


def fourier_mix(u, w_f):
    B, n = u.shape[0], u.shape[1]
    ug = u.reshape(B, n, FNET_GROUPS, HEAD_DIM).astype(jnp.float32)
    f = jnp.fft.fft2(ug, axes=(1, 3), norm='ortho').real
    y = jnp.einsum('bngc,gce->bnge', f.astype(u.dtype), w_f)
    return y.reshape(B, n, FNET_WIDTH)


def gla_chunked(q, k, v, log_a, s0):
    B, T, H, dk = q.shape
    dv = v.shape[-1]
    nc = T // GLA_CHUNK
    rs = lambda t: t.reshape(B, nc, GLA_CHUNK, H, t.shape[-1]).astype(jnp.float32)
    qc, kc, vc = rs(q), rs(k), rs(v)
    bcum = jnp.cumsum(rs(log_a), axis=2)
    btot = bcum[:, :, -1]
    q_in = qc * jnp.exp(bcum)
    k_in = kc * jnp.exp(-bcum)
    k_out = kc * jnp.exp(btot[:, :, None] - bcum)
    lower = jnp.tril(jnp.ones((GLA_CHUNK, GLA_CHUNK), dtype=bool))
    att = jnp.einsum('bnthd,bnshd->bnhts', q_in, k_in)
    att = jnp.where(lower, att, 0.0)
    o_intra = jnp.einsum('bnhts,bnshe->bnthe', att, vc)
    u = jnp.einsum('bnshd,bnshe->bnhde', k_out, vc)
    decay = jnp.exp(btot)

    def step(S, inp):
        d, uu = inp
        return d[..., None] * S + uu, S

    s_final, s_starts = lax.scan(step, s0.astype(jnp.float32),
                                 (jnp.moveaxis(decay, 1, 0), jnp.moveaxis(u, 1, 0)))
    s_starts = jnp.moveaxis(s_starts, 0, 1)
    o_inter = jnp.einsum('bnthd,bnhde->bnthe', q_in, s_starts)
    o = (o_intra + o_inter).reshape(B, T, H, dv)
    return o, s_final


def gla_gates(z_lr, w_gate, b_gate):
    z = (z_lr @ w_gate + b_gate).astype(jnp.float32)
    la = jax.nn.log_sigmoid(z) / GLA_TAU
    return la.reshape(la.shape[0], la.shape[1], GLA_HEADS, HEAD_DIM)


def gla_mixer(lat, ctxp, wgf, bgf, wgb, bgb, norm_g, need_ctx):
    def prep(q, k, v, r, z):
        q = split_heads(q, GLA_HEADS) * GLA_SCALE
        k = split_heads(k, GLA_HEADS)
        v = split_heads(v, GLA_HEADS)
        laf = gla_gates(z[..., :GLA_GATE_RANK], wgf, bgf)
        lab = gla_gates(z[..., GLA_GATE_RANK:], wgb, bgb)
        return q, k, v, laf, lab

    ql, kl, vl, lfl, lbl = prep(*lat)
    qc, kc, vc, lfc, lbc = prep(*ctxp)
    B = ql.shape[0]
    s0 = jnp.zeros((B, GLA_HEADS, HEAD_DIM, HEAD_DIM), jnp.float32)
    flip = lambda t: t[:, ::-1]
    o_cf, s_cf = gla_chunked(qc, kc, vc, lfc, s0)
    o_cb, s_cb = gla_chunked(flip(qc), flip(kc), flip(vc), flip(lbc), s0)
    o_lf, _ = gla_chunked(ql, kl, vl, lfl, s_cf)
    o_lb, _ = gla_chunked(flip(ql), flip(kl), flip(vl), flip(lbl), s_cb)

    def finish(o, r):
        y = rms_norm(o, norm_g).reshape(o.shape[0], o.shape[1], GLA_WIDTH).astype(r.dtype)
        return y * jax.nn.silu(r)

    y_l = finish(o_lf + flip(o_lb), lat[3])
    y_c = finish(o_cf + flip(o_cb), ctxp[3]) if need_ctx else None
    return y_l, y_c


def token_mixers(h, hc, w_in, q_g, k_g, sink, w_f, wgf, bgf, wgb, bgb, gla_g, rope, need_ctx):
    aq, ak, av, fu, gq, gk, gv, gr, gz = jnp.split(h @ w_in, IN_SPLITS, axis=-1)
    caq, cak, cav, cfu, cgq, cgk, cgv, cgr, cgz = jnp.split(hc @ w_in, IN_SPLITS, axis=-1)
    q = apply_axial_rope(rms_norm(split_heads(aq, ATT_HEADS), q_g), rope)
    k = apply_axial_rope(rms_norm(split_heads(ak, ATT_KV_HEADS), k_g), rope)
    v = split_heads(av, ATT_KV_HEADS)
    kc = rms_norm(split_heads(cak, ATT_KV_HEADS), k_g)
    vc = split_heads(cav, ATT_KV_HEADS)
    att = window_attention(q, k, v, kc, vc, sink)
    four = fourier_mix(fu, w_f)
    gla, gla_c = gla_mixer((gq, gk, gv, gr, gz), (cgq, cgk, cgv, cgr, cgz), wgf, bgf, wgb, bgb, gla_g, need_ctx)
    mix = jnp.concatenate([att, four, gla], axis=-1)
    mix_c = None
    if need_ctx:
        qc = rms_norm(split_heads(caq, ATT_HEADS), q_g)
        att_c = context_attention(qc, kc, vc, sink)
        four_c = fourier_mix(cfu, w_f)
        mix_c = jnp.concatenate([att_c, four_c, gla_c], axis=-1)
    return mix, mix_c


def swiglu(h, w_ffn_in, w_ffn_out):
    gate, up = jnp.split(h @ w_ffn_in, 2, axis=-1)
    return (jax.nn.silu(gate) * up) @ w_ffn_out


def setup_inputs(seed: int = 0) -> dict:
    key = jax.random.key(seed)
    ks = jax.random.split(key, 24)
    nrm = lambda k, shape, scale: jax.random.normal(k, shape, jnp.float32) * scale
    return {
        'x': nrm(ks[0], (BATCH, SEQ, D_MODEL), 1.0),
        'c': nrm(ks[1], (BATCH, D_MODEL), 1.0),
        'ctx': nrm(ks[2], (BATCH, CTX_LEN, D_MODEL), 1.0),
        'c_ctx': nrm(ks[3], (D_MODEL,), 1.0),
        'w_mod': nrm(ks[4], (DEPTH, D_MODEL, 6 * D_MODEL), 0.5 * D_MODEL ** -0.5),
        'b_mod': nrm(ks[5], (DEPTH, 6 * D_MODEL), 0.02),
        'g_norm1': 1.0 + nrm(ks[6], (DEPTH, D_MODEL), 0.02),
        'w_in': nrm(ks[7], (DEPTH, D_MODEL, IN_WIDTH), D_MODEL ** -0.5),
        'q_norm_g': 1.0 + nrm(ks[8], (DEPTH, HEAD_DIM), 0.02),
        'k_norm_g': 1.0 + nrm(ks[9], (DEPTH, HEAD_DIM), 0.02),
        'attn_sink': nrm(ks[10], (DEPTH, ATT_HEADS), 0.5),
        'w_fourier': nrm(ks[11], (DEPTH, FNET_GROUPS, HEAD_DIM, HEAD_DIM), HEAD_DIM ** -0.5),
        'gla_w_gate_f': nrm(ks[12], (DEPTH, GLA_GATE_RANK, GLA_WIDTH), GLA_GATE_RANK ** -0.5),
        'gla_b_gate_f': nrm(ks[13], (DEPTH, GLA_WIDTH), 0.1),
        'gla_w_gate_b': nrm(ks[14], (DEPTH, GLA_GATE_RANK, GLA_WIDTH), GLA_GATE_RANK ** -0.5),
        'gla_b_gate_b': nrm(ks[15], (DEPTH, GLA_WIDTH), 0.1),
        'gla_norm_g': 1.0 + nrm(ks[16], (DEPTH, HEAD_DIM), 0.02),
        'w_out': nrm(ks[17], (DEPTH, MIX_WIDTH, D_MODEL), MIX_WIDTH ** -0.5),
        'g_norm2': 1.0 + nrm(ks[18], (DEPTH, D_MODEL), 0.02),
        'w_ffn_in': nrm(ks[19], (DEPTH, D_MODEL, 2 * FFN_HIDDEN), D_MODEL ** -0.5),
        'w_ffn_out': nrm(ks[20], (DEPTH, FFN_HIDDEN, D_MODEL), FFN_HIDDEN ** -0.5),
    }


def reference(x, c, ctx, c_ctx, w_mod, b_mod, g_norm1, w_in, q_norm_g, k_norm_g, attn_sink, w_fourier,
              gla_w_gate_f, gla_b_gate_f, gla_w_gate_b, gla_b_gate_b, gla_norm_g, w_out, g_norm2,
              w_ffn_in, w_ffn_out):
    n = x.shape[1]
    rope = axial_rope_tables(n)
    xc = ctx
    for l in range(DEPTH):
        need_ctx = l < DEPTH - 1
        mod_l = (jax.nn.silu(c) @ w_mod[l] + b_mod[l])[:, None, :]
        mod_c = (jax.nn.silu(c_ctx) @ w_mod[l] + b_mod[l])[None, None, :]
        sh1, sc1, gt1, sh2, sc2, gt2 = jnp.split(mod_l, 6, axis=-1)
        csh1, csc1, cgt1, csh2, csc2, cgt2 = jnp.split(mod_c, 6, axis=-1)
        h = rms_norm(x, g_norm1[l]) * (1.0 + sc1) + sh1
        hc = rms_norm(xc, g_norm1[l]) * (1.0 + csc1) + csh1
        mix, mix_c = token_mixers(h, hc, w_in[l], q_norm_g[l], k_norm_g[l], attn_sink[l], w_fourier[l],
                                  gla_w_gate_f[l], gla_b_gate_f[l], gla_w_gate_b[l], gla_b_gate_b[l],
                                  gla_norm_g[l], rope, need_ctx)
        x = x + gt1 * (mix @ w_out[l])
        h2 = rms_norm(x, g_norm2[l]) * (1.0 + sc2) + sh2
        x = x + gt2 * swiglu(h2, w_ffn_in[l], w_ffn_out[l])
        if need_ctx:
            xc = xc + cgt1 * (mix_c @ w_out[l])
            hc2 = rms_norm(xc, g_norm2[l]) * (1.0 + csc2) + csh2
            xc = xc + cgt2 * swiglu(hc2, w_ffn_in[l], w_ffn_out[l])
    return x
```

```python
import functools

import numpy as np
import jax
import jax.numpy as jnp
from jax import lax
from jax.experimental import pallas as pl
from jax.experimental.pallas import tpu as pltpu

F32 = jnp.float32
BF16 = jnp.bfloat16

D_MODEL = 1024
HEAD_DIM = 64
GRID_W = 64
ROPE_BASE = 10000.0
ATT_HEADS = 8
ATT_KV_HEADS = 2
ATT_WIDTH = ATT_HEADS * HEAD_DIM
KV_WIDTH = ATT_KV_HEADS * HEAD_DIM
ATT_BLOCK = 128
ATT_SCALE = HEAD_DIM ** -0.5
NEG_INF = -1e30
FNET_GROUPS = 4
FNET_WIDTH = FNET_GROUPS * HEAD_DIM
GLA_HEADS = 4
GLA_WIDTH = GLA_HEADS * HEAD_DIM
GLA_GATE_RANK = 16
GLA_TAU = 16.0
GLA_CHUNK = 64
GLA_SCALE = HEAD_DIM ** -0.5
MAIN_WIDTH = ATT_WIDTH + 2 * KV_WIDTH + FNET_WIDTH + 4 * GLA_WIDTH
FFN_HIDDEN = 2816
EPS = 1e-6
LANES = 128
DFT_N1 = 128
VMEM_LIMIT = 56 * 1024 * 1024

NT_DIMS = (((1,), (1,)), ((), ()))
TN_DIMS = (((0,), (0,)), ((), ()))


def _params(*sem):
    return pltpu.CompilerParams(dimension_semantics=sem, vmem_limit_bytes=VMEM_LIMIT)


def _dot(a, b):
    return jnp.dot(a, b, preferred_element_type=F32)


def _silu(x):
    return x / (1.0 + jnp.exp(-x))


def _full(shape):
    nd = len(shape)
    return pl.BlockSpec(shape, lambda *_: (0,) * nd)


def _group_mean(t, bd_ref):
    return _dot((t * t).astype(BF16), bd_ref[...])


def _mod_kernel(c_ref, w_ref, b_ref, o_ref):
    s = _silu(c_ref[...]).astype(BF16)
    o_ref[...] = _dot(s, w_ref[...].astype(BF16)) + b_ref[...]


def _modulation(cc, w_mod, b_mod):
    depth, d, width = w_mod.shape
    tn = 1536
    return pl.pallas_call(
        _mod_kernel,
        out_shape=jax.ShapeDtypeStruct((depth, 8, width), F32),
        grid=(depth, width // tn),
        in_specs=[
            _full((8, d)),
            pl.BlockSpec((None, d, tn), lambda l, j: (l, 0, j)),
            pl.BlockSpec((None, 1, tn), lambda l, j: (l, 0, j)),
        ],
        out_specs=pl.BlockSpec((None, 8, tn), lambda l, j: (l, 0, j)),
        compiler_params=_params("parallel", "parallel"),
        name="modulation",
    )(cc, w_mod, b_mod.reshape(depth, 1, width))


def _inproj_kernel(*refs, rope):
    if rope:
        (x_ref, mod_ref, g1_ref, wm_ref, wz_ref, wg_ref, bg_ref, qg_ref, kg_ref, bd_ref, cos_ref, sin_ref,
         q_ref, k4_ref, v4_ref, fu_ref, gq_ref, gk_ref, gv_ref, gr_ref, la_ref) = refs
    else:
        (x_ref, mod_ref, g1_ref, wm_ref, wz_ref, wg_ref, bg_ref, qg_ref, kg_ref, bd_ref,
         q_ref, k4_ref, v4_ref, fu_ref, gq_ref, gk_ref, gv_ref, gr_ref, la_ref) = refs
    d = D_MODEL
    x = x_ref[...]
    mod = mod_ref[...]
    sh, sc = mod[:, 0:d], mod[:, d:2 * d]
    ms = jnp.mean(x * x, axis=-1, keepdims=True)
    h = (x * lax.rsqrt(ms + EPS) * g1_ref[...]) * (1.0 + sc) + sh
    hb = h.astype(BF16)
    acc = _dot(hb, wm_ref[...])

    lane = lax.broadcasted_iota(jnp.int32, (1, LANES), 1)
    lo = lane < HEAD_DIM
    second_half = (lane & 16) != 0

    def head_norm(t, g_ref):
        return t * lax.rsqrt(_group_mean(t, bd_ref) + EPS) * g_ref[...]

    def rotary(t):
        if not rope:
            return t
        partner = jnp.where(second_half, pltpu.roll(t, 16, 1), pltpu.roll(t, LANES - 16, 1))
        return t * cos_ref[...] + partner * sin_ref[...]

    def spread(t, out_ref):
        tr = pltpu.roll(t, HEAD_DIM, 1)
        zero = jnp.zeros_like(t)
        out_ref[:, 0:128] = jnp.where(lo, t, zero).astype(out_ref.dtype)
        out_ref[:, 128:256] = jnp.where(lo, zero, tr).astype(out_ref.dtype)
        out_ref[:, 256:384] = jnp.where(lo, tr, zero).astype(out_ref.dtype)
        out_ref[:, 384:512] = jnp.where(lo, zero, t).astype(out_ref.dtype)

    for j in range(ATT_WIDTH // LANES):
        t = rotary(head_norm(acc[:, LANES * j:LANES * (j + 1)], qg_ref)) * ATT_SCALE
        q_ref[:, LANES * j:LANES * (j + 1)] = t.astype(q_ref.dtype)
    o1 = ATT_WIDTH
    spread(rotary(head_norm(acc[:, o1:o1 + KV_WIDTH], kg_ref)), k4_ref)
    o2 = o1 + KV_WIDTH
    spread(acc[:, o2:o2 + KV_WIDTH], v4_ref)
    o3 = o2 + KV_WIDTH
    fu_ref[...] = acc[:, o3:o3 + FNET_WIDTH].astype(fu_ref.dtype)
    o4 = o3 + FNET_WIDTH
    gq_ref[...] = acc[:, o4:o4 + GLA_WIDTH] * GLA_SCALE
    gk_ref[...] = acc[:, o4 + GLA_WIDTH:o4 + 2 * GLA_WIDTH]
    gv_ref[...] = acc[:, o4 + 2 * GLA_WIDTH:o4 + 3 * GLA_WIDTH]
    gr_ref[...] = acc[:, o4 + 3 * GLA_WIDTH:o4 + 4 * GLA_WIDTH]

    gz = _dot(hb, wz_ref[...])
    z = _dot(gz.astype(BF16), wg_ref[...]) + bg_ref[...]
    log_sig = jnp.minimum(z, 0.0) - jnp.log(1.0 + jnp.exp(-jnp.abs(z)))
    la_ref[...] = log_sig * (1.0 / GLA_TAU)


def _inproj(x, mod3, mod_row, lw, rope_tabs, tm):
    b, n, d = x.shape
    rope = rope_tabs is not None
    row = (lambda bi: bi) if mod_row is None else (lambda bi: mod_row)
    tok = lambda w: pl.BlockSpec((None, tm, w), lambda bi, i: (bi, i, 0))
    in_specs = [
        tok(d),
        pl.BlockSpec((None, 1, 6 * d), lambda bi, i: (row(bi), 0, 0)),
        _full((1, d)), _full((d, MAIN_WIDTH)), _full((d, LANES)), _full((LANES, 2 * GLA_WIDTH)),
        _full((1, 2 * GLA_WIDTH)), _full((1, LANES)), _full((1, LANES)), _full((LANES, LANES)),
    ]
    args = [x, mod3, lw["g1"], lw["w_main"], lw["w_z"], lw["w_gate"], lw["b_gate"], lw["q_g"], lw["k_g"], lw["bd128"]]
    if rope:
        in_specs += [pl.BlockSpec((tm, LANES), lambda bi, i: (i, 0))] * 2
        args += list(rope_tabs)
    sds = lambda w, dt: jax.ShapeDtypeStruct((b, n, w), dt)
    out_shape = [sds(ATT_WIDTH, BF16), sds(4 * LANES, BF16), sds(4 * LANES, BF16), sds(FNET_WIDTH, BF16),
                 sds(GLA_WIDTH, F32), sds(GLA_WIDTH, F32), sds(GLA_WIDTH, F32), sds(GLA_WIDTH, F32),
                 sds(2 * GLA_WIDTH, F32)]
    out_specs = [tok(s.shape[-1]) for s in out_shape]
    return pl.pallas_call(
        functools.partial(_inproj_kernel, rope=rope),
        out_shape=out_shape, grid=(b, n // tm), in_specs=in_specs, out_specs=out_specs,
        compiler_params=_params("parallel", "parallel"),
        name="inproj_rope" if rope else "inproj_ctx",
    )(*args)


def _softmax_pv(scores, values, sink_col):
    m = sink_col
    for s in scores:
        m = jnp.maximum(m, jnp.max(s, axis=-1, keepdims=True))
    den = jnp.exp(sink_col - m)
    out = None
    for s, v in zip(scores, values):
        p = jnp.exp(s - m)
        den = den + jnp.sum(p, axis=-1, keepdims=True)
        pv = _dot(p.astype(BF16), v)
        out = pv if out is None else out + pv
    return out * (1.0 / den)


def _attn_kernel(sink_ref, q_ref, kl_ref, km_ref, kr_ref, kc_ref, vl_ref, vm_ref, vr_ref, vc_ref, o_ref):
    kvh = pl.program_id(1)
    i = pl.program_id(2)
    nb = pl.num_programs(2)
    blk = ATT_BLOCK
    q2 = jnp.concatenate([q_ref[:, 0:LANES], q_ref[:, LANES:2 * LANES]], axis=0)
    row = lax.broadcasted_iota(jnp.int32, (2 * blk, blk), 0) & (blk - 1)
    col = lax.broadcasted_iota(jnp.int32, (2 * blk, blk), 1)
    mask_l = jnp.logical_and(col >= row, i > 0)
    mask_r = jnp.logical_and(col <= row, i < nb - 1)
    upper_rows = lax.broadcasted_iota(jnp.int32, (2 * blk, 1), 0) >= blk
    acc = None
    for par in range(2):
        ks = slice(LANES * par, LANES * (par + 1))
        s_l = jnp.where(mask_l, lax.dot_general(q2, kl_ref[:, ks], NT_DIMS, preferred_element_type=F32), NEG_INF)
        s_m = lax.dot_general(q2, km_ref[:, ks], NT_DIMS, preferred_element_type=F32)
        s_r = jnp.where(mask_r, lax.dot_general(q2, kr_ref[:, ks], NT_DIMS, preferred_element_type=F32), NEG_INF)
        s_c = lax.dot_general(q2, kc_ref[:, ks], NT_DIMS, preferred_element_type=F32)
        base = ATT_HEADS // ATT_KV_HEADS * kvh + par
        sink_col = jnp.where(upper_rows, sink_ref[base + 2], sink_ref[base])
        o = _softmax_pv([s_l, s_m, s_r, s_c], [vl_ref[:, ks], vm_ref[:, ks], vr_ref[:, ks], vc_ref[:, ks]], sink_col)
        acc = o if acc is None else acc + o
    o_ref[:, 0:LANES] = acc[0:blk].astype(o_ref.dtype)
    o_ref[:, LANES:2 * LANES] = acc[blk:2 * blk].astype(o_ref.dtype)


def _attention(q, k4, v4, ck4, cv4, sink):
    b, n, _ = q.shape
    lc = ck4.shape[1]
    nb = n // ATT_BLOCK
    gw = 2 * LANES
    blk3 = lambda f: pl.BlockSpec((None, ATT_BLOCK, gw), f)
    left = lambda bi, h, i: (bi, jnp.maximum(i - 1, 0), h)
    mid = lambda bi, h, i: (bi, i, h)
    right = lambda bi, h, i: (bi, jnp.minimum(i + 1, nb - 1), h)
    ctx = pl.BlockSpec((None, lc, gw), lambda bi, h, i: (bi, 0, h))
    return pl.pallas_call(
        _attn_kernel,
        out_shape=jax.ShapeDtypeStruct((b, n, ATT_WIDTH), BF16),
        grid=(b, ATT_KV_HEADS, nb),
        in_specs=[pl.BlockSpec(memory_space=pltpu.SMEM),
                  blk3(mid), blk3(left), blk3(mid), blk3(right), ctx, blk3(left), blk3(mid), blk3(right), ctx],
        out_specs=blk3(mid),
        compiler_params=_params("parallel", "parallel", "parallel"),
        name="window_attention",
    )(sink, q, k4, k4, k4, ck4, v4, v4, v4, cv4)


def _attn_ctx_kernel(sink_ref, q_ref, kc_ref, vc_ref, o_ref):
    kvh = pl.program_id(1)
    lc = q_ref.shape[0]
    q2 = jnp.concatenate([q_ref[:, 0:LANES], q_ref[:, LANES:2 * LANES]], axis=0)
    upper_rows = lax.broadcasted_iota(jnp.int32, (2 * lc, 1), 0) >= lc
    acc = None
    for par in range(2):
        ks = slice(LANES * par, LANES * (par + 1))
        s_c = lax.dot_general(q2, kc_ref[:, ks], NT_DIMS, preferred_element_type=F32)
        base = ATT_HEADS // ATT_KV_HEADS * kvh + par
        sink_col = jnp.where(upper_rows, sink_ref[base + 2], sink_ref[base])
        o = _softmax_pv([s_c], [vc_ref[:, ks]], sink_col)
        acc = o if acc is None else acc + o
    o_ref[:, 0:LANES] = acc[0:lc].astype(o_ref.dtype)
    o_ref[:, LANES:2 * LANES] = acc[lc:2 * lc].astype(o_ref.dtype)


def _attention_ctx(cq, ck4, cv4, sink):
    b, lc, _ = cq.shape
    gw = 2 * LANES
    spec = pl.BlockSpec((None, lc, gw), lambda bi, h: (bi, 0, h))
    return pl.pallas_call(
        _attn_ctx_kernel,
        out_shape=jax.ShapeDtypeStruct((b, lc, ATT_WIDTH), BF16),
        grid=(b, ATT_KV_HEADS),
        in_specs=[pl.BlockSpec(memory_space=pltpu.SMEM), spec, spec, spec],
        out_specs=spec,
        compiler_params=_params("parallel", "parallel"),
        name="context_attention",
    )(sink, cq, ck4, cv4)


def _dft_consts(n):
    n1, n2 = DFT_N1, n // DFT_N1
    ang = lambda a, m: 2.0 * np.pi * np.outer(np.arange(a), np.arange(a)) / m
    c = HEAD_DIM
    cc, sc = np.cos(ang(c, c)) / np.sqrt(c), np.sin(ang(c, c)) / np.sqrt(c)
    eye = np.eye(FNET_GROUPS)
    w_chan = np.concatenate([np.kron(eye, cc), -np.kron(eye, sc)], axis=1)
    c1, s1 = np.cos(ang(n1, n1)) / np.sqrt(n1), np.sin(ang(n1, n1)) / np.sqrt(n1)
    m1 = np.block([[c1, s1], [-s1, c1]])
    tw = 2.0 * np.pi * np.outer(np.arange(n2), np.arange(n1)) / n
    twr = np.repeat(np.cos(tw)[:, :, None], LANES, axis=2)
    twi = np.repeat(-np.sin(tw)[:, :, None], LANES, axis=2)
    c2, s2 = np.cos(ang(n2, n2)) / np.sqrt(n2), np.sin(ang(n2, n2)) / np.sqrt(n2)
    m3 = np.concatenate([c2, s2], axis=1)
    return tuple(jnp.asarray(a, F32) for a in (w_chan, m1, twr, twi, m3))


def _dft_ctx_consts(lc):
    ang = lambda a, m: 2.0 * np.pi * np.outer(np.arange(a), np.arange(a)) / m
    c = HEAD_DIM
    cc, sc = np.cos(ang(c, c)) / np.sqrt(c), np.sin(ang(c, c)) / np.sqrt(c)
    eye = np.eye(FNET_GROUPS)
    w_chan = np.concatenate([np.kron(eye, cc), -np.kron(eye, sc)], axis=1)
    cl, sl = np.cos(ang(lc, lc)) / np.sqrt(lc), np.sin(ang(lc, lc)) / np.sqrt(lc)
    return jnp.asarray(w_chan, F32), jnp.asarray(np.concatenate([cl, sl], axis=1), F32)


def _four1_kernel(u_ref, wc_ref, m1_ref, twr_ref, twi_ref, p_ref):
    w = FNET_WIDTH
    for tt in range(p_ref.shape[1]):
        z = _dot(u_ref[:, w * tt:w * (tt + 1)], wc_ref[...])
        zs = jnp.concatenate([z[:, 0:w], z[:, w:2 * w]], axis=0).astype(BF16)
        a = _dot(m1_ref[...], zs)
        are, aim = a[0:DFT_N1], a[DFT_N1:2 * DFT_N1]
        twr = jnp.concatenate([twr_ref[tt], twr_ref[tt]], axis=1)
        twi = jnp.concatenate([twi_ref[tt], twi_ref[tt]], axis=1)
        p_ref[0, tt] = (are * twr - aim * twi).astype(p_ref.dtype)
        p_ref[1, tt] = (are * twi + aim * twr).astype(p_ref.dtype)


def _four3_kernel(p_ref, m3_ref, wf_ref, y_ref):
    w = FNET_WIDTH
    y = _dot(m3_ref[...], p_ref[...])
    for kk in range(y_ref.shape[1] // w):
        y_ref[:, w * kk:w * (kk + 1)] = _dot(y[:, w * kk:w * (kk + 1)].astype(BF16), wf_ref[...]).astype(y_ref.dtype)


def _fourier(fu, wf_bd, consts):
    b, n, w = fu.shape
    n1, n2 = DFT_N1, n // DFT_N1
    w_chan, m1, twr, twi, m3 = consts
    w_chan, m1, m3 = w_chan.astype(BF16), m1.astype(BF16), m3.astype(BF16)
    t = 8
    p = pl.pallas_call(
        _four1_kernel,
        out_shape=jax.ShapeDtypeStruct((b, 2, n2, n1, w), BF16),
        grid=(b, n2 // t),
        in_specs=[pl.BlockSpec((None, n1, t * w), lambda bi, j: (bi, 0, j)),
                  _full((w, 2 * w)), _full((2 * n1, 2 * n1)),
                  pl.BlockSpec((t, n1, LANES), lambda bi, j: (j, 0, 0)),
                  pl.BlockSpec((t, n1, LANES), lambda bi, j: (j, 0, 0))],
        out_specs=pl.BlockSpec((None, 2, t, n1, w), lambda bi, j: (bi, 0, j, 0, 0)),
        compiler_params=_params("parallel", "parallel"),
        name="fourier_stage1",
    )(fu.reshape(b, n1, n2 * w), w_chan, m1, twr, twi)
    tc = 2048
    y = pl.pallas_call(
        _four3_kernel,
        out_shape=jax.ShapeDtypeStruct((b, n2, n1 * w), BF16),
        grid=(b, n1 * w // tc),
        in_specs=[pl.BlockSpec((None, 2 * n2, tc), lambda bi, j: (bi, 0, j)),
                  _full((n2, 2 * n2)), _full((w, w))],
        out_specs=pl.BlockSpec((None, n2, tc), lambda bi, j: (bi, 0, j)),
        compiler_params=_params("parallel", "parallel"),
        name="fourier_stage2",
    )(p.reshape(b, 2 * n2, n1 * w), m3, wf_bd)
    return y.reshape(b, n, w)


def _four_ctx_kernel(u_ref, wc_ref, m_ref, wf_ref, y_ref):
    w = FNET_WIDTH
    z = _dot(u_ref[...], wc_ref[...])
    zs = jnp.concatenate([z[:, 0:w], z[:, w:2 * w]], axis=0).astype(BF16)
    y = _dot(m_ref[...], zs)
    y_ref[...] = _dot(y.astype(BF16), wf_ref[...]).astype(y_ref.dtype)


def _fourier_ctx(cfu, wf_bd, consts):
    b, lc, w = cfu.shape
    w_chan, m = consts[0].astype(BF16), consts[1].astype(BF16)
    return pl.pallas_call(
        _four_ctx_kernel,
        out_shape=jax.ShapeDtypeStruct((b, lc, w), BF16),
        grid=(b,),
        in_specs=[pl.BlockSpec((None, lc, w), lambda bi: (bi, 0, 0)), _full((w, 2 * w)), _full((lc, 2 * lc)),
                  _full((w, w))],
        out_specs=pl.BlockSpec((None, lc, w), lambda bi: (bi, 0, 0)),
        compiler_params=_params("parallel"),
        name="fourier_context",
    )(cfu, w_chan, m, wf_bd)


def _gla_chunk(q, k, v, la, st, reverse, consts):
    tri, att_mask, head_lane, bd_mask = consts
    c = GLA_CHUNK
    la_hi = la.astype(BF16)
    la_lo = (la - la_hi.astype(F32)).astype(BF16)
    bcum = _dot(tri, la_hi) + _dot(tri, la_lo)
    btot = bcum[0:1] if reverse else bcum[c - 1:c]
    q_in = q * jnp.exp(bcum)
    k_in = (k * jnp.exp(-bcum)).astype(BF16)
    k_out = (k * jnp.exp(btot - bcum)).astype(BF16)
    vb = v.astype(BF16)
    zero = jnp.zeros_like(q_in)
    q_stack = jnp.concatenate([jnp.where(head_lane == h, q_in, zero) for h in range(GLA_HEADS)], axis=0)
    att = lax.dot_general(q_stack.astype(BF16), k_in, NT_DIMS, preferred_element_type=F32)
    att = jnp.where(att_mask, att, 0.0)
    r = _dot(att.astype(BF16), vb)
    o_intra = None
    for h in range(GLA_HEADS):
        part = jnp.where(head_lane == h, r[c * h:c * (h + 1)], 0.0)
        o_intra = part if o_intra is None else o_intra + part
    o_inter = lax.dot_general(q_in.astype(BF16), st.astype(BF16), NT_DIMS, preferred_element_type=F32)
    ut = lax.dot_general(vb, k_out, TN_DIMS, preferred_element_type=F32)
    st_new = st * jnp.exp(btot) + jnp.where(bd_mask, ut, 0.0)
    return o_intra + o_inter, st_new


def _gla_kernel(qf_ref, kf_ref, vf_ref, laf_ref, qb_ref, kb_ref, vb_ref, lab_ref, s0_ref,
                of_ref, ob_ref, sfin_ref, stf_ref, stb_ref):
    i = pl.program_id(1)
    c = GLA_CHUNK
    nchunk = qf_ref.shape[0] // c

    @pl.when(i == 0)
    def _():
        stf_ref[...] = s0_ref[0]
        stb_ref[...] = s0_ref[1]

    r64 = lax.broadcasted_iota(jnp.int32, (c, c), 0)
    c64 = lax.broadcasted_iota(jnp.int32, (c, c), 1)
    tri_f = (c64 <= r64).astype(BF16)
    tri_b = (c64 >= r64).astype(BF16)
    ar = lax.broadcasted_iota(jnp.int32, (GLA_WIDTH, c), 0) & (c - 1)
    ac = lax.broadcasted_iota(jnp.int32, (GLA_WIDTH, c), 1)
    head_shift = HEAD_DIM.bit_length() - 1
    head_lane = lax.broadcasted_iota(jnp.int32, (1, GLA_WIDTH), 1) >> head_shift
    br = lax.broadcasted_iota(jnp.int32, (GLA_WIDTH, GLA_WIDTH), 0) >> head_shift
    bc = lax.broadcasted_iota(jnp.int32, (GLA_WIDTH, GLA_WIDTH), 1) >> head_shift
    bd_mask = br == bc
    consts_f = (tri_f, ac <= ar, head_lane, bd_mask)
    consts_b = (tri_b, ac >= ar, head_lane, bd_mask)

    st_f = stf_ref[...]
    st_b = stb_ref[...]
    for cf in range(nchunk):
        sl = slice(c * cf, c * (cf + 1))
        o, st_f = _gla_chunk(qf_ref[sl], kf_ref[sl], vf_ref[sl], laf_ref[sl], st_f, False, consts_f)
        of_ref[sl] = o
        cb = nchunk - 1 - cf
        sl = slice(c * cb, c * (cb + 1))
        o, st_b = _gla_chunk(qb_ref[sl], kb_ref[sl], vb_ref[sl], lab_ref[sl], st_b, True, consts_b)
        ob_ref[sl] = o
    stf_ref[...] = st_f
    stb_ref[...] = st_b

    @pl.when(i == pl.num_programs(1) - 1)
    def _():
        sfin_ref[0] = st_f
        sfin_ref[1] = st_b


def _gla(gq, gk, gv, la, s0, tb):
    b, n, w = gq.shape
    nblk = n // tb
    fwd = lambda bi, i: (bi, i, 0)
    bwd = lambda bi, i: (bi, nblk - 1 - i, 0)
    bwd_la = lambda bi, i: (bi, nblk - 1 - i, 1)
    tok = lambda f: pl.BlockSpec((None, tb, w), f)
    state = pl.BlockSpec((None, 2, w, w), lambda bi, i: (bi, 0, 0, 0))
    return pl.pallas_call(
        _gla_kernel,
        out_shape=[jax.ShapeDtypeStruct((b, n, w), F32), jax.ShapeDtypeStruct((b, n, w), F32),
                   jax.ShapeDtypeStruct((b, 2, w, w), F32)],
        grid=(b, nblk),
        in_specs=[tok(fwd), tok(fwd), tok(fwd), tok(fwd), tok(bwd), tok(bwd), tok(bwd), tok(bwd_la), state],
        out_specs=[tok(fwd), tok(bwd), state],
        scratch_shapes=[pltpu.VMEM((w, w), F32), pltpu.VMEM((w, w), F32)],
        compiler_params=_params("parallel", "arbitrary"),
        name="gla_scan",
    )(gq, gk, gv, la, gq, gk, gv, la, s0)


def _outproj_kernel(att_ref, four_ref, of_ref, ob_ref, r_ref, x_ref, mod_ref, gg_ref, bd_ref, wo_ref, o_ref):
    d = D_MODEL
    o = of_ref[...] + ob_ref[...]
    y = o * lax.rsqrt(_group_mean(o, bd_ref) + EPS) * gg_ref[...]
    y = y * _silu(r_ref[...])
    a0, a1, a2 = ATT_WIDTH, ATT_WIDTH + FNET_WIDTH, ATT_WIDTH + FNET_WIDTH + GLA_WIDTH
    mix = (_dot(att_ref[...], wo_ref[0:a0]) + _dot(four_ref[...], wo_ref[a0:a1])
           + _dot(y.astype(BF16), wo_ref[a1:a2]))
    gate = mod_ref[:, 2 * d:3 * d]
    o_ref[...] = x_ref[...] + gate * mix


def _outproj(att, four, of, ob, gr, x, mod3, mod_row, lw, tm):
    b, n, d = x.shape
    row = (lambda bi: bi) if mod_row is None else (lambda bi: mod_row)
    tok = lambda w: pl.BlockSpec((None, tm, w), lambda bi, i: (bi, i, 0))
    return pl.pallas_call(
        _outproj_kernel,
        out_shape=jax.ShapeDtypeStruct((b, n, d), F32),
        grid=(b, n // tm),
        in_specs=[tok(ATT_WIDTH), tok(FNET_WIDTH), tok(GLA_WIDTH), tok(GLA_WIDTH), tok(GLA_WIDTH), tok(d),
                  pl.BlockSpec((None, 1, 6 * d), lambda bi, i: (row(bi), 0, 0)),
                  _full((1, GLA_WIDTH)), _full((GLA_WIDTH, GLA_WIDTH)), _full((d, d))],
        out_specs=tok(d),
        compiler_params=_params("parallel", "parallel"),
        name="outproj",
    )(att, four, of, ob, gr, x, mod3, lw["gla_g"], lw["bd256"], lw["w_out"])


def _ffn_kernel(x_ref, mod_ref, g2_ref, wg_ref, wu_ref, wo_ref, o_ref, h_ref, acc_ref):
    d = D_MODEL
    j = pl.program_id(2)

    @pl.when(j == 0)
    def _():
        x = x_ref[...]
        ms = jnp.mean(x * x, axis=-1, keepdims=True)
        h = (x * lax.rsqrt(ms + EPS) * g2_ref[...]) * (1.0 + mod_ref[:, 4 * d:5 * d]) + mod_ref[:, 3 * d:4 * d]
        h_ref[...] = h.astype(BF16)
        acc_ref[...] = jnp.zeros_like(acc_ref)

    hb = h_ref[...]
    g = _dot(hb, wg_ref[...])
    u = _dot(hb, wu_ref[...])
    acc_ref[...] += _dot((_silu(g) * u).astype(BF16), wo_ref[...])

    @pl.when(j == pl.num_programs(2) - 1)
    def _():
        o_ref[...] = x_ref[...] + mod_ref[:, 5 * d:6 * d] * acc_ref[...]


def _ffn(x, mod3, mod_row, lw, tm, th):
    b, n, d = x.shape
    nh = FFN_HIDDEN // th
    row = (lambda bi: bi) if mod_row is None else (lambda bi: mod_row)
    tok = pl.BlockSpec((None, tm, d), lambda bi, i, j: (bi, i, 0))
    return pl.pallas_call(
        _ffn_kernel,
        out_shape=jax.ShapeDtypeStruct((b, n, d), F32),
        grid=(b, n // tm, nh),
        in_specs=[tok,
                  pl.BlockSpec((None, 1, 6 * d), lambda bi, i, j: (row(bi), 0, 0)),
                  pl.BlockSpec((1, d), lambda bi, i, j: (0, 0)),
                  pl.BlockSpec((d, th), lambda bi, i, j: (0, j)),
                  pl.BlockSpec((d, th), lambda bi, i, j: (0, j + nh)),
                  pl.BlockSpec((th, d), lambda bi, i, j: (j, 0))],
        out_specs=tok,
        scratch_shapes=[pltpu.VMEM((tm, d), BF16), pltpu.VMEM((tm, d), F32)],
        compiler_params=_params("parallel", "parallel", "arbitrary"),
        name="ffn",
    )(x, mod3, lw["g2"], lw["w_ffn_in"], lw["w_ffn_in"], lw["w_ffn_out"])


def _rope_tables(n):
    axis_dim = HEAD_DIM // 2
    inv_freq = ROPE_BASE ** (-jnp.arange(0, axis_dim, 2, dtype=F32) / axis_dim)
    t = jnp.arange(n)
    row = (t // GRID_W).astype(F32)
    colp = (t % GRID_W).astype(F32)
    ang_r = row[:, None] * inv_freq[None, :]
    ang_c = colp[:, None] * inv_freq[None, :]
    cos = jnp.concatenate([jnp.cos(ang_r)] * 2 + [jnp.cos(ang_c)] * 2, axis=1)
    sin = jnp.concatenate([-jnp.sin(ang_r), jnp.sin(ang_r), -jnp.sin(ang_c), jnp.sin(ang_c)], axis=1)
    return jnp.concatenate([cos, cos], axis=1), jnp.concatenate([sin, sin], axis=1)


def _block_diag_mean(width):
    g = np.arange(width) // HEAD_DIM
    return jnp.asarray((g[:, None] == g[None, :]) / HEAD_DIM, BF16)


def _layer_weights(l, w_in, g_norm1, q_norm_g, k_norm_g, w_fourier, wgf, bgf, wgb, bgb, gla_norm_g, w_out,
                   g_norm2, w_ffn_in, w_ffn_out):
    r = GLA_GATE_RANK
    w_l = w_in[l]
    w_z = jnp.zeros((D_MODEL, LANES), F32).at[:, 0:2 * r].set(w_l[:, MAIN_WIDTH:MAIN_WIDTH + 2 * r])
    w_gate = jnp.zeros((LANES, 2 * GLA_WIDTH), F32)
    w_gate = w_gate.at[0:r, 0:GLA_WIDTH].set(wgf[l]).at[r:2 * r, GLA_WIDTH:].set(wgb[l])
    wf_bd = jnp.zeros((FNET_WIDTH, FNET_WIDTH), F32)
    for g in range(FNET_GROUPS):
        wf_bd = wf_bd.at[HEAD_DIM * g:HEAD_DIM * (g + 1), HEAD_DIM * g:HEAD_DIM * (g + 1)].set(w_fourier[l, g])
    return {
        "g1": g_norm1[l][None, :],
        "w_main": w_l[:, 0:MAIN_WIDTH].astype(BF16),
        "w_z": w_z.astype(BF16),
        "w_gate": w_gate.astype(BF16),
        "b_gate": jnp.concatenate([bgf[l], bgb[l]])[None, :],
        "q_g": jnp.tile(q_norm_g[l], 2)[None, :],
        "k_g": jnp.tile(k_norm_g[l], 2)[None, :],
        "bd128": _block_diag_mean(LANES),
        "bd256": _block_diag_mean(GLA_WIDTH),
        "wf_bd": wf_bd.astype(BF16),
        "gla_g": jnp.tile(gla_norm_g[l], GLA_HEADS)[None, :],
        "w_out": w_out[l].astype(BF16),
        "g2": g_norm2[l][None, :],
        "w_ffn_in": w_ffn_in[l].astype(BF16),
        "w_ffn_out": w_ffn_out[l].astype(BF16),
    }


def kernel(x, c, ctx, c_ctx, w_mod, b_mod, g_norm1, w_in, q_norm_g, k_norm_g, attn_sink, w_fourier, gla_w_gate_f,
           gla_b_gate_f, gla_w_gate_b, gla_b_gate_b, gla_norm_g, w_out, g_norm2, w_ffn_in, w_ffn_out):
    b, n, d = x.shape
    lc = ctx.shape[1]
    depth = w_mod.shape[0]
    assert d == D_MODEL and b <= 7 and n % 512 == 0 and n % (DFT_N1 * 8) == 0 and lc % GLA_CHUNK == 0

    cc = jnp.zeros((8, d), F32).at[0:b].set(c).at[b].set(c_ctx)
    mod = _modulation(cc, w_mod, b_mod)
    rope_tabs = _rope_tables(n)
    dft = _dft_consts(n)
    dft_ctx = _dft_ctx_consts(lc)
    tm_lat, tm_ctx, tb_lat = 512, lc, 512
    xc = ctx
    for l in range(depth):
        need_ctx = l < depth - 1
        lw = _layer_weights(l, w_in, g_norm1, q_norm_g, k_norm_g, w_fourier, gla_w_gate_f, gla_b_gate_f,
                            gla_w_gate_b, gla_b_gate_b, gla_norm_g, w_out, g_norm2, w_ffn_in, w_ffn_out)
        mod3 = mod[l].reshape(8, 1, 6 * d)
        sink = attn_sink[l]
        cq, ck4, cv4, cfu, cgq, cgk, cgv, cgr, cla = _inproj(xc, mod3, b, lw, None, tm_ctx)
        q, k4, v4, fu, gq, gk, gv, gr, la = _inproj(x, mod3, None, lw, rope_tabs, tm_lat)
        att = _attention(q, k4, v4, ck4, cv4, sink)
        four = _fourier(fu, lw["wf_bd"], dft)
        s_zero = jnp.zeros((b, 2, GLA_WIDTH, GLA_WIDTH), F32)
        ocf, ocb, s_ctx = _gla(cgq, cgk, cgv, cla, s_zero, lc)
        olf, olb, _ = _gla(gq, gk, gv, la, s_ctx, tb_lat)
        x = _outproj(att, four, olf, olb, gr, x, mod3, None, lw, tm_lat)
        x = _ffn(x, mod3, None, lw, 1024, 256)
        if need_ctx:
            att_c = _attention_ctx(cq, ck4, cv4, sink)
            four_c = _fourier_ctx(cfu, lw["wf_bd"], dft_ctx)
            xc = _outproj(att_c, four_c, ocf, ocb, cgr, xc, mod3, b, lw, tm_ctx)
            xc = _ffn(xc, mod3, b, lw, tm_ctx, 256)
    return x
```

```python
import functools

import numpy as np
import jax
import jax.numpy as jnp
from jax import lax
from jax.experimental import pallas as pl
from jax.experimental.pallas import tpu as pltpu

F32 = jnp.float32
BF16 = jnp.bfloat16

D_MODEL = 1024
HEAD_DIM = 64
GRID_W = 64
ROPE_BASE = 10000.0
ATT_HEADS = 8
ATT_KV_HEADS = 2
ATT_WIDTH = ATT_HEADS * HEAD_DIM
KV_WIDTH = ATT_KV_HEADS * HEAD_DIM
ATT_BLOCK = 128
ATT_SCALE = HEAD_DIM ** -0.5
NEG_INF = -1e30
FNET_GROUPS = 4
FNET_WIDTH = FNET_GROUPS * HEAD_DIM
GLA_HEADS = 4
GLA_WIDTH = GLA_HEADS * HEAD_DIM
GLA_GATE_RANK = 16
GLA_TAU = 16.0
GLA_CHUNK = 64
GLA_SCALE = HEAD_DIM ** -0.5
MAIN_WIDTH = ATT_WIDTH + 2 * KV_WIDTH + FNET_WIDTH + 4 * GLA_WIDTH
FFN_HIDDEN = 2816
FFN_CHUNK = 256
EPS = 1e-6
LANES = 128
DFT_N1 = 128
VMEM_LIMIT = 56 * 1024 * 1024

NT_DIMS = (((1,), (1,)), ((), ()))
TN_DIMS = (((0,), (0,)), ((), ()))


def _params(*sem):
    return pltpu.CompilerParams(dimension_semantics=sem, vmem_limit_bytes=VMEM_LIMIT)


def _dot(a, b):
    return jnp.dot(a, b, preferred_element_type=F32)


def _silu(x):
    return x / (1.0 + jnp.exp(-x))


def _full(shape):
    nd = len(shape)
    return pl.BlockSpec(shape, lambda *_: (0,) * nd)


def _group_mean(t, bd_ref):
    return _dot((t * t).astype(BF16), bd_ref[...])


def _mod_kernel(c_ref, w_ref, b_ref, o_ref):
    s = _silu(c_ref[...]).astype(BF16)
    o_ref[...] = _dot(s, w_ref[...].astype(BF16)) + b_ref[...]


def _modulation(cc, w_mod, b_mod):
    depth, d, width = w_mod.shape
    tn = 1536
    return pl.pallas_call(
        _mod_kernel,
        out_shape=jax.ShapeDtypeStruct((depth, 8, width), F32),
        grid=(depth, width // tn),
        in_specs=[
            _full((8, d)),
            pl.BlockSpec((None, d, tn), lambda l, j: (l, 0, j)),
            pl.BlockSpec((None, 1, tn), lambda l, j: (l, 0, j)),
        ],
        out_specs=pl.BlockSpec((None, 8, tn), lambda l, j: (l, 0, j)),
        compiler_params=_params("parallel", "parallel"),
        name="modulation",
    )(cc, w_mod, b_mod.reshape(depth, 1, width))


def _inproj_kernel(*refs, rope):
    if rope:
        (x_ref, mod_ref, g1_ref, wm_ref, wz_ref, wg_ref, bg_ref, qg_ref, kg_ref, bd_ref, cos_ref, sin_ref,
         q_ref, k4_ref, v4_ref, fu_ref, gq_ref, gk_ref, gv_ref, gr_ref, la_ref) = refs
    else:
        (x_ref, mod_ref, g1_ref, wm_ref, wz_ref, wg_ref, bg_ref, qg_ref, kg_ref, bd_ref,
         q_ref, k4_ref, v4_ref, fu_ref, gq_ref, gk_ref, gv_ref, gr_ref, la_ref) = refs
    d = D_MODEL
    x = x_ref[...]
    mod = mod_ref[...]
    sh, sc = mod[:, 0:d], mod[:, d:2 * d]
    ms = jnp.mean(x * x, axis=-1, keepdims=True)
    h = (x * lax.rsqrt(ms + EPS) * g1_ref[...]) * (1.0 + sc) + sh
    hb = h.astype(BF16)
    acc = _dot(hb, wm_ref[...])

    lane = lax.broadcasted_iota(jnp.int32, (1, LANES), 1)
    lo = lane < HEAD_DIM
    second_half = (lane & 16) != 0

    def head_norm(t, g_ref):
        return t * lax.rsqrt(_group_mean(t, bd_ref) + EPS) * g_ref[...]

    def rotary(t):
        if not rope:
            return t
        partner = jnp.where(second_half, pltpu.roll(t, 16, 1), pltpu.roll(t, LANES - 16, 1))
        return t * cos_ref[...] + partner * sin_ref[...]

    def spread(t, out_ref, idle):
        tr = pltpu.roll(t, HEAD_DIM, 1)
        fill = jnp.full_like(t, idle)
        out_ref[:, 0:128] = jnp.where(lo, t, fill).astype(out_ref.dtype)
        out_ref[:, 128:256] = jnp.where(lo, fill, tr).astype(out_ref.dtype)
        out_ref[:, 256:384] = jnp.where(lo, tr, fill).astype(out_ref.dtype)
        out_ref[:, 384:512] = jnp.where(lo, fill, t).astype(out_ref.dtype)

    for j in range(ATT_WIDTH // LANES):
        t = rotary(head_norm(acc[:, LANES * j:LANES * (j + 1)], qg_ref)) * ATT_SCALE
        q_ref[:, LANES * j:LANES * (j + 1)] = t.astype(q_ref.dtype)
    o1 = ATT_WIDTH
    spread(rotary(head_norm(acc[:, o1:o1 + KV_WIDTH], kg_ref)), k4_ref, 0.0)
    o2 = o1 + KV_WIDTH
    spread(acc[:, o2:o2 + KV_WIDTH], v4_ref, 1.0)
    o3 = o2 + KV_WIDTH
    fu_ref[...] = acc[:, o3:o3 + FNET_WIDTH].astype(fu_ref.dtype)
    o4 = o3 + FNET_WIDTH
    gq_ref[...] = acc[:, o4:o4 + GLA_WIDTH] * GLA_SCALE
    gk_ref[...] = acc[:, o4 + GLA_WIDTH:o4 + 2 * GLA_WIDTH]
    gv_ref[...] = acc[:, o4 + 2 * GLA_WIDTH:o4 + 3 * GLA_WIDTH]
    gr_ref[...] = acc[:, o4 + 3 * GLA_WIDTH:o4 + 4 * GLA_WIDTH]

    gz = _dot(hb, wz_ref[...])
    z = _dot(gz.astype(BF16), wg_ref[...]) + bg_ref[...]
    log_sig = jnp.minimum(z, 0.0) - jnp.log(1.0 + jnp.exp(-jnp.abs(z)))
    la_ref[...] = log_sig * (1.0 / GLA_TAU)


def _inproj(x, mod3, mod_row, lw, rope_tabs, tm):
    b, n, d = x.shape
    rope = rope_tabs is not None
    row = (lambda bi: bi) if mod_row is None else (lambda bi: mod_row)
    tok = lambda w: pl.BlockSpec((None, tm, w), lambda bi, i: (bi, i, 0))
    in_specs = [
        tok(d),
        pl.BlockSpec((None, 1, 6 * d), lambda bi, i: (row(bi), 0, 0)),
        _full((1, d)), _full((d, MAIN_WIDTH)), _full((d, LANES)), _full((LANES, 2 * GLA_WIDTH)),
        _full((1, 2 * GLA_WIDTH)), _full((1, LANES)), _full((1, LANES)), _full((LANES, LANES)),
    ]
    args = [x, mod3, lw["g1"], lw["w_main"], lw["w_z"], lw["w_gate"], lw["b_gate"], lw["q_g"], lw["k_g"], lw["bd128"]]
    if rope:
        in_specs += [pl.BlockSpec((tm, LANES), lambda bi, i: (i, 0))] * 2
        args += list(rope_tabs)
    sds = lambda w, dt: jax.ShapeDtypeStruct((b, n, w), dt)
    out_shape = [sds(ATT_WIDTH, BF16), sds(4 * LANES, BF16), sds(4 * LANES, BF16), sds(FNET_WIDTH, BF16),
                 sds(GLA_WIDTH, F32), sds(GLA_WIDTH, F32), sds(GLA_WIDTH, F32), sds(GLA_WIDTH, F32),
                 sds(2 * GLA_WIDTH, F32)]
    out_specs = [tok(s.shape[-1]) for s in out_shape]
    return pl.pallas_call(
        functools.partial(_inproj_kernel, rope=rope),
        out_shape=out_shape, grid=(b, n // tm), in_specs=in_specs, out_specs=out_specs,
        compiler_params=_params("parallel", "parallel"),
        name="inproj_rope" if rope else "inproj_ctx",
    )(*args)


def _scores(q2, key_tiles, masks):
    cols = []
    for k, tile_masks in zip(key_tiles, masks):
        s = lax.dot_general(q2, k, NT_DIMS, preferred_element_type=F32)
        for j, mk in enumerate(tile_masks):
            c = s[:, LANES * j:LANES * (j + 1)]
            cols.append(c if mk is None else jnp.where(mk, c, NEG_INF))
    return cols


def _softmax_pv(cols, val_tiles, sink_col):
    mx = cols[0]
    for c in cols[1:]:
        mx = jnp.maximum(mx, c)
    m = jnp.maximum(jnp.max(mx, axis=-1, keepdims=True), sink_col)
    p = jnp.concatenate([jnp.exp(c - m).astype(BF16) for c in cols], axis=1)
    pv = _dot(p, jnp.concatenate(val_tiles, axis=0))
    den = pltpu.roll(pv, HEAD_DIM, 1) + jnp.exp(sink_col - m)
    return pv / den


def _attn_kernel(sink_ref, q_ref, kl_ref, km_ref, kr_ref, kc_ref, vl_ref, vm_ref, vr_ref, vc_ref, o_ref):
    i = pl.program_id(1)
    nb = pl.num_programs(1)
    blk = ATT_BLOCK
    group = ATT_HEADS // ATT_KV_HEADS
    row = lax.broadcasted_iota(jnp.int32, (2 * blk, blk), 0) & (blk - 1)
    col = lax.broadcasted_iota(jnp.int32, (2 * blk, blk), 1)
    mask_l = jnp.logical_and(col >= row, i > 0)
    mask_r = jnp.logical_and(col <= row, i < nb - 1)
    masks = [(mask_l, None), (mask_r, None), (None,)]
    upper_rows = lax.broadcasted_iota(jnp.int32, (2 * blk, 1), 0) >= blk
    lo = lax.broadcasted_iota(jnp.int32, (1, LANES), 1) < HEAD_DIM

    def tiles(l, m_, r, c, ks):
        return [jnp.concatenate([l[:, ks], m_[:, ks]], axis=0), jnp.concatenate([r[:, ks], c[0:blk, ks]], axis=0),
                c[blk:, ks]]

    slots = [(kvh, par) for kvh in range(ATT_KV_HEADS) for par in range(2)]
    lane_slice = lambda kvh, par: slice(2 * LANES * kvh + LANES * par, 2 * LANES * kvh + LANES * (par + 1))
    cols = {}
    for kvh, par in slots:
        qs = 2 * LANES * kvh
        q2 = jnp.concatenate([q_ref[:, qs:qs + LANES], q_ref[:, qs + LANES:qs + 2 * LANES]], axis=0)
        cols[kvh, par] = _scores(q2, tiles(kl_ref, km_ref, kr_ref, kc_ref, lane_slice(kvh, par)), masks)
    outs = {}
    for kvh, par in slots:
        base = group * kvh + par
        sink_col = jnp.where(upper_rows, sink_ref[base + 2], sink_ref[base])
        outs[kvh, par] = _softmax_pv(cols[kvh, par], tiles(vl_ref, vm_ref, vr_ref, vc_ref, lane_slice(kvh, par)),
                                     sink_col)
    for kvh in range(ATT_KV_HEADS):
        qs = 2 * LANES * kvh
        o = jnp.where(lo, outs[kvh, 0], outs[kvh, 1])
        o_ref[:, qs:qs + LANES] = o[0:blk].astype(o_ref.dtype)
        o_ref[:, qs + LANES:qs + 2 * LANES] = o[blk:2 * blk].astype(o_ref.dtype)


def _attention(q, k4, v4, ck4, cv4, sink):
    b, n, _ = q.shape
    lc = ck4.shape[1]
    assert lc == 2 * ATT_BLOCK
    nb = n // ATT_BLOCK
    blk3 = lambda f: pl.BlockSpec((None, ATT_BLOCK, 4 * LANES), f)
    left = lambda bi, i: (bi, jnp.maximum(i - 1, 0), 0)
    mid = lambda bi, i: (bi, i, 0)
    right = lambda bi, i: (bi, jnp.minimum(i + 1, nb - 1), 0)
    ctx = pl.BlockSpec((None, lc, 4 * LANES), lambda bi, i: (bi, 0, 0))
    return pl.pallas_call(
        _attn_kernel,
        out_shape=jax.ShapeDtypeStruct((b, n, ATT_WIDTH), BF16),
        grid=(b, nb),
        in_specs=[pl.BlockSpec(memory_space=pltpu.SMEM),
                  blk3(mid), blk3(left), blk3(mid), blk3(right), ctx, blk3(left), blk3(mid), blk3(right), ctx],
        out_specs=blk3(mid),
        compiler_params=_params("parallel", "parallel"),
        name="window_attention",
    )(sink, q, k4, k4, k4, ck4, v4, v4, v4, cv4)


def _attn_ctx_kernel(sink_ref, q_ref, kc_ref, vc_ref, o_ref):
    kvh = pl.program_id(1)
    lc = q_ref.shape[0]
    q2 = jnp.concatenate([q_ref[:, 0:LANES], q_ref[:, LANES:2 * LANES]], axis=0)
    upper_rows = lax.broadcasted_iota(jnp.int32, (2 * lc, 1), 0) >= lc
    lo = lax.broadcasted_iota(jnp.int32, (1, LANES), 1) < HEAD_DIM
    outs = []
    for par in range(2):
        ks = slice(LANES * par, LANES * (par + 1))
        base = ATT_HEADS // ATT_KV_HEADS * kvh + par
        sink_col = jnp.where(upper_rows, sink_ref[base + 2], sink_ref[base])
        cols = _scores(q2, [kc_ref[:, ks]], [(None,) * (lc // LANES)])
        outs.append(_softmax_pv(cols, [vc_ref[:, ks]], sink_col))
    o = jnp.where(lo, outs[0], outs[1])
    o_ref[:, 0:LANES] = o[0:lc].astype(o_ref.dtype)
    o_ref[:, LANES:2 * LANES] = o[lc:2 * lc].astype(o_ref.dtype)


def _attention_ctx(cq, ck4, cv4, sink):
    b, lc, _ = cq.shape
    gw = 2 * LANES
    spec = pl.BlockSpec((None, lc, gw), lambda bi, h: (bi, 0, h))
    return pl.pallas_call(
        _attn_ctx_kernel,
        out_shape=jax.ShapeDtypeStruct((b, lc, ATT_WIDTH), BF16),
        grid=(b, ATT_KV_HEADS),
        in_specs=[pl.BlockSpec(memory_space=pltpu.SMEM), spec, spec, spec],
        out_specs=spec,
        compiler_params=_params("parallel", "parallel"),
        name="context_attention",
    )(sink, cq, ck4, cv4)


def _dft_consts(n):
    n1, n2 = DFT_N1, n // DFT_N1
    ang = lambda a, m: 2.0 * np.pi * np.outer(np.arange(a), np.arange(a)) / m
    c = HEAD_DIM
    cc, sc = np.cos(ang(c, c)) / np.sqrt(c), np.sin(ang(c, c)) / np.sqrt(c)
    eye = np.eye(FNET_GROUPS)
    w_chan = np.concatenate([np.kron(eye, cc), -np.kron(eye, sc)], axis=1)
    c1, s1 = np.cos(ang(n1, n1)) / np.sqrt(n1), np.sin(ang(n1, n1)) / np.sqrt(n1)
    m1 = np.block([[c1, s1], [-s1, c1]])
    tw = 2.0 * np.pi * np.outer(np.arange(n2), np.arange(n1)) / n
    twr = np.repeat(np.cos(tw)[:, :, None], LANES, axis=2)
    twi = np.repeat(-np.sin(tw)[:, :, None], LANES, axis=2)
    c2, s2 = np.cos(ang(n2, n2)) / np.sqrt(n2), np.sin(ang(n2, n2)) / np.sqrt(n2)
    m3 = np.concatenate([c2, s2], axis=1)
    return tuple(jnp.asarray(a, F32) for a in (w_chan, m1, twr, twi, m3))


def _dft_ctx_consts(lc):
    ang = lambda a, m: 2.0 * np.pi * np.outer(np.arange(a), np.arange(a)) / m
    c = HEAD_DIM
    cc, sc = np.cos(ang(c, c)) / np.sqrt(c), np.sin(ang(c, c)) / np.sqrt(c)
    eye = np.eye(FNET_GROUPS)
    w_chan = np.concatenate([np.kron(eye, cc), -np.kron(eye, sc)], axis=1)
    cl, sl = np.cos(ang(lc, lc)) / np.sqrt(lc), np.sin(ang(lc, lc)) / np.sqrt(lc)
    return jnp.asarray(w_chan, F32), jnp.asarray(np.concatenate([cl, sl], axis=1), F32)


def _four1_kernel(u_ref, wc_ref, m1_ref, twr_ref, twi_ref, p_ref):
    w = FNET_WIDTH
    for tt in range(p_ref.shape[1]):
        z = _dot(u_ref[:, w * tt:w * (tt + 1)], wc_ref[...])
        zs = jnp.concatenate([z[:, 0:w], z[:, w:2 * w]], axis=0).astype(BF16)
        a = _dot(m1_ref[...], zs)
        are, aim = a[0:DFT_N1], a[DFT_N1:2 * DFT_N1]
        twr = jnp.concatenate([twr_ref[tt], twr_ref[tt]], axis=1)
        twi = jnp.concatenate([twi_ref[tt], twi_ref[tt]], axis=1)
        p_ref[0, tt] = (are * twr - aim * twi).astype(p_ref.dtype)
        p_ref[1, tt] = (are * twi + aim * twr).astype(p_ref.dtype)


def _four3_kernel(p_ref, m3_ref, wf_ref, y_ref):
    w = FNET_WIDTH
    y = _dot(m3_ref[...], p_ref[...])
    for kk in range(y_ref.shape[1] // w):
        y_ref[:, w * kk:w * (kk + 1)] = _dot(y[:, w * kk:w * (kk + 1)].astype(BF16), wf_ref[...]).astype(y_ref.dtype)


def _fourier(fu, wf_bd, consts):
    b, n, w = fu.shape
    n1, n2 = DFT_N1, n // DFT_N1
    w_chan, m1, twr, twi, m3 = consts
    w_chan, m1, m3 = w_chan.astype(BF16), m1.astype(BF16), m3.astype(BF16)
    t = 8
    p = pl.pallas_call(
        _four1_kernel,
        out_shape=jax.ShapeDtypeStruct((b, 2, n2, n1, w), BF16),
        grid=(b, n2 // t),
        in_specs=[pl.BlockSpec((None, n1, t * w), lambda bi, j: (bi, 0, j)),
                  _full((w, 2 * w)), _full((2 * n1, 2 * n1)),
                  pl.BlockSpec((t, n1, LANES), lambda bi, j: (j, 0, 0)),
                  pl.BlockSpec((t, n1, LANES), lambda bi, j: (j, 0, 0))],
        out_specs=pl.BlockSpec((None, 2, t, n1, w), lambda bi, j: (bi, 0, j, 0, 0)),
        compiler_params=_params("parallel", "parallel"),
        name="fourier_stage1",
    )(fu.reshape(b, n1, n2 * w), w_chan, m1, twr, twi)
    tc = 2048
    y = pl.pallas_call(
        _four3_kernel,
        out_shape=jax.ShapeDtypeStruct((b, n2, n1 * w), BF16),
        grid=(b, n1 * w // tc),
        in_specs=[pl.BlockSpec((None, 2 * n2, tc), lambda bi, j: (bi, 0, j)),
                  _full((n2, 2 * n2)), _full((w, w))],
        out_specs=pl.BlockSpec((None, n2, tc), lambda bi, j: (bi, 0, j)),
        compiler_params=_params("parallel", "parallel"),
        name="fourier_stage2",
    )(p.reshape(b, 2 * n2, n1 * w), m3, wf_bd)
    return y.reshape(b, n, w)


def _four_ctx_kernel(u_ref, wc_ref, m_ref, wf_ref, y_ref):
    w = FNET_WIDTH
    z = _dot(u_ref[...], wc_ref[...])
    zs = jnp.concatenate([z[:, 0:w], z[:, w:2 * w]], axis=0).astype(BF16)
    y = _dot(m_ref[...], zs)
    y_ref[...] = _dot(y.astype(BF16), wf_ref[...]).astype(y_ref.dtype)


def _fourier_ctx(cfu, wf_bd, consts):
    b, lc, w = cfu.shape
    w_chan, m = consts[0].astype(BF16), consts[1].astype(BF16)
    return pl.pallas_call(
        _four_ctx_kernel,
        out_shape=jax.ShapeDtypeStruct((b, lc, w), BF16),
        grid=(b,),
        in_specs=[pl.BlockSpec((None, lc, w), lambda bi: (bi, 0, 0)), _full((w, 2 * w)), _full((lc, 2 * lc)),
                  _full((w, w))],
        out_specs=pl.BlockSpec((None, lc, w), lambda bi: (bi, 0, 0)),
        compiler_params=_params("parallel"),
        name="fourier_context",
    )(cfu, w_chan, m, wf_bd)


def _head_stack(x, head_lane):
    zero = jnp.zeros_like(x)
    return jnp.concatenate([jnp.where(head_lane == h, x, zero) for h in range(GLA_HEADS)], axis=0)


def _gla_kernel(qf_ref, kf_ref, vf_ref, laf_ref, qb_ref, kb_ref, vb_ref, lab_ref, s0_ref,
                of_ref, ob_ref, sfin_ref, stf_ref, stb_ref):
    i = pl.program_id(1)
    c = GLA_CHUNK
    nchunk = qf_ref.shape[0] // c

    @pl.when(i == 0)
    def _():
        stf_ref[...] = s0_ref[0]
        stb_ref[...] = s0_ref[1]

    r64 = lax.broadcasted_iota(jnp.int32, (c, c), 0)
    c64 = lax.broadcasted_iota(jnp.int32, (c, c), 1)
    tri = ((c64 <= r64).astype(BF16), (c64 >= r64).astype(BF16))
    at = lax.broadcasted_iota(jnp.int32, (c, GLA_WIDTH), 0)
    as_ = lax.broadcasted_iota(jnp.int32, (c, GLA_WIDTH), 1) & (c - 1)
    att_mask = (as_ <= at, as_ >= at)
    head_shift = HEAD_DIM.bit_length() - 1
    head_lane = lax.broadcasted_iota(jnp.int32, (1, GLA_WIDTH), 1) >> head_shift
    br = lax.broadcasted_iota(jnp.int32, (GLA_WIDTH, GLA_WIDTH), 0) >> head_shift
    bc = lax.broadcasted_iota(jnp.int32, (GLA_WIDTH, GLA_WIDTH), 1) >> head_shift
    bd_mask = br == bc
    in_refs = ((qf_ref, kf_ref, vf_ref, laf_ref), (qb_ref, kb_ref, vb_ref, lab_ref))
    out_refs = (of_ref, ob_ref)
    items = [(d, step if d == 0 else nchunk - 1 - step) for step in range(nchunk) for d in range(2)]
    rows = lambda ch: slice(c * ch, c * (ch + 1))

    bcum = {}
    for d, ch in items:
        la = in_refs[d][3][rows(ch)]
        la_hi = la.astype(BF16)
        la_lo = (la - la_hi.astype(F32)).astype(BF16)
        bcum[d, ch] = _dot(tri[d], la_hi) + _dot(tri[d], la_lo)
    work = {}
    for d, ch in items:
        q_ref, k_ref, v_ref, _ = in_refs[d]
        b = bcum[d, ch]
        btot = b[0:1] if d == 1 else b[c - 1:c]
        k = k_ref[rows(ch)]
        q_in = (q_ref[rows(ch)] * jnp.exp(b)).astype(BF16)
        k_in = (k * jnp.exp(-b)).astype(BF16)
        k_out = (k * jnp.exp(btot - b)).astype(BF16)
        vb = v_ref[rows(ch)].astype(BF16)
        att = lax.dot_general(q_in, _head_stack(k_in, head_lane), NT_DIMS, preferred_element_type=F32)
        work[d, ch] = (btot, q_in, k_out, vb, att)
    o_intra, ut = {}, {}
    for d, ch in items:
        btot, q_in, k_out, vb, att = work[d, ch]
        att = jnp.where(att_mask[d], att, 0.0).astype(BF16)
        o_intra[d, ch] = _dot(att, _head_stack(vb, head_lane))
        ut[d, ch] = lax.dot_general(vb, k_out, TN_DIMS, preferred_element_type=F32)
    st = [stf_ref[...], stb_ref[...]]
    for d, ch in items:
        btot, q_in = work[d, ch][0:2]
        o_inter = lax.dot_general(q_in, st[d].astype(BF16), NT_DIMS, preferred_element_type=F32)
        out_refs[d][rows(ch)] = o_intra[d, ch] + o_inter
        st[d] = st[d] * jnp.exp(btot) + jnp.where(bd_mask, ut[d, ch], 0.0)
    stf_ref[...] = st[0]
    stb_ref[...] = st[1]

    @pl.when(i == pl.num_programs(1) - 1)
    def _():
        sfin_ref[0] = st[0]
        sfin_ref[1] = st[1]


def _gla(gq, gk, gv, la, s0, tb):
    b, n, w = gq.shape
    nblk = n // tb
    fwd = lambda bi, i: (bi, i, 0)
    bwd = lambda bi, i: (bi, nblk - 1 - i, 0)
    bwd_la = lambda bi, i: (bi, nblk - 1 - i, 1)
    tok = lambda f: pl.BlockSpec((None, tb, w), f)
    state = pl.BlockSpec((None, 2, w, w), lambda bi, i: (bi, 0, 0, 0))
    return pl.pallas_call(
        _gla_kernel,
        out_shape=[jax.ShapeDtypeStruct((b, n, w), F32), jax.ShapeDtypeStruct((b, n, w), F32),
                   jax.ShapeDtypeStruct((b, 2, w, w), F32)],
        grid=(b, nblk),
        in_specs=[tok(fwd), tok(fwd), tok(fwd), tok(fwd), tok(bwd), tok(bwd), tok(bwd), tok(bwd_la), state],
        out_specs=[tok(fwd), tok(bwd), state],
        scratch_shapes=[pltpu.VMEM((w, w), F32), pltpu.VMEM((w, w), F32)],
        compiler_params=_params("parallel", "arbitrary"),
        name="gla_scan",
    )(gq, gk, gv, la, gq, gk, gv, la, s0)


def _outproj_kernel(att_ref, four_ref, of_ref, ob_ref, r_ref, x_ref, mod_ref, gg_ref, bd_ref, wo_ref, o_ref):
    d = D_MODEL
    o = of_ref[...] + ob_ref[...]
    y = o * lax.rsqrt(_group_mean(o, bd_ref) + EPS) * gg_ref[...]
    y = y * _silu(r_ref[...])
    a0, a1, a2 = ATT_WIDTH, ATT_WIDTH + FNET_WIDTH, ATT_WIDTH + FNET_WIDTH + GLA_WIDTH
    mix = (_dot(att_ref[...], wo_ref[0:a0]) + _dot(four_ref[...], wo_ref[a0:a1])
           + _dot(y.astype(BF16), wo_ref[a1:a2]))
    gate = mod_ref[:, 2 * d:3 * d]
    o_ref[...] = x_ref[...] + gate * mix


def _outproj(att, four, of, ob, gr, x, mod3, mod_row, lw, tm):
    b, n, d = x.shape
    row = (lambda bi: bi) if mod_row is None else (lambda bi: mod_row)
    tok = lambda w: pl.BlockSpec((None, tm, w), lambda bi, i: (bi, i, 0))
    return pl.pallas_call(
        _outproj_kernel,
        out_shape=jax.ShapeDtypeStruct((b, n, d), F32),
        grid=(b, n // tm),
        in_specs=[tok(ATT_WIDTH), tok(FNET_WIDTH), tok(GLA_WIDTH), tok(GLA_WIDTH), tok(GLA_WIDTH), tok(d),
                  pl.BlockSpec((None, 1, 6 * d), lambda bi, i: (row(bi), 0, 0)),
                  _full((1, GLA_WIDTH)), _full((GLA_WIDTH, GLA_WIDTH)), _full((d, d))],
        out_specs=tok(d),
        compiler_params=_params("parallel", "parallel"),
        name="outproj",
    )(att, four, of, ob, gr, x, mod3, lw["gla_g"], lw["bd256"], lw["w_out"])


def _ffn_kernel(x_ref, mod_ref, g2_ref, wi_ref, wo_ref, o_ref, a_ref):
    d = D_MODEL
    x = x_ref[...]
    ms = jnp.mean(x * x, axis=-1, keepdims=True)
    h = (x * lax.rsqrt(ms + EPS) * g2_ref[...]) * (1.0 + mod_ref[:, 4 * d:5 * d]) + mod_ref[:, 3 * d:4 * d]
    hb = h.astype(BF16)
    for c0 in range(0, FFN_HIDDEN, FFN_CHUNK):
        g = _dot(hb, wi_ref[:, c0:c0 + FFN_CHUNK])
        u = _dot(hb, wi_ref[:, FFN_HIDDEN + c0:FFN_HIDDEN + c0 + FFN_CHUNK])
        a_ref[:, c0:c0 + FFN_CHUNK] = (_silu(g) * u).astype(BF16)
    o_ref[...] = x + mod_ref[:, 5 * d:6 * d] * _dot(a_ref[...], wo_ref[...])


def _ffn(x, mod3, mod_row, lw, tm):
    b, n, d = x.shape
    row = (lambda bi: bi) if mod_row is None else (lambda bi: mod_row)
    tok = pl.BlockSpec((None, tm, d), lambda bi, i: (bi, i, 0))
    resident = lambda shape: pl.BlockSpec(shape, lambda bi, i: (0, 0), pipeline_mode=pl.Buffered(1))
    return pl.pallas_call(
        _ffn_kernel,
        out_shape=jax.ShapeDtypeStruct((b, n, d), F32),
        grid=(b, n // tm),
        in_specs=[tok,
                  pl.BlockSpec((None, 1, 6 * d), lambda bi, i: (row(bi), 0, 0)),
                  _full((1, d)), resident((d, 2 * FFN_HIDDEN)), resident((FFN_HIDDEN, d))],
        out_specs=tok,
        scratch_shapes=[pltpu.VMEM((tm, FFN_HIDDEN), BF16)],
        compiler_params=_params("parallel", "parallel"),
        name="ffn",
    )(x, mod3, lw["g2"], lw["w_ffn_in"], lw["w_ffn_out"])


def _rope_tables(n):
    axis_dim = HEAD_DIM // 2
    inv_freq = ROPE_BASE ** (-jnp.arange(0, axis_dim, 2, dtype=F32) / axis_dim)
    t = jnp.arange(n)
    row = (t // GRID_W).astype(F32)
    colp = (t % GRID_W).astype(F32)
    ang_r = row[:, None] * inv_freq[None, :]
    ang_c = colp[:, None] * inv_freq[None, :]
    cos = jnp.concatenate([jnp.cos(ang_r)] * 2 + [jnp.cos(ang_c)] * 2, axis=1)
    sin = jnp.concatenate([-jnp.sin(ang_r), jnp.sin(ang_r), -jnp.sin(ang_c), jnp.sin(ang_c)], axis=1)
    return jnp.concatenate([cos, cos], axis=1), jnp.concatenate([sin, sin], axis=1)


def _block_diag_mean(width):
    g = np.arange(width) // HEAD_DIM
    return jnp.asarray((g[:, None] == g[None, :]) / HEAD_DIM, BF16)


def _layer_weights(l, w_in, g_norm1, q_norm_g, k_norm_g, w_fourier, wgf, bgf, wgb, bgb, gla_norm_g, w_out,
                   g_norm2, w_ffn_in, w_ffn_out):
    r = GLA_GATE_RANK
    w_l = w_in[l]
    w_z = jnp.zeros((D_MODEL, LANES), F32).at[:, 0:2 * r].set(w_l[:, MAIN_WIDTH:MAIN_WIDTH + 2 * r])
    w_gate = jnp.zeros((LANES, 2 * GLA_WIDTH), F32)
    w_gate = w_gate.at[0:r, 0:GLA_WIDTH].set(wgf[l]).at[r:2 * r, GLA_WIDTH:].set(wgb[l])
    wf_bd = jnp.zeros((FNET_WIDTH, FNET_WIDTH), F32)
    for g in range(FNET_GROUPS):
        wf_bd = wf_bd.at[HEAD_DIM * g:HEAD_DIM * (g + 1), HEAD_DIM * g:HEAD_DIM * (g + 1)].set(w_fourier[l, g])
    return {
        "g1": g_norm1[l][None, :],
        "w_main": w_l[:, 0:MAIN_WIDTH].astype(BF16),
        "w_z": w_z.astype(BF16),
        "w_gate": w_gate.astype(BF16),
        "b_gate": jnp.concatenate([bgf[l], bgb[l]])[None, :],
        "q_g": jnp.tile(q_norm_g[l], 2)[None, :],
        "k_g": jnp.tile(k_norm_g[l], 2)[None, :],
        "bd128": _block_diag_mean(LANES),
        "bd256": _block_diag_mean(GLA_WIDTH),
        "wf_bd": wf_bd.astype(BF16),
        "gla_g": jnp.tile(gla_norm_g[l], GLA_HEADS)[None, :],
        "w_out": w_out[l].astype(BF16),
        "g2": g_norm2[l][None, :],
        "w_ffn_in": w_ffn_in[l].astype(BF16),
        "w_ffn_out": w_ffn_out[l].astype(BF16),
    }


def kernel(x, c, ctx, c_ctx, w_mod, b_mod, g_norm1, w_in, q_norm_g, k_norm_g, attn_sink, w_fourier, gla_w_gate_f,
           gla_b_gate_f, gla_w_gate_b, gla_b_gate_b, gla_norm_g, w_out, g_norm2, w_ffn_in, w_ffn_out):
    b, n, d = x.shape
    lc = ctx.shape[1]
    depth = w_mod.shape[0]
    assert d == D_MODEL and b <= 7 and n % 512 == 0 and n % (DFT_N1 * 8) == 0 and lc % GLA_CHUNK == 0

    cc = jnp.zeros((8, d), F32).at[0:b].set(c).at[b].set(c_ctx)
    mod = _modulation(cc, w_mod, b_mod)
    rope_tabs = _rope_tables(n)
    dft = _dft_consts(n)
    dft_ctx = _dft_ctx_consts(lc)
    tm_lat, tm_ctx, tb_lat = 512, lc, 512
    xc = ctx
    for l in range(depth):
        need_ctx = l < depth - 1
        lw = _layer_weights(l, w_in, g_norm1, q_norm_g, k_norm_g, w_fourier, gla_w_gate_f, gla_b_gate_f,
                            gla_w_gate_b, gla_b_gate_b, gla_norm_g, w_out, g_norm2, w_ffn_in, w_ffn_out)
        mod3 = mod[l].reshape(8, 1, 6 * d)
        sink = attn_sink[l]
        cq, ck4, cv4, cfu, cgq, cgk, cgv, cgr, cla = _inproj(xc, mod3, b, lw, None, tm_ctx)
        q, k4, v4, fu, gq, gk, gv, gr, la = _inproj(x, mod3, None, lw, rope_tabs, tm_lat)
        att = _attention(q, k4, v4, ck4, cv4, sink)
        four = _fourier(fu, lw["wf_bd"], dft)
        s_zero = jnp.zeros((b, 2, GLA_WIDTH, GLA_WIDTH), F32)
        ocf, ocb, s_ctx = _gla(cgq, cgk, cgv, cla, s_zero, lc)
        olf, olb, _ = _gla(gq, gk, gv, la, s_ctx, tb_lat)
        x = _outproj(att, four, olf, olb, gr, x, mod3, None, lw, tm_lat)
        x = _ffn(x, mod3, None, lw, tm_lat)
        if need_ctx:
            att_c = _attention_ctx(cq, ck4, cv4, sink)
            four_c = _fourier_ctx(cfu, lw["wf_bd"], dft_ctx)
            xc = _outproj(att_c, four_c, ocf, ocb, cgr, xc, mod3, b, lw, tm_ctx)
            xc = _ffn(xc, mod3, b, lw, tm_ctx)
    return x
```

```python
import functools

import numpy as np
import jax
import jax.numpy as jnp
from jax import lax
from jax.experimental import pallas as pl
from jax.experimental.pallas import tpu as pltpu

F32 = jnp.float32
BF16 = jnp.bfloat16

D_MODEL = 1024
HEAD_DIM = 64
GRID_W = 64
ROPE_BASE = 10000.0
ATT_HEADS = 8
ATT_KV_HEADS = 2
ATT_WIDTH = ATT_HEADS * HEAD_DIM
KV_WIDTH = ATT_KV_HEADS * HEAD_DIM
ATT_BLOCK = 128
ATT_SCALE = HEAD_DIM ** -0.5
NEG_INF = -1e30
FNET_GROUPS = 4
FNET_WIDTH = FNET_GROUPS * HEAD_DIM
GLA_HEADS = 4
GLA_WIDTH = GLA_HEADS * HEAD_DIM
GLA_GATE_RANK = 16
GLA_TAU = 16.0
GLA_CHUNK = 64
GLA_SCALE = HEAD_DIM ** -0.5
MAIN_WIDTH = ATT_WIDTH + 2 * KV_WIDTH + FNET_WIDTH + 4 * GLA_WIDTH
FFN_HIDDEN = 2816
FFN_CHUNK = 256
EPS = 1e-6
LANES = 128
DFT_N1 = 128
DFT_STEP = 8
VMEM_LIMIT = 56 * 1024 * 1024

NT_DIMS = (((1,), (1,)), ((), ()))
TN_DIMS = (((0,), (0,)), ((), ()))


def _params(*sem):
    return pltpu.CompilerParams(dimension_semantics=sem, vmem_limit_bytes=VMEM_LIMIT)


def _dot(a, b):
    return jnp.dot(a, b, preferred_element_type=F32)


def _silu(x):
    return x / (1.0 + jnp.exp(-x))


def _full(shape):
    nd = len(shape)
    return pl.BlockSpec(shape, lambda *_: (0,) * nd)


def _group_mean(t, bd_ref):
    return _dot((t * t).astype(BF16), bd_ref[...])


def _mod_kernel(c_ref, w_ref, b_ref, o_ref):
    s = _silu(c_ref[...]).astype(BF16)
    o_ref[...] = _dot(s, w_ref[...].astype(BF16)) + b_ref[...]


def _modulation(cc, w_mod, b_mod):
    depth, d, width = w_mod.shape
    tn = 1536
    return pl.pallas_call(
        _mod_kernel,
        out_shape=jax.ShapeDtypeStruct((depth, 8, width), F32),
        grid=(depth, width // tn),
        in_specs=[
            _full((8, d)),
            pl.BlockSpec((None, d, tn), lambda l, j: (l, 0, j)),
            pl.BlockSpec((None, 1, tn), lambda l, j: (l, 0, j)),
        ],
        out_specs=pl.BlockSpec((None, 8, tn), lambda l, j: (l, 0, j)),
        compiler_params=_params("parallel", "parallel"),
        name="modulation",
    )(cc, w_mod, b_mod.reshape(depth, 1, width))


def _inproj_kernel(*refs, rope):
    if rope:
        (x_ref, mod_ref, g1_ref, wm_ref, wz_ref, wg_ref, bg_ref, qg_ref, kg_ref, bd_ref, cos_ref, sin_ref,
         q_ref, k4_ref, v4_ref, fu_ref, gq_ref, gk_ref, gv_ref, gr_ref, la_ref) = refs
    else:
        (x_ref, mod_ref, g1_ref, wm_ref, wz_ref, wg_ref, bg_ref, qg_ref, kg_ref, bd_ref,
         q_ref, k4_ref, v4_ref, fu_ref, gq_ref, gk_ref, gv_ref, gr_ref, la_ref) = refs
    d = D_MODEL
    x = x_ref[...]
    mod = mod_ref[...]
    sh, sc = mod[:, 0:d], mod[:, d:2 * d]
    ms = jnp.mean(x * x, axis=-1, keepdims=True)
    h = (x * lax.rsqrt(ms + EPS) * g1_ref[...]) * (1.0 + sc) + sh
    hb = h.astype(BF16)
    acc = _dot(hb, wm_ref[...])

    lane = lax.broadcasted_iota(jnp.int32, (1, LANES), 1)
    lo = lane < HEAD_DIM
    second_half = (lane & 16) != 0

    def head_norm(t, g_ref):
        return t * lax.rsqrt(_group_mean(t, bd_ref) + EPS) * g_ref[...]

    def rotary(t):
        if not rope:
            return t
        partner = jnp.where(second_half, pltpu.roll(t, 16, 1), pltpu.roll(t, LANES - 16, 1))
        return t * cos_ref[...] + partner * sin_ref[...]

    def spread(t, out_ref, idle):
        tr = pltpu.roll(t, HEAD_DIM, 1)
        fill = jnp.full_like(t, idle)
        out_ref[:, 0:128] = jnp.where(lo, t, fill).astype(out_ref.dtype)
        out_ref[:, 128:256] = jnp.where(lo, fill, tr).astype(out_ref.dtype)
        out_ref[:, 256:384] = jnp.where(lo, tr, fill).astype(out_ref.dtype)
        out_ref[:, 384:512] = jnp.where(lo, fill, t).astype(out_ref.dtype)

    for j in range(ATT_WIDTH // LANES):
        t = rotary(head_norm(acc[:, LANES * j:LANES * (j + 1)], qg_ref)) * ATT_SCALE
        q_ref[:, LANES * j:LANES * (j + 1)] = t.astype(q_ref.dtype)
    o1 = ATT_WIDTH
    spread(rotary(head_norm(acc[:, o1:o1 + KV_WIDTH], kg_ref)), k4_ref, 0.0)
    o2 = o1 + KV_WIDTH
    spread(acc[:, o2:o2 + KV_WIDTH], v4_ref, 1.0)
    o3 = o2 + KV_WIDTH
    fu_ref[...] = acc[:, o3:o3 + FNET_WIDTH].astype(fu_ref.dtype)
    o4 = o3 + FNET_WIDTH
    gq_ref[...] = acc[:, o4:o4 + GLA_WIDTH] * GLA_SCALE
    gk_ref[...] = acc[:, o4 + GLA_WIDTH:o4 + 2 * GLA_WIDTH]
    gv_ref[...] = acc[:, o4 + 2 * GLA_WIDTH:o4 + 3 * GLA_WIDTH]
    gr_ref[...] = acc[:, o4 + 3 * GLA_WIDTH:o4 + 4 * GLA_WIDTH]

    gz = _dot(hb, wz_ref[...])
    z = _dot(gz.astype(BF16), wg_ref[...]) + bg_ref[...]
    log_sig = jnp.minimum(z, 0.0) - jnp.log(1.0 + jnp.exp(-jnp.abs(z)))
    la_ref[...] = log_sig * (1.0 / GLA_TAU)


def _inproj(x, mod3, mod_row, lw, rope_tabs, tm):
    b, n, d = x.shape
    rope = rope_tabs is not None
    row = (lambda bi: bi) if mod_row is None else (lambda bi: mod_row)
    tok = lambda w: pl.BlockSpec((None, tm, w), lambda bi, i: (bi, i, 0))
    in_specs = [
        tok(d),
        pl.BlockSpec((None, 1, 6 * d), lambda bi, i: (row(bi), 0, 0)),
        _full((1, d)), _full((d, MAIN_WIDTH)), _full((d, LANES)), _full((LANES, 2 * GLA_WIDTH)),
        _full((1, 2 * GLA_WIDTH)), _full((1, LANES)), _full((1, LANES)), _full((LANES, LANES)),
    ]
    args = [x, mod3, lw["g1"], lw["w_main"], lw["w_z"], lw["w_gate"], lw["b_gate"], lw["q_g"], lw["k_g"], lw["bd128"]]
    if rope:
        in_specs += [pl.BlockSpec((tm, LANES), lambda bi, i: (i, 0))] * 2
        args += list(rope_tabs)
    sds = lambda w, dt: jax.ShapeDtypeStruct((b, n, w), dt)
    out_shape = [sds(ATT_WIDTH, BF16), sds(4 * LANES, BF16), sds(4 * LANES, BF16), sds(FNET_WIDTH, F32),
                 sds(GLA_WIDTH, F32), sds(GLA_WIDTH, F32), sds(GLA_WIDTH, F32), sds(GLA_WIDTH, F32),
                 sds(2 * GLA_WIDTH, F32)]
    out_specs = [tok(s.shape[-1]) for s in out_shape]
    return pl.pallas_call(
        functools.partial(_inproj_kernel, rope=rope),
        out_shape=out_shape, grid=(b, n // tm), in_specs=in_specs, out_specs=out_specs,
        compiler_params=_params("parallel", "parallel"),
        name="inproj_rope" if rope else "inproj_ctx",
    )(*args)


def _scores(q2, key_tiles, masks):
    cols = []
    for k, tile_masks in zip(key_tiles, masks):
        s = lax.dot_general(q2, k, NT_DIMS, preferred_element_type=F32)
        for j, mk in enumerate(tile_masks):
            c = s[:, LANES * j:LANES * (j + 1)]
            cols.append(c if mk is None else jnp.where(mk, c, NEG_INF))
    return cols


def _softmax_pv(cols, val_tiles, sink_col):
    mx = cols[0]
    for c in cols[1:]:
        mx = jnp.maximum(mx, c)
    m = jnp.maximum(jnp.max(mx, axis=-1, keepdims=True), sink_col)
    p = jnp.concatenate([jnp.exp(c - m).astype(BF16) for c in cols], axis=1)
    pv = _dot(p, jnp.concatenate(val_tiles, axis=0))
    den = pltpu.roll(pv, HEAD_DIM, 1) + jnp.exp(sink_col - m)
    return pv / den


def _attn_kernel(sink_ref, q_ref, kl_ref, km_ref, kr_ref, kc_ref, vl_ref, vm_ref, vr_ref, vc_ref, o_ref):
    i = pl.program_id(1)
    nb = pl.num_programs(1)
    blk = ATT_BLOCK
    group = ATT_HEADS // ATT_KV_HEADS
    row = lax.broadcasted_iota(jnp.int32, (2 * blk, blk), 0) & (blk - 1)
    col = lax.broadcasted_iota(jnp.int32, (2 * blk, blk), 1)
    mask_l = jnp.logical_and(col >= row, i > 0)
    mask_r = jnp.logical_and(col <= row, i < nb - 1)
    masks = [(mask_l, None), (mask_r, None), (None,)]
    upper_rows = lax.broadcasted_iota(jnp.int32, (2 * blk, 1), 0) >= blk
    lo = lax.broadcasted_iota(jnp.int32, (1, LANES), 1) < HEAD_DIM

    def tiles(l, m_, r, c, ks):
        return [jnp.concatenate([l[:, ks], m_[:, ks]], axis=0), jnp.concatenate([r[:, ks], c[0:blk, ks]], axis=0),
                c[blk:, ks]]

    slots = [(kvh, par) for kvh in range(ATT_KV_HEADS) for par in range(2)]
    lane_slice = lambda kvh, par: slice(2 * LANES * kvh + LANES * par, 2 * LANES * kvh + LANES * (par + 1))
    cols = {}
    for kvh, par in slots:
        qs = 2 * LANES * kvh
        q2 = jnp.concatenate([q_ref[:, qs:qs + LANES], q_ref[:, qs + LANES:qs + 2 * LANES]], axis=0)
        cols[kvh, par] = _scores(q2, tiles(kl_ref, km_ref, kr_ref, kc_ref, lane_slice(kvh, par)), masks)
    outs = {}
    for kvh, par in slots:
        base = group * kvh + par
        sink_col = jnp.where(upper_rows, sink_ref[base + 2], sink_ref[base])
        outs[kvh, par] = _softmax_pv(cols[kvh, par], tiles(vl_ref, vm_ref, vr_ref, vc_ref, lane_slice(kvh, par)),
                                     sink_col)
    for kvh in range(ATT_KV_HEADS):
        qs = 2 * LANES * kvh
        o = jnp.where(lo, outs[kvh, 0], outs[kvh, 1])
        o_ref[:, qs:qs + LANES] = o[0:blk].astype(o_ref.dtype)
        o_ref[:, qs + LANES:qs + 2 * LANES] = o[blk:2 * blk].astype(o_ref.dtype)


def _attention(q, k4, v4, ck4, cv4, sink):
    b, n, _ = q.shape
    lc = ck4.shape[1]
    assert lc == 2 * ATT_BLOCK
    nb = n // ATT_BLOCK
    blk3 = lambda f: pl.BlockSpec((None, ATT_BLOCK, 4 * LANES), f)
    left = lambda bi, i: (bi, jnp.maximum(i - 1, 0), 0)
    mid = lambda bi, i: (bi, i, 0)
    right = lambda bi, i: (bi, jnp.minimum(i + 1, nb - 1), 0)
    ctx = pl.BlockSpec((None, lc, 4 * LANES), lambda bi, i: (bi, 0, 0))
    return pl.pallas_call(
        _attn_kernel,
        out_shape=jax.ShapeDtypeStruct((b, n, ATT_WIDTH), BF16),
        grid=(b, nb),
        in_specs=[pl.BlockSpec(memory_space=pltpu.SMEM),
                  blk3(mid), blk3(left), blk3(mid), blk3(right), ctx, blk3(left), blk3(mid), blk3(right), ctx],
        out_specs=blk3(mid),
        compiler_params=_params("parallel", "parallel"),
        name="window_attention",
    )(sink, q, k4, k4, k4, ck4, v4, v4, v4, cv4)


def _attn_ctx_kernel(sink_ref, q_ref, kc_ref, vc_ref, o_ref):
    kvh = pl.program_id(1)
    lc = q_ref.shape[0]
    q2 = jnp.concatenate([q_ref[:, 0:LANES], q_ref[:, LANES:2 * LANES]], axis=0)
    upper_rows = lax.broadcasted_iota(jnp.int32, (2 * lc, 1), 0) >= lc
    lo = lax.broadcasted_iota(jnp.int32, (1, LANES), 1) < HEAD_DIM
    outs = []
    for par in range(2):
        ks = slice(LANES * par, LANES * (par + 1))
        base = ATT_HEADS // ATT_KV_HEADS * kvh + par
        sink_col = jnp.where(upper_rows, sink_ref[base + 2], sink_ref[base])
        cols = _scores(q2, [kc_ref[:, ks]], [(None,) * (lc // LANES)])
        outs.append(_softmax_pv(cols, [vc_ref[:, ks]], sink_col))
    o = jnp.where(lo, outs[0], outs[1])
    o_ref[:, 0:LANES] = o[0:lc].astype(o_ref.dtype)
    o_ref[:, LANES:2 * LANES] = o[lc:2 * lc].astype(o_ref.dtype)


def _attention_ctx(cq, ck4, cv4, sink):
    b, lc, _ = cq.shape
    gw = 2 * LANES
    spec = pl.BlockSpec((None, lc, gw), lambda bi, h: (bi, 0, h))
    return pl.pallas_call(
        _attn_ctx_kernel,
        out_shape=jax.ShapeDtypeStruct((b, lc, ATT_WIDTH), BF16),
        grid=(b, ATT_KV_HEADS),
        in_specs=[pl.BlockSpec(memory_space=pltpu.SMEM), spec, spec, spec],
        out_specs=spec,
        compiler_params=_params("parallel", "parallel"),
        name="context_attention",
    )(sink, cq, ck4, cv4)


def _dft_consts(n):
    n1, n2 = DFT_N1, n // DFT_N1
    ang = lambda a, m: 2.0 * np.pi * np.outer(np.arange(a), np.arange(a)) / m
    c = HEAD_DIM
    cc, sc = np.cos(ang(c, c)) / np.sqrt(c), np.sin(ang(c, c)) / np.sqrt(c)
    eye = np.eye(FNET_GROUPS)
    w_chan = np.concatenate([np.kron(eye, cc), -np.kron(eye, sc)], axis=1)
    c1, s1 = np.cos(ang(n1, n1)) / np.sqrt(n1), np.sin(ang(n1, n1)) / np.sqrt(n1)
    m1 = np.block([[c1, s1], [-s1, c1]])
    kk = np.arange(n1)[:, None, None] + n1 * np.arange(n2)[None, :, None]
    ph = 2.0 * np.pi * (kk * np.arange(n2)[None, None, :] % n) / n
    m3 = np.concatenate([np.cos(ph), np.sin(ph)], axis=2) / np.sqrt(n2)
    return tuple(jnp.asarray(a, F32) for a in (w_chan, m1, m3))


def _dft_ctx_consts(lc):
    ang = lambda a, m: 2.0 * np.pi * np.outer(np.arange(a), np.arange(a)) / m
    c = HEAD_DIM
    cc, sc = np.cos(ang(c, c)) / np.sqrt(c), np.sin(ang(c, c)) / np.sqrt(c)
    eye = np.eye(FNET_GROUPS)
    w_chan = np.concatenate([np.kron(eye, cc), -np.kron(eye, sc)], axis=1)
    cl, sl = np.cos(ang(lc, lc)) / np.sqrt(lc), np.sin(ang(lc, lc)) / np.sqrt(lc)
    return jnp.asarray(w_chan, F32), jnp.asarray(np.concatenate([cl, sl], axis=1), F32)


def _four1_kernel(u_ref, wc_ref, m1_ref, p_ref, z_ref):
    w, t = FNET_WIDTH, DFT_STEP
    u = u_ref[...].reshape(DFT_N1 * t, w).astype(BF16)
    z_ref[...] = _dot(u, wc_ref[...]).reshape(DFT_N1, t, 2 * w)
    for tt in range(t):
        z = z_ref[:, tt, :]
        zs = jnp.concatenate([z[:, 0:w], z[:, w:2 * w]], axis=0).astype(BF16)
        a = _dot(m1_ref[...], zs)
        p_ref[0, :, tt, :] = a[0:DFT_N1]
        p_ref[1, :, tt, :] = a[DFT_N1:2 * DFT_N1]


def _four2_kernel(p_ref, m3_ref, wf_ref, y_ref):
    w, t = FNET_WIDTH, DFT_STEP
    n2 = p_ref.shape[2]
    ys = []
    for kk in range(t):
        rhs = jnp.concatenate([p_ref[0, kk], p_ref[1, kk]], axis=0).astype(BF16)
        ys.append(_dot(m3_ref[kk].astype(BF16), rhs).astype(BF16))
    out = _dot(jnp.concatenate(ys, axis=0), wf_ref[...])
    for kk in range(t):
        y_ref[:, kk, :] = out[n2 * kk:n2 * (kk + 1)]


def _fourier(fu, wf_bd, consts):
    b, n, w = fu.shape
    n1, n2, t = DFT_N1, n // DFT_N1, DFT_STEP
    w_chan, m1, m3 = consts
    p = pl.pallas_call(
        _four1_kernel,
        out_shape=jax.ShapeDtypeStruct((b, 2, n1, n2, w), F32),
        grid=(b, n2 // t),
        in_specs=[pl.BlockSpec((None, n1, t, w), lambda bi, j: (bi, 0, j, 0)),
                  _full((w, 2 * w)), _full((2 * n1, 2 * n1))],
        out_specs=pl.BlockSpec((None, 2, n1, t, w), lambda bi, j: (bi, 0, 0, j, 0)),
        scratch_shapes=[pltpu.VMEM((n1, t, 2 * w), F32)],
        compiler_params=_params("parallel", "parallel"),
        name="fourier_stage1",
    )(fu.reshape(b, n1, n2, w), w_chan.astype(BF16), m1.astype(BF16))
    y = pl.pallas_call(
        _four2_kernel,
        out_shape=jax.ShapeDtypeStruct((b, n2, n1, w), F32),
        grid=(b, n1 // t),
        in_specs=[pl.BlockSpec((None, 2, t, n2, w), lambda bi, j: (bi, 0, j, 0, 0)),
                  pl.BlockSpec((t, n2, 2 * n2), lambda bi, j: (j, 0, 0)), _full((w, w))],
        out_specs=pl.BlockSpec((None, n2, t, w), lambda bi, j: (bi, 0, j, 0)),
        compiler_params=_params("parallel", "parallel"),
        name="fourier_stage2",
    )(p, m3, wf_bd)
    return y.reshape(b, n, w)


def _four_ctx_kernel(u_ref, wc_ref, m_ref, wf_ref, y_ref):
    w = FNET_WIDTH
    z = _dot(u_ref[...].astype(BF16), wc_ref[...])
    zs = jnp.concatenate([z[:, 0:w], z[:, w:2 * w]], axis=0).astype(BF16)
    y = _dot(m_ref[...], zs)
    y_ref[...] = _dot(y.astype(BF16), wf_ref[...]).astype(y_ref.dtype)


def _fourier_ctx(cfu, wf_bd, consts):
    b, lc, w = cfu.shape
    w_chan, m = consts[0].astype(BF16), consts[1].astype(BF16)
    return pl.pallas_call(
        _four_ctx_kernel,
        out_shape=jax.ShapeDtypeStruct((b, lc, w), BF16),
        grid=(b,),
        in_specs=[pl.BlockSpec((None, lc, w), lambda bi: (bi, 0, 0)), _full((w, 2 * w)), _full((lc, 2 * lc)),
                  _full((w, w))],
        out_specs=pl.BlockSpec((None, lc, w), lambda bi: (bi, 0, 0)),
        compiler_params=_params("parallel"),
        name="fourier_context",
    )(cfu, w_chan, m, wf_bd)


def _head_stack(x, head_lane):
    zero = jnp.zeros_like(x)
    return jnp.concatenate([jnp.where(head_lane == h, x, zero) for h in range(GLA_HEADS)], axis=0)


def _gla_kernel(qf_ref, kf_ref, vf_ref, laf_ref, qb_ref, kb_ref, vb_ref, lab_ref, s0_ref,
                of_ref, ob_ref, sfin_ref, stf_ref, stb_ref):
    i = pl.program_id(1)
    c = GLA_CHUNK
    nchunk = qf_ref.shape[0] // c

    @pl.when(i == 0)
    def _():
        stf_ref[...] = s0_ref[0]
        stb_ref[...] = s0_ref[1]

    r64 = lax.broadcasted_iota(jnp.int32, (c, c), 0)
    c64 = lax.broadcasted_iota(jnp.int32, (c, c), 1)
    tri = ((c64 <= r64).astype(BF16), (c64 >= r64).astype(BF16))
    at = lax.broadcasted_iota(jnp.int32, (c, GLA_WIDTH), 0)
    as_ = lax.broadcasted_iota(jnp.int32, (c, GLA_WIDTH), 1) & (c - 1)
    att_mask = (as_ <= at, as_ >= at)
    head_shift = HEAD_DIM.bit_length() - 1
    head_lane = lax.broadcasted_iota(jnp.int32, (1, GLA_WIDTH), 1) >> head_shift
    br = lax.broadcasted_iota(jnp.int32, (GLA_WIDTH, GLA_WIDTH), 0) >> head_shift
    bc = lax.broadcasted_iota(jnp.int32, (GLA_WIDTH, GLA_WIDTH), 1) >> head_shift
    bd_mask = br == bc
    in_refs = ((qf_ref, kf_ref, vf_ref, laf_ref), (qb_ref, kb_ref, vb_ref, lab_ref))
    out_refs = (of_ref, ob_ref)
    items = [(d, step if d == 0 else nchunk - 1 - step) for step in range(nchunk) for d in range(2)]
    rows = lambda ch: slice(c * ch, c * (ch + 1))

    bcum = {}
    for d, ch in items:
        la = in_refs[d][3][rows(ch)]
        la_hi = la.astype(BF16)
        la_lo = (la - la_hi.astype(F32)).astype(BF16)
        bcum[d, ch] = _dot(tri[d], la_hi) + _dot(tri[d], la_lo)
    work = {}
    for d, ch in items:
        q_ref, k_ref, v_ref, _ = in_refs[d]
        b = bcum[d, ch]
        btot = b[0:1] if d == 1 else b[c - 1:c]
        k = k_ref[rows(ch)]
        q_in = (q_ref[rows(ch)] * jnp.exp(b)).astype(BF16)
        k_in = (k * jnp.exp(-b)).astype(BF16)
        k_out = (k * jnp.exp(btot - b)).astype(BF16)
        vb = v_ref[rows(ch)].astype(BF16)
        att = lax.dot_general(q_in, _head_stack(k_in, head_lane), NT_DIMS, preferred_element_type=F32)
        work[d, ch] = (btot, q_in, k_out, vb, att)
    o_intra, ut = {}, {}
    for d, ch in items:
        btot, q_in, k_out, vb, att = work[d, ch]
        att = jnp.where(att_mask[d], att, 0.0).astype(BF16)
        o_intra[d, ch] = _dot(att, _head_stack(vb, head_lane))
        ut[d, ch] = lax.dot_general(vb, k_out, TN_DIMS, preferred_element_type=F32)
    st = [stf_ref[...], stb_ref[...]]
    for d, ch in items:
        btot, q_in = work[d, ch][0:2]
        o_inter = lax.dot_general(q_in, st[d].astype(BF16), NT_DIMS, preferred_element_type=F32)
        out_refs[d][rows(ch)] = o_intra[d, ch] + o_inter
        st[d] = st[d] * jnp.exp(btot) + jnp.where(bd_mask, ut[d, ch], 0.0)
    stf_ref[...] = st[0]
    stb_ref[...] = st[1]

    @pl.when(i == pl.num_programs(1) - 1)
    def _():
        sfin_ref[0] = st[0]
        sfin_ref[1] = st[1]


def _gla(gq, gk, gv, la, s0, tb):
    b, n, w = gq.shape
    nblk = n // tb
    fwd = lambda bi, i: (bi, i, 0)
    bwd = lambda bi, i: (bi, nblk - 1 - i, 0)
    bwd_la = lambda bi, i: (bi, nblk - 1 - i, 1)
    tok = lambda f: pl.BlockSpec((None, tb, w), f)
    state = pl.BlockSpec((None, 2, w, w), lambda bi, i: (bi, 0, 0, 0))
    return pl.pallas_call(
        _gla_kernel,
        out_shape=[jax.ShapeDtypeStruct((b, n, w), F32), jax.ShapeDtypeStruct((b, n, w), F32),
                   jax.ShapeDtypeStruct((b, 2, w, w), F32)],
        grid=(b, nblk),
        in_specs=[tok(fwd), tok(fwd), tok(fwd), tok(fwd), tok(bwd), tok(bwd), tok(bwd), tok(bwd_la), state],
        out_specs=[tok(fwd), tok(bwd), state],
        scratch_shapes=[pltpu.VMEM((w, w), F32), pltpu.VMEM((w, w), F32)],
        compiler_params=_params("parallel", "arbitrary"),
        name="gla_scan",
    )(gq, gk, gv, la, gq, gk, gv, la, s0)


def _outproj_kernel(att_ref, four_ref, of_ref, ob_ref, r_ref, x_ref, mod_ref, gg_ref, bd_ref, wo_ref, o_ref):
    d = D_MODEL
    o = of_ref[...] + ob_ref[...]
    y = o * lax.rsqrt(_group_mean(o, bd_ref) + EPS) * gg_ref[...]
    y = y * _silu(r_ref[...])
    a0, a1, a2 = ATT_WIDTH, ATT_WIDTH + FNET_WIDTH, ATT_WIDTH + FNET_WIDTH + GLA_WIDTH
    mix = (_dot(att_ref[...], wo_ref[0:a0]) + _dot(four_ref[...].astype(BF16), wo_ref[a0:a1])
           + _dot(y.astype(BF16), wo_ref[a1:a2]))
    gate = mod_ref[:, 2 * d:3 * d]
    o_ref[...] = x_ref[...] + gate * mix


def _outproj(att, four, of, ob, gr, x, mod3, mod_row, lw, tm):
    b, n, d = x.shape
    row = (lambda bi: bi) if mod_row is None else (lambda bi: mod_row)
    tok = lambda w: pl.BlockSpec((None, tm, w), lambda bi, i: (bi, i, 0))
    return pl.pallas_call(
        _outproj_kernel,
        out_shape=jax.ShapeDtypeStruct((b, n, d), F32),
        grid=(b, n // tm),
        in_specs=[tok(ATT_WIDTH), tok(FNET_WIDTH), tok(GLA_WIDTH), tok(GLA_WIDTH), tok(GLA_WIDTH), tok(d),
                  pl.BlockSpec((None, 1, 6 * d), lambda bi, i: (row(bi), 0, 0)),
                  _full((1, GLA_WIDTH)), _full((GLA_WIDTH, GLA_WIDTH)), _full((d, d))],
        out_specs=tok(d),
        compiler_params=_params("parallel", "parallel"),
        name="outproj",
    )(att, four, of, ob, gr, x, mod3, lw["gla_g"], lw["bd256"], lw["w_out"])


def _ffn_kernel(x_ref, mod_ref, g2_ref, wi_ref, wo_ref, o_ref, a_ref):
    d = D_MODEL
    x = x_ref[...]
    ms = jnp.mean(x * x, axis=-1, keepdims=True)
    h = (x * lax.rsqrt(ms + EPS) * g2_ref[...]) * (1.0 + mod_ref[:, 4 * d:5 * d]) + mod_ref[:, 3 * d:4 * d]
    hb = h.astype(BF16)
    for c0 in range(0, FFN_HIDDEN, FFN_CHUNK):
        g = _dot(hb, wi_ref[:, c0:c0 + FFN_CHUNK])
        u = _dot(hb, wi_ref[:, FFN_HIDDEN + c0:FFN_HIDDEN + c0 + FFN_CHUNK])
        a_ref[:, c0:c0 + FFN_CHUNK] = (_silu(g) * u).astype(BF16)
    o_ref[...] = x + mod_ref[:, 5 * d:6 * d] * _dot(a_ref[...], wo_ref[...])


def _ffn(x, mod3, mod_row, lw, tm):
    b, n, d = x.shape
    row = (lambda bi: bi) if mod_row is None else (lambda bi: mod_row)
    tok = pl.BlockSpec((None, tm, d), lambda bi, i: (bi, i, 0))
    resident = lambda shape: pl.BlockSpec(shape, lambda bi, i: (0, 0), pipeline_mode=pl.Buffered(1))
    return pl.pallas_call(
        _ffn_kernel,
        out_shape=jax.ShapeDtypeStruct((b, n, d), F32),
        grid=(b, n // tm),
        in_specs=[tok,
                  pl.BlockSpec((None, 1, 6 * d), lambda bi, i: (row(bi), 0, 0)),
                  _full((1, d)), resident((d, 2 * FFN_HIDDEN)), resident((FFN_HIDDEN, d))],
        out_specs=tok,
        scratch_shapes=[pltpu.VMEM((tm, FFN_HIDDEN), BF16)],
        compiler_params=_params("parallel", "parallel"),
        name="ffn",
    )(x, mod3, lw["g2"], lw["w_ffn_in"], lw["w_ffn_out"])


def _rope_tables(n):
    axis_dim = HEAD_DIM // 2
    inv_freq = ROPE_BASE ** (-np.arange(0, axis_dim, 2, dtype=np.float64) / axis_dim)
    t = np.arange(n)
    ang_r = (t // GRID_W)[:, None] * inv_freq[None, :]
    ang_c = (t % GRID_W)[:, None] * inv_freq[None, :]
    cos = np.concatenate([np.cos(ang_r)] * 2 + [np.cos(ang_c)] * 2, axis=1)
    sin = np.concatenate([-np.sin(ang_r), np.sin(ang_r), -np.sin(ang_c), np.sin(ang_c)], axis=1)
    return jnp.asarray(np.tile(cos, (1, 2)), F32), jnp.asarray(np.tile(sin, (1, 2)), F32)


def _block_diag_mean(width):
    g = np.arange(width) // HEAD_DIM
    return jnp.asarray((g[:, None] == g[None, :]) / HEAD_DIM, BF16)


def _layer_weights(l, w_in, g_norm1, q_norm_g, k_norm_g, w_fourier, wgf, bgf, wgb, bgb, gla_norm_g, w_out,
                   g_norm2, w_ffn_in, w_ffn_out):
    r = GLA_GATE_RANK
    w_l = w_in[l]
    w_z = jnp.zeros((D_MODEL, LANES), F32).at[:, 0:2 * r].set(w_l[:, MAIN_WIDTH:MAIN_WIDTH + 2 * r])
    w_gate = jnp.zeros((LANES, 2 * GLA_WIDTH), F32)
    w_gate = w_gate.at[0:r, 0:GLA_WIDTH].set(wgf[l]).at[r:2 * r, GLA_WIDTH:].set(wgb[l])
    wf_bd = jnp.zeros((FNET_WIDTH, FNET_WIDTH), F32)
    for g in range(FNET_GROUPS):
        wf_bd = wf_bd.at[HEAD_DIM * g:HEAD_DIM * (g + 1), HEAD_DIM * g:HEAD_DIM * (g + 1)].set(w_fourier[l, g])
    return {
        "g1": g_norm1[l][None, :],
        "w_main": w_l[:, 0:MAIN_WIDTH].astype(BF16),
        "w_z": w_z.astype(BF16),
        "w_gate": w_gate.astype(BF16),
        "b_gate": jnp.concatenate([bgf[l], bgb[l]])[None, :],
        "q_g": jnp.tile(q_norm_g[l], 2)[None, :],
        "k_g": jnp.tile(k_norm_g[l], 2)[None, :],
        "bd128": _block_diag_mean(LANES),
        "bd256": _block_diag_mean(GLA_WIDTH),
        "wf_bd": wf_bd.astype(BF16),
        "gla_g": jnp.tile(gla_norm_g[l], GLA_HEADS)[None, :],
        "w_out": w_out[l].astype(BF16),
        "g2": g_norm2[l][None, :],
        "w_ffn_in": w_ffn_in[l].astype(BF16),
        "w_ffn_out": w_ffn_out[l].astype(BF16),
    }


def kernel(x, c, ctx, c_ctx, w_mod, b_mod, g_norm1, w_in, q_norm_g, k_norm_g, attn_sink, w_fourier, gla_w_gate_f,
           gla_b_gate_f, gla_w_gate_b, gla_b_gate_b, gla_norm_g, w_out, g_norm2, w_ffn_in, w_ffn_out):
    b, n, d = x.shape
    lc = ctx.shape[1]
    depth = w_mod.shape[0]
    assert d == D_MODEL and b <= 7 and n % 512 == 0 and n % (DFT_N1 * 8) == 0 and lc % GLA_CHUNK == 0

    cc = jnp.zeros((8, d), F32).at[0:b].set(c).at[b].set(c_ctx)
    mod = _modulation(cc, w_mod, b_mod)
    rope_tabs = _rope_tables(n)
    dft = _dft_consts(n)
    dft_ctx = _dft_ctx_consts(lc)
    tm_lat, tm_ctx, tb_lat = 512, lc, 512
    xc = ctx
    for l in range(depth):
        need_ctx = l < depth - 1
        lw = _layer_weights(l, w_in, g_norm1, q_norm_g, k_norm_g, w_fourier, gla_w_gate_f, gla_b_gate_f,
                            gla_w_gate_b, gla_b_gate_b, gla_norm_g, w_out, g_norm2, w_ffn_in, w_ffn_out)
        mod3 = mod[l].reshape(8, 1, 6 * d)
        sink = attn_sink[l]
        cq, ck4, cv4, cfu, cgq, cgk, cgv, cgr, cla = _inproj(xc, mod3, b, lw, None, tm_ctx)
        q, k4, v4, fu, gq, gk, gv, gr, la = _inproj(x, mod3, None, lw, rope_tabs, tm_lat)
        att = _attention(q, k4, v4, ck4, cv4, sink)
        four = _fourier(fu, lw["wf_bd"], dft)
        s_zero = jnp.zeros((b, 2, GLA_WIDTH, GLA_WIDTH), F32)
        ocf, ocb, s_ctx = _gla(cgq, cgk, cgv, cla, s_zero, lc)
        olf, olb, _ = _gla(gq, gk, gv, la, s_ctx, tb_lat)
        x = _outproj(att, four, olf, olb, gr, x, mod3, None, lw, tm_lat)
        x = _ffn(x, mod3, None, lw, tm_lat)
        if need_ctx:
            att_c = _attention_ctx(cq, ck4, cv4, sink)
            four_c = _fourier_ctx(cfu, lw["wf_bd"], dft_ctx)
            xc = _outproj(att_c, four_c, ocf, ocb, cgr, xc, mod3, b, lw, tm_ctx)
            xc = _ffn(xc, mod3, b, lw, tm_ctx)
    return x
```

```python
import functools

import numpy as np
import jax
import jax.numpy as jnp
from jax import lax
from jax.experimental import pallas as pl
from jax.experimental.pallas import tpu as pltpu

F32 = jnp.float32
BF16 = jnp.bfloat16

D_MODEL = 1024
HEAD_DIM = 64
GRID_W = 64
ROPE_BASE = 10000.0
ATT_HEADS = 8
ATT_KV_HEADS = 2
ATT_WIDTH = ATT_HEADS * HEAD_DIM
KV_WIDTH = ATT_KV_HEADS * HEAD_DIM
ATT_BLOCK = 128
ATT_SCALE = HEAD_DIM ** -0.5
LOG2E = 1.4426950408889634
NEG_INF = -1e30
FNET_GROUPS = 4
FNET_WIDTH = FNET_GROUPS * HEAD_DIM
GLA_HEADS = 4
GLA_WIDTH = GLA_HEADS * HEAD_DIM
GLA_GATE_RANK = 16
GLA_TAU = 16.0
GLA_CHUNK = 64
GLA_SCALE = HEAD_DIM ** -0.5
MAIN_WIDTH = ATT_WIDTH + 2 * KV_WIDTH + FNET_WIDTH + 4 * GLA_WIDTH
FFN_HIDDEN = 2816
FFN_CHUNK = 256
EPS = 1e-6
LANES = 128
DFT_N1 = 128
DFT_STEP = 8
VMEM_LIMIT = 56 * 1024 * 1024

NT_DIMS = (((1,), (1,)), ((), ()))
TN_DIMS = (((0,), (0,)), ((), ()))


def _params(*sem):
    return pltpu.CompilerParams(dimension_semantics=sem, vmem_limit_bytes=VMEM_LIMIT)


def _dot(a, b):
    return jnp.dot(a, b, preferred_element_type=F32)


def _silu(x):
    return x / (1.0 + jnp.exp(-x))


def _full(shape):
    nd = len(shape)
    return pl.BlockSpec(shape, lambda *_: (0,) * nd)


def _group_mean(t, bd_ref):
    return _dot((t * t).astype(BF16), bd_ref[...])


def _mod_kernel(c_ref, w_ref, b_ref, o_ref):
    s = _silu(c_ref[...]).astype(BF16)
    o_ref[...] = _dot(s, w_ref[...].astype(BF16)) + b_ref[...]


def _modulation(cc, w_mod, b_mod):
    depth, d, width = w_mod.shape
    tn = 1536
    return pl.pallas_call(
        _mod_kernel,
        out_shape=jax.ShapeDtypeStruct((depth, 8, width), F32),
        grid=(depth, width // tn),
        in_specs=[
            _full((8, d)),
            pl.BlockSpec((None, d, tn), lambda l, j: (l, 0, j)),
            pl.BlockSpec((None, 1, tn), lambda l, j: (l, 0, j)),
        ],
        out_specs=pl.BlockSpec((None, 8, tn), lambda l, j: (l, 0, j)),
        compiler_params=_params("parallel", "parallel"),
        name="modulation",
    )(cc, w_mod, b_mod.reshape(depth, 1, width))


def _inproj_kernel(*refs, rope):
    if rope:
        (x_ref, mod_ref, g1_ref, wm_ref, wz_ref, wg_ref, bg_ref, qg_ref, kg_ref, bd_ref, cos_ref, sin_ref,
         q_ref, k4_ref, v4_ref, fu_ref, gq_ref, gk_ref, gv_ref, gr_ref, la_ref) = refs
    else:
        (x_ref, mod_ref, g1_ref, wm_ref, wz_ref, wg_ref, bg_ref, qg_ref, kg_ref, bd_ref,
         q_ref, k4_ref, v4_ref, fu_ref, gq_ref, gk_ref, gv_ref, gr_ref, la_ref) = refs
    d = D_MODEL
    x = x_ref[...]
    mod = mod_ref[...]
    sh, sc = mod[:, 0:d], mod[:, d:2 * d]
    ms = jnp.mean(x * x, axis=-1, keepdims=True)
    h = (x * lax.rsqrt(ms + EPS) * g1_ref[...]) * (1.0 + sc) + sh
    hb = h.astype(BF16)
    group = lambda g: _dot(hb, wm_ref[:, 4 * LANES * g:4 * LANES * (g + 1)])

    lane = lax.broadcasted_iota(jnp.int32, (1, LANES), 1)
    lo = lane < HEAD_DIM
    second_half = (lane & 16) != 0

    def head_norm(t, g_ref):
        return t * lax.rsqrt(_group_mean(t, bd_ref) + EPS) * g_ref[...]

    def rotary(t):
        if not rope:
            return t
        partner = jnp.where(second_half, pltpu.roll(t, 16, 1), pltpu.roll(t, LANES - 16, 1))
        return t * cos_ref[...] + partner * sin_ref[...]

    def spread(t, out_ref, idle):
        tr = pltpu.roll(t, HEAD_DIM, 1)
        fill = jnp.full_like(t, idle)
        out_ref[:, 0:128] = jnp.where(lo, t, fill).astype(out_ref.dtype)
        out_ref[:, 128:256] = jnp.where(lo, fill, tr).astype(out_ref.dtype)
        out_ref[:, 256:384] = jnp.where(lo, tr, fill).astype(out_ref.dtype)
        out_ref[:, 384:512] = jnp.where(lo, fill, t).astype(out_ref.dtype)

    assert (ATT_WIDTH, 2 * KV_WIDTH + FNET_WIDTH, 2 * GLA_WIDTH) == (4 * LANES,) * 3
    gz = _dot(hb, wz_ref[...])
    a_q = group(0)
    z = _dot(gz.astype(BF16), wg_ref[...]) + bg_ref[...]
    log_sig = jnp.minimum(z, 0.0) - jnp.log(1.0 + jnp.exp(-jnp.abs(z)))
    la_ref[...] = log_sig * (1.0 / GLA_TAU)
    a_kvf = group(1)
    for j in range(ATT_WIDTH // LANES):
        t = rotary(head_norm(a_q[:, LANES * j:LANES * (j + 1)], qg_ref)) * (ATT_SCALE * LOG2E)
        q_ref[:, LANES * j:LANES * (j + 1)] = t.astype(q_ref.dtype)
    a_qk = group(2)
    spread(rotary(head_norm(a_kvf[:, 0:KV_WIDTH], kg_ref)), k4_ref, 0.0)
    spread(a_kvf[:, KV_WIDTH:2 * KV_WIDTH], v4_ref, 1.0)
    fu_ref[...] = a_kvf[:, 2 * KV_WIDTH:].astype(fu_ref.dtype)
    a_vr = group(3)
    gq_ref[...] = a_qk[:, 0:GLA_WIDTH] * GLA_SCALE
    gk_ref[...] = a_qk[:, GLA_WIDTH:]
    gv_ref[...] = a_vr[:, 0:GLA_WIDTH]
    gr_ref[...] = a_vr[:, GLA_WIDTH:]


def _inproj(x, mod3, mod_row, lw, rope_tabs, tm):
    b, n, d = x.shape
    rope = rope_tabs is not None
    row = (lambda bi: bi) if mod_row is None else (lambda bi: mod_row)
    tok = lambda w: pl.BlockSpec((None, tm, w), lambda bi, i: (bi, i, 0))
    in_specs = [
        tok(d),
        pl.BlockSpec((None, 1, 6 * d), lambda bi, i: (row(bi), 0, 0)),
        _full((1, d)), _full((d, MAIN_WIDTH)), _full((d, LANES)), _full((LANES, 2 * GLA_WIDTH)),
        _full((1, 2 * GLA_WIDTH)), _full((1, LANES)), _full((1, LANES)), _full((LANES, LANES)),
    ]
    args = [x, mod3, lw["g1"], lw["w_main"], lw["w_z"], lw["w_gate"], lw["b_gate"], lw["q_g"], lw["k_g"], lw["bd128"]]
    if rope:
        in_specs += [pl.BlockSpec((tm, LANES), lambda bi, i: (i, 0))] * 2
        args += list(rope_tabs)
    sds = lambda w, dt: jax.ShapeDtypeStruct((b, n, w), dt)
    out_shape = [sds(ATT_WIDTH, BF16), sds(4 * LANES, BF16), sds(4 * LANES, BF16), sds(FNET_WIDTH, F32),
                 sds(GLA_WIDTH, F32), sds(GLA_WIDTH, F32), sds(GLA_WIDTH, F32), sds(GLA_WIDTH, F32),
                 sds(2 * GLA_WIDTH, F32)]
    out_specs = [tok(s.shape[-1]) for s in out_shape]
    return pl.pallas_call(
        functools.partial(_inproj_kernel, rope=rope),
        out_shape=out_shape, grid=(b, n // tm), in_specs=in_specs, out_specs=out_specs,
        compiler_params=_params("parallel", "parallel"),
        name="inproj_rope" if rope else "inproj_ctx",
    )(*args)


def _scores(q2, key_tiles, masks):
    cols = []
    for k, tile_masks in zip(key_tiles, masks):
        s = lax.dot_general(q2, k, NT_DIMS, preferred_element_type=F32)
        for j, mk in enumerate(tile_masks):
            c = s[:, LANES * j:LANES * (j + 1)]
            cols.append(c if mk is None else jnp.where(mk, c, NEG_INF))
    return cols


def _softmax_pv(cols, val_tiles, sink_col):
    mx = cols[0]
    for c in cols[1:]:
        mx = jnp.maximum(mx, c)
    m = jnp.maximum(jnp.max(mx, axis=-1, keepdims=True), sink_col)
    p = jnp.concatenate([jnp.exp2(c - m).astype(BF16) for c in cols], axis=1)
    pv = _dot(p, jnp.concatenate(val_tiles, axis=0))
    den = pltpu.roll(pv, HEAD_DIM, 1) + jnp.exp2(sink_col - m)
    return pv / den


def _attn_kernel(sink_ref, q_ref, kl_ref, km_ref, kr_ref, kc_ref, vl_ref, vm_ref, vr_ref, vc_ref, o_ref):
    i = pl.program_id(1)
    nb = pl.num_programs(1)
    blk = ATT_BLOCK
    group = ATT_HEADS // ATT_KV_HEADS
    row = lax.broadcasted_iota(jnp.int32, (2 * blk, blk), 0) & (blk - 1)
    col = lax.broadcasted_iota(jnp.int32, (2 * blk, blk), 1)
    mask_l = jnp.logical_and(col >= row, i > 0)
    mask_r = jnp.logical_and(col <= row, i < nb - 1)
    masks = [(mask_l, None), (mask_r, None), (None,)]
    upper_rows = lax.broadcasted_iota(jnp.int32, (2 * blk, 1), 0) >= blk
    lo = lax.broadcasted_iota(jnp.int32, (1, LANES), 1) < HEAD_DIM

    def tiles(l, m_, r, c, ks):
        return [jnp.concatenate([l[:, ks], m_[:, ks]], axis=0), jnp.concatenate([r[:, ks], c[0:blk, ks]], axis=0),
                c[blk:, ks]]

    slots = [(kvh, par) for kvh in range(ATT_KV_HEADS) for par in range(2)]
    lane_slice = lambda kvh, par: slice(2 * LANES * kvh + LANES * par, 2 * LANES * kvh + LANES * (par + 1))
    cols = {}
    for kvh, par in slots:
        qs = 2 * LANES * kvh
        q2 = jnp.concatenate([q_ref[:, qs:qs + LANES], q_ref[:, qs + LANES:qs + 2 * LANES]], axis=0)
        cols[kvh, par] = _scores(q2, tiles(kl_ref, km_ref, kr_ref, kc_ref, lane_slice(kvh, par)), masks)
    outs = {}
    for kvh, par in slots:
        base = group * kvh + par
        sink_col = jnp.where(upper_rows, sink_ref[base + 2], sink_ref[base]) * LOG2E
        outs[kvh, par] = _softmax_pv(cols[kvh, par], tiles(vl_ref, vm_ref, vr_ref, vc_ref, lane_slice(kvh, par)),
                                     sink_col)
    for kvh in range(ATT_KV_HEADS):
        qs = 2 * LANES * kvh
        o = jnp.where(lo, outs[kvh, 0], outs[kvh, 1])
        o_ref[:, qs:qs + LANES] = o[0:blk].astype(o_ref.dtype)
        o_ref[:, qs + LANES:qs + 2 * LANES] = o[blk:2 * blk].astype(o_ref.dtype)


def _attention(q, k4, v4, ck4, cv4, sink):
    b, n, _ = q.shape
    lc = ck4.shape[1]
    assert lc == 2 * ATT_BLOCK
    nb = n // ATT_BLOCK
    blk3 = lambda f: pl.BlockSpec((None, ATT_BLOCK, 4 * LANES), f)
    left = lambda bi, i: (bi, jnp.maximum(i - 1, 0), 0)
    mid = lambda bi, i: (bi, i, 0)
    right = lambda bi, i: (bi, jnp.minimum(i + 1, nb - 1), 0)
    ctx = pl.BlockSpec((None, lc, 4 * LANES), lambda bi, i: (bi, 0, 0))
    return pl.pallas_call(
        _attn_kernel,
        out_shape=jax.ShapeDtypeStruct((b, n, ATT_WIDTH), BF16),
        grid=(b, nb),
        in_specs=[pl.BlockSpec(memory_space=pltpu.SMEM),
                  blk3(mid), blk3(left), blk3(mid), blk3(right), ctx, blk3(left), blk3(mid), blk3(right), ctx],
        out_specs=blk3(mid),
        compiler_params=_params("parallel", "parallel"),
        name="window_attention",
    )(sink, q, k4, k4, k4, ck4, v4, v4, v4, cv4)


def _attn_ctx_kernel(sink_ref, q_ref, kc_ref, vc_ref, o_ref):
    kvh = pl.program_id(1)
    lc = q_ref.shape[0]
    q2 = jnp.concatenate([q_ref[:, 0:LANES], q_ref[:, LANES:2 * LANES]], axis=0)
    upper_rows = lax.broadcasted_iota(jnp.int32, (2 * lc, 1), 0) >= lc
    lo = lax.broadcasted_iota(jnp.int32, (1, LANES), 1) < HEAD_DIM
    outs = []
    for par in range(2):
        ks = slice(LANES * par, LANES * (par + 1))
        base = ATT_HEADS // ATT_KV_HEADS * kvh + par
        sink_col = jnp.where(upper_rows, sink_ref[base + 2], sink_ref[base]) * LOG2E
        cols = _scores(q2, [kc_ref[:, ks]], [(None,) * (lc // LANES)])
        outs.append(_softmax_pv(cols, [vc_ref[:, ks]], sink_col))
    o = jnp.where(lo, outs[0], outs[1])
    o_ref[:, 0:LANES] = o[0:lc].astype(o_ref.dtype)
    o_ref[:, LANES:2 * LANES] = o[lc:2 * lc].astype(o_ref.dtype)


def _attention_ctx(cq, ck4, cv4, sink):
    b, lc, _ = cq.shape
    gw = 2 * LANES
    spec = pl.BlockSpec((None, lc, gw), lambda bi, h: (bi, 0, h))
    return pl.pallas_call(
        _attn_ctx_kernel,
        out_shape=jax.ShapeDtypeStruct((b, lc, ATT_WIDTH), BF16),
        grid=(b, ATT_KV_HEADS),
        in_specs=[pl.BlockSpec(memory_space=pltpu.SMEM), spec, spec, spec],
        out_specs=spec,
        compiler_params=_params("parallel", "parallel"),
        name="context_attention",
    )(sink, cq, ck4, cv4)


def _dft_consts(n):
    n1, n2 = DFT_N1, n // DFT_N1
    ang = lambda a, m: 2.0 * np.pi * np.outer(np.arange(a), np.arange(a)) / m
    c = HEAD_DIM
    cc, sc = np.cos(ang(c, c)) / np.sqrt(c), np.sin(ang(c, c)) / np.sqrt(c)
    eye = np.eye(FNET_GROUPS)
    w_chan = np.concatenate([np.kron(eye, cc), -np.kron(eye, sc)], axis=1)
    c1, s1 = np.cos(ang(n1, n1)) / np.sqrt(n1), np.sin(ang(n1, n1)) / np.sqrt(n1)
    m1 = np.block([[c1, s1], [-s1, c1]])
    kk = np.arange(n1)[:, None, None] + n1 * np.arange(n2)[None, :, None]
    ph = 2.0 * np.pi * (kk * np.arange(n2)[None, None, :] % n) / n
    m3 = np.concatenate([np.cos(ph), np.sin(ph)], axis=2) / np.sqrt(n2)
    return tuple(jnp.asarray(a, F32) for a in (w_chan, m1, m3))


def _dft_ctx_consts(lc):
    ang = lambda a, m: 2.0 * np.pi * np.outer(np.arange(a), np.arange(a)) / m
    c = HEAD_DIM
    cc, sc = np.cos(ang(c, c)) / np.sqrt(c), np.sin(ang(c, c)) / np.sqrt(c)
    eye = np.eye(FNET_GROUPS)
    w_chan = np.concatenate([np.kron(eye, cc), -np.kron(eye, sc)], axis=1)
    cl, sl = np.cos(ang(lc, lc)) / np.sqrt(lc), np.sin(ang(lc, lc)) / np.sqrt(lc)
    return jnp.asarray(w_chan, F32), jnp.asarray(np.concatenate([cl, sl], axis=1), F32)


def _four1_kernel(u_ref, wc_ref, m1_ref, p_ref, z_ref):
    w, t = FNET_WIDTH, DFT_STEP
    u = u_ref[...].reshape(DFT_N1 * t, w).astype(BF16)
    z_ref[...] = _dot(u, wc_ref[...]).reshape(DFT_N1, t, 2 * w)
    for tt in range(t):
        z = z_ref[:, tt, :]
        zs = jnp.concatenate([z[:, 0:w], z[:, w:2 * w]], axis=0).astype(BF16)
        a = _dot(m1_ref[...], zs)
        p_ref[0, :, tt, :] = a[0:DFT_N1]
        p_ref[1, :, tt, :] = a[DFT_N1:2 * DFT_N1]


def _four2_kernel(p_ref, m3_ref, wf_ref, y_ref):
    w, t = FNET_WIDTH, DFT_STEP
    n2 = p_ref.shape[2]
    ys = []
    for kk in range(t):
        rhs = jnp.concatenate([p_ref[0, kk], p_ref[1, kk]], axis=0).astype(BF16)
        ys.append(_dot(m3_ref[kk].astype(BF16), rhs).astype(BF16))
    out = _dot(jnp.concatenate(ys, axis=0), wf_ref[...])
    for kk in range(t):
        y_ref[:, kk, :] = out[n2 * kk:n2 * (kk + 1)]


def _fourier(fu, wf_bd, consts):
    b, n, w = fu.shape
    n1, n2, t = DFT_N1, n // DFT_N1, DFT_STEP
    w_chan, m1, m3 = consts
    p = pl.pallas_call(
        _four1_kernel,
        out_shape=jax.ShapeDtypeStruct((b, 2, n1, n2, w), F32),
        grid=(b, n2 // t),
        in_specs=[pl.BlockSpec((None, n1, t, w), lambda bi, j: (bi, 0, j, 0)),
                  _full((w, 2 * w)), _full((2 * n1, 2 * n1))],
        out_specs=pl.BlockSpec((None, 2, n1, t, w), lambda bi, j: (bi, 0, 0, j, 0)),
        scratch_shapes=[pltpu.VMEM((n1, t, 2 * w), F32)],
        compiler_params=_params("parallel", "parallel"),
        name="fourier_stage1",
    )(fu.reshape(b, n1, n2, w), w_chan.astype(BF16), m1.astype(BF16))
    y = pl.pallas_call(
        _four2_kernel,
        out_shape=jax.ShapeDtypeStruct((b, n2, n1, w), F32),
        grid=(b, n1 // t),
        in_specs=[pl.BlockSpec((None, 2, t, n2, w), lambda bi, j: (bi, 0, j, 0, 0)),
                  pl.BlockSpec((t, n2, 2 * n2), lambda bi, j: (j, 0, 0)), _full((w, w))],
        out_specs=pl.BlockSpec((None, n2, t, w), lambda bi, j: (bi, 0, j, 0)),
        compiler_params=_params("parallel", "parallel"),
        name="fourier_stage2",
    )(p, m3, wf_bd)
    return y.reshape(b, n, w)


def _four_ctx_kernel(u_ref, wc_ref, m_ref, wf_ref, y_ref):
    w = FNET_WIDTH
    z = _dot(u_ref[...].astype(BF16), wc_ref[...])
    zs = jnp.concatenate([z[:, 0:w], z[:, w:2 * w]], axis=0).astype(BF16)
    y = _dot(m_ref[...], zs)
    y_ref[...] = _dot(y.astype(BF16), wf_ref[...]).astype(y_ref.dtype)


def _fourier_ctx(cfu, wf_bd, consts):
    b, lc, w = cfu.shape
    w_chan, m = consts[0].astype(BF16), consts[1].astype(BF16)
    return pl.pallas_call(
        _four_ctx_kernel,
        out_shape=jax.ShapeDtypeStruct((b, lc, w), BF16),
        grid=(b,),
        in_specs=[pl.BlockSpec((None, lc, w), lambda bi: (bi, 0, 0)), _full((w, 2 * w)), _full((lc, 2 * lc)),
                  _full((w, w))],
        out_specs=pl.BlockSpec((None, lc, w), lambda bi: (bi, 0, 0)),
        compiler_params=_params("parallel"),
        name="fourier_context",
    )(cfu, w_chan, m, wf_bd)


def _head_stack(x, head_lane):
    zero = jnp.zeros_like(x)
    return jnp.concatenate([jnp.where(head_lane == h, x, zero) for h in range(GLA_HEADS)], axis=0)


def _gla_kernel(qf_ref, kf_ref, vf_ref, laf_ref, qb_ref, kb_ref, vb_ref, lab_ref, s0_ref,
                of_ref, ob_ref, sfin_ref, stf_ref, stb_ref):
    i = pl.program_id(1)
    c = GLA_CHUNK
    nchunk = qf_ref.shape[0] // c

    @pl.when(i == 0)
    def _():
        stf_ref[...] = s0_ref[0]
        stb_ref[...] = s0_ref[1]

    r64 = lax.broadcasted_iota(jnp.int32, (c, c), 0)
    c64 = lax.broadcasted_iota(jnp.int32, (c, c), 1)
    tri = ((c64 <= r64).astype(BF16), (c64 >= r64).astype(BF16))
    at = lax.broadcasted_iota(jnp.int32, (c, GLA_WIDTH), 0)
    as_ = lax.broadcasted_iota(jnp.int32, (c, GLA_WIDTH), 1) & (c - 1)
    att_mask = (as_ <= at, as_ >= at)
    head_shift = HEAD_DIM.bit_length() - 1
    head_lane = lax.broadcasted_iota(jnp.int32, (1, GLA_WIDTH), 1) >> head_shift
    br = lax.broadcasted_iota(jnp.int32, (GLA_WIDTH, GLA_WIDTH), 0) >> head_shift
    bc = lax.broadcasted_iota(jnp.int32, (GLA_WIDTH, GLA_WIDTH), 1) >> head_shift
    bd_mask = br == bc
    in_refs = ((qf_ref, kf_ref, vf_ref, laf_ref), (qb_ref, kb_ref, vb_ref, lab_ref))
    out_refs = (of_ref, ob_ref)
    items = [(d, step if d == 0 else nchunk - 1 - step) for step in range(nchunk) for d in range(2)]
    rows = lambda ch: slice(c * ch, c * (ch + 1))

    bcum = {}
    for d, ch in items:
        la = in_refs[d][3][rows(ch)]
        la_hi = la.astype(BF16)
        la_lo = (la - la_hi.astype(F32)).astype(BF16)
        bcum[d, ch] = _dot(tri[d], la_hi) + _dot(tri[d], la_lo)
    work = {}
    for d, ch in items:
        q_ref, k_ref, v_ref, _ = in_refs[d]
        b = bcum[d, ch]
        btot = b[0:1] if d == 1 else b[c - 1:c]
        k = k_ref[rows(ch)]
        q_in = (q_ref[rows(ch)] * jnp.exp(b)).astype(BF16)
        k_in = (k * jnp.exp(-b)).astype(BF16)
        k_out = (k * jnp.exp(btot - b)).astype(BF16)
        vb = v_ref[rows(ch)].astype(BF16)
        att = lax.dot_general(q_in, _head_stack(k_in, head_lane), NT_DIMS, preferred_element_type=F32)
        work[d, ch] = (btot, q_in, k_out, vb, att)
    o_intra, ut = {}, {}
    for d, ch in items:
        btot, q_in, k_out, vb, att = work[d, ch]
        att = jnp.where(att_mask[d], att, 0.0).astype(BF16)
        o_intra[d, ch] = _dot(att, _head_stack(vb, head_lane))
        ut[d, ch] = lax.dot_general(vb, k_out, TN_DIMS, preferred_element_type=F32)
    st = [stf_ref[...], stb_ref[...]]
    for d, ch in items:
        btot, q_in = work[d, ch][0:2]
        o_inter = lax.dot_general(q_in, st[d].astype(BF16), NT_DIMS, preferred_element_type=F32)
        out_refs[d][rows(ch)] = o_intra[d, ch] + o_inter
        st[d] = st[d] * jnp.exp(btot) + jnp.where(bd_mask, ut[d, ch], 0.0)
    stf_ref[...] = st[0]
    stb_ref[...] = st[1]

    @pl.when(i == pl.num_programs(1) - 1)
    def _():
        sfin_ref[0] = st[0]
        sfin_ref[1] = st[1]


def _gla(gq, gk, gv, la, s0, tb):
    b, n, w = gq.shape
    nblk = n // tb
    fwd = lambda bi, i: (bi, i, 0)
    bwd = lambda bi, i: (bi, nblk - 1 - i, 0)
    bwd_la = lambda bi, i: (bi, nblk - 1 - i, 1)
    tok = lambda f: pl.BlockSpec((None, tb, w), f)
    state = pl.BlockSpec((None, 2, w, w), lambda bi, i: (bi, 0, 0, 0))
    return pl.pallas_call(
        _gla_kernel,
        out_shape=[jax.ShapeDtypeStruct((b, n, w), F32), jax.ShapeDtypeStruct((b, n, w), F32),
                   jax.ShapeDtypeStruct((b, 2, w, w), F32)],
        grid=(b, nblk),
        in_specs=[tok(fwd), tok(fwd), tok(fwd), tok(fwd), tok(bwd), tok(bwd), tok(bwd), tok(bwd_la), state],
        out_specs=[tok(fwd), tok(bwd), state],
        scratch_shapes=[pltpu.VMEM((w, w), F32), pltpu.VMEM((w, w), F32)],
        compiler_params=_params("parallel", "arbitrary"),
        name="gla_scan",
    )(gq, gk, gv, la, gq, gk, gv, la, s0)


def _tail_kernel(att_ref, four_ref, of_ref, ob_ref, r_ref, x_ref, mod_ref, gg_ref, bd_ref, wmix_ref, g2_ref, wi_ref,
                 wo_ref, o_ref, a_ref):
    d = D_MODEL
    mod = lambda j: mod_ref[:, j * d:(j + 1) * d]
    o = of_ref[...] + ob_ref[...]
    y = o * lax.rsqrt(_group_mean(o, bd_ref) + EPS) * gg_ref[...]
    y = y * _silu(r_ref[...])
    a0, a1, a2 = ATT_WIDTH, ATT_WIDTH + FNET_WIDTH, ATT_WIDTH + FNET_WIDTH + GLA_WIDTH
    mix = (_dot(att_ref[...], wmix_ref[0:a0]) + _dot(four_ref[...].astype(BF16), wmix_ref[a0:a1])
           + _dot(y.astype(BF16), wmix_ref[a1:a2]))
    x = x_ref[...] + mod(2) * mix
    ms = jnp.mean(x * x, axis=-1, keepdims=True)
    hb = ((x * lax.rsqrt(ms + EPS) * g2_ref[...]) * (1.0 + mod(4)) + mod(3)).astype(BF16)
    for c0 in range(0, FFN_HIDDEN, FFN_CHUNK):
        g = _dot(hb, wi_ref[:, c0:c0 + FFN_CHUNK])
        u = _dot(hb, wi_ref[:, FFN_HIDDEN + c0:FFN_HIDDEN + c0 + FFN_CHUNK])
        a_ref[:, c0:c0 + FFN_CHUNK] = (_silu(g) * u).astype(BF16)
    o_ref[...] = x + mod(5) * _dot(a_ref[...], wo_ref[...])


def _tail(att, four, of, ob, gr, x, mod3, mod_row, lw, tm):
    b, n, d = x.shape
    row = (lambda bi: bi) if mod_row is None else (lambda bi: mod_row)
    tok = lambda w: pl.BlockSpec((None, tm, w), lambda bi, i: (bi, i, 0))
    resident = lambda shape: pl.BlockSpec(shape, lambda bi, i: (0, 0), pipeline_mode=pl.Buffered(1))
    return pl.pallas_call(
        _tail_kernel,
        out_shape=jax.ShapeDtypeStruct((b, n, d), F32),
        grid=(b, n // tm),
        in_specs=[tok(ATT_WIDTH), tok(FNET_WIDTH), tok(GLA_WIDTH), tok(GLA_WIDTH), tok(GLA_WIDTH), tok(d),
                  pl.BlockSpec((None, 1, 6 * d), lambda bi, i: (row(bi), 0, 0)),
                  _full((1, GLA_WIDTH)), _full((GLA_WIDTH, GLA_WIDTH)), resident((d, d)),
                  _full((1, d)), resident((d, 2 * FFN_HIDDEN)), resident((FFN_HIDDEN, d))],
        out_specs=tok(d),
        scratch_shapes=[pltpu.VMEM((tm, FFN_HIDDEN), BF16)],
        compiler_params=_params("parallel", "parallel"),
        name="mix_ffn",
    )(att, four, of, ob, gr, x, mod3, lw["gla_g"], lw["bd256"], lw["w_out"], lw["g2"], lw["w_ffn_in"],
      lw["w_ffn_out"])


def _rope_tables(n):
    axis_dim = HEAD_DIM // 2
    inv_freq = ROPE_BASE ** (-np.arange(0, axis_dim, 2, dtype=np.float64) / axis_dim)
    t = np.arange(n)
    ang_r = (t // GRID_W)[:, None] * inv_freq[None, :]
    ang_c = (t % GRID_W)[:, None] * inv_freq[None, :]
    cos = np.concatenate([np.cos(ang_r)] * 2 + [np.cos(ang_c)] * 2, axis=1)
    sin = np.concatenate([-np.sin(ang_r), np.sin(ang_r), -np.sin(ang_c), np.sin(ang_c)], axis=1)
    return jnp.asarray(np.tile(cos, (1, 2)), F32), jnp.asarray(np.tile(sin, (1, 2)), F32)


def _block_diag_mean(width):
    g = np.arange(width) // HEAD_DIM
    return jnp.asarray((g[:, None] == g[None, :]) / HEAD_DIM, BF16)


def _layer_weights(l, w_in, g_norm1, q_norm_g, k_norm_g, w_fourier, wgf, bgf, wgb, bgb, gla_norm_g, w_out,
                   g_norm2, w_ffn_in, w_ffn_out):
    r = GLA_GATE_RANK
    w_l = w_in[l]
    w_z = jnp.zeros((D_MODEL, LANES), F32).at[:, 0:2 * r].set(w_l[:, MAIN_WIDTH:MAIN_WIDTH + 2 * r])
    w_gate = jnp.zeros((LANES, 2 * GLA_WIDTH), F32)
    w_gate = w_gate.at[0:r, 0:GLA_WIDTH].set(wgf[l]).at[r:2 * r, GLA_WIDTH:].set(wgb[l])
    wf_bd = jnp.zeros((FNET_WIDTH, FNET_WIDTH), F32)
    for g in range(FNET_GROUPS):
        wf_bd = wf_bd.at[HEAD_DIM * g:HEAD_DIM * (g + 1), HEAD_DIM * g:HEAD_DIM * (g + 1)].set(w_fourier[l, g])
    return {
        "g1": g_norm1[l][None, :],
        "w_main": w_l[:, 0:MAIN_WIDTH].astype(BF16),
        "w_z": w_z.astype(BF16),
        "w_gate": w_gate.astype(BF16),
        "b_gate": jnp.concatenate([bgf[l], bgb[l]])[None, :],
        "q_g": jnp.tile(q_norm_g[l], 2)[None, :],
        "k_g": jnp.tile(k_norm_g[l], 2)[None, :],
        "bd128": _block_diag_mean(LANES),
        "bd256": _block_diag_mean(GLA_WIDTH),
        "wf_bd": wf_bd.astype(BF16),
        "gla_g": jnp.tile(gla_norm_g[l], GLA_HEADS)[None, :],
        "w_out": w_out[l].astype(BF16),
        "g2": g_norm2[l][None, :],
        "w_ffn_in": w_ffn_in[l].astype(BF16),
        "w_ffn_out": w_ffn_out[l].astype(BF16),
    }


def kernel(x, c, ctx, c_ctx, w_mod, b_mod, g_norm1, w_in, q_norm_g, k_norm_g, attn_sink, w_fourier, gla_w_gate_f,
           gla_b_gate_f, gla_w_gate_b, gla_b_gate_b, gla_norm_g, w_out, g_norm2, w_ffn_in, w_ffn_out):
    b, n, d = x.shape
    lc = ctx.shape[1]
    depth = w_mod.shape[0]
    assert d == D_MODEL and b <= 7 and n % 512 == 0 and n % (DFT_N1 * 8) == 0 and lc % GLA_CHUNK == 0

    cc = jnp.zeros((8, d), F32).at[0:b].set(c).at[b].set(c_ctx)
    mod = _modulation(cc, w_mod, b_mod)
    rope_tabs = _rope_tables(n)
    dft = _dft_consts(n)
    dft_ctx = _dft_ctx_consts(lc)
    tm_lat, tm_ctx, tb_lat = 512, lc, 512
    xc = ctx
    for l in range(depth):
        need_ctx = l < depth - 1
        lw = _layer_weights(l, w_in, g_norm1, q_norm_g, k_norm_g, w_fourier, gla_w_gate_f, gla_b_gate_f,
                            gla_w_gate_b, gla_b_gate_b, gla_norm_g, w_out, g_norm2, w_ffn_in, w_ffn_out)
        mod3 = mod[l].reshape(8, 1, 6 * d)
        sink = attn_sink[l]
        cq, ck4, cv4, cfu, cgq, cgk, cgv, cgr, cla = _inproj(xc, mod3, b, lw, None, tm_ctx)
        q, k4, v4, fu, gq, gk, gv, gr, la = _inproj(x, mod3, None, lw, rope_tabs, tm_lat)
        att = _attention(q, k4, v4, ck4, cv4, sink)
        four = _fourier(fu, lw["wf_bd"], dft)
        s_zero = jnp.zeros((b, 2, GLA_WIDTH, GLA_WIDTH), F32)
        ocf, ocb, s_ctx = _gla(cgq, cgk, cgv, cla, s_zero, lc)
        olf, olb, _ = _gla(gq, gk, gv, la, s_ctx, tb_lat)
        x = _tail(att, four, olf, olb, gr, x, mod3, None, lw, tm_lat)
        if need_ctx:
            att_c = _attention_ctx(cq, ck4, cv4, sink)
            four_c = _fourier_ctx(cfu, lw["wf_bd"], dft_ctx)
            xc = _tail(att_c, four_c, ocf, ocb, cgr, xc, mod3, b, lw, tm_ctx)
    return x
```

```python
import functools

import numpy as np
import jax
import jax.numpy as jnp
from jax import lax
from jax.experimental import pallas as pl
from jax.experimental.pallas import tpu as pltpu

F32 = jnp.float32
BF16 = jnp.bfloat16

D_MODEL = 1024
HEAD_DIM = 64
GRID_W = 64
ROPE_BASE = 10000.0
ATT_HEADS = 8
ATT_KV_HEADS = 2
ATT_WIDTH = ATT_HEADS * HEAD_DIM
KV_WIDTH = ATT_KV_HEADS * HEAD_DIM
ATT_BLOCK = 128
ATT_QBLOCKS = 4
ATT_SCALE = HEAD_DIM ** -0.5
LOG2E = 1.4426950408889634
NEG_INF = -1e30
FNET_GROUPS = 4
FNET_WIDTH = FNET_GROUPS * HEAD_DIM
GLA_HEADS = 4
GLA_WIDTH = GLA_HEADS * HEAD_DIM
GLA_GATE_RANK = 16
GLA_TAU = 16.0
GLA_CHUNK = 64
GLA_SCALE = HEAD_DIM ** -0.5
MAIN_WIDTH = ATT_WIDTH + 2 * KV_WIDTH + FNET_WIDTH + 4 * GLA_WIDTH
FFN_HIDDEN = 2816
FFN_CHUNK = 256
EPS = 1e-6
LANES = 128
DFT_N1 = 128
DFT_STEP = 8
VMEM_LIMIT = 56 * 1024 * 1024

NT_DIMS = (((1,), (1,)), ((), ()))
TN_DIMS = (((0,), (0,)), ((), ()))


def _params(*sem):
    return pltpu.CompilerParams(dimension_semantics=sem, vmem_limit_bytes=VMEM_LIMIT)


def _dot(a, b):
    return jnp.dot(a, b, preferred_element_type=F32)


def _silu(x):
    return x / (1.0 + jnp.exp(-x))


def _full(shape):
    nd = len(shape)
    return pl.BlockSpec(shape, lambda *_: (0,) * nd)


def _group_mean(t, bd_ref):
    return _dot((t * t).astype(BF16), bd_ref[...])


def _mod_kernel(c_ref, w_ref, b_ref, o_ref):
    s = _silu(c_ref[...]).astype(BF16)
    o_ref[...] = _dot(s, w_ref[...].astype(BF16)) + b_ref[...]


def _modulation(cc, w_mod, b_mod):
    depth, d, width = w_mod.shape
    tn = 1536
    return pl.pallas_call(
        _mod_kernel,
        out_shape=jax.ShapeDtypeStruct((depth, 8, width), F32),
        grid=(depth, width // tn),
        in_specs=[
            _full((8, d)),
            pl.BlockSpec((None, d, tn), lambda l, j: (l, 0, j)),
            pl.BlockSpec((None, 1, tn), lambda l, j: (l, 0, j)),
        ],
        out_specs=pl.BlockSpec((None, 8, tn), lambda l, j: (l, 0, j)),
        compiler_params=_params("parallel", "parallel"),
        name="modulation",
    )(cc, w_mod, b_mod.reshape(depth, 1, width))


def _inproj_kernel(*refs, rope):
    if rope:
        (x_ref, mod_ref, g1_ref, wm_ref, wz_ref, wg_ref, bg_ref, qg_ref, kg_ref, bd_ref, cos_ref, sin_ref,
         q_ref, k4_ref, v4_ref, fu_ref, gq_ref, gk_ref, gv_ref, gr_ref, la_ref) = refs
    else:
        (x_ref, mod_ref, g1_ref, wm_ref, wz_ref, wg_ref, bg_ref, qg_ref, kg_ref, bd_ref,
         q_ref, k4_ref, v4_ref, fu_ref, gq_ref, gk_ref, gv_ref, gr_ref, la_ref) = refs
    d = D_MODEL
    x = x_ref[...]
    mod = mod_ref[...]
    sh, sc = mod[:, 0:d], mod[:, d:2 * d]
    ms = jnp.mean(x * x, axis=-1, keepdims=True)
    h = (x * lax.rsqrt(ms + EPS) * g1_ref[...]) * (1.0 + sc) + sh
    hb = h.astype(BF16)
    group = lambda g: _dot(hb, wm_ref[:, 4 * LANES * g:4 * LANES * (g + 1)])

    lane = lax.broadcasted_iota(jnp.int32, (1, LANES), 1)
    lo = lane < HEAD_DIM
    second_half = (lane & 16) != 0

    def head_norm(t, g_ref):
        return t * lax.rsqrt(_group_mean(t, bd_ref) + EPS) * g_ref[...]

    def rotary(t):
        if not rope:
            return t
        partner = jnp.where(second_half, pltpu.roll(t, 16, 1), pltpu.roll(t, LANES - 16, 1))
        return t * cos_ref[...] + partner * sin_ref[...]

    def spread(t, out_ref, idle):
        tr = pltpu.roll(t, HEAD_DIM, 1)
        fill = jnp.full_like(t, idle)
        out_ref[:, 0:128] = jnp.where(lo, t, fill).astype(out_ref.dtype)
        out_ref[:, 128:256] = jnp.where(lo, fill, tr).astype(out_ref.dtype)
        out_ref[:, 256:384] = jnp.where(lo, tr, fill).astype(out_ref.dtype)
        out_ref[:, 384:512] = jnp.where(lo, fill, t).astype(out_ref.dtype)

    assert (ATT_WIDTH, 2 * KV_WIDTH + FNET_WIDTH, 2 * GLA_WIDTH) == (4 * LANES,) * 3
    gz = _dot(hb, wz_ref[...])
    a_q = group(0)
    z = _dot(gz.astype(BF16), wg_ref[...]) + bg_ref[...]
    log_sig = jnp.minimum(z, 0.0) - jnp.log(1.0 + jnp.exp(-jnp.abs(z)))
    la_ref[...] = log_sig * (1.0 / GLA_TAU)
    a_kvf = group(1)
    for j in range(ATT_WIDTH // LANES):
        t = rotary(head_norm(a_q[:, LANES * j:LANES * (j + 1)], qg_ref)) * (ATT_SCALE * LOG2E)
        q_ref[:, LANES * j:LANES * (j + 1)] = t.astype(q_ref.dtype)
    a_qk = group(2)
    spread(rotary(head_norm(a_kvf[:, 0:KV_WIDTH], kg_ref)), k4_ref, 0.0)
    spread(a_kvf[:, KV_WIDTH:2 * KV_WIDTH], v4_ref, 1.0)
    fu_ref[...] = a_kvf[:, 2 * KV_WIDTH:].astype(fu_ref.dtype)
    a_vr = group(3)
    gq_ref[...] = a_qk[:, 0:GLA_WIDTH] * GLA_SCALE
    gk_ref[...] = a_qk[:, GLA_WIDTH:]
    gv_ref[...] = a_vr[:, 0:GLA_WIDTH]
    gr_ref[...] = a_vr[:, GLA_WIDTH:]


def _inproj(x, mod3, mod_row, lw, rope_tabs, tm):
    b, n, d = x.shape
    rope = rope_tabs is not None
    row = (lambda bi: bi) if mod_row is None else (lambda bi: mod_row)
    tok = lambda w: pl.BlockSpec((None, tm, w), lambda bi, i: (bi, i, 0))
    in_specs = [
        tok(d),
        pl.BlockSpec((None, 1, 6 * d), lambda bi, i: (row(bi), 0, 0)),
        _full((1, d)), _full((d, MAIN_WIDTH)), _full((d, LANES)), _full((LANES, 2 * GLA_WIDTH)),
        _full((1, 2 * GLA_WIDTH)), _full((1, LANES)), _full((1, LANES)), _full((LANES, LANES)),
    ]
    args = [x, mod3, lw["g1"], lw["w_main"], lw["w_z"], lw["w_gate"], lw["b_gate"], lw["q_g"], lw["k_g"], lw["bd128"]]
    if rope:
        in_specs += [pl.BlockSpec((tm, LANES), lambda bi, i: (i, 0))] * 2
        args += list(rope_tabs)
    sds = lambda w, dt: jax.ShapeDtypeStruct((b, n, w), dt)
    out_shape = [sds(ATT_WIDTH, BF16), sds(4 * LANES, BF16), sds(4 * LANES, BF16), sds(FNET_WIDTH, F32),
                 sds(GLA_WIDTH, F32), sds(GLA_WIDTH, F32), sds(GLA_WIDTH, F32), sds(GLA_WIDTH, F32),
                 sds(2 * GLA_WIDTH, F32)]
    out_specs = [tok(s.shape[-1]) for s in out_shape]
    return pl.pallas_call(
        functools.partial(_inproj_kernel, rope=rope),
        out_shape=out_shape, grid=(b, n // tm), in_specs=in_specs, out_specs=out_specs,
        compiler_params=_params("parallel", "parallel"),
        name="inproj_rope" if rope else "inproj_ctx",
    )(*args)


def _scores(q2, key_tiles, masks):
    cols = []
    for k, tile_masks in zip(key_tiles, masks):
        s = lax.dot_general(q2, k, NT_DIMS, preferred_element_type=F32)
        for j, mk in enumerate(tile_masks):
            c = s[:, LANES * j:LANES * (j + 1)]
            cols.append(c if mk is None else jnp.where(mk, c, NEG_INF))
    return cols


def _softmax_pv(cols, val_tiles, sink_col):
    mx = cols[0]
    for c in cols[1:]:
        mx = jnp.maximum(mx, c)
    m = jnp.maximum(jnp.max(mx, axis=-1, keepdims=True), sink_col)
    p = jnp.concatenate([jnp.exp2(c - m).astype(BF16) for c in cols], axis=1)
    pv = _dot(p, jnp.concatenate(val_tiles, axis=0))
    den = pltpu.roll(pv, HEAD_DIM, 1) + jnp.exp2(sink_col - m)
    return pv / den


def _attn_kernel(*refs):
    nq = ATT_QBLOCKS
    sink_ref, q_ref = refs[0:2]
    k_refs, kc_ref = refs[2:nq + 4], refs[nq + 4]
    v_refs, vc_ref = refs[nq + 5:2 * nq + 7], refs[2 * nq + 7]
    o_ref = refs[2 * nq + 8]
    i = pl.program_id(1)
    last = pl.num_programs(1) - 1
    blk = ATT_BLOCK
    group = ATT_HEADS // ATT_KV_HEADS
    row = lax.broadcasted_iota(jnp.int32, (2 * blk, blk), 0) & (blk - 1)
    col = lax.broadcasted_iota(jnp.int32, (2 * blk, blk), 1)
    band_l, band_r = col >= row, col <= row
    mask_l = [jnp.logical_and(band_l, i > 0) if s == 0 else band_l for s in range(nq)]
    mask_r = [jnp.logical_and(band_r, i < last) if s == nq - 1 else band_r for s in range(nq)]
    upper_rows = lax.broadcasted_iota(jnp.int32, (2 * blk, 1), 0) >= blk
    lo = lax.broadcasted_iota(jnp.int32, (1, LANES), 1) < HEAD_DIM

    def tiles(blocks, c, s, ks):
        l, m_, r = blocks[s:s + 3]
        return [jnp.concatenate([l[:, ks], m_[:, ks]], axis=0), jnp.concatenate([r[:, ks], c[0:blk, ks]], axis=0),
                c[blk:, ks]]

    slots = [(s, kvh, par) for s in range(nq) for kvh in range(ATT_KV_HEADS) for par in range(2)]
    lane_slice = lambda kvh, par: slice(2 * LANES * kvh + LANES * par, 2 * LANES * kvh + LANES * (par + 1))
    cols = {}
    for s, kvh, par in slots:
        qs, rs = 2 * LANES * kvh, slice(blk * s, blk * (s + 1))
        q2 = jnp.concatenate([q_ref[rs, qs:qs + LANES], q_ref[rs, qs + LANES:qs + 2 * LANES]], axis=0)
        cols[s, kvh, par] = _scores(q2, tiles(k_refs, kc_ref, s, lane_slice(kvh, par)),
                                    [(mask_l[s], None), (mask_r[s], None), (None,)])
    outs = {}
    for s, kvh, par in slots:
        base = group * kvh + par
        sink_col = jnp.where(upper_rows, sink_ref[base + 2], sink_ref[base]) * LOG2E
        outs[s, kvh, par] = _softmax_pv(cols[s, kvh, par], tiles(v_refs, vc_ref, s, lane_slice(kvh, par)), sink_col)
    for s in range(nq):
        for kvh in range(ATT_KV_HEADS):
            qs, r0 = 2 * LANES * kvh, blk * s
            o = jnp.where(lo, outs[s, kvh, 0], outs[s, kvh, 1])
            o_ref[r0:r0 + blk, qs:qs + LANES] = o[0:blk].astype(o_ref.dtype)
            o_ref[r0:r0 + blk, qs + LANES:qs + 2 * LANES] = o[blk:2 * blk].astype(o_ref.dtype)


def _attention(q, k4, v4, ck4, cv4, sink):
    b, n, _ = q.shape
    lc = ck4.shape[1]
    nq = ATT_QBLOCKS
    nb = n // ATT_BLOCK
    assert lc == 2 * ATT_BLOCK and nb % nq == 0
    qspec = pl.BlockSpec((None, nq * ATT_BLOCK, 4 * LANES), lambda bi, i: (bi, i, 0))
    kv = [pl.BlockSpec((None, ATT_BLOCK, 4 * LANES),
                       functools.partial(lambda bi, i, j: (bi, jnp.clip(nq * i + j - 1, 0, nb - 1), 0), j=j))
          for j in range(nq + 2)]
    ctx = pl.BlockSpec((None, lc, 4 * LANES), lambda bi, i: (bi, 0, 0))
    return pl.pallas_call(
        _attn_kernel,
        out_shape=jax.ShapeDtypeStruct((b, n, ATT_WIDTH), BF16),
        grid=(b, nb // nq),
        in_specs=[pl.BlockSpec(memory_space=pltpu.SMEM), qspec] + kv + [ctx] + kv + [ctx],
        out_specs=qspec,
        compiler_params=_params("parallel", "parallel"),
        name="window_attention",
    )(sink, q, *([k4] * (nq + 2)), ck4, *([v4] * (nq + 2)), cv4)


def _attn_ctx_kernel(sink_ref, q_ref, kc_ref, vc_ref, o_ref):
    kvh = pl.program_id(1)
    lc = q_ref.shape[0]
    q2 = jnp.concatenate([q_ref[:, 0:LANES], q_ref[:, LANES:2 * LANES]], axis=0)
    upper_rows = lax.broadcasted_iota(jnp.int32, (2 * lc, 1), 0) >= lc
    lo = lax.broadcasted_iota(jnp.int32, (1, LANES), 1) < HEAD_DIM
    outs = []
    for par in range(2):
        ks = slice(LANES * par, LANES * (par + 1))
        base = ATT_HEADS // ATT_KV_HEADS * kvh + par
        sink_col = jnp.where(upper_rows, sink_ref[base + 2], sink_ref[base]) * LOG2E
        cols = _scores(q2, [kc_ref[:, ks]], [(None,) * (lc // LANES)])
        outs.append(_softmax_pv(cols, [vc_ref[:, ks]], sink_col))
    o = jnp.where(lo, outs[0], outs[1])
    o_ref[:, 0:LANES] = o[0:lc].astype(o_ref.dtype)
    o_ref[:, LANES:2 * LANES] = o[lc:2 * lc].astype(o_ref.dtype)


def _attention_ctx(cq, ck4, cv4, sink):
    b, lc, _ = cq.shape
    gw = 2 * LANES
    spec = pl.BlockSpec((None, lc, gw), lambda bi, h: (bi, 0, h))
    return pl.pallas_call(
        _attn_ctx_kernel,
        out_shape=jax.ShapeDtypeStruct((b, lc, ATT_WIDTH), BF16),
        grid=(b, ATT_KV_HEADS),
        in_specs=[pl.BlockSpec(memory_space=pltpu.SMEM), spec, spec, spec],
        out_specs=spec,
        compiler_params=_params("parallel", "parallel"),
        name="context_attention",
    )(sink, cq, ck4, cv4)


def _dft_consts(n):
    n1, n2 = DFT_N1, n // DFT_N1
    ang = lambda a, m: 2.0 * np.pi * np.outer(np.arange(a), np.arange(a)) / m
    c = HEAD_DIM
    cc, sc = np.cos(ang(c, c)) / np.sqrt(c), np.sin(ang(c, c)) / np.sqrt(c)
    eye = np.eye(FNET_GROUPS)
    w_chan = np.concatenate([np.kron(eye, cc), -np.kron(eye, sc)], axis=1)
    c1, s1 = np.cos(ang(n1, n1)) / np.sqrt(n1), np.sin(ang(n1, n1)) / np.sqrt(n1)
    m1 = np.block([[c1, s1], [-s1, c1]])
    kk = np.arange(n1)[:, None, None] + n1 * np.arange(n2)[None, :, None]
    ph = 2.0 * np.pi * (kk * np.arange(n2)[None, None, :] % n) / n
    m3 = np.concatenate([np.cos(ph), np.sin(ph)], axis=2) / np.sqrt(n2)
    return tuple(jnp.asarray(a, F32) for a in (w_chan, m1, m3))


def _dft_ctx_consts(lc):
    ang = lambda a, m: 2.0 * np.pi * np.outer(np.arange(a), np.arange(a)) / m
    c = HEAD_DIM
    cc, sc = np.cos(ang(c, c)) / np.sqrt(c), np.sin(ang(c, c)) / np.sqrt(c)
    eye = np.eye(FNET_GROUPS)
    w_chan = np.concatenate([np.kron(eye, cc), -np.kron(eye, sc)], axis=1)
    cl, sl = np.cos(ang(lc, lc)) / np.sqrt(lc), np.sin(ang(lc, lc)) / np.sqrt(lc)
    return jnp.asarray(w_chan, F32), jnp.asarray(np.concatenate([cl, sl], axis=1), F32)


def _four1_kernel(u_ref, wc_ref, m1_ref, p_ref, z_ref):
    w, t = FNET_WIDTH, DFT_STEP
    u = u_ref[...].reshape(DFT_N1 * t, w).astype(BF16)
    z_ref[...] = _dot(u, wc_ref[...]).reshape(DFT_N1, t, 2 * w)
    for tt in range(t):
        z = z_ref[:, tt, :]
        zs = jnp.concatenate([z[:, 0:w], z[:, w:2 * w]], axis=0).astype(BF16)
        a = _dot(m1_ref[...], zs)
        p_ref[0, :, tt, :] = a[0:DFT_N1]
        p_ref[1, :, tt, :] = a[DFT_N1:2 * DFT_N1]


def _four2_kernel(p_ref, m3_ref, wf_ref, y_ref):
    w, t = FNET_WIDTH, DFT_STEP
    n2 = p_ref.shape[2]
    ys = []
    for kk in range(t):
        rhs = jnp.concatenate([p_ref[0, kk], p_ref[1, kk]], axis=0).astype(BF16)
        ys.append(_dot(m3_ref[kk].astype(BF16), rhs).astype(BF16))
    out = _dot(jnp.concatenate(ys, axis=0), wf_ref[...])
    for kk in range(t):
        y_ref[:, kk, :] = out[n2 * kk:n2 * (kk + 1)]


def _fourier(fu, wf_bd, consts):
    b, n, w = fu.shape
    n1, n2, t = DFT_N1, n // DFT_N1, DFT_STEP
    w_chan, m1, m3 = consts
    p = pl.pallas_call(
        _four1_kernel,
        out_shape=jax.ShapeDtypeStruct((b, 2, n1, n2, w), F32),
        grid=(b, n2 // t),
        in_specs=[pl.BlockSpec((None, n1, t, w), lambda bi, j: (bi, 0, j, 0)),
                  _full((w, 2 * w)), _full((2 * n1, 2 * n1))],
        out_specs=pl.BlockSpec((None, 2, n1, t, w), lambda bi, j: (bi, 0, 0, j, 0)),
        scratch_shapes=[pltpu.VMEM((n1, t, 2 * w), F32)],
        compiler_params=_params("parallel", "parallel"),
        name="fourier_stage1",
    )(fu.reshape(b, n1, n2, w), w_chan.astype(BF16), m1.astype(BF16))
    y = pl.pallas_call(
        _four2_kernel,
        out_shape=jax.ShapeDtypeStruct((b, n2, n1, w), F32),
        grid=(b, n1 // t),
        in_specs=[pl.BlockSpec((None, 2, t, n2, w), lambda bi, j: (bi, 0, j, 0, 0)),
                  pl.BlockSpec((t, n2, 2 * n2), lambda bi, j: (j, 0, 0)), _full((w, w))],
        out_specs=pl.BlockSpec((None, n2, t, w), lambda bi, j: (bi, 0, j, 0)),
        compiler_params=_params("parallel", "parallel"),
        name="fourier_stage2",
    )(p, m3, wf_bd)
    return y.reshape(b, n, w)


def _four_ctx_kernel(u_ref, wc_ref, m_ref, wf_ref, y_ref):
    w = FNET_WIDTH
    z = _dot(u_ref[...].astype(BF16), wc_ref[...])
    zs = jnp.concatenate([z[:, 0:w], z[:, w:2 * w]], axis=0).astype(BF16)
    y = _dot(m_ref[...], zs)
    y_ref[...] = _dot(y.astype(BF16), wf_ref[...]).astype(y_ref.dtype)


def _fourier_ctx(cfu, wf_bd, consts):
    b, lc, w = cfu.shape
    w_chan, m = consts[0].astype(BF16), consts[1].astype(BF16)
    return pl.pallas_call(
        _four_ctx_kernel,
        out_shape=jax.ShapeDtypeStruct((b, lc, w), BF16),
        grid=(b,),
        in_specs=[pl.BlockSpec((None, lc, w), lambda bi: (bi, 0, 0)), _full((w, 2 * w)), _full((lc, 2 * lc)),
                  _full((w, w))],
        out_specs=pl.BlockSpec((None, lc, w), lambda bi: (bi, 0, 0)),
        compiler_params=_params("parallel"),
        name="fourier_context",
    )(cfu, w_chan, m, wf_bd)


def _pair_stack(x, lo):
    zero = jnp.zeros_like(x)
    return jnp.concatenate([jnp.where(lo, x, zero), jnp.where(lo, zero, x)], axis=0)


def _gla_kernel(qf_ref, kf_ref, vf_ref, laf_ref, qb_ref, kb_ref, vb_ref, lab_ref, s0_ref,
                of_ref, ob_ref, sfin_ref, stf_ref, stb_ref):
    i = pl.program_id(1)
    c = GLA_CHUNK
    nchunk = qf_ref.shape[0] // c

    @pl.when(i == 0)
    def _():
        stf_ref[...] = s0_ref[0]
        stb_ref[...] = s0_ref[1]

    r64 = lax.broadcasted_iota(jnp.int32, (c, c), 0)
    c64 = lax.broadcasted_iota(jnp.int32, (c, c), 1)
    tri = ((c64 <= r64).astype(BF16), (c64 >= r64).astype(BF16))
    at = lax.broadcasted_iota(jnp.int32, (c, LANES), 0)
    as_ = lax.broadcasted_iota(jnp.int32, (c, LANES), 1) & (c - 1)
    att_mask = (as_ <= at, as_ >= at)
    lo = lax.broadcasted_iota(jnp.int32, (1, LANES), 1) < HEAD_DIM
    br = lax.broadcasted_iota(jnp.int32, (LANES, LANES), 0) < HEAD_DIM
    bc = lax.broadcasted_iota(jnp.int32, (LANES, LANES), 1) < HEAD_DIM
    bd_mask = br == bc
    pairs = [slice(LANES * p, LANES * (p + 1)) for p in range(GLA_WIDTH // LANES)]
    in_refs = ((qf_ref, kf_ref, vf_ref, laf_ref), (qb_ref, kb_ref, vb_ref, lab_ref))
    out_refs = (of_ref, ob_ref)
    items = [(d, step if d == 0 else nchunk - 1 - step) for step in range(nchunk) for d in range(2)]
    rows = lambda ch: slice(c * ch, c * (ch + 1))

    bcum = {}
    for d, ch in items:
        la = in_refs[d][3][rows(ch)]
        la_hi = la.astype(BF16)
        la_lo = (la - la_hi.astype(F32)).astype(BF16)
        bcum[d, ch] = _dot(tri[d], la_hi) + _dot(tri[d], la_lo)
    work = {}
    for d, ch in items:
        q_ref, k_ref, v_ref, _ = in_refs[d]
        b = bcum[d, ch]
        btot = b[0:1] if d == 1 else b[c - 1:c]
        k = k_ref[rows(ch)]
        q_in = (q_ref[rows(ch)] * jnp.exp(b)).astype(BF16)
        k_in = (k * jnp.exp(-b)).astype(BF16)
        k_out = (k * jnp.exp(btot - b)).astype(BF16)
        vb = v_ref[rows(ch)].astype(BF16)
        att = [lax.dot_general(q_in[:, p], _pair_stack(k_in[:, p], lo), NT_DIMS, preferred_element_type=F32)
               for p in pairs]
        work[d, ch] = (btot, q_in, k_out, vb, att)
    o_intra, ut = {}, {}
    for d, ch in items:
        btot, q_in, k_out, vb, att = work[d, ch]
        o_intra[d, ch] = [_dot(jnp.where(att_mask[d], a, 0.0).astype(BF16), _pair_stack(vb[:, p], lo))
                          for a, p in zip(att, pairs)]
        ut[d, ch] = [lax.dot_general(vb[:, p], k_out[:, p], TN_DIMS, preferred_element_type=F32) for p in pairs]
    st_refs = (stf_ref, stb_ref)
    st = [[st_refs[d][LANES * j:LANES * (j + 1)] for j in range(len(pairs))] for d in range(2)]
    for d, ch in items:
        btot, q_in = work[d, ch][0:2]
        outs = []
        for j, p in enumerate(pairs):
            o_inter = lax.dot_general(q_in[:, p], st[d][j].astype(BF16), NT_DIMS, preferred_element_type=F32)
            outs.append(o_intra[d, ch][j] + o_inter)
            st[d][j] = st[d][j] * jnp.exp(btot[:, p]) + jnp.where(bd_mask, ut[d, ch][j], 0.0)
        out_refs[d][rows(ch)] = jnp.concatenate(outs, axis=1)
    for d in range(2):
        st_refs[d][...] = jnp.concatenate(st[d], axis=0)

    @pl.when(i == pl.num_programs(1) - 1)
    def _():
        for d in range(2):
            sfin_ref[d] = jnp.concatenate(st[d], axis=0)


def _gla(gq, gk, gv, la, s0, tb):
    b, n, w = gq.shape
    nblk = n // tb
    fwd = lambda bi, i: (bi, i, 0)
    bwd = lambda bi, i: (bi, nblk - 1 - i, 0)
    bwd_la = lambda bi, i: (bi, nblk - 1 - i, 1)
    tok = lambda f: pl.BlockSpec((None, tb, w), f)
    state = pl.BlockSpec((None, 2, w, LANES), lambda bi, i: (bi, 0, 0, 0))
    return pl.pallas_call(
        _gla_kernel,
        out_shape=[jax.ShapeDtypeStruct((b, n, w), F32), jax.ShapeDtypeStruct((b, n, w), F32),
                   jax.ShapeDtypeStruct((b, 2, w, LANES), F32)],
        grid=(b, nblk),
        in_specs=[tok(fwd), tok(fwd), tok(fwd), tok(fwd), tok(bwd), tok(bwd), tok(bwd), tok(bwd_la), state],
        out_specs=[tok(fwd), tok(bwd), state],
        scratch_shapes=[pltpu.VMEM((w, LANES), F32), pltpu.VMEM((w, LANES), F32)],
        compiler_params=_params("parallel", "arbitrary"),
        name="gla_scan",
    )(gq, gk, gv, la, gq, gk, gv, la, s0)


def _tail_kernel(att_ref, four_ref, of_ref, ob_ref, r_ref, x_ref, mod_ref, gg_ref, bd_ref, wmix_ref, g2_ref, wi_ref,
                 wo_ref, o_ref, a_ref):
    d = D_MODEL
    mod = lambda j: mod_ref[:, j * d:(j + 1) * d]
    o = of_ref[...] + ob_ref[...]
    y = o * lax.rsqrt(_group_mean(o, bd_ref) + EPS) * gg_ref[...]
    y = y * _silu(r_ref[...])
    a0, a1, a2 = ATT_WIDTH, ATT_WIDTH + FNET_WIDTH, ATT_WIDTH + FNET_WIDTH + GLA_WIDTH
    mix = (_dot(att_ref[...], wmix_ref[0:a0]) + _dot(four_ref[...].astype(BF16), wmix_ref[a0:a1])
           + _dot(y.astype(BF16), wmix_ref[a1:a2]))
    x = x_ref[...] + mod(2) * mix
    ms = jnp.mean(x * x, axis=-1, keepdims=True)
    hb = ((x * lax.rsqrt(ms + EPS) * g2_ref[...]) * (1.0 + mod(4)) + mod(3)).astype(BF16)
    for c0 in range(0, FFN_HIDDEN, FFN_CHUNK):
        g = _dot(hb, wi_ref[:, c0:c0 + FFN_CHUNK])
        u = _dot(hb, wi_ref[:, FFN_HIDDEN + c0:FFN_HIDDEN + c0 + FFN_CHUNK])
        a_ref[:, c0:c0 + FFN_CHUNK] = (_silu(g) * u).astype(BF16)
    o_ref[...] = x + mod(5) * _dot(a_ref[...], wo_ref[...])


def _tail(att, four, of, ob, gr, x, mod3, mod_row, lw, tm):
    b, n, d = x.shape
    row = (lambda bi: bi) if mod_row is None else (lambda bi: mod_row)
    tok = lambda w: pl.BlockSpec((None, tm, w), lambda bi, i: (bi, i, 0))
    resident = lambda shape: pl.BlockSpec(shape, lambda bi, i: (0, 0), pipeline_mode=pl.Buffered(1))
    return pl.pallas_call(
        _tail_kernel,
        out_shape=jax.ShapeDtypeStruct((b, n, d), F32),
        grid=(b, n // tm),
        in_specs=[tok(ATT_WIDTH), tok(FNET_WIDTH), tok(GLA_WIDTH), tok(GLA_WIDTH), tok(GLA_WIDTH), tok(d),
                  pl.BlockSpec((None, 1, 6 * d), lambda bi, i: (row(bi), 0, 0)),
                  _full((1, GLA_WIDTH)), _full((GLA_WIDTH, GLA_WIDTH)), resident((d, d)),
                  _full((1, d)), resident((d, 2 * FFN_HIDDEN)), resident((FFN_HIDDEN, d))],
        out_specs=tok(d),
        scratch_shapes=[pltpu.VMEM((tm, FFN_HIDDEN), BF16)],
        compiler_params=_params("parallel", "parallel"),
        name="mix_ffn",
    )(att, four, of, ob, gr, x, mod3, lw["gla_g"], lw["bd256"], lw["w_out"], lw["g2"], lw["w_ffn_in"],
      lw["w_ffn_out"])


def _rope_tables(n):
    axis_dim = HEAD_DIM // 2
    inv_freq = ROPE_BASE ** (-np.arange(0, axis_dim, 2, dtype=np.float64) / axis_dim)
    t = np.arange(n)
    ang_r = (t // GRID_W)[:, None] * inv_freq[None, :]
    ang_c = (t % GRID_W)[:, None] * inv_freq[None, :]
    cos = np.concatenate([np.cos(ang_r)] * 2 + [np.cos(ang_c)] * 2, axis=1)
    sin = np.concatenate([-np.sin(ang_r), np.sin(ang_r), -np.sin(ang_c), np.sin(ang_c)], axis=1)
    return jnp.asarray(np.tile(cos, (1, 2)), F32), jnp.asarray(np.tile(sin, (1, 2)), F32)


def _block_diag_mean(width):
    g = np.arange(width) // HEAD_DIM
    return jnp.asarray((g[:, None] == g[None, :]) / HEAD_DIM, BF16)


def _layer_weights(l, w_in, g_norm1, q_norm_g, k_norm_g, w_fourier, wgf, bgf, wgb, bgb, gla_norm_g, w_out,
                   g_norm2, w_ffn_in, w_ffn_out):
    r = GLA_GATE_RANK
    w_l = w_in[l]
    w_z = jnp.zeros((D_MODEL, LANES), F32).at[:, 0:2 * r].set(w_l[:, MAIN_WIDTH:MAIN_WIDTH + 2 * r])
    w_gate = jnp.zeros((LANES, 2 * GLA_WIDTH), F32)
    w_gate = w_gate.at[0:r, 0:GLA_WIDTH].set(wgf[l]).at[r:2 * r, GLA_WIDTH:].set(wgb[l])
    wf_bd = jnp.zeros((FNET_WIDTH, FNET_WIDTH), F32)
    for g in range(FNET_GROUPS):
        wf_bd = wf_bd.at[HEAD_DIM * g:HEAD_DIM * (g + 1), HEAD_DIM * g:HEAD_DIM * (g + 1)].set(w_fourier[l, g])
    return {
        "g1": g_norm1[l][None, :],
        "w_main": w_l[:, 0:MAIN_WIDTH].astype(BF16),
        "w_z": w_z.astype(BF16),
        "w_gate": w_gate.astype(BF16),
        "b_gate": jnp.concatenate([bgf[l], bgb[l]])[None, :],
        "q_g": jnp.tile(q_norm_g[l], 2)[None, :],
        "k_g": jnp.tile(k_norm_g[l], 2)[None, :],
        "bd128": _block_diag_mean(LANES),
        "bd256": _block_diag_mean(GLA_WIDTH),
        "wf_bd": wf_bd.astype(BF16),
        "gla_g": jnp.tile(gla_norm_g[l], GLA_HEADS)[None, :],
        "w_out": w_out[l].astype(BF16),
        "g2": g_norm2[l][None, :],
        "w_ffn_in": w_ffn_in[l].astype(BF16),
        "w_ffn_out": w_ffn_out[l].astype(BF16),
    }


def kernel(x, c, ctx, c_ctx, w_mod, b_mod, g_norm1, w_in, q_norm_g, k_norm_g, attn_sink, w_fourier, gla_w_gate_f,
           gla_b_gate_f, gla_w_gate_b, gla_b_gate_b, gla_norm_g, w_out, g_norm2, w_ffn_in, w_ffn_out):
    b, n, d = x.shape
    lc = ctx.shape[1]
    depth = w_mod.shape[0]
    assert d == D_MODEL and b <= 7 and n % 512 == 0 and n % (DFT_N1 * 8) == 0 and lc % GLA_CHUNK == 0

    cc = jnp.zeros((8, d), F32).at[0:b].set(c).at[b].set(c_ctx)
    mod = _modulation(cc, w_mod, b_mod)
    rope_tabs = _rope_tables(n)
    dft = _dft_consts(n)
    dft_ctx = _dft_ctx_consts(lc)
    tm_lat, tm_ctx, tb_lat = 512, lc, 512
    xc = ctx
    for l in range(depth):
        need_ctx = l < depth - 1
        lw = _layer_weights(l, w_in, g_norm1, q_norm_g, k_norm_g, w_fourier, gla_w_gate_f, gla_b_gate_f,
                            gla_w_gate_b, gla_b_gate_b, gla_norm_g, w_out, g_norm2, w_ffn_in, w_ffn_out)
        mod3 = mod[l].reshape(8, 1, 6 * d)
        sink = attn_sink[l]
        cq, ck4, cv4, cfu, cgq, cgk, cgv, cgr, cla = _inproj(xc, mod3, b, lw, None, tm_ctx)
        q, k4, v4, fu, gq, gk, gv, gr, la = _inproj(x, mod3, None, lw, rope_tabs, tm_lat)
        att = _attention(q, k4, v4, ck4, cv4, sink)
        four = _fourier(fu, lw["wf_bd"], dft)
        s_zero = jnp.zeros((b, 2, GLA_WIDTH, LANES), F32)
        ocf, ocb, s_ctx = _gla(cgq, cgk, cgv, cla, s_zero, lc)
        olf, olb, _ = _gla(gq, gk, gv, la, s_ctx, tb_lat)
        x = _tail(att, four, olf, olb, gr, x, mod3, None, lw, tm_lat)
        if need_ctx:
            att_c = _attention_ctx(cq, ck4, cv4, sink)
            four_c = _fourier_ctx(cfu, lw["wf_bd"], dft_ctx)
            xc = _tail(att_c, four_c, ocf, ocb, cgr, xc, mod3, b, lw, tm_ctx)
    return x
```

```python
import functools

import numpy as np
import jax
import jax.numpy as jnp
from jax import lax
from jax.experimental import pallas as pl
from jax.experimental.pallas import tpu as pltpu

F32 = jnp.float32
BF16 = jnp.bfloat16

D_MODEL = 1024
HEAD_DIM = 64
GRID_W = 64
ROPE_BASE = 10000.0
ATT_HEADS = 8
ATT_KV_HEADS = 2
ATT_WIDTH = ATT_HEADS * HEAD_DIM
KV_WIDTH = ATT_KV_HEADS * HEAD_DIM
ATT_BLOCK = 128
ATT_QBLOCKS = 4
ATT_SCALE = HEAD_DIM ** -0.5
LOG2E = 1.4426950408889634
NEG_INF = -1e30
FNET_GROUPS = 4
FNET_WIDTH = FNET_GROUPS * HEAD_DIM
GLA_HEADS = 4
GLA_WIDTH = GLA_HEADS * HEAD_DIM
GLA_GATE_RANK = 16
GLA_TAU = 16.0
GLA_CHUNK = 64
GLA_SCALE = HEAD_DIM ** -0.5
MAIN_WIDTH = ATT_WIDTH + 2 * KV_WIDTH + FNET_WIDTH + 4 * GLA_WIDTH
FFN_HIDDEN = 2816
FFN_CHUNK = 256
EPS = 1e-6
LANES = 128
DFT_N1 = 128
DFT_STEP = 8
VMEM_LIMIT = 56 * 1024 * 1024

NT_DIMS = (((1,), (1,)), ((), ()))
TN_DIMS = (((0,), (0,)), ((), ()))


def _params(*sem):
    return pltpu.CompilerParams(dimension_semantics=sem, vmem_limit_bytes=VMEM_LIMIT)


def _dot(a, b):
    return jnp.dot(a, b, preferred_element_type=F32)


def _silu(x):
    return x / (1.0 + jnp.exp(-x))


def _layer_block(shape, layer):
    nd = len(shape)
    return pl.BlockSpec((None,) + tuple(shape), lambda *_: (layer,) + (0,) * nd, pipeline_mode=pl.Buffered(1))


def _full(shape):
    nd = len(shape)
    return pl.BlockSpec(shape, lambda *_: (0,) * nd)


def _group_mean(t, bd_ref):
    return _dot((t * t).astype(BF16), bd_ref[...])


def _mod_kernel(c_ref, w_ref, b_ref, o_ref):
    s = _silu(c_ref[...]).astype(BF16)
    o_ref[...] = _dot(s, w_ref[...].astype(BF16)) + b_ref[...]


def _modulation(cc, w_mod, b_mod):
    depth, d, width = w_mod.shape
    tn = 1536
    return pl.pallas_call(
        _mod_kernel,
        out_shape=jax.ShapeDtypeStruct((depth, 8, width), F32),
        grid=(depth, width // tn),
        in_specs=[
            _full((8, d)),
            pl.BlockSpec((None, d, tn), lambda l, j: (l, 0, j)),
            pl.BlockSpec((None, 1, tn), lambda l, j: (l, 0, j)),
        ],
        out_specs=pl.BlockSpec((None, 8, tn), lambda l, j: (l, 0, j)),
        compiler_params=_params("parallel", "parallel"),
        name="modulation",
    )(cc, w_mod, b_mod.reshape(depth, 1, width))


def _inproj_kernel(*refs, rope):
    if rope:
        (x_ref, mod_ref, g1_ref, wm_ref, wz_ref, wg_ref, bg_ref, qg_ref, kg_ref, bd_ref, cos_ref, sin_ref,
         q_ref, k4_ref, v4_ref, fu_ref, gq_ref, gk_ref, gv_ref, gr_ref, la_ref) = refs
    else:
        (x_ref, mod_ref, g1_ref, wm_ref, wz_ref, wg_ref, bg_ref, qg_ref, kg_ref, bd_ref,
         q_ref, k4_ref, v4_ref, fu_ref, gq_ref, gk_ref, gv_ref, gr_ref, la_ref) = refs
    d = D_MODEL
    x = x_ref[...]
    mod = mod_ref[...]
    sh, sc = mod[:, 0:d], mod[:, d:2 * d]
    ms = jnp.mean(x * x, axis=-1, keepdims=True)
    h = (x * lax.rsqrt(ms + EPS) * g1_ref[...]) * (1.0 + sc) + sh
    hb = h.astype(BF16)
    group = lambda g: _dot(hb, wm_ref[:, 4 * LANES * g:4 * LANES * (g + 1)])

    lane = lax.broadcasted_iota(jnp.int32, (1, LANES), 1)
    lo = lane < HEAD_DIM
    second_half = (lane & 16) != 0

    def head_norm(t, g_ref):
        return t * lax.rsqrt(_group_mean(t, bd_ref) + EPS) * g_ref[...]

    def rotary(t):
        if not rope:
            return t
        partner = jnp.where(second_half, pltpu.roll(t, 16, 1), pltpu.roll(t, LANES - 16, 1))
        return t * cos_ref[...] + partner * sin_ref[...]

    def spread(t, out_ref, idle):
        tr = pltpu.roll(t, HEAD_DIM, 1)
        fill = jnp.full_like(t, idle)
        out_ref[:, 0:128] = jnp.where(lo, t, fill).astype(out_ref.dtype)
        out_ref[:, 128:256] = jnp.where(lo, fill, tr).astype(out_ref.dtype)
        out_ref[:, 256:384] = jnp.where(lo, tr, fill).astype(out_ref.dtype)
        out_ref[:, 384:512] = jnp.where(lo, fill, t).astype(out_ref.dtype)

    assert (ATT_WIDTH, 2 * KV_WIDTH + FNET_WIDTH, 2 * GLA_WIDTH) == (4 * LANES,) * 3
    gz = _dot(hb, wz_ref[...])
    a_q = group(0)
    z = _dot(gz.astype(BF16), wg_ref[...]) + bg_ref[...]
    log_sig = jnp.minimum(z, 0.0) - jnp.log(1.0 + jnp.exp(-jnp.abs(z)))
    la_ref[...] = log_sig * (1.0 / GLA_TAU)
    a_kvf = group(1)
    for j in range(ATT_WIDTH // LANES):
        t = rotary(head_norm(a_q[:, LANES * j:LANES * (j + 1)], qg_ref)) * (ATT_SCALE * LOG2E)
        q_ref[:, LANES * j:LANES * (j + 1)] = t.astype(q_ref.dtype)
    a_qk = group(2)
    spread(rotary(head_norm(a_kvf[:, 0:KV_WIDTH], kg_ref)), k4_ref, 0.0)
    spread(a_kvf[:, KV_WIDTH:2 * KV_WIDTH], v4_ref, 1.0)
    fu_ref[...] = a_kvf[:, 2 * KV_WIDTH:].astype(fu_ref.dtype)
    a_vr = group(3)
    gq_ref[...] = a_qk[:, 0:GLA_WIDTH] * GLA_SCALE
    gk_ref[...] = a_qk[:, GLA_WIDTH:]
    gv_ref[...] = a_vr[:, 0:GLA_WIDTH]
    gr_ref[...] = a_vr[:, GLA_WIDTH:]


def _inproj(x, mod3, mod_row, lw, rope_tabs, tm):
    b, n, d = x.shape
    rope = rope_tabs is not None
    row = (lambda bi: bi) if mod_row is None else (lambda bi: mod_row)
    tok = lambda w: pl.BlockSpec((None, tm, w), lambda bi, i: (bi, i, 0))
    in_specs = [
        tok(d),
        pl.BlockSpec((None, 1, 6 * d), lambda bi, i: (row(bi), 0, 0)),
        _full((1, d)), _layer_block((d, MAIN_WIDTH), lw["layer"]), _full((d, LANES)), _full((LANES, 2 * GLA_WIDTH)),
        _full((1, 2 * GLA_WIDTH)), _full((1, LANES)), _full((1, LANES)), _full((LANES, LANES)),
    ]
    args = [x, mod3, lw["g1"], lw["w_main"], lw["w_z"], lw["w_gate"], lw["b_gate"], lw["q_g"], lw["k_g"], lw["bd128"]]
    if rope:
        in_specs += [pl.BlockSpec((tm, LANES), lambda bi, i: (i, 0))] * 2
        args += list(rope_tabs)
    sds = lambda w, dt: jax.ShapeDtypeStruct((b, n, w), dt)
    out_shape = [sds(ATT_WIDTH, BF16), sds(4 * LANES, BF16), sds(4 * LANES, BF16), sds(FNET_WIDTH, F32),
                 sds(GLA_WIDTH, F32), sds(GLA_WIDTH, F32), sds(GLA_WIDTH, F32), sds(GLA_WIDTH, F32),
                 sds(2 * GLA_WIDTH, F32)]
    out_specs = [tok(s.shape[-1]) for s in out_shape]
    return pl.pallas_call(
        functools.partial(_inproj_kernel, rope=rope),
        out_shape=out_shape, grid=(b, n // tm), in_specs=in_specs, out_specs=out_specs,
        compiler_params=_params("parallel", "parallel"),
        name="inproj_rope" if rope else "inproj_ctx",
    )(*args)


def _scores(q2, key_tiles, masks):
    cols = []
    for k, tile_masks in zip(key_tiles, masks):
        s = lax.dot_general(q2, k, NT_DIMS, preferred_element_type=F32)
        for j, mk in enumerate(tile_masks):
            c = s[:, LANES * j:LANES * (j + 1)]
            cols.append(c if mk is None else jnp.where(mk, c, NEG_INF))
    return cols


def _softmax_pv(cols, val_tiles, sink_col):
    mx = cols[0]
    for c in cols[1:]:
        mx = jnp.maximum(mx, c)
    m = jnp.maximum(jnp.max(mx, axis=-1, keepdims=True), sink_col)
    p = jnp.concatenate([jnp.exp2(c - m).astype(BF16) for c in cols], axis=1)
    pv = _dot(p, jnp.concatenate(val_tiles, axis=0))
    den = pltpu.roll(pv, HEAD_DIM, 1) + jnp.exp2(sink_col - m)
    return pv / den


def _attn_kernel(*refs):
    nq = ATT_QBLOCKS
    sink_ref, q_ref = refs[0:2]
    k_refs, kc_ref = refs[2:nq + 4], refs[nq + 4]
    v_refs, vc_ref = refs[nq + 5:2 * nq + 7], refs[2 * nq + 7]
    o_ref = refs[2 * nq + 8]
    i = pl.program_id(1)
    last = pl.num_programs(1) - 1
    blk = ATT_BLOCK
    group = ATT_HEADS // ATT_KV_HEADS
    row = lax.broadcasted_iota(jnp.int32, (2 * blk, blk), 0) & (blk - 1)
    col = lax.broadcasted_iota(jnp.int32, (2 * blk, blk), 1)
    band_l, band_r = col >= row, col <= row
    mask_l = [jnp.logical_and(band_l, i > 0) if s == 0 else band_l for s in range(nq)]
    mask_r = [jnp.logical_and(band_r, i < last) if s == nq - 1 else band_r for s in range(nq)]
    upper_rows = lax.broadcasted_iota(jnp.int32, (2 * blk, 1), 0) >= blk
    lo = lax.broadcasted_iota(jnp.int32, (1, LANES), 1) < HEAD_DIM

    def tiles(blocks, c, s, ks):
        l, m_, r = blocks[s:s + 3]
        return [jnp.concatenate([l[:, ks], m_[:, ks]], axis=0), jnp.concatenate([r[:, ks], c[0:blk, ks]], axis=0),
                c[blk:, ks]]

    slots = [(s, kvh, par) for s in range(nq) for kvh in range(ATT_KV_HEADS) for par in range(2)]
    lane_slice = lambda kvh, par: slice(2 * LANES * kvh + LANES * par, 2 * LANES * kvh + LANES * (par + 1))
    cols = {}
    for s, kvh, par in slots:
        qs, rs = 2 * LANES * kvh, slice(blk * s, blk * (s + 1))
        q2 = jnp.concatenate([q_ref[rs, qs:qs + LANES], q_ref[rs, qs + LANES:qs + 2 * LANES]], axis=0)
        cols[s, kvh, par] = _scores(q2, tiles(k_refs, kc_ref, s, lane_slice(kvh, par)),
                                    [(mask_l[s], None), (mask_r[s], None), (None,)])
    outs = {}
    for s, kvh, par in slots:
        base = group * kvh + par
        sink_col = jnp.where(upper_rows, sink_ref[base + 2], sink_ref[base]) * LOG2E
        outs[s, kvh, par] = _softmax_pv(cols[s, kvh, par], tiles(v_refs, vc_ref, s, lane_slice(kvh, par)), sink_col)
    for s in range(nq):
        for kvh in range(ATT_KV_HEADS):
            qs, r0 = 2 * LANES * kvh, blk * s
            o = jnp.where(lo, outs[s, kvh, 0], outs[s, kvh, 1])
            o_ref[r0:r0 + blk, qs:qs + LANES] = o[0:blk].astype(o_ref.dtype)
            o_ref[r0:r0 + blk, qs + LANES:qs + 2 * LANES] = o[blk:2 * blk].astype(o_ref.dtype)


def _attention(q, k4, v4, ck4, cv4, sink):
    b, n, _ = q.shape
    lc = ck4.shape[1]
    nq = ATT_QBLOCKS
    nb = n // ATT_BLOCK
    assert lc == 2 * ATT_BLOCK and nb % nq == 0
    qspec = pl.BlockSpec((None, nq * ATT_BLOCK, 4 * LANES), lambda bi, i: (bi, i, 0))
    kv = [pl.BlockSpec((None, ATT_BLOCK, 4 * LANES),
                       functools.partial(lambda bi, i, j: (bi, jnp.clip(nq * i + j - 1, 0, nb - 1), 0), j=j))
          for j in range(nq + 2)]
    ctx = pl.BlockSpec((None, lc, 4 * LANES), lambda bi, i: (bi, 0, 0))
    return pl.pallas_call(
        _attn_kernel,
        out_shape=jax.ShapeDtypeStruct((b, n, ATT_WIDTH), BF16),
        grid=(b, nb // nq),
        in_specs=[pl.BlockSpec(memory_space=pltpu.SMEM), qspec] + kv + [ctx] + kv + [ctx],
        out_specs=qspec,
        compiler_params=_params("parallel", "parallel"),
        name="window_attention",
    )(sink, q, *([k4] * (nq + 2)), ck4, *([v4] * (nq + 2)), cv4)


def _attn_ctx_kernel(sink_ref, q_ref, kc_ref, vc_ref, o_ref):
    kvh = pl.program_id(1)
    lc = q_ref.shape[0]
    q2 = jnp.concatenate([q_ref[:, 0:LANES], q_ref[:, LANES:2 * LANES]], axis=0)
    upper_rows = lax.broadcasted_iota(jnp.int32, (2 * lc, 1), 0) >= lc
    lo = lax.broadcasted_iota(jnp.int32, (1, LANES), 1) < HEAD_DIM
    outs = []
    for par in range(2):
        ks = slice(LANES * par, LANES * (par + 1))
        base = ATT_HEADS // ATT_KV_HEADS * kvh + par
        sink_col = jnp.where(upper_rows, sink_ref[base + 2], sink_ref[base]) * LOG2E
        cols = _scores(q2, [kc_ref[:, ks]], [(None,) * (lc // LANES)])
        outs.append(_softmax_pv(cols, [vc_ref[:, ks]], sink_col))
    o = jnp.where(lo, outs[0], outs[1])
    o_ref[:, 0:LANES] = o[0:lc].astype(o_ref.dtype)
    o_ref[:, LANES:2 * LANES] = o[lc:2 * lc].astype(o_ref.dtype)


def _attention_ctx(cq, ck4, cv4, sink):
    b, lc, _ = cq.shape
    gw = 2 * LANES
    spec = pl.BlockSpec((None, lc, gw), lambda bi, h: (bi, 0, h))
    return pl.pallas_call(
        _attn_ctx_kernel,
        out_shape=jax.ShapeDtypeStruct((b, lc, ATT_WIDTH), BF16),
        grid=(b, ATT_KV_HEADS),
        in_specs=[pl.BlockSpec(memory_space=pltpu.SMEM), spec, spec, spec],
        out_specs=spec,
        compiler_params=_params("parallel", "parallel"),
        name="context_attention",
    )(sink, cq, ck4, cv4)


def _dft_consts(n):
    n1, n2 = DFT_N1, n // DFT_N1
    ang = lambda a, m: 2.0 * np.pi * np.outer(np.arange(a), np.arange(a)) / m
    c = HEAD_DIM
    cc, sc = np.cos(ang(c, c)) / np.sqrt(c), np.sin(ang(c, c)) / np.sqrt(c)
    eye = np.eye(FNET_GROUPS)
    w_chan = np.concatenate([np.kron(eye, cc), -np.kron(eye, sc)], axis=1)
    c1, s1 = np.cos(ang(n1, n1)) / np.sqrt(n1), np.sin(ang(n1, n1)) / np.sqrt(n1)
    m1 = np.block([[c1, s1], [-s1, c1]])
    kk = np.arange(n1)[:, None, None] + n1 * np.arange(n2)[None, :, None]
    ph = 2.0 * np.pi * (kk * np.arange(n2)[None, None, :] % n) / n
    m3 = np.concatenate([np.cos(ph), np.sin(ph)], axis=2) / np.sqrt(n2)
    return tuple(jnp.asarray(a, F32) for a in (w_chan, m1, m3))


def _dft_ctx_consts(lc):
    ang = lambda a, m: 2.0 * np.pi * np.outer(np.arange(a), np.arange(a)) / m
    c = HEAD_DIM
    cc, sc = np.cos(ang(c, c)) / np.sqrt(c), np.sin(ang(c, c)) / np.sqrt(c)
    eye = np.eye(FNET_GROUPS)
    w_chan = np.concatenate([np.kron(eye, cc), -np.kron(eye, sc)], axis=1)
    cl, sl = np.cos(ang(lc, lc)) / np.sqrt(lc), np.sin(ang(lc, lc)) / np.sqrt(lc)
    return jnp.asarray(w_chan, F32), jnp.asarray(np.concatenate([cl, sl], axis=1), F32)


def _fourier_kernel(u_ref, wc_ref, m1_ref, m3_ref, wf_ref, y_ref, z_ref, p_ref):
    w, t = FNET_WIDTH, DFT_STEP
    s = pl.program_id(1)
    n_a = p_ref.shape[1]
    n2 = n_a * t

    @pl.when(s < n_a)
    def _():
        u = u_ref[...].reshape(DFT_N1 * t, w).astype(BF16)
        z_ref[...] = _dot(u, wc_ref[...]).reshape(DFT_N1, t, 2 * w)
        for tt in range(t):
            z = z_ref[:, tt, :]
            zs = jnp.concatenate([z[:, 0:w], z[:, w:2 * w]], axis=0).astype(BF16)
            a = _dot(m1_ref[...], zs)
            p_ref[0, s, :, tt, :] = a[0:DFT_N1]
            p_ref[1, s, :, tt, :] = a[DFT_N1:2 * DFT_N1]

    @pl.when(s >= n_a)
    def _():
        k1_0 = (s - n_a) * t
        ys = []
        for kk in range(t):
            parts = [p_ref[part, :, k1_0 + kk].reshape(n2, w) for part in range(2)]
            rhs = jnp.concatenate(parts, axis=0).astype(BF16)
            ys.append(_dot(m3_ref[kk].astype(BF16), rhs).astype(BF16))
        out = _dot(jnp.concatenate(ys, axis=0), wf_ref[...])
        for kk in range(t):
            y_ref[:, kk, :] = out[n2 * kk:n2 * (kk + 1)]


def _fourier(fu, wf_bd, consts):
    b, n, w = fu.shape
    n1, n2, t = DFT_N1, n // DFT_N1, DFT_STEP
    n_a, n_b = n2 // t, n1 // t
    w_chan, m1, m3 = consts
    y = pl.pallas_call(
        _fourier_kernel,
        out_shape=jax.ShapeDtypeStruct((b, n2, n1, w), F32),
        grid=(b, n_a + n_b),
        in_specs=[pl.BlockSpec((None, n1, t, w), lambda bi, s: (bi, 0, jnp.minimum(s, n_a - 1), 0)),
                  _full((w, 2 * w)), _full((2 * n1, 2 * n1)),
                  pl.BlockSpec((t, n2, 2 * n2), lambda bi, s: (jnp.maximum(s - n_a, 0), 0, 0)), _full((w, w))],
        out_specs=pl.BlockSpec((None, n2, t, w), lambda bi, s: (bi, 0, jnp.maximum(s - n_a, 0), 0)),
        scratch_shapes=[pltpu.VMEM((n1, t, 2 * w), F32), pltpu.VMEM((2, n_a, n1, t, w), F32)],
        compiler_params=_params("parallel", "arbitrary"),
        name="fourier_mix",
    )(fu.reshape(b, n1, n2, w), w_chan.astype(BF16), m1.astype(BF16), m3, wf_bd)
    return y.reshape(b, n, w)


def _four_ctx_kernel(u_ref, wc_ref, m_ref, wf_ref, y_ref):
    w = FNET_WIDTH
    z = _dot(u_ref[...].astype(BF16), wc_ref[...])
    zs = jnp.concatenate([z[:, 0:w], z[:, w:2 * w]], axis=0).astype(BF16)
    y = _dot(m_ref[...], zs)
    y_ref[...] = _dot(y.astype(BF16), wf_ref[...]).astype(y_ref.dtype)


def _fourier_ctx(cfu, wf_bd, consts):
    b, lc, w = cfu.shape
    w_chan, m = consts[0].astype(BF16), consts[1].astype(BF16)
    return pl.pallas_call(
        _four_ctx_kernel,
        out_shape=jax.ShapeDtypeStruct((b, lc, w), BF16),
        grid=(b,),
        in_specs=[pl.BlockSpec((None, lc, w), lambda bi: (bi, 0, 0)), _full((w, 2 * w)), _full((lc, 2 * lc)),
                  _full((w, w))],
        out_specs=pl.BlockSpec((None, lc, w), lambda bi: (bi, 0, 0)),
        compiler_params=_params("parallel"),
        name="fourier_context",
    )(cfu, w_chan, m, wf_bd)


def _pair_stack(x, lo):
    zero = jnp.zeros_like(x)
    return jnp.concatenate([jnp.where(lo, x, zero), jnp.where(lo, zero, x)], axis=0)


def _gla_kernel(qf_ref, kf_ref, vf_ref, laf_ref, qb_ref, kb_ref, vb_ref, lab_ref, s0_ref,
                of_ref, ob_ref, sfin_ref, stf_ref, stb_ref):
    i = pl.program_id(1)
    c = GLA_CHUNK
    nchunk = qf_ref.shape[0] // c

    @pl.when(i == 0)
    def _():
        stf_ref[...] = s0_ref[0]
        stb_ref[...] = s0_ref[1]

    r64 = lax.broadcasted_iota(jnp.int32, (c, c), 0)
    c64 = lax.broadcasted_iota(jnp.int32, (c, c), 1)
    tri = ((c64 <= r64).astype(BF16), (c64 >= r64).astype(BF16))
    at = lax.broadcasted_iota(jnp.int32, (c, LANES), 0)
    as_ = lax.broadcasted_iota(jnp.int32, (c, LANES), 1) & (c - 1)
    att_mask = (as_ <= at, as_ >= at)
    lo = lax.broadcasted_iota(jnp.int32, (1, LANES), 1) < HEAD_DIM
    br = lax.broadcasted_iota(jnp.int32, (LANES, LANES), 0) < HEAD_DIM
    bc = lax.broadcasted_iota(jnp.int32, (LANES, LANES), 1) < HEAD_DIM
    bd_mask = br == bc
    pairs = [slice(LANES * p, LANES * (p + 1)) for p in range(GLA_WIDTH // LANES)]
    in_refs = ((qf_ref, kf_ref, vf_ref, laf_ref), (qb_ref, kb_ref, vb_ref, lab_ref))
    out_refs = (of_ref, ob_ref)
    items = [(d, step if d == 0 else nchunk - 1 - step) for step in range(nchunk) for d in range(2)]
    rows = lambda ch: slice(c * ch, c * (ch + 1))

    bcum = {}
    for d, ch in items:
        la = in_refs[d][3][rows(ch)]
        la_hi = la.astype(BF16)
        la_lo = (la - la_hi.astype(F32)).astype(BF16)
        bcum[d, ch] = _dot(tri[d], la_hi) + _dot(tri[d], la_lo)
    work = {}
    for d, ch in items:
        q_ref, k_ref, v_ref, _ = in_refs[d]
        b = bcum[d, ch]
        btot = b[0:1] if d == 1 else b[c - 1:c]
        k = k_ref[rows(ch)]
        q_in = (q_ref[rows(ch)] * jnp.exp(b)).astype(BF16)
        k_in = (k * jnp.exp(-b)).astype(BF16)
        k_out = (k * jnp.exp(btot - b)).astype(BF16)
        vb = v_ref[rows(ch)].astype(BF16)
        att = [lax.dot_general(q_in[:, p], _pair_stack(k_in[:, p], lo), NT_DIMS, preferred_element_type=F32)
               for p in pairs]
        work[d, ch] = (btot, q_in, k_out, vb, att)
    o_intra, ut = {}, {}
    for d, ch in items:
        btot, q_in, k_out, vb, att = work[d, ch]
        o_intra[d, ch] = [_dot(jnp.where(att_mask[d], a, 0.0).astype(BF16), _pair_stack(vb[:, p], lo))
                          for a, p in zip(att, pairs)]
        ut[d, ch] = [lax.dot_general(vb[:, p], k_out[:, p], TN_DIMS, preferred_element_type=F32) for p in pairs]
    st_refs = (stf_ref, stb_ref)
    st = [[st_refs[d][LANES * j:LANES * (j + 1)] for j in range(len(pairs))] for d in range(2)]
    for d, ch in items:
        btot, q_in = work[d, ch][0:2]
        outs = []
        for j, p in enumerate(pairs):
            o_inter = lax.dot_general(q_in[:, p], st[d][j].astype(BF16), NT_DIMS, preferred_element_type=F32)
            outs.append(o_intra[d, ch][j] + o_inter)
            st[d][j] = st[d][j] * jnp.exp(btot[:, p]) + jnp.where(bd_mask, ut[d, ch][j], 0.0)
        out_refs[d][rows(ch)] = jnp.concatenate(outs, axis=1)
    for d in range(2):
        st_refs[d][...] = jnp.concatenate(st[d], axis=0)

    @pl.when(i == pl.num_programs(1) - 1)
    def _():
        for d in range(2):
            sfin_ref[d] = jnp.concatenate(st[d], axis=0)


def _gla(gq, gk, gv, la, s0, tb):
    b, n, w = gq.shape
    nblk = n // tb
    fwd = lambda bi, i: (bi, i, 0)
    bwd = lambda bi, i: (bi, nblk - 1 - i, 0)
    bwd_la = lambda bi, i: (bi, nblk - 1 - i, 1)
    tok = lambda f: pl.BlockSpec((None, tb, w), f)
    state = pl.BlockSpec((None, 2, w, LANES), lambda bi, i: (bi, 0, 0, 0))
    return pl.pallas_call(
        _gla_kernel,
        out_shape=[jax.ShapeDtypeStruct((b, n, w), F32), jax.ShapeDtypeStruct((b, n, w), F32),
                   jax.ShapeDtypeStruct((b, 2, w, LANES), F32)],
        grid=(b, nblk),
        in_specs=[tok(fwd), tok(fwd), tok(fwd), tok(fwd), tok(bwd), tok(bwd), tok(bwd), tok(bwd_la), state],
        out_specs=[tok(fwd), tok(bwd), state],
        scratch_shapes=[pltpu.VMEM((w, LANES), F32), pltpu.VMEM((w, LANES), F32)],
        compiler_params=_params("parallel", "arbitrary"),
        name="gla_scan",
    )(gq, gk, gv, la, gq, gk, gv, la, s0)


def _tail_kernel(att_ref, four_ref, of_ref, ob_ref, r_ref, x_ref, mod_ref, gg_ref, bd_ref, wmix_ref, g2_ref, wi_ref,
                 wo_ref, o_ref, a_ref):
    d = D_MODEL
    mod = lambda j: mod_ref[:, j * d:(j + 1) * d]
    o = of_ref[...] + ob_ref[...]
    y = o * lax.rsqrt(_group_mean(o, bd_ref) + EPS) * gg_ref[...]
    y = y * _silu(r_ref[...])
    a0, a1, a2 = ATT_WIDTH, ATT_WIDTH + FNET_WIDTH, ATT_WIDTH + FNET_WIDTH + GLA_WIDTH
    mix = (_dot(att_ref[...], wmix_ref[0:a0]) + _dot(four_ref[...].astype(BF16), wmix_ref[a0:a1])
           + _dot(y.astype(BF16), wmix_ref[a1:a2]))
    x = x_ref[...] + mod(2) * mix
    ms = jnp.mean(x * x, axis=-1, keepdims=True)
    hb = ((x * lax.rsqrt(ms + EPS) * g2_ref[...]) * (1.0 + mod(4)) + mod(3)).astype(BF16)
    for c0 in range(0, FFN_HIDDEN, FFN_CHUNK):
        g = _dot(hb, wi_ref[:, c0:c0 + FFN_CHUNK])
        u = _dot(hb, wi_ref[:, FFN_HIDDEN + c0:FFN_HIDDEN + c0 + FFN_CHUNK])
        a_ref[:, c0:c0 + FFN_CHUNK] = (_silu(g) * u).astype(BF16)
    o_ref[...] = x + mod(5) * _dot(a_ref[...], wo_ref[...])


def _tail(att, four, of, ob, gr, x, mod3, mod_row, lw, tm):
    b, n, d = x.shape
    row = (lambda bi: bi) if mod_row is None else (lambda bi: mod_row)
    tok = lambda w: pl.BlockSpec((None, tm, w), lambda bi, i: (bi, i, 0))
    resident = lambda shape: _layer_block(shape, lw["layer"])
    return pl.pallas_call(
        _tail_kernel,
        out_shape=jax.ShapeDtypeStruct((b, n, d), F32),
        grid=(b, n // tm),
        in_specs=[tok(ATT_WIDTH), tok(FNET_WIDTH), tok(GLA_WIDTH), tok(GLA_WIDTH), tok(GLA_WIDTH), tok(d),
                  pl.BlockSpec((None, 1, 6 * d), lambda bi, i: (row(bi), 0, 0)),
                  _full((1, GLA_WIDTH)), _full((GLA_WIDTH, GLA_WIDTH)), resident((d, d)),
                  _full((1, d)), resident((d, 2 * FFN_HIDDEN)), resident((FFN_HIDDEN, d))],
        out_specs=tok(d),
        scratch_shapes=[pltpu.VMEM((tm, FFN_HIDDEN), BF16)],
        compiler_params=_params("parallel", "parallel"),
        name="mix_ffn",
    )(att, four, of, ob, gr, x, mod3, lw["gla_g"], lw["bd256"], lw["w_out"], lw["g2"], lw["w_ffn_in"],
      lw["w_ffn_out"])


def _rope_tables(n):
    axis_dim = HEAD_DIM // 2
    inv_freq = ROPE_BASE ** (-np.arange(0, axis_dim, 2, dtype=np.float64) / axis_dim)
    t = np.arange(n)
    ang_r = (t // GRID_W)[:, None] * inv_freq[None, :]
    ang_c = (t % GRID_W)[:, None] * inv_freq[None, :]
    cos = np.concatenate([np.cos(ang_r)] * 2 + [np.cos(ang_c)] * 2, axis=1)
    sin = np.concatenate([-np.sin(ang_r), np.sin(ang_r), -np.sin(ang_c), np.sin(ang_c)], axis=1)
    return jnp.asarray(np.tile(cos, (1, 2)), F32), jnp.asarray(np.tile(sin, (1, 2)), F32)


def _block_diag_mean(width):
    g = np.arange(width) // HEAD_DIM
    return jnp.asarray((g[:, None] == g[None, :]) / HEAD_DIM, BF16)


def _layer_weights(l, big, w_in, g_norm1, q_norm_g, k_norm_g, w_fourier, wgf, bgf, wgb, bgb, gla_norm_g, g_norm2):
    r = GLA_GATE_RANK
    w_z = jnp.zeros((D_MODEL, LANES), F32).at[:, 0:2 * r].set(w_in[l, :, MAIN_WIDTH:MAIN_WIDTH + 2 * r])
    w_gate = jnp.zeros((LANES, 2 * GLA_WIDTH), F32)
    w_gate = w_gate.at[0:r, 0:GLA_WIDTH].set(wgf[l]).at[r:2 * r, GLA_WIDTH:].set(wgb[l])
    wf_bd = jnp.zeros((FNET_WIDTH, FNET_WIDTH), F32)
    for g in range(FNET_GROUPS):
        wf_bd = wf_bd.at[HEAD_DIM * g:HEAD_DIM * (g + 1), HEAD_DIM * g:HEAD_DIM * (g + 1)].set(w_fourier[l, g])
    return {
        "g1": g_norm1[l][None, :],
        "layer": l,
        "w_main": big["w_in"],
        "w_z": w_z.astype(BF16),
        "w_gate": w_gate.astype(BF16),
        "b_gate": jnp.concatenate([bgf[l], bgb[l]])[None, :],
        "q_g": jnp.tile(q_norm_g[l], 2)[None, :],
        "k_g": jnp.tile(k_norm_g[l], 2)[None, :],
        "bd128": _block_diag_mean(LANES),
        "bd256": _block_diag_mean(GLA_WIDTH),
        "wf_bd": wf_bd.astype(BF16),
        "gla_g": jnp.tile(gla_norm_g[l], GLA_HEADS)[None, :],
        "w_out": big["w_out"],
        "g2": g_norm2[l][None, :],
        "w_ffn_in": big["w_ffn_in"],
        "w_ffn_out": big["w_ffn_out"],
    }


def kernel(x, c, ctx, c_ctx, w_mod, b_mod, g_norm1, w_in, q_norm_g, k_norm_g, attn_sink, w_fourier, gla_w_gate_f,
           gla_b_gate_f, gla_w_gate_b, gla_b_gate_b, gla_norm_g, w_out, g_norm2, w_ffn_in, w_ffn_out):
    b, n, d = x.shape
    lc = ctx.shape[1]
    depth = w_mod.shape[0]
    assert d == D_MODEL and b <= 7 and n % 512 == 0 and n % (DFT_N1 * 8) == 0 and lc % GLA_CHUNK == 0

    cc = jnp.zeros((8, d), F32).at[0:b].set(c).at[b].set(c_ctx)
    mod = _modulation(cc, w_mod, b_mod)
    rope_tabs = _rope_tables(n)
    dft = _dft_consts(n)
    dft_ctx = _dft_ctx_consts(lc)
    tm_lat, tm_ctx, tb_lat = 512, lc, 512
    xc = ctx
    big = {"w_in": w_in.astype(BF16), "w_out": w_out.astype(BF16), "w_ffn_in": w_ffn_in.astype(BF16),
           "w_ffn_out": w_ffn_out.astype(BF16)}
    for l in range(depth):
        need_ctx = l < depth - 1
        lw = _layer_weights(l, big, w_in, g_norm1, q_norm_g, k_norm_g, w_fourier, gla_w_gate_f, gla_b_gate_f,
                            gla_w_gate_b, gla_b_gate_b, gla_norm_g, g_norm2)
        mod3 = mod[l].reshape(8, 1, 6 * d)
        sink = attn_sink[l]
        cq, ck4, cv4, cfu, cgq, cgk, cgv, cgr, cla = _inproj(xc, mod3, b, lw, None, tm_ctx)
        q, k4, v4, fu, gq, gk, gv, gr, la = _inproj(x, mod3, None, lw, rope_tabs, tm_lat)
        att = _attention(q, k4, v4, ck4, cv4, sink)
        four = _fourier(fu, lw["wf_bd"], dft)
        s_zero = jnp.zeros((b, 2, GLA_WIDTH, LANES), F32)
        ocf, ocb, s_ctx = _gla(cgq, cgk, cgv, cla, s_zero, lc)
        olf, olb, _ = _gla(gq, gk, gv, la, s_ctx, tb_lat)
        x = _tail(att, four, olf, olb, gr, x, mod3, None, lw, tm_lat)
        if need_ctx:
            att_c = _attention_ctx(cq, ck4, cv4, sink)
            four_c = _fourier_ctx(cfu, lw["wf_bd"], dft_ctx)
            xc = _tail(att_c, four_c, ocf, ocb, cgr, xc, mod3, b, lw, tm_ctx)
    return x
```

```python
import functools

import numpy as np
import jax
import jax.numpy as jnp
from jax import lax
from jax.experimental import pallas as pl
from jax.experimental.pallas import tpu as pltpu

F32 = jnp.float32
BF16 = jnp.bfloat16

D_MODEL = 1024
HEAD_DIM = 64
GRID_W = 64
ROPE_BASE = 10000.0
ATT_HEADS = 8
ATT_KV_HEADS = 2
ATT_WIDTH = ATT_HEADS * HEAD_DIM
KV_WIDTH = ATT_KV_HEADS * HEAD_DIM
ATT_BLOCK = 128
ATT_QBLOCKS = 4
ATT_SCALE = HEAD_DIM ** -0.5
LOG2E = 1.4426950408889634
NEG_INF = -1e30
FNET_GROUPS = 4
FNET_WIDTH = FNET_GROUPS * HEAD_DIM
GLA_HEADS = 4
GLA_WIDTH = GLA_HEADS * HEAD_DIM
GLA_GATE_RANK = 16
GLA_TAU = 16.0
GLA_CHUNK = 64
GLA_SCALE = HEAD_DIM ** -0.5
MAIN_WIDTH = ATT_WIDTH + 2 * KV_WIDTH + FNET_WIDTH + 4 * GLA_WIDTH
FFN_HIDDEN = 2816
FFN_CHUNK = 256
EPS = 1e-6
LANES = 128
DFT_N1 = 128
DFT_STEP = 8
VMEM_LIMIT = 56 * 1024 * 1024

NT_DIMS = (((1,), (1,)), ((), ()))
TN_DIMS = (((0,), (0,)), ((), ()))


def _params(*sem):
    return pltpu.CompilerParams(dimension_semantics=sem, vmem_limit_bytes=VMEM_LIMIT)


def _dot(a, b):
    return jnp.dot(a, b, preferred_element_type=F32)


def _silu(x):
    return x / (1.0 + jnp.exp(-x))


def _layer_block(shape, layer):
    nd = len(shape)
    return pl.BlockSpec((None,) + tuple(shape), lambda *_: (layer,) + (0,) * nd, pipeline_mode=pl.Buffered(1))


def _full(shape):
    nd = len(shape)
    return pl.BlockSpec(shape, lambda *_: (0,) * nd)


def _group_mean(t, bd_ref):
    return _dot((t * t).astype(BF16), bd_ref[...])


def _mod_kernel(c_ref, w_ref, b_ref, o_ref):
    s = _silu(c_ref[...]).astype(BF16)
    o_ref[...] = _dot(s, w_ref[...].astype(BF16)) + b_ref[...]


def _modulation(cc, w_mod, b_mod):
    depth, d, width = w_mod.shape
    tn = 1536
    return pl.pallas_call(
        _mod_kernel,
        out_shape=jax.ShapeDtypeStruct((depth, 8, width), F32),
        grid=(depth, width // tn),
        in_specs=[
            _full((8, d)),
            pl.BlockSpec((None, d, tn), lambda l, j: (l, 0, j)),
            pl.BlockSpec((None, 1, tn), lambda l, j: (l, 0, j)),
        ],
        out_specs=pl.BlockSpec((None, 8, tn), lambda l, j: (l, 0, j)),
        compiler_params=_params("parallel", "parallel"),
        name="modulation",
    )(cc, w_mod, b_mod.reshape(depth, 1, width))


def _inproj_kernel(*refs, rope):
    if rope:
        (x_ref, mod_ref, g1_ref, wm_ref, wz_ref, wg_ref, bg_ref, qg_ref, kg_ref, bd_ref, cos_ref, sin_ref,
         q_ref, k4_ref, v4_ref, fu_ref, gq_ref, gk_ref, gv_ref, gr_ref, la_ref) = refs
    else:
        (x_ref, mod_ref, g1_ref, wm_ref, wz_ref, wg_ref, bg_ref, qg_ref, kg_ref, bd_ref,
         q_ref, k4_ref, v4_ref, fu_ref, gq_ref, gk_ref, gv_ref, gr_ref, la_ref) = refs
    d = D_MODEL
    x = x_ref[...]
    mod = mod_ref[...]
    sh, sc = mod[:, 0:d], mod[:, d:2 * d]
    ms = jnp.mean(x * x, axis=-1, keepdims=True)
    h = (x * lax.rsqrt(ms + EPS) * g1_ref[...]) * (1.0 + sc) + sh
    hb = h.astype(BF16)
    group = lambda g: _dot(hb, wm_ref[:, 4 * LANES * g:4 * LANES * (g + 1)])

    lane = lax.broadcasted_iota(jnp.int32, (1, LANES), 1)
    lo = lane < HEAD_DIM
    second_half = (lane & 16) != 0

    def head_norm(t, g_ref):
        return t * lax.rsqrt(_group_mean(t, bd_ref) + EPS) * g_ref[...]

    def rotary(t):
        if not rope:
            return t
        partner = jnp.where(second_half, pltpu.roll(t, 16, 1), pltpu.roll(t, LANES - 16, 1))
        return t * cos_ref[...] + partner * sin_ref[...]

    def spread(t, out_ref, idle):
        tr = pltpu.roll(t, HEAD_DIM, 1)
        fill = jnp.full_like(t, idle)
        out_ref[:, 0:128] = jnp.where(lo, t, fill).astype(out_ref.dtype)
        out_ref[:, 128:256] = jnp.where(lo, fill, tr).astype(out_ref.dtype)
        out_ref[:, 256:384] = jnp.where(lo, tr, fill).astype(out_ref.dtype)
        out_ref[:, 384:512] = jnp.where(lo, fill, t).astype(out_ref.dtype)

    assert (ATT_WIDTH, 2 * KV_WIDTH + FNET_WIDTH, 2 * GLA_WIDTH) == (4 * LANES,) * 3
    gz = _dot(hb, wz_ref[...])
    a_q = group(0)
    z = _dot(gz.astype(BF16), wg_ref[...]) + bg_ref[...]
    log_sig = jnp.minimum(z, 0.0) - jnp.log(1.0 + jnp.exp(-jnp.abs(z)))
    la_ref[...] = log_sig * (1.0 / GLA_TAU)
    a_kvf = group(1)
    for j in range(ATT_WIDTH // LANES):
        t = rotary(head_norm(a_q[:, LANES * j:LANES * (j + 1)], qg_ref)) * (ATT_SCALE * LOG2E)
        q_ref[:, LANES * j:LANES * (j + 1)] = t.astype(q_ref.dtype)
    a_qk = group(2)
    spread(rotary(head_norm(a_kvf[:, 0:KV_WIDTH], kg_ref)), k4_ref, 0.0)
    spread(a_kvf[:, KV_WIDTH:2 * KV_WIDTH], v4_ref, 1.0)
    fu_ref[...] = a_kvf[:, 2 * KV_WIDTH:].astype(fu_ref.dtype)
    a_vr = group(3)
    gq_ref[...] = a_qk[:, 0:GLA_WIDTH] * GLA_SCALE
    gk_ref[...] = a_qk[:, GLA_WIDTH:]
    gv_ref[...] = a_vr[:, 0:GLA_WIDTH]
    gr_ref[...] = a_vr[:, GLA_WIDTH:]


def _inproj(x, mod3, mod_row, lw, rope_tabs, tm):
    b, n, d = x.shape
    rope = rope_tabs is not None
    row = (lambda bi: bi) if mod_row is None else (lambda bi: mod_row)
    tok = lambda w: pl.BlockSpec((None, tm, w), lambda bi, i: (bi, i, 0))
    in_specs = [
        tok(d),
        pl.BlockSpec((None, 1, 6 * d), lambda bi, i: (row(bi), 0, 0)),
        _full((1, d)), _layer_block((d, MAIN_WIDTH), lw["layer"]), _full((d, LANES)), _full((LANES, 2 * GLA_WIDTH)),
        _full((1, 2 * GLA_WIDTH)), _full((1, LANES)), _full((1, LANES)), _full((LANES, LANES)),
    ]
    args = [x, mod3, lw["g1"], lw["w_main"], lw["w_z"], lw["w_gate"], lw["b_gate"], lw["q_g"], lw["k_g"], lw["bd128"]]
    if rope:
        in_specs += [pl.BlockSpec((tm, LANES), lambda bi, i: (i, 0))] * 2
        args += list(rope_tabs)
    sds = lambda w, dt: jax.ShapeDtypeStruct((b, n, w), dt)
    out_shape = [sds(ATT_WIDTH, BF16), sds(4 * LANES, BF16), sds(4 * LANES, BF16), sds(FNET_WIDTH, F32),
                 sds(GLA_WIDTH, F32), sds(GLA_WIDTH, F32), sds(GLA_WIDTH, F32), sds(GLA_WIDTH, F32),
                 sds(2 * GLA_WIDTH, F32)]
    out_specs = [tok(s.shape[-1]) for s in out_shape]
    return pl.pallas_call(
        functools.partial(_inproj_kernel, rope=rope),
        out_shape=out_shape, grid=(b, n // tm), in_specs=in_specs, out_specs=out_specs,
        compiler_params=_params("parallel", "parallel"),
        name="inproj_rope" if rope else "inproj_ctx",
    )(*args)


def _scores(q2, key_tiles, masks):
    cols = []
    for k, tile_masks in zip(key_tiles, masks):
        s = lax.dot_general(q2, k, NT_DIMS, preferred_element_type=F32)
        for j, mk in enumerate(tile_masks):
            c = s[:, LANES * j:LANES * (j + 1)]
            cols.append(c if mk is None else jnp.where(mk, c, NEG_INF))
    return cols


def _softmax_pv(cols, val_tiles, sink_col):
    mx = cols[0]
    for c in cols[1:]:
        mx = jnp.maximum(mx, c)
    m = jnp.maximum(jnp.max(mx, axis=-1, keepdims=True), sink_col)
    p = jnp.concatenate([jnp.exp2(c - m).astype(BF16) for c in cols], axis=1)
    pv = _dot(p, jnp.concatenate(val_tiles, axis=0))
    den = pltpu.roll(pv, HEAD_DIM, 1) + jnp.exp2(sink_col - m)
    return pv / den


def _attn_kernel(*refs):
    nq = ATT_QBLOCKS
    sink_ref, q_ref = refs[0:2]
    k_refs, kc_ref = refs[2:nq + 4], refs[nq + 4]
    v_refs, vc_ref = refs[nq + 5:2 * nq + 7], refs[2 * nq + 7]
    o_ref = refs[2 * nq + 8]
    i = pl.program_id(1)
    last = pl.num_programs(1) - 1
    blk = ATT_BLOCK
    group = ATT_HEADS // ATT_KV_HEADS
    row = lax.broadcasted_iota(jnp.int32, (2 * blk, blk), 0) & (blk - 1)
    col = lax.broadcasted_iota(jnp.int32, (2 * blk, blk), 1)
    band_l, band_r = col >= row, col <= row
    mask_l = [jnp.logical_and(band_l, i > 0) if s == 0 else band_l for s in range(nq)]
    mask_r = [jnp.logical_and(band_r, i < last) if s == nq - 1 else band_r for s in range(nq)]
    upper_rows = lax.broadcasted_iota(jnp.int32, (2 * blk, 1), 0) >= blk
    lo = lax.broadcasted_iota(jnp.int32, (1, LANES), 1) < HEAD_DIM

    def tiles(blocks, c, s, ks):
        l, m_, r = blocks[s:s + 3]
        return [jnp.concatenate([l[:, ks], m_[:, ks]], axis=0), jnp.concatenate([r[:, ks], c[0:blk, ks]], axis=0),
                c[blk:, ks]]

    slots = [(s, kvh, par) for s in range(nq) for kvh in range(ATT_KV_HEADS) for par in range(2)]
    lane_slice = lambda kvh, par: slice(2 * LANES * kvh + LANES * par, 2 * LANES * kvh + LANES * (par + 1))
    cols = {}
    for s, kvh, par in slots:
        qs, rs = 2 * LANES * kvh, slice(blk * s, blk * (s + 1))
        q2 = jnp.concatenate([q_ref[rs, qs:qs + LANES], q_ref[rs, qs + LANES:qs + 2 * LANES]], axis=0)
        cols[s, kvh, par] = _scores(q2, tiles(k_refs, kc_ref, s, lane_slice(kvh, par)),
                                    [(mask_l[s], None), (mask_r[s], None), (None,)])
    outs = {}
    for s, kvh, par in slots:
        base = group * kvh + par
        sink_col = jnp.where(upper_rows, sink_ref[base + 2], sink_ref[base]) * LOG2E
        outs[s, kvh, par] = _softmax_pv(cols[s, kvh, par], tiles(v_refs, vc_ref, s, lane_slice(kvh, par)), sink_col)
    for s in range(nq):
        for kvh in range(ATT_KV_HEADS):
            qs, r0 = 2 * LANES * kvh, blk * s
            o = jnp.where(lo, outs[s, kvh, 0], outs[s, kvh, 1])
            o_ref[r0:r0 + blk, qs:qs + LANES] = o[0:blk].astype(o_ref.dtype)
            o_ref[r0:r0 + blk, qs + LANES:qs + 2 * LANES] = o[blk:2 * blk].astype(o_ref.dtype)


def _attention(q, k4, v4, ck4, cv4, sink):
    b, n, _ = q.shape
    lc = ck4.shape[1]
    nq = ATT_QBLOCKS
    nb = n // ATT_BLOCK
    assert lc == 2 * ATT_BLOCK and nb % nq == 0
    qspec = pl.BlockSpec((None, nq * ATT_BLOCK, 4 * LANES), lambda bi, i: (bi, i, 0))
    kv = [pl.BlockSpec((None, ATT_BLOCK, 4 * LANES),
                       functools.partial(lambda bi, i, j: (bi, jnp.clip(nq * i + j - 1, 0, nb - 1), 0), j=j))
          for j in range(nq + 2)]
    ctx = pl.BlockSpec((None, lc, 4 * LANES), lambda bi, i: (bi, 0, 0))
    return pl.pallas_call(
        _attn_kernel,
        out_shape=jax.ShapeDtypeStruct((b, n, ATT_WIDTH), BF16),
        grid=(b, nb // nq),
        in_specs=[pl.BlockSpec(memory_space=pltpu.SMEM), qspec] + kv + [ctx] + kv + [ctx],
        out_specs=qspec,
        compiler_params=_params("parallel", "parallel"),
        name="window_attention",
    )(sink, q, *([k4] * (nq + 2)), ck4, *([v4] * (nq + 2)), cv4)


def _attn_ctx_kernel(sink_ref, q_ref, kc_ref, vc_ref, o_ref):
    kvh = pl.program_id(1)
    lc = q_ref.shape[0]
    q2 = jnp.concatenate([q_ref[:, 0:LANES], q_ref[:, LANES:2 * LANES]], axis=0)
    upper_rows = lax.broadcasted_iota(jnp.int32, (2 * lc, 1), 0) >= lc
    lo = lax.broadcasted_iota(jnp.int32, (1, LANES), 1) < HEAD_DIM
    outs = []
    for par in range(2):
        ks = slice(LANES * par, LANES * (par + 1))
        base = ATT_HEADS // ATT_KV_HEADS * kvh + par
        sink_col = jnp.where(upper_rows, sink_ref[base + 2], sink_ref[base]) * LOG2E
        cols = _scores(q2, [kc_ref[:, ks]], [(None,) * (lc // LANES)])
        outs.append(_softmax_pv(cols, [vc_ref[:, ks]], sink_col))
    o = jnp.where(lo, outs[0], outs[1])
    o_ref[:, 0:LANES] = o[0:lc].astype(o_ref.dtype)
    o_ref[:, LANES:2 * LANES] = o[lc:2 * lc].astype(o_ref.dtype)


def _attention_ctx(cq, ck4, cv4, sink):
    b, lc, _ = cq.shape
    gw = 2 * LANES
    spec = pl.BlockSpec((None, lc, gw), lambda bi, h: (bi, 0, h))
    return pl.pallas_call(
        _attn_ctx_kernel,
        out_shape=jax.ShapeDtypeStruct((b, lc, ATT_WIDTH), BF16),
        grid=(b, ATT_KV_HEADS),
        in_specs=[pl.BlockSpec(memory_space=pltpu.SMEM), spec, spec, spec],
        out_specs=spec,
        compiler_params=_params("parallel", "parallel"),
        name="context_attention",
    )(sink, cq, ck4, cv4)


def _dft_consts(n):
    n1, n2 = DFT_N1, n // DFT_N1
    ang = lambda a, m: 2.0 * np.pi * np.outer(np.arange(a), np.arange(a)) / m
    c = HEAD_DIM
    cc, sc = np.cos(ang(c, c)) / np.sqrt(c), np.sin(ang(c, c)) / np.sqrt(c)
    eye = np.eye(FNET_GROUPS)
    w_chan = np.concatenate([np.kron(eye, cc), -np.kron(eye, sc)], axis=1)
    c1, s1 = np.cos(ang(n1, n1)) / np.sqrt(n1), np.sin(ang(n1, n1)) / np.sqrt(n1)
    m1 = np.block([[c1, s1], [-s1, c1]])
    kk = np.arange(n1)[:, None, None] + n1 * np.arange(n2)[None, :, None]
    ph = 2.0 * np.pi * (kk * np.arange(n2)[None, None, :] % n) / n
    m3 = np.concatenate([np.cos(ph), np.sin(ph)], axis=2) / np.sqrt(n2)
    return tuple(jnp.asarray(a, F32) for a in (w_chan, m1, m3))


def _dft_ctx_consts(lc):
    ang = lambda a, m: 2.0 * np.pi * np.outer(np.arange(a), np.arange(a)) / m
    c = HEAD_DIM
    cc, sc = np.cos(ang(c, c)) / np.sqrt(c), np.sin(ang(c, c)) / np.sqrt(c)
    eye = np.eye(FNET_GROUPS)
    w_chan = np.concatenate([np.kron(eye, cc), -np.kron(eye, sc)], axis=1)
    cl, sl = np.cos(ang(lc, lc)) / np.sqrt(lc), np.sin(ang(lc, lc)) / np.sqrt(lc)
    return jnp.asarray(w_chan, F32), jnp.asarray(np.concatenate([cl, sl], axis=1), F32)


def _fourier_kernel(u_ref, wc_ref, m1_ref, m3_ref, wf_ref, y_ref, z_ref, p_ref):
    w, t = FNET_WIDTH, DFT_STEP
    s = pl.program_id(1)
    n_a = p_ref.shape[1]
    n2 = n_a * t

    @pl.when(s < n_a)
    def _():
        u = u_ref[...].reshape(DFT_N1 * t, w).astype(BF16)
        z = pltpu.einshape("(at)c->(ta)c", _dot(u, wc_ref[...]), t=t)
        a_re, a_im = [], []
        for tt in range(t):
            zt = z[DFT_N1 * tt:DFT_N1 * (tt + 1)]
            zs = jnp.concatenate([zt[:, 0:w], zt[:, w:2 * w]], axis=0).astype(BF16)
            a = _dot(m1_ref[...], zs)
            a_re.append(a[0:DFT_N1])
            a_im.append(a[DFT_N1:2 * DFT_N1])
        for part, rows in enumerate((a_re, a_im)):
            a = pltpu.einshape("(tk)c->(kt)c", jnp.concatenate(rows, axis=0), t=t)
            p_ref[part, s] = a.reshape(DFT_N1, t, w)

    @pl.when(s >= n_a)
    def _():
        k1_0 = (s - n_a) * t
        ys = []
        for kk in range(t):
            parts = [p_ref[part, :, k1_0 + kk].reshape(n2, w) for part in range(2)]
            rhs = jnp.concatenate(parts, axis=0).astype(BF16)
            ys.append(_dot(m3_ref[kk].astype(BF16), rhs).astype(BF16))
        out = _dot(jnp.concatenate(ys, axis=0), wf_ref[...])
        y_ref[...] = pltpu.einshape("(kq)c->(qk)c", out, k=t).reshape(n2, t, w)


def _fourier(fu, wf_bd, consts):
    b, n, w = fu.shape
    n1, n2, t = DFT_N1, n // DFT_N1, DFT_STEP
    n_a, n_b = n2 // t, n1 // t
    w_chan, m1, m3 = consts
    y = pl.pallas_call(
        _fourier_kernel,
        out_shape=jax.ShapeDtypeStruct((b, n2, n1, w), F32),
        grid=(b, n_a + n_b),
        in_specs=[pl.BlockSpec((None, n1, t, w), lambda bi, s: (bi, 0, jnp.minimum(s, n_a - 1), 0)),
                  _full((w, 2 * w)), _full((2 * n1, 2 * n1)),
                  pl.BlockSpec((t, n2, 2 * n2), lambda bi, s: (jnp.maximum(s - n_a, 0), 0, 0)), _full((w, w))],
        out_specs=pl.BlockSpec((None, n2, t, w), lambda bi, s: (bi, 0, jnp.maximum(s - n_a, 0), 0)),
        scratch_shapes=[pltpu.VMEM((n1, t, 2 * w), F32), pltpu.VMEM((2, n_a, n1, t, w), F32)],
        compiler_params=_params("parallel", "arbitrary"),
        name="fourier_mix",
    )(fu.reshape(b, n1, n2, w), w_chan.astype(BF16), m1.astype(BF16), m3, wf_bd)
    return y.reshape(b, n, w)


def _four_ctx_kernel(u_ref, wc_ref, m_ref, wf_ref, y_ref):
    w = FNET_WIDTH
    z = _dot(u_ref[...].astype(BF16), wc_ref[...])
    zs = jnp.concatenate([z[:, 0:w], z[:, w:2 * w]], axis=0).astype(BF16)
    y = _dot(m_ref[...], zs)
    y_ref[...] = _dot(y.astype(BF16), wf_ref[...]).astype(y_ref.dtype)


def _fourier_ctx(cfu, wf_bd, consts):
    b, lc, w = cfu.shape
    w_chan, m = consts[0].astype(BF16), consts[1].astype(BF16)
    return pl.pallas_call(
        _four_ctx_kernel,
        out_shape=jax.ShapeDtypeStruct((b, lc, w), BF16),
        grid=(b,),
        in_specs=[pl.BlockSpec((None, lc, w), lambda bi: (bi, 0, 0)), _full((w, 2 * w)), _full((lc, 2 * lc)),
                  _full((w, w))],
        out_specs=pl.BlockSpec((None, lc, w), lambda bi: (bi, 0, 0)),
        compiler_params=_params("parallel"),
        name="fourier_context",
    )(cfu, w_chan, m, wf_bd)


def _pair_stack(x, lo):
    zero = jnp.zeros_like(x)
    return jnp.concatenate([jnp.where(lo, x, zero), jnp.where(lo, zero, x)], axis=0)


def _gla_kernel(qf_ref, kf_ref, vf_ref, laf_ref, qb_ref, kb_ref, vb_ref, lab_ref, s0_ref,
                of_ref, ob_ref, sfin_ref, stf_ref, stb_ref):
    i = pl.program_id(1)
    c = GLA_CHUNK
    nchunk = qf_ref.shape[0] // c

    @pl.when(i == 0)
    def _():
        stf_ref[...] = s0_ref[0]
        stb_ref[...] = s0_ref[1]

    r64 = lax.broadcasted_iota(jnp.int32, (c, c), 0)
    c64 = lax.broadcasted_iota(jnp.int32, (c, c), 1)
    tri = ((c64 <= r64).astype(BF16), (c64 >= r64).astype(BF16))
    at = lax.broadcasted_iota(jnp.int32, (c, LANES), 0)
    as_ = lax.broadcasted_iota(jnp.int32, (c, LANES), 1) & (c - 1)
    att_mask = (as_ <= at, as_ >= at)
    lo = lax.broadcasted_iota(jnp.int32, (1, LANES), 1) < HEAD_DIM
    br = lax.broadcasted_iota(jnp.int32, (LANES, LANES), 0) < HEAD_DIM
    bc = lax.broadcasted_iota(jnp.int32, (LANES, LANES), 1) < HEAD_DIM
    bd_mask = br == bc
    pairs = [slice(LANES * p, LANES * (p + 1)) for p in range(GLA_WIDTH // LANES)]
    in_refs = ((qf_ref, kf_ref, vf_ref, laf_ref), (qb_ref, kb_ref, vb_ref, lab_ref))
    out_refs = (of_ref, ob_ref)
    items = [(d, step if d == 0 else nchunk - 1 - step) for step in range(nchunk) for d in range(2)]
    rows = lambda ch: slice(c * ch, c * (ch + 1))

    bcum = {}
    for d, ch in items:
        la = in_refs[d][3][rows(ch)]
        la_hi = la.astype(BF16)
        la_lo = (la - la_hi.astype(F32)).astype(BF16)
        bcum[d, ch] = _dot(tri[d], la_hi) + _dot(tri[d], la_lo)
    work = {}
    for d, ch in items:
        q_ref, k_ref, v_ref, _ = in_refs[d]
        b = bcum[d, ch]
        btot = b[0:1] if d == 1 else b[c - 1:c]
        k = k_ref[rows(ch)]
        q_in = (q_ref[rows(ch)] * jnp.exp(b)).astype(BF16)
        k_in = (k * jnp.exp(-b)).astype(BF16)
        k_out = (k * jnp.exp(btot - b)).astype(BF16)
        vb = v_ref[rows(ch)].astype(BF16)
        att = [lax.dot_general(q_in[:, p], _pair_stack(k_in[:, p], lo), NT_DIMS, preferred_element_type=F32)
               for p in pairs]
        work[d, ch] = (btot, q_in, k_out, vb, att)
    o_intra, ut = {}, {}
    for d, ch in items:
        btot, q_in, k_out, vb, att = work[d, ch]
        o_intra[d, ch] = [_dot(jnp.where(att_mask[d], a, 0.0).astype(BF16), _pair_stack(vb[:, p], lo))
                          for a, p in zip(att, pairs)]
        ut[d, ch] = [lax.dot_general(vb[:, p], k_out[:, p], TN_DIMS, preferred_element_type=F32) for p in pairs]
    st_refs = (stf_ref, stb_ref)
    st = [[st_refs[d][LANES * j:LANES * (j + 1)] for j in range(len(pairs))] for d in range(2)]
    for d, ch in items:
        btot, q_in = work[d, ch][0:2]
        outs = []
        for j, p in enumerate(pairs):
            o_inter = lax.dot_general(q_in[:, p], st[d][j].astype(BF16), NT_DIMS, preferred_element_type=F32)
            outs.append(o_intra[d, ch][j] + o_inter)
            st[d][j] = st[d][j] * jnp.exp(btot[:, p]) + jnp.where(bd_mask, ut[d, ch][j], 0.0)
        out_refs[d][rows(ch)] = jnp.concatenate(outs, axis=1)
    for d in range(2):
        st_refs[d][...] = jnp.concatenate(st[d], axis=0)

    @pl.when(i == pl.num_programs(1) - 1)
    def _():
        for d in range(2):
            sfin_ref[d] = jnp.concatenate(st[d], axis=0)


def _gla(gq, gk, gv, la, s0, tb):
    b, n, w = gq.shape
    nblk = n // tb
    fwd = lambda bi, i: (bi, i, 0)
    bwd = lambda bi, i: (bi, nblk - 1 - i, 0)
    bwd_la = lambda bi, i: (bi, nblk - 1 - i, 1)
    tok = lambda f: pl.BlockSpec((None, tb, w), f)
    state = pl.BlockSpec((None, 2, w, LANES), lambda bi, i: (bi, 0, 0, 0))
    return pl.pallas_call(
        _gla_kernel,
        out_shape=[jax.ShapeDtypeStruct((b, n, w), F32), jax.ShapeDtypeStruct((b, n, w), F32),
                   jax.ShapeDtypeStruct((b, 2, w, LANES), F32)],
        grid=(b, nblk),
        in_specs=[tok(fwd), tok(fwd), tok(fwd), tok(fwd), tok(bwd), tok(bwd), tok(bwd), tok(bwd_la), state],
        out_specs=[tok(fwd), tok(bwd), state],
        scratch_shapes=[pltpu.VMEM((w, LANES), F32), pltpu.VMEM((w, LANES), F32)],
        compiler_params=_params("parallel", "arbitrary"),
        name="gla_scan",
    )(gq, gk, gv, la, gq, gk, gv, la, s0)


def _tail_kernel(att_ref, four_ref, of_ref, ob_ref, r_ref, x_ref, mod_ref, gg_ref, bd_ref, wmix_ref, g2_ref, wi_ref,
                 wo_ref, o_ref, a_ref):
    d = D_MODEL
    mod = lambda j: mod_ref[:, j * d:(j + 1) * d]
    o = of_ref[...] + ob_ref[...]
    y = o * lax.rsqrt(_group_mean(o, bd_ref) + EPS) * gg_ref[...]
    y = y * _silu(r_ref[...])
    a0, a1, a2 = ATT_WIDTH, ATT_WIDTH + FNET_WIDTH, ATT_WIDTH + FNET_WIDTH + GLA_WIDTH
    mix = (_dot(att_ref[...], wmix_ref[0:a0]) + _dot(four_ref[...].astype(BF16), wmix_ref[a0:a1])
           + _dot(y.astype(BF16), wmix_ref[a1:a2]))
    x = x_ref[...] + mod(2) * mix
    ms = jnp.mean(x * x, axis=-1, keepdims=True)
    hb = ((x * lax.rsqrt(ms + EPS) * g2_ref[...]) * (1.0 + mod(4)) + mod(3)).astype(BF16)
    for c0 in range(0, FFN_HIDDEN, FFN_CHUNK):
        g = _dot(hb, wi_ref[:, c0:c0 + FFN_CHUNK])
        u = _dot(hb, wi_ref[:, FFN_HIDDEN + c0:FFN_HIDDEN + c0 + FFN_CHUNK])
        a_ref[:, c0:c0 + FFN_CHUNK] = (_silu(g) * u).astype(BF16)
    o_ref[...] = x + mod(5) * _dot(a_ref[...], wo_ref[...])


def _tail(att, four, of, ob, gr, x, mod3, mod_row, lw, tm):
    b, n, d = x.shape
    row = (lambda bi: bi) if mod_row is None else (lambda bi: mod_row)
    tok = lambda w: pl.BlockSpec((None, tm, w), lambda bi, i: (bi, i, 0))
    resident = lambda shape: _layer_block(shape, lw["layer"])
    return pl.pallas_call(
        _tail_kernel,
        out_shape=jax.ShapeDtypeStruct((b, n, d), F32),
        grid=(b, n // tm),
        in_specs=[tok(ATT_WIDTH), tok(FNET_WIDTH), tok(GLA_WIDTH), tok(GLA_WIDTH), tok(GLA_WIDTH), tok(d),
                  pl.BlockSpec((None, 1, 6 * d), lambda bi, i: (row(bi), 0, 0)),
                  _full((1, GLA_WIDTH)), _full((GLA_WIDTH, GLA_WIDTH)), resident((d, d)),
                  _full((1, d)), resident((d, 2 * FFN_HIDDEN)), resident((FFN_HIDDEN, d))],
        out_specs=tok(d),
        scratch_shapes=[pltpu.VMEM((tm, FFN_HIDDEN), BF16)],
        compiler_params=_params("parallel", "parallel"),
        name="mix_ffn",
    )(att, four, of, ob, gr, x, mod3, lw["gla_g"], lw["bd256"], lw["w_out"], lw["g2"], lw["w_ffn_in"],
      lw["w_ffn_out"])


def _rope_tables(n):
    axis_dim = HEAD_DIM // 2
    inv_freq = ROPE_BASE ** (-np.arange(0, axis_dim, 2, dtype=np.float64) / axis_dim)
    t = np.arange(n)
    ang_r = (t // GRID_W)[:, None] * inv_freq[None, :]
    ang_c = (t % GRID_W)[:, None] * inv_freq[None, :]
    cos = np.concatenate([np.cos(ang_r)] * 2 + [np.cos(ang_c)] * 2, axis=1)
    sin = np.concatenate([-np.sin(ang_r), np.sin(ang_r), -np.sin(ang_c), np.sin(ang_c)], axis=1)
    return jnp.asarray(np.tile(cos, (1, 2)), F32), jnp.asarray(np.tile(sin, (1, 2)), F32)


def _block_diag_mean(width):
    g = np.arange(width) // HEAD_DIM
    return jnp.asarray((g[:, None] == g[None, :]) / HEAD_DIM, BF16)


def _layer_weights(l, big, w_in, g_norm1, q_norm_g, k_norm_g, w_fourier, wgf, bgf, wgb, bgb, gla_norm_g, g_norm2):
    r = GLA_GATE_RANK
    w_z = jnp.zeros((D_MODEL, LANES), F32).at[:, 0:2 * r].set(w_in[l, :, MAIN_WIDTH:MAIN_WIDTH + 2 * r])
    w_gate = jnp.zeros((LANES, 2 * GLA_WIDTH), F32)
    w_gate = w_gate.at[0:r, 0:GLA_WIDTH].set(wgf[l]).at[r:2 * r, GLA_WIDTH:].set(wgb[l])
    wf_bd = jnp.zeros((FNET_WIDTH, FNET_WIDTH), F32)
    for g in range(FNET_GROUPS):
        wf_bd = wf_bd.at[HEAD_DIM * g:HEAD_DIM * (g + 1), HEAD_DIM * g:HEAD_DIM * (g + 1)].set(w_fourier[l, g])
    return {
        "g1": g_norm1[l][None, :],
        "layer": l,
        "w_main": big["w_in"],
        "w_z": w_z.astype(BF16),
        "w_gate": w_gate.astype(BF16),
        "b_gate": jnp.concatenate([bgf[l], bgb[l]])[None, :],
        "q_g": jnp.tile(q_norm_g[l], 2)[None, :],
        "k_g": jnp.tile(k_norm_g[l], 2)[None, :],
        "bd128": _block_diag_mean(LANES),
        "bd256": _block_diag_mean(GLA_WIDTH),
        "wf_bd": wf_bd.astype(BF16),
        "gla_g": jnp.tile(gla_norm_g[l], GLA_HEADS)[None, :],
        "w_out": big["w_out"],
        "g2": g_norm2[l][None, :],
        "w_ffn_in": big["w_ffn_in"],
        "w_ffn_out": big["w_ffn_out"],
    }


def kernel(x, c, ctx, c_ctx, w_mod, b_mod, g_norm1, w_in, q_norm_g, k_norm_g, attn_sink, w_fourier, gla_w_gate_f,
           gla_b_gate_f, gla_w_gate_b, gla_b_gate_b, gla_norm_g, w_out, g_norm2, w_ffn_in, w_ffn_out):
    b, n, d = x.shape
    lc = ctx.shape[1]
    depth = w_mod.shape[0]
    assert d == D_MODEL and b <= 7 and n % 512 == 0 and n % (DFT_N1 * 8) == 0 and lc % GLA_CHUNK == 0

    cc = jnp.zeros((8, d), F32).at[0:b].set(c).at[b].set(c_ctx)
    mod = _modulation(cc, w_mod, b_mod)
    rope_tabs = _rope_tables(n)
    dft = _dft_consts(n)
    dft_ctx = _dft_ctx_consts(lc)
    tm_lat, tm_ctx, tb_lat = 512, lc, 512
    xc = ctx
    big = {"w_in": w_in.astype(BF16), "w_out": w_out.astype(BF16), "w_ffn_in": w_ffn_in.astype(BF16),
           "w_ffn_out": w_ffn_out.astype(BF16)}
    for l in range(depth):
        need_ctx = l < depth - 1
        lw = _layer_weights(l, big, w_in, g_norm1, q_norm_g, k_norm_g, w_fourier, gla_w_gate_f, gla_b_gate_f,
                            gla_w_gate_b, gla_b_gate_b, gla_norm_g, g_norm2)
        mod3 = mod[l].reshape(8, 1, 6 * d)
        sink = attn_sink[l]
        cq, ck4, cv4, cfu, cgq, cgk, cgv, cgr, cla = _inproj(xc, mod3, b, lw, None, tm_ctx)
        q, k4, v4, fu, gq, gk, gv, gr, la = _inproj(x, mod3, None, lw, rope_tabs, tm_lat)
        att = _attention(q, k4, v4, ck4, cv4, sink)
        four = _fourier(fu, lw["wf_bd"], dft)
        s_zero = jnp.zeros((b, 2, GLA_WIDTH, LANES), F32)
        ocf, ocb, s_ctx = _gla(cgq, cgk, cgv, cla, s_zero, lc)
        olf, olb, _ = _gla(gq, gk, gv, la, s_ctx, tb_lat)
        x = _tail(att, four, olf, olb, gr, x, mod3, None, lw, tm_lat)
        if need_ctx:
            att_c = _attention_ctx(cq, ck4, cv4, sink)
            four_c = _fourier_ctx(cfu, lw["wf_bd"], dft_ctx)
            xc = _tail(att_c, four_c, ocf, ocb, cgr, xc, mod3, b, lw, tm_ctx)
    return x
```

```python
import functools

import numpy as np
import jax
import jax.numpy as jnp
from jax import lax
from jax.experimental import pallas as pl
from jax.experimental.pallas import tpu as pltpu

F32 = jnp.float32
BF16 = jnp.bfloat16

D_MODEL = 1024
HEAD_DIM = 64
GRID_W = 64
ROPE_BASE = 10000.0
ATT_HEADS = 8
ATT_KV_HEADS = 2
ATT_WIDTH = ATT_HEADS * HEAD_DIM
KV_WIDTH = ATT_KV_HEADS * HEAD_DIM
ATT_BLOCK = 128
ATT_QBLOCKS = 4
ATT_SCALE = HEAD_DIM ** -0.5
LOG2E = 1.4426950408889634
NEG_INF = -1e30
FNET_GROUPS = 4
FNET_WIDTH = FNET_GROUPS * HEAD_DIM
GLA_HEADS = 4
GLA_WIDTH = GLA_HEADS * HEAD_DIM
GLA_GATE_RANK = 16
GLA_TAU = 16.0
GLA_CHUNK = 64
GLA_SCALE = HEAD_DIM ** -0.5
MAIN_WIDTH = ATT_WIDTH + 2 * KV_WIDTH + FNET_WIDTH + 4 * GLA_WIDTH
FFN_HIDDEN = 2816
FFN_CHUNK = 256
EPS = 1e-6
LANES = 128
DFT_N1 = 128
DFT_STEP = 16
VMEM_LIMIT = 56 * 1024 * 1024

NT_DIMS = (((1,), (1,)), ((), ()))
TN_DIMS = (((0,), (0,)), ((), ()))


def _params(*sem):
    return pltpu.CompilerParams(dimension_semantics=sem, vmem_limit_bytes=VMEM_LIMIT)


def _dot(a, b):
    return jnp.dot(a, b, preferred_element_type=F32)


def _silu(x):
    return x / (1.0 + jnp.exp(-x))


def _layer_block(shape, layer):
    nd = len(shape)
    return pl.BlockSpec((None,) + tuple(shape), lambda *_: (layer,) + (0,) * nd, pipeline_mode=pl.Buffered(1))


def _full(shape):
    nd = len(shape)
    return pl.BlockSpec(shape, lambda *_: (0,) * nd)


def _group_mean(t, bd_ref):
    return _dot((t * t).astype(BF16), bd_ref[...])


def _mod_kernel(c_ref, w_ref, b_ref, o_ref):
    s = _silu(c_ref[...]).astype(BF16)
    o_ref[...] = _dot(s, w_ref[...].astype(BF16)) + b_ref[...]


def _modulation(cc, w_mod, b_mod):
    depth, d, width = w_mod.shape
    tn = 1536
    return pl.pallas_call(
        _mod_kernel,
        out_shape=jax.ShapeDtypeStruct((depth, 8, width), F32),
        grid=(depth, width // tn),
        in_specs=[
            _full((8, d)),
            pl.BlockSpec((None, d, tn), lambda l, j: (l, 0, j)),
            pl.BlockSpec((None, 1, tn), lambda l, j: (l, 0, j)),
        ],
        out_specs=pl.BlockSpec((None, 8, tn), lambda l, j: (l, 0, j)),
        compiler_params=_params("parallel", "parallel"),
        name="modulation",
    )(cc, w_mod, b_mod.reshape(depth, 1, width))


def _inproj_kernel(*refs, rope):
    if rope:
        (x_ref, mod_ref, g1_ref, wm_ref, wz_ref, wg_ref, bg_ref, qg_ref, kg_ref, bd_ref, cos_ref, sin_ref,
         q_ref, k4_ref, v4_ref, fu_ref, gq_ref, gk_ref, gv_ref, gr_ref, la_ref) = refs
    else:
        (x_ref, mod_ref, g1_ref, wm_ref, wz_ref, wg_ref, bg_ref, qg_ref, kg_ref, bd_ref,
         q_ref, k4_ref, v4_ref, fu_ref, gq_ref, gk_ref, gv_ref, gr_ref, la_ref) = refs
    d = D_MODEL
    x = x_ref[...]
    mod = mod_ref[...]
    sh, sc = mod[:, 0:d], mod[:, d:2 * d]
    ms = jnp.mean(x * x, axis=-1, keepdims=True)
    h = (x * lax.rsqrt(ms + EPS) * g1_ref[...]) * (1.0 + sc) + sh
    hb = h.astype(BF16)
    group = lambda g: _dot(hb, wm_ref[:, 4 * LANES * g:4 * LANES * (g + 1)])

    lane = lax.broadcasted_iota(jnp.int32, (1, LANES), 1)
    lo = lane < HEAD_DIM
    second_half = (lane & 16) != 0

    def head_norm(t, g_ref):
        wdt = t.shape[1]
        ms = _dot((t * t).astype(BF16), bd_ref[0:wdt, 0:wdt])
        g = g_ref[...] if wdt == LANES else jnp.concatenate([g_ref[...]] * (wdt // LANES), axis=1)
        return t * lax.rsqrt(ms + EPS) * g

    def rotary(t):
        if not rope:
            return t
        partner = jnp.where(second_half, pltpu.roll(t, 16, 1), pltpu.roll(t, LANES - 16, 1))
        return t * cos_ref[...] + partner * sin_ref[...]

    def spread(t, out_ref, idle):
        tr = pltpu.roll(t, HEAD_DIM, 1)
        fill = jnp.full_like(t, idle)
        out_ref[:, 0:128] = jnp.where(lo, t, fill).astype(out_ref.dtype)
        out_ref[:, 128:256] = jnp.where(lo, fill, tr).astype(out_ref.dtype)
        out_ref[:, 256:384] = jnp.where(lo, tr, fill).astype(out_ref.dtype)
        out_ref[:, 384:512] = jnp.where(lo, fill, t).astype(out_ref.dtype)

    assert (ATT_WIDTH, 2 * KV_WIDTH + FNET_WIDTH, 2 * GLA_WIDTH) == (4 * LANES,) * 3
    gz = _dot(hb, wz_ref[...])
    a_q = group(0)
    z = _dot(gz.astype(BF16), wg_ref[...]) + bg_ref[...]
    log_sig = jnp.minimum(z, 0.0) - jnp.log(1.0 + jnp.exp(-jnp.abs(z)))
    la_ref[...] = log_sig * (1.0 / GLA_TAU)
    a_kvf = group(1)
    for j2 in range(ATT_WIDTH // (2 * LANES)):
        qn = head_norm(a_q[:, 2 * LANES * j2:2 * LANES * (j2 + 1)], qg_ref)
        for j in range(2):
            t = rotary(qn[:, LANES * j:LANES * (j + 1)]) * (ATT_SCALE * LOG2E)
            q_ref[:, LANES * (2 * j2 + j):LANES * (2 * j2 + j + 1)] = t.astype(q_ref.dtype)
    a_qk = group(2)
    spread(rotary(head_norm(a_kvf[:, 0:KV_WIDTH], kg_ref)), k4_ref, 0.0)
    spread(a_kvf[:, KV_WIDTH:2 * KV_WIDTH], v4_ref, 1.0)
    fu_ref[...] = a_kvf[:, 2 * KV_WIDTH:].astype(fu_ref.dtype)
    a_vr = group(3)
    gq_ref[...] = a_qk[:, 0:GLA_WIDTH] * GLA_SCALE
    gk_ref[...] = a_qk[:, GLA_WIDTH:]
    gv_ref[...] = a_vr[:, 0:GLA_WIDTH]
    gr_ref[...] = a_vr[:, GLA_WIDTH:]


def _inproj(x, mod3, mod_row, lw, rope_tabs, tm):
    b, n, d = x.shape
    rope = rope_tabs is not None
    row = (lambda bi: bi) if mod_row is None else (lambda bi: mod_row)
    tok = lambda w: pl.BlockSpec((None, tm, w), lambda bi, i: (bi, i, 0))
    in_specs = [
        tok(d),
        pl.BlockSpec((None, 1, 6 * d), lambda bi, i: (row(bi), 0, 0)),
        _full((1, d)), _layer_block((d, MAIN_WIDTH), lw["layer"]), _full((d, LANES)), _full((LANES, 2 * GLA_WIDTH)),
        _full((1, 2 * GLA_WIDTH)), _full((1, LANES)), _full((1, LANES)), _full((2 * LANES, 2 * LANES)),
    ]
    args = [x, mod3, lw["g1"], lw["w_main"], lw["w_z"], lw["w_gate"], lw["b_gate"], lw["q_g"], lw["k_g"], lw["bd256"]]
    if rope:
        in_specs += [pl.BlockSpec((tm, LANES), lambda bi, i: (i, 0))] * 2
        args += list(rope_tabs)
    sds = lambda w, dt: jax.ShapeDtypeStruct((b, n, w), dt)
    out_shape = [sds(ATT_WIDTH, BF16), sds(4 * LANES, BF16), sds(4 * LANES, BF16), sds(FNET_WIDTH, F32),
                 sds(GLA_WIDTH, F32), sds(GLA_WIDTH, F32), sds(GLA_WIDTH, F32), sds(GLA_WIDTH, F32),
                 sds(2 * GLA_WIDTH, F32)]
    out_specs = [tok(s.shape[-1]) for s in out_shape]
    return pl.pallas_call(
        functools.partial(_inproj_kernel, rope=rope),
        out_shape=out_shape, grid=(b, n // tm), in_specs=in_specs, out_specs=out_specs,
        compiler_params=_params("parallel", "parallel"),
        name="inproj_rope" if rope else "inproj_ctx",
    )(*args)


def _scores(q2, key_tiles, masks):
    cols = []
    for k, tile_masks in zip(key_tiles, masks):
        s = lax.dot_general(q2, k, NT_DIMS, preferred_element_type=F32)
        for j, mk in enumerate(tile_masks):
            c = s[:, LANES * j:LANES * (j + 1)]
            cols.append(c if mk is None else jnp.where(mk, c, NEG_INF))
    return cols


def _softmax_pv(cols, val_tiles, sink_col):
    mx = cols[0]
    for c in cols[1:]:
        mx = jnp.maximum(mx, c)
    m = jnp.maximum(jnp.max(mx, axis=-1, keepdims=True), sink_col)
    p = jnp.concatenate([jnp.exp2(c - m).astype(BF16) for c in cols], axis=1)
    pv = _dot(p, jnp.concatenate(val_tiles, axis=0))
    den = pltpu.roll(pv, HEAD_DIM, 1) + jnp.exp2(sink_col - m)
    return pv / den


def _attn_kernel(*refs):
    nq = ATT_QBLOCKS
    sink_ref, q_ref = refs[0:2]
    k_refs, kc_ref = refs[2:nq + 4], refs[nq + 4]
    v_refs, vc_ref = refs[nq + 5:2 * nq + 7], refs[2 * nq + 7]
    o_ref = refs[2 * nq + 8]
    i = pl.program_id(1)
    last = pl.num_programs(1) - 1
    blk = ATT_BLOCK
    group = ATT_HEADS // ATT_KV_HEADS
    row = lax.broadcasted_iota(jnp.int32, (2 * blk, blk), 0) & (blk - 1)
    col = lax.broadcasted_iota(jnp.int32, (2 * blk, blk), 1)
    band_l, band_r = col >= row, col <= row
    mask_l = [jnp.logical_and(band_l, i > 0) if s == 0 else band_l for s in range(nq)]
    mask_r = [jnp.logical_and(band_r, i < last) if s == nq - 1 else band_r for s in range(nq)]
    upper_rows = lax.broadcasted_iota(jnp.int32, (2 * blk, 1), 0) >= blk
    lo = lax.broadcasted_iota(jnp.int32, (1, LANES), 1) < HEAD_DIM

    def tiles(blocks, c, s, ks):
        l, m_, r = blocks[s:s + 3]
        return [jnp.concatenate([l[:, ks], m_[:, ks]], axis=0), jnp.concatenate([r[:, ks], c[0:blk, ks]], axis=0),
                c[blk:, ks]]

    slots = [(s, kvh, par) for s in range(nq) for kvh in range(ATT_KV_HEADS) for par in range(2)]
    lane_slice = lambda kvh, par: slice(2 * LANES * kvh + LANES * par, 2 * LANES * kvh + LANES * (par + 1))
    cols = {}
    for s, kvh, par in slots:
        qs, rs = 2 * LANES * kvh, slice(blk * s, blk * (s + 1))
        q2 = jnp.concatenate([q_ref[rs, qs:qs + LANES], q_ref[rs, qs + LANES:qs + 2 * LANES]], axis=0)
        cols[s, kvh, par] = _scores(q2, tiles(k_refs, kc_ref, s, lane_slice(kvh, par)),
                                    [(mask_l[s], None), (mask_r[s], None), (None,)])
    outs = {}
    for s, kvh, par in slots:
        base = group * kvh + par
        sink_col = jnp.where(upper_rows, sink_ref[base + 2], sink_ref[base]) * LOG2E
        outs[s, kvh, par] = _softmax_pv(cols[s, kvh, par], tiles(v_refs, vc_ref, s, lane_slice(kvh, par)), sink_col)
    for s in range(nq):
        for kvh in range(ATT_KV_HEADS):
            qs, r0 = 2 * LANES * kvh, blk * s
            o = jnp.where(lo, outs[s, kvh, 0], outs[s, kvh, 1])
            o_ref[r0:r0 + blk, qs:qs + LANES] = o[0:blk].astype(o_ref.dtype)
            o_ref[r0:r0 + blk, qs + LANES:qs + 2 * LANES] = o[blk:2 * blk].astype(o_ref.dtype)


def _attention(q, k4, v4, ck4, cv4, sink):
    b, n, _ = q.shape
    lc = ck4.shape[1]
    nq = ATT_QBLOCKS
    nb = n // ATT_BLOCK
    assert lc == 2 * ATT_BLOCK and nb % nq == 0
    qspec = pl.BlockSpec((None, nq * ATT_BLOCK, 4 * LANES), lambda bi, i: (bi, i, 0))
    kv = [pl.BlockSpec((None, ATT_BLOCK, 4 * LANES),
                       functools.partial(lambda bi, i, j: (bi, jnp.clip(nq * i + j - 1, 0, nb - 1), 0), j=j))
          for j in range(nq + 2)]
    ctx = pl.BlockSpec((None, lc, 4 * LANES), lambda bi, i: (bi, 0, 0))
    return pl.pallas_call(
        _attn_kernel,
        out_shape=jax.ShapeDtypeStruct((b, n, ATT_WIDTH), BF16),
        grid=(b, nb // nq),
        in_specs=[pl.BlockSpec(memory_space=pltpu.SMEM), qspec] + kv + [ctx] + kv + [ctx],
        out_specs=qspec,
        compiler_params=_params("parallel", "parallel"),
        name="window_attention",
    )(sink, q, *([k4] * (nq + 2)), ck4, *([v4] * (nq + 2)), cv4)


def _attn_ctx_kernel(sink_ref, q_ref, kc_ref, vc_ref, o_ref):
    kvh = pl.program_id(1)
    lc = q_ref.shape[0]
    q2 = jnp.concatenate([q_ref[:, 0:LANES], q_ref[:, LANES:2 * LANES]], axis=0)
    upper_rows = lax.broadcasted_iota(jnp.int32, (2 * lc, 1), 0) >= lc
    lo = lax.broadcasted_iota(jnp.int32, (1, LANES), 1) < HEAD_DIM
    outs = []
    for par in range(2):
        ks = slice(LANES * par, LANES * (par + 1))
        base = ATT_HEADS // ATT_KV_HEADS * kvh + par
        sink_col = jnp.where(upper_rows, sink_ref[base + 2], sink_ref[base]) * LOG2E
        cols = _scores(q2, [kc_ref[:, ks]], [(None,) * (lc // LANES)])
        outs.append(_softmax_pv(cols, [vc_ref[:, ks]], sink_col))
    o = jnp.where(lo, outs[0], outs[1])
    o_ref[:, 0:LANES] = o[0:lc].astype(o_ref.dtype)
    o_ref[:, LANES:2 * LANES] = o[lc:2 * lc].astype(o_ref.dtype)


def _attention_ctx(cq, ck4, cv4, sink):
    b, lc, _ = cq.shape
    gw = 2 * LANES
    spec = pl.BlockSpec((None, lc, gw), lambda bi, h: (bi, 0, h))
    return pl.pallas_call(
        _attn_ctx_kernel,
        out_shape=jax.ShapeDtypeStruct((b, lc, ATT_WIDTH), BF16),
        grid=(b, ATT_KV_HEADS),
        in_specs=[pl.BlockSpec(memory_space=pltpu.SMEM), spec, spec, spec],
        out_specs=spec,
        compiler_params=_params("parallel", "parallel"),
        name="context_attention",
    )(sink, cq, ck4, cv4)


def _dft_consts(n):
    n1, n2 = DFT_N1, n // DFT_N1
    ang = lambda a, m: 2.0 * np.pi * np.outer(np.arange(a), np.arange(a)) / m
    c = HEAD_DIM
    cc, sc = np.cos(ang(c, c)) / np.sqrt(c), np.sin(ang(c, c)) / np.sqrt(c)
    eye = np.eye(FNET_GROUPS)
    w_chan = np.concatenate([np.kron(eye, cc), -np.kron(eye, sc)], axis=1)
    c1, s1 = np.cos(ang(n1, n1)) / np.sqrt(n1), np.sin(ang(n1, n1)) / np.sqrt(n1)
    m1 = np.block([[c1, s1], [-s1, c1]])
    kk = np.arange(n1)[:, None, None] + n1 * np.arange(n2)[None, :, None]
    ph = 2.0 * np.pi * (kk * np.arange(n2)[None, None, :] % n) / n
    m3 = np.concatenate([np.cos(ph), np.sin(ph)], axis=2) / np.sqrt(n2)
    return tuple(jnp.asarray(a, F32) for a in (w_chan, m1, m3))


def _dft_ctx_consts(lc):
    ang = lambda a, m: 2.0 * np.pi * np.outer(np.arange(a), np.arange(a)) / m
    c = HEAD_DIM
    cc, sc = np.cos(ang(c, c)) / np.sqrt(c), np.sin(ang(c, c)) / np.sqrt(c)
    eye = np.eye(FNET_GROUPS)
    w_chan = np.concatenate([np.kron(eye, cc), -np.kron(eye, sc)], axis=1)
    cl, sl = np.cos(ang(lc, lc)) / np.sqrt(lc), np.sin(ang(lc, lc)) / np.sqrt(lc)
    return jnp.asarray(w_chan, F32), jnp.asarray(np.concatenate([cl, sl], axis=1), F32)


def _regroup_rows(x, outer):
    r, c = x.shape
    return jnp.swapaxes(x.reshape(outer, r // outer, c), 0, 1).reshape(r, c)


def _fourier_kernel(u_ref, wc_ref, m1_ref, m3_ref, wf_ref, y_ref, p_ref):
    w, t = FNET_WIDTH, DFT_STEP
    s = pl.program_id(1)
    n_a = p_ref.shape[1]
    n2 = n_a * t

    @pl.when(s < n_a)
    def _():
        u = u_ref[...].reshape(DFT_N1 * t, w).astype(BF16)
        z = _regroup_rows(_dot(u, wc_ref[...]), DFT_N1)
        a_re, a_im = [], []
        for tt in range(t):
            zt = z[DFT_N1 * tt:DFT_N1 * (tt + 1)]
            zs = jnp.concatenate([zt[:, 0:w], zt[:, w:2 * w]], axis=0).astype(BF16)
            a = _dot(m1_ref[...], zs)
            a_re.append(a[0:DFT_N1])
            a_im.append(a[DFT_N1:2 * DFT_N1])
        for part, rows in enumerate((a_re, a_im)):
            a = _regroup_rows(jnp.concatenate(rows, axis=0), t)
            p_ref[part, s] = a.reshape(DFT_N1, t, w)

    @pl.when(s >= n_a)
    def _():
        k1_0 = (s - n_a) * t
        ys = []
        for kk in range(t):
            parts = [p_ref[part, :, k1_0 + kk].reshape(n2, w) for part in range(2)]
            rhs = jnp.concatenate(parts, axis=0).astype(BF16)
            ys.append(_dot(m3_ref[kk].astype(BF16), rhs).astype(BF16))
        out = _dot(jnp.concatenate(ys, axis=0), wf_ref[...])
        y_ref[...] = _regroup_rows(out, t).reshape(n2, t, w)


def _fourier(fu, wf_bd, consts):
    b, n, w = fu.shape
    n1, n2, t = DFT_N1, n // DFT_N1, DFT_STEP
    n_a, n_b = n2 // t, n1 // t
    w_chan, m1, m3 = consts
    y = pl.pallas_call(
        _fourier_kernel,
        out_shape=jax.ShapeDtypeStruct((b, n2, n1, w), F32),
        grid=(b, n_a + n_b),
        in_specs=[pl.BlockSpec((None, n1, t, w), lambda bi, s: (bi, 0, jnp.minimum(s, n_a - 1), 0)),
                  _full((w, 2 * w)), _full((2 * n1, 2 * n1)),
                  pl.BlockSpec((t, n2, 2 * n2), lambda bi, s: (jnp.maximum(s - n_a, 0), 0, 0)), _full((w, w))],
        out_specs=pl.BlockSpec((None, n2, t, w), lambda bi, s: (bi, 0, jnp.maximum(s - n_a, 0), 0)),
        scratch_shapes=[pltpu.VMEM((2, n_a, n1, t, w), F32)],
        compiler_params=_params("parallel", "arbitrary"),
        name="fourier_mix",
    )(fu.reshape(b, n1, n2, w), w_chan.astype(BF16), m1.astype(BF16), m3, wf_bd)
    return y.reshape(b, n, w)


def _four_ctx_kernel(u_ref, wc_ref, m_ref, wf_ref, y_ref):
    w = FNET_WIDTH
    z = _dot(u_ref[...].astype(BF16), wc_ref[...])
    zs = jnp.concatenate([z[:, 0:w], z[:, w:2 * w]], axis=0).astype(BF16)
    y = _dot(m_ref[...], zs)
    y_ref[...] = _dot(y.astype(BF16), wf_ref[...]).astype(y_ref.dtype)


def _fourier_ctx(cfu, wf_bd, consts):
    b, lc, w = cfu.shape
    w_chan, m = consts[0].astype(BF16), consts[1].astype(BF16)
    return pl.pallas_call(
        _four_ctx_kernel,
        out_shape=jax.ShapeDtypeStruct((b, lc, w), BF16),
        grid=(b,),
        in_specs=[pl.BlockSpec((None, lc, w), lambda bi: (bi, 0, 0)), _full((w, 2 * w)), _full((lc, 2 * lc)),
                  _full((w, w))],
        out_specs=pl.BlockSpec((None, lc, w), lambda bi: (bi, 0, 0)),
        compiler_params=_params("parallel"),
        name="fourier_context",
    )(cfu, w_chan, m, wf_bd)


def _pair_stack(x, lo):
    zero = jnp.zeros_like(x)
    return jnp.concatenate([jnp.where(lo, x, zero), jnp.where(lo, zero, x)], axis=0)


def _gla_kernel(qf_ref, kf_ref, vf_ref, laf_ref, qb_ref, kb_ref, vb_ref, lab_ref, s0_ref,
                of_ref, ob_ref, sfin_ref, stf_ref, stb_ref):
    i = pl.program_id(1)
    c = GLA_CHUNK
    nchunk = qf_ref.shape[0] // c

    @pl.when(i == 0)
    def _():
        stf_ref[...] = s0_ref[0]
        stb_ref[...] = s0_ref[1]

    r64 = lax.broadcasted_iota(jnp.int32, (c, c), 0)
    c64 = lax.broadcasted_iota(jnp.int32, (c, c), 1)
    tri = ((c64 <= r64).astype(BF16), (c64 >= r64).astype(BF16))
    at = lax.broadcasted_iota(jnp.int32, (c, LANES), 0)
    as_ = lax.broadcasted_iota(jnp.int32, (c, LANES), 1) & (c - 1)
    att_mask = (as_ <= at, as_ >= at)
    lo = lax.broadcasted_iota(jnp.int32, (1, LANES), 1) < HEAD_DIM
    br = lax.broadcasted_iota(jnp.int32, (LANES, LANES), 0) < HEAD_DIM
    bc = lax.broadcasted_iota(jnp.int32, (LANES, LANES), 1) < HEAD_DIM
    bd_mask = br == bc
    pairs = [slice(LANES * p, LANES * (p + 1)) for p in range(GLA_WIDTH // LANES)]
    in_refs = ((qf_ref, kf_ref, vf_ref, laf_ref), (qb_ref, kb_ref, vb_ref, lab_ref))
    out_refs = (of_ref, ob_ref)
    items = [(d, step if d == 0 else nchunk - 1 - step) for step in range(nchunk) for d in range(2)]
    rows = lambda ch: slice(c * ch, c * (ch + 1))

    bcum = {}
    for d, ch in items:
        la = in_refs[d][3][rows(ch)]
        la_hi = la.astype(BF16)
        la_lo = (la - la_hi.astype(F32)).astype(BF16)
        bcum[d, ch] = _dot(tri[d], la_hi) + _dot(tri[d], la_lo)
    work = {}
    for d, ch in items:
        q_ref, k_ref, v_ref, _ = in_refs[d]
        b = bcum[d, ch]
        btot = b[0:1] if d == 1 else b[c - 1:c]
        k = k_ref[rows(ch)]
        q_in = (q_ref[rows(ch)] * jnp.exp(b)).astype(BF16)
        k_in = (k * jnp.exp(-b)).astype(BF16)
        k_out = (k * jnp.exp(btot - b)).astype(BF16)
        vb = v_ref[rows(ch)].astype(BF16)
        att = [lax.dot_general(q_in[:, p], _pair_stack(k_in[:, p], lo), NT_DIMS, preferred_element_type=F32)
               for p in pairs]
        work[d, ch] = (btot, q_in, k_out, vb, att)
    o_intra, ut = {}, {}
    for d, ch in items:
        btot, q_in, k_out, vb, att = work[d, ch]
        o_intra[d, ch] = [_dot(jnp.where(att_mask[d], a, 0.0).astype(BF16), _pair_stack(vb[:, p], lo))
                          for a, p in zip(att, pairs)]
        ut[d, ch] = [lax.dot_general(vb[:, p], k_out[:, p], TN_DIMS, preferred_element_type=F32) for p in pairs]
    st_refs = (stf_ref, stb_ref)
    st = [[st_refs[d][LANES * j:LANES * (j + 1)] for j in range(len(pairs))] for d in range(2)]
    for d, ch in items:
        btot, q_in = work[d, ch][0:2]
        outs = []
        for j, p in enumerate(pairs):
            o_inter = lax.dot_general(q_in[:, p], st[d][j].astype(BF16), NT_DIMS, preferred_element_type=F32)
            outs.append(o_intra[d, ch][j] + o_inter)
            st[d][j] = st[d][j] * jnp.exp(btot[:, p]) + jnp.where(bd_mask, ut[d, ch][j], 0.0)
        out_refs[d][rows(ch)] = jnp.concatenate(outs, axis=1)
    for d in range(2):
        st_refs[d][...] = jnp.concatenate(st[d], axis=0)

    @pl.when(i == pl.num_programs(1) - 1)
    def _():
        for d in range(2):
            sfin_ref[d] = jnp.concatenate(st[d], axis=0)


def _gla(gq, gk, gv, la, s0, tb):
    b, n, w = gq.shape
    nblk = n // tb
    fwd = lambda bi, i: (bi, i, 0)
    bwd = lambda bi, i: (bi, nblk - 1 - i, 0)
    bwd_la = lambda bi, i: (bi, nblk - 1 - i, 1)
    tok = lambda f: pl.BlockSpec((None, tb, w), f)
    state = pl.BlockSpec((None, 2, w, LANES), lambda bi, i: (bi, 0, 0, 0))
    return pl.pallas_call(
        _gla_kernel,
        out_shape=[jax.ShapeDtypeStruct((b, n, w), F32), jax.ShapeDtypeStruct((b, n, w), F32),
                   jax.ShapeDtypeStruct((b, 2, w, LANES), F32)],
        grid=(b, nblk),
        in_specs=[tok(fwd), tok(fwd), tok(fwd), tok(fwd), tok(bwd), tok(bwd), tok(bwd), tok(bwd_la), state],
        out_specs=[tok(fwd), tok(bwd), state],
        scratch_shapes=[pltpu.VMEM((w, LANES), F32), pltpu.VMEM((w, LANES), F32)],
        compiler_params=_params("parallel", "arbitrary"),
        name="gla_scan",
    )(gq, gk, gv, la, gq, gk, gv, la, s0)


def _tail_kernel(att_ref, four_ref, of_ref, ob_ref, r_ref, x_ref, mod_ref, gg_ref, bd_ref, wmix_ref, g2_ref, wi_ref,
                 wo_ref, o_ref, a_ref):
    d = D_MODEL
    mod = lambda j: mod_ref[:, j * d:(j + 1) * d]
    o = of_ref[...] + ob_ref[...]
    y = o * lax.rsqrt(_group_mean(o, bd_ref) + EPS) * gg_ref[...]
    y = y * _silu(r_ref[...])
    a0, a1, a2 = ATT_WIDTH, ATT_WIDTH + FNET_WIDTH, ATT_WIDTH + FNET_WIDTH + GLA_WIDTH
    mix = (_dot(att_ref[...], wmix_ref[0:a0]) + _dot(four_ref[...].astype(BF16), wmix_ref[a0:a1])
           + _dot(y.astype(BF16), wmix_ref[a1:a2]))
    x = x_ref[...] + mod(2) * mix
    ms = jnp.mean(x * x, axis=-1, keepdims=True)
    hb = ((x * lax.rsqrt(ms + EPS) * g2_ref[...]) * (1.0 + mod(4)) + mod(3)).astype(BF16)
    for c0 in range(0, FFN_HIDDEN, FFN_CHUNK):
        g = _dot(hb, wi_ref[:, c0:c0 + FFN_CHUNK])
        u = _dot(hb, wi_ref[:, FFN_HIDDEN + c0:FFN_HIDDEN + c0 + FFN_CHUNK])
        a_ref[:, c0:c0 + FFN_CHUNK] = (_silu(g) * u).astype(BF16)
    o_ref[...] = x + mod(5) * _dot(a_ref[...], wo_ref[...])


def _tail(att, four, of, ob, gr, x, mod3, mod_row, lw, tm):
    b, n, d = x.shape
    row = (lambda bi: bi) if mod_row is None else (lambda bi: mod_row)
    tok = lambda w: pl.BlockSpec((None, tm, w), lambda bi, i: (bi, i, 0))
    resident = lambda shape: _layer_block(shape, lw["layer"])
    return pl.pallas_call(
        _tail_kernel,
        out_shape=jax.ShapeDtypeStruct((b, n, d), F32),
        grid=(b, n // tm),
        in_specs=[tok(ATT_WIDTH), tok(FNET_WIDTH), tok(GLA_WIDTH), tok(GLA_WIDTH), tok(GLA_WIDTH), tok(d),
                  pl.BlockSpec((None, 1, 6 * d), lambda bi, i: (row(bi), 0, 0)),
                  _full((1, GLA_WIDTH)), _full((GLA_WIDTH, GLA_WIDTH)), resident((d, d)),
                  _full((1, d)), resident((d, 2 * FFN_HIDDEN)), resident((FFN_HIDDEN, d))],
        out_specs=tok(d),
        scratch_shapes=[pltpu.VMEM((tm, FFN_HIDDEN), BF16)],
        compiler_params=_params("parallel", "parallel"),
        name="mix_ffn",
    )(att, four, of, ob, gr, x, mod3, lw["gla_g"], lw["bd256"], lw["w_out"], lw["g2"], lw["w_ffn_in"],
      lw["w_ffn_out"])


def _rope_tables(n):
    axis_dim = HEAD_DIM // 2
    inv_freq = ROPE_BASE ** (-np.arange(0, axis_dim, 2, dtype=np.float64) / axis_dim)
    t = np.arange(n)
    ang_r = (t // GRID_W)[:, None] * inv_freq[None, :]
    ang_c = (t % GRID_W)[:, None] * inv_freq[None, :]
    cos = np.concatenate([np.cos(ang_r)] * 2 + [np.cos(ang_c)] * 2, axis=1)
    sin = np.concatenate([-np.sin(ang_r), np.sin(ang_r), -np.sin(ang_c), np.sin(ang_c)], axis=1)
    return jnp.asarray(np.tile(cos, (1, 2)), F32), jnp.asarray(np.tile(sin, (1, 2)), F32)


def _block_diag_mean(width):
    g = np.arange(width) // HEAD_DIM
    return jnp.asarray((g[:, None] == g[None, :]) / HEAD_DIM, BF16)


def _layer_weights(l, big, w_in, g_norm1, q_norm_g, k_norm_g, w_fourier, wgf, bgf, wgb, bgb, gla_norm_g, g_norm2):
    r = GLA_GATE_RANK
    w_z = jnp.zeros((D_MODEL, LANES), F32).at[:, 0:2 * r].set(w_in[l, :, MAIN_WIDTH:MAIN_WIDTH + 2 * r])
    w_gate = jnp.zeros((LANES, 2 * GLA_WIDTH), F32)
    w_gate = w_gate.at[0:r, 0:GLA_WIDTH].set(wgf[l]).at[r:2 * r, GLA_WIDTH:].set(wgb[l])
    wf_bd = jnp.zeros((FNET_WIDTH, FNET_WIDTH), F32)
    for g in range(FNET_GROUPS):
        wf_bd = wf_bd.at[HEAD_DIM * g:HEAD_DIM * (g + 1), HEAD_DIM * g:HEAD_DIM * (g + 1)].set(w_fourier[l, g])
    return {
        "g1": g_norm1[l][None, :],
        "layer": l,
        "w_main": big["w_in"],
        "w_z": w_z.astype(BF16),
        "w_gate": w_gate.astype(BF16),
        "b_gate": jnp.concatenate([bgf[l], bgb[l]])[None, :],
        "q_g": jnp.tile(q_norm_g[l], 2)[None, :],
        "k_g": jnp.tile(k_norm_g[l], 2)[None, :],
        "bd256": _block_diag_mean(GLA_WIDTH),
        "wf_bd": wf_bd.astype(BF16),
        "gla_g": jnp.tile(gla_norm_g[l], GLA_HEADS)[None, :],
        "w_out": big["w_out"],
        "g2": g_norm2[l][None, :],
        "w_ffn_in": big["w_ffn_in"],
        "w_ffn_out": big["w_ffn_out"],
    }


def kernel(x, c, ctx, c_ctx, w_mod, b_mod, g_norm1, w_in, q_norm_g, k_norm_g, attn_sink, w_fourier, gla_w_gate_f,
           gla_b_gate_f, gla_w_gate_b, gla_b_gate_b, gla_norm_g, w_out, g_norm2, w_ffn_in, w_ffn_out):
    b, n, d = x.shape
    lc = ctx.shape[1]
    depth = w_mod.shape[0]
    assert d == D_MODEL and b <= 7 and n % 512 == 0 and n % (DFT_N1 * DFT_STEP) == 0 and lc % GLA_CHUNK == 0

    cc = jnp.zeros((8, d), F32).at[0:b].set(c).at[b].set(c_ctx)
    mod = _modulation(cc, w_mod, b_mod)
    rope_tabs = _rope_tables(n)
    dft = _dft_consts(n)
    dft_ctx = _dft_ctx_consts(lc)
    tm_lat, tm_ctx, tb_lat = 512, lc, 1024
    xc = ctx
    big = {"w_in": w_in.astype(BF16), "w_out": w_out.astype(BF16), "w_ffn_in": w_ffn_in.astype(BF16),
           "w_ffn_out": w_ffn_out.astype(BF16)}
    for l in range(depth):
        need_ctx = l < depth - 1
        lw = _layer_weights(l, big, w_in, g_norm1, q_norm_g, k_norm_g, w_fourier, gla_w_gate_f, gla_b_gate_f,
                            gla_w_gate_b, gla_b_gate_b, gla_norm_g, g_norm2)
        mod3 = mod[l].reshape(8, 1, 6 * d)
        sink = attn_sink[l]
        cq, ck4, cv4, cfu, cgq, cgk, cgv, cgr, cla = _inproj(xc, mod3, b, lw, None, tm_ctx)
        q, k4, v4, fu, gq, gk, gv, gr, la = _inproj(x, mod3, None, lw, rope_tabs, tm_lat)
        att = _attention(q, k4, v4, ck4, cv4, sink)
        four = _fourier(fu, lw["wf_bd"], dft)
        s_zero = jnp.zeros((b, 2, GLA_WIDTH, LANES), F32)
        ocf, ocb, s_ctx = _gla(cgq, cgk, cgv, cla, s_zero, lc)
        olf, olb, _ = _gla(gq, gk, gv, la, s_ctx, tb_lat)
        x = _tail(att, four, olf, olb, gr, x, mod3, None, lw, tm_lat)
        if need_ctx:
            att_c = _attention_ctx(cq, ck4, cv4, sink)
            four_c = _fourier_ctx(cfu, lw["wf_bd"], dft_ctx)
            xc = _tail(att_c, four_c, ocf, ocb, cgr, xc, mod3, b, lw, tm_ctx)
    return x
```

```python
import functools

import numpy as np
import jax
import jax.numpy as jnp
from jax import lax
from jax.experimental import pallas as pl
from jax.experimental.pallas import tpu as pltpu

F32 = jnp.float32
BF16 = jnp.bfloat16

D_MODEL = 1024
HEAD_DIM = 64
GRID_W = 64
ROPE_BASE = 10000.0
ATT_HEADS = 8
ATT_KV_HEADS = 2
ATT_WIDTH = ATT_HEADS * HEAD_DIM
KV_WIDTH = ATT_KV_HEADS * HEAD_DIM
ATT_BLOCK = 128
ATT_QBLOCKS = 4
ATT_SCALE = HEAD_DIM ** -0.5
LOG2E = 1.4426950408889634
NEG_INF = -1e30
FNET_GROUPS = 4
FNET_WIDTH = FNET_GROUPS * HEAD_DIM
GLA_HEADS = 4
GLA_WIDTH = GLA_HEADS * HEAD_DIM
GLA_GATE_RANK = 16
GLA_TAU = 16.0
GLA_CHUNK = 64
GLA_SCALE = HEAD_DIM ** -0.5
MAIN_WIDTH = ATT_WIDTH + 2 * KV_WIDTH + FNET_WIDTH + 4 * GLA_WIDTH
FFN_HIDDEN = 2816
FFN_CHUNK = 256
INPROJ_SUB = 512
EPS = 1e-6
LANES = 128
DFT_N1 = 128
DFT_STEP = 16
VMEM_LIMIT = 56 * 1024 * 1024

NT_DIMS = (((1,), (1,)), ((), ()))
TN_DIMS = (((0,), (0,)), ((), ()))


def _params(*sem):
    return pltpu.CompilerParams(dimension_semantics=sem, vmem_limit_bytes=VMEM_LIMIT)


def _dot(a, b):
    return jnp.dot(a, b, preferred_element_type=F32)


def _silu(x):
    return x / (1.0 + jnp.exp(-x))


def _layer_block(shape, layer):
    nd = len(shape)
    return pl.BlockSpec((None,) + tuple(shape), lambda *_: (layer,) + (0,) * nd, pipeline_mode=pl.Buffered(1))


def _full(shape):
    nd = len(shape)
    return pl.BlockSpec(shape, lambda *_: (0,) * nd)


def _group_mean(t, bd_ref):
    return _dot((t * t).astype(BF16), bd_ref[...])


def _mod_kernel(c_ref, w_ref, b_ref, o_ref):
    s = _silu(c_ref[...]).astype(BF16)
    o_ref[...] = _dot(s, w_ref[...].astype(BF16)) + b_ref[...]


def _modulation(cc, w_mod, b_mod):
    depth, d, width = w_mod.shape
    tn = 1536
    return pl.pallas_call(
        _mod_kernel,
        out_shape=jax.ShapeDtypeStruct((depth, 8, width), F32),
        grid=(depth, width // tn),
        in_specs=[
            _full((8, d)),
            pl.BlockSpec((None, d, tn), lambda l, j: (l, 0, j)),
            pl.BlockSpec((None, 1, tn), lambda l, j: (l, 0, j)),
        ],
        out_specs=pl.BlockSpec((None, 8, tn), lambda l, j: (l, 0, j)),
        compiler_params=_params("parallel", "parallel"),
        name="modulation",
    )(cc, w_mod, b_mod.reshape(depth, 1, width))


def _inproj_kernel(*refs, rope):
    if rope:
        (x_ref, mod_ref, g1_ref, wm_ref, wz_ref, wg_ref, bg_ref, qg_ref, kg_ref, bd_ref, cos_ref, sin_ref,
         q_ref, k4_ref, v4_ref, fu_ref, gq_ref, gk_ref, gv_ref, gr_ref, la_ref) = refs
    else:
        (x_ref, mod_ref, g1_ref, wm_ref, wz_ref, wg_ref, bg_ref, qg_ref, kg_ref, bd_ref,
         q_ref, k4_ref, v4_ref, fu_ref, gq_ref, gk_ref, gv_ref, gr_ref, la_ref) = refs
    d = D_MODEL
    mod = mod_ref[...]
    sh, sc = mod[:, 0:d], mod[:, d:2 * d]
    sub = min(x_ref.shape[0], INPROJ_SUB)
    subs = [slice(sub * j, sub * (j + 1)) for j in range(x_ref.shape[0] // sub)]

    def normed(rs):
        x = x_ref[rs]
        ms = jnp.mean(x * x, axis=-1, keepdims=True)
        return ((x * lax.rsqrt(ms + EPS) * g1_ref[...]) * (1.0 + sc) + sh).astype(BF16)

    lane = lax.broadcasted_iota(jnp.int32, (1, LANES), 1)
    lo = lane < HEAD_DIM
    second_half = (lane & 16) != 0

    def head_norm(t, g_ref):
        wdt = t.shape[1]
        ms = _dot((t * t).astype(BF16), bd_ref[0:wdt, 0:wdt])
        g = g_ref[...] if wdt == LANES else jnp.concatenate([g_ref[...]] * (wdt // LANES), axis=1)
        return t * lax.rsqrt(ms + EPS) * g

    def rotary(t, rs):
        if not rope:
            return t
        partner = jnp.where(second_half, pltpu.roll(t, 16, 1), pltpu.roll(t, LANES - 16, 1))
        return t * cos_ref[rs] + partner * sin_ref[rs]

    def spread(t, out_ref, rs, idle):
        tr = pltpu.roll(t, HEAD_DIM, 1)
        fill = jnp.full_like(t, idle)
        out_ref[rs, 0:128] = jnp.where(lo, t, fill).astype(out_ref.dtype)
        out_ref[rs, 128:256] = jnp.where(lo, fill, tr).astype(out_ref.dtype)
        out_ref[rs, 256:384] = jnp.where(lo, tr, fill).astype(out_ref.dtype)
        out_ref[rs, 384:512] = jnp.where(lo, fill, t).astype(out_ref.dtype)

    assert (ATT_WIDTH, 2 * KV_WIDTH + FNET_WIDTH, 2 * GLA_WIDTH) == (4 * LANES,) * 3

    def project(rs, hb):
        group = lambda g: _dot(hb, wm_ref[:, 4 * LANES * g:4 * LANES * (g + 1)])
        gz = _dot(hb, wz_ref[...])
        a_q = group(0)
        z = _dot(gz.astype(BF16), wg_ref[...]) + bg_ref[...]
        log_sig = jnp.minimum(z, 0.0) - jnp.log(1.0 + jnp.exp(-jnp.abs(z)))
        la_ref[rs] = log_sig * (1.0 / GLA_TAU)
        a_kvf = group(1)
        for j2 in range(ATT_WIDTH // (2 * LANES)):
            qn = head_norm(a_q[:, 2 * LANES * j2:2 * LANES * (j2 + 1)], qg_ref)
            for j in range(2):
                t = rotary(qn[:, LANES * j:LANES * (j + 1)], rs) * (ATT_SCALE * LOG2E)
                q_ref[rs, LANES * (2 * j2 + j):LANES * (2 * j2 + j + 1)] = t.astype(q_ref.dtype)
        a_qk = group(2)
        spread(rotary(head_norm(a_kvf[:, 0:KV_WIDTH], kg_ref), rs), k4_ref, rs, 0.0)
        spread(a_kvf[:, KV_WIDTH:2 * KV_WIDTH], v4_ref, rs, 1.0)
        fu_ref[rs] = a_kvf[:, 2 * KV_WIDTH:].astype(fu_ref.dtype)
        a_vr = group(3)
        gq_ref[rs] = (a_qk[:, 0:GLA_WIDTH] * GLA_SCALE).astype(gq_ref.dtype)
        gk_ref[rs] = a_qk[:, GLA_WIDTH:].astype(gk_ref.dtype)
        gv_ref[rs] = a_vr[:, 0:GLA_WIDTH].astype(gv_ref.dtype)
        gr_ref[rs] = a_vr[:, GLA_WIDTH:].astype(gr_ref.dtype)

    hbs = [normed(rs) for rs in subs]
    for rs, hb in zip(subs, hbs):
        project(rs, hb)


def _inproj(x, mod3, mod_row, lw, rope_tabs, tm):
    b, n, d = x.shape
    rope = rope_tabs is not None
    row = (lambda bi: bi) if mod_row is None else (lambda bi: mod_row)
    tok = lambda w: pl.BlockSpec((None, tm, w), lambda bi, i: (bi, i, 0))
    in_specs = [
        tok(d),
        pl.BlockSpec((None, 1, 6 * d), lambda bi, i: (row(bi), 0, 0)),
        _full((1, d)), _layer_block((d, MAIN_WIDTH), lw["layer"]), _full((d, LANES)), _full((LANES, 2 * GLA_WIDTH)),
        _full((1, 2 * GLA_WIDTH)), _full((1, LANES)), _full((1, LANES)), _full((2 * LANES, 2 * LANES)),
    ]
    args = [x, mod3, lw["g1"], lw["w_main"], lw["w_z"], lw["w_gate"], lw["b_gate"], lw["q_g"], lw["k_g"], lw["bd256"]]
    if rope:
        in_specs += [pl.BlockSpec((tm, LANES), lambda bi, i: (i, 0))] * 2
        args += list(rope_tabs)
    sds = lambda w, dt: jax.ShapeDtypeStruct((b, n, w), dt)
    out_shape = [sds(ATT_WIDTH, BF16), sds(4 * LANES, BF16), sds(4 * LANES, BF16), sds(FNET_WIDTH, F32),
                 sds(GLA_WIDTH, BF16), sds(GLA_WIDTH, BF16), sds(GLA_WIDTH, BF16), sds(GLA_WIDTH, BF16),
                 sds(2 * GLA_WIDTH, F32)]
    out_specs = [tok(s.shape[-1]) for s in out_shape]
    return pl.pallas_call(
        functools.partial(_inproj_kernel, rope=rope),
        out_shape=out_shape, grid=(b, n // tm), in_specs=in_specs, out_specs=out_specs,
        compiler_params=_params("parallel", "parallel"),
        name="inproj_rope" if rope else "inproj_ctx",
    )(*args)


def _scores(q2, key_tiles, masks):
    cols = []
    for k, tile_masks in zip(key_tiles, masks):
        s = lax.dot_general(q2, k, NT_DIMS, preferred_element_type=F32)
        for j, mk in enumerate(tile_masks):
            c = s[:, LANES * j:LANES * (j + 1)]
            cols.append(c if mk is None else jnp.where(mk, c, NEG_INF))
    return cols


def _softmax_pv(cols, val_tiles, sink_col):
    mx = cols[0]
    for c in cols[1:]:
        mx = jnp.maximum(mx, c)
    m = jnp.maximum(jnp.max(mx, axis=-1, keepdims=True), sink_col)
    p = jnp.concatenate([jnp.exp2(c - m).astype(BF16) for c in cols], axis=1)
    pv = _dot(p, jnp.concatenate(val_tiles, axis=0))
    den = pltpu.roll(pv, HEAD_DIM, 1) + jnp.exp2(sink_col - m)
    return pv / den


def _attn_kernel(*refs):
    nq = ATT_QBLOCKS
    sink_ref, q_ref = refs[0:2]
    k_refs, kc_ref = refs[2:nq + 4], refs[nq + 4]
    v_refs, vc_ref = refs[nq + 5:2 * nq + 7], refs[2 * nq + 7]
    o_ref = refs[2 * nq + 8]
    i = pl.program_id(1)
    last = pl.num_programs(1) - 1
    blk = ATT_BLOCK
    group = ATT_HEADS // ATT_KV_HEADS
    row = lax.broadcasted_iota(jnp.int32, (2 * blk, blk), 0) & (blk - 1)
    col = lax.broadcasted_iota(jnp.int32, (2 * blk, blk), 1)
    band_l, band_r = col >= row, col <= row
    mask_l = [jnp.logical_and(band_l, i > 0) if s == 0 else band_l for s in range(nq)]
    mask_r = [jnp.logical_and(band_r, i < last) if s == nq - 1 else band_r for s in range(nq)]
    upper_rows = lax.broadcasted_iota(jnp.int32, (2 * blk, 1), 0) >= blk
    lo = lax.broadcasted_iota(jnp.int32, (1, LANES), 1) < HEAD_DIM

    def tiles(blocks, c, s, ks):
        l, m_, r = blocks[s:s + 3]
        return [jnp.concatenate([l[:, ks], m_[:, ks]], axis=0), jnp.concatenate([r[:, ks], c[0:blk, ks]], axis=0),
                c[blk:, ks]]

    slots = [(s, kvh, par) for s in range(nq) for kvh in range(ATT_KV_HEADS) for par in range(2)]
    lane_slice = lambda kvh, par: slice(2 * LANES * kvh + LANES * par, 2 * LANES * kvh + LANES * (par + 1))
    cols = {}
    for s, kvh, par in slots:
        qs, rs = 2 * LANES * kvh, slice(blk * s, blk * (s + 1))
        q2 = jnp.concatenate([q_ref[rs, qs:qs + LANES], q_ref[rs, qs + LANES:qs + 2 * LANES]], axis=0)
        cols[s, kvh, par] = _scores(q2, tiles(k_refs, kc_ref, s, lane_slice(kvh, par)),
                                    [(mask_l[s], None), (mask_r[s], None), (None,)])
    outs = {}
    for s, kvh, par in slots:
        base = group * kvh + par
        sink_col = jnp.where(upper_rows, sink_ref[base + 2], sink_ref[base]) * LOG2E
        outs[s, kvh, par] = _softmax_pv(cols[s, kvh, par], tiles(v_refs, vc_ref, s, lane_slice(kvh, par)), sink_col)
    for s in range(nq):
        for kvh in range(ATT_KV_HEADS):
            qs, r0 = 2 * LANES * kvh, blk * s
            o = jnp.where(lo, outs[s, kvh, 0], outs[s, kvh, 1])
            o_ref[r0:r0 + blk, qs:qs + LANES] = o[0:blk].astype(o_ref.dtype)
            o_ref[r0:r0 + blk, qs + LANES:qs + 2 * LANES] = o[blk:2 * blk].astype(o_ref.dtype)


def _attention(q, k4, v4, ck4, cv4, sink):
    b, n, _ = q.shape
    lc = ck4.shape[1]
    nq = ATT_QBLOCKS
    nb = n // ATT_BLOCK
    assert lc == 2 * ATT_BLOCK and nb % nq == 0
    qspec = pl.BlockSpec((None, nq * ATT_BLOCK, 4 * LANES), lambda bi, i: (bi, i, 0))
    kv = [pl.BlockSpec((None, ATT_BLOCK, 4 * LANES),
                       functools.partial(lambda bi, i, j: (bi, jnp.clip(nq * i + j - 1, 0, nb - 1), 0), j=j))
          for j in range(nq + 2)]
    ctx = pl.BlockSpec((None, lc, 4 * LANES), lambda bi, i: (bi, 0, 0))
    return pl.pallas_call(
        _attn_kernel,
        out_shape=jax.ShapeDtypeStruct((b, n, ATT_WIDTH), BF16),
        grid=(b, nb // nq),
        in_specs=[pl.BlockSpec(memory_space=pltpu.SMEM), qspec] + kv + [ctx] + kv + [ctx],
        out_specs=qspec,
        compiler_params=_params("parallel", "parallel"),
        name="window_attention",
    )(sink, q, *([k4] * (nq + 2)), ck4, *([v4] * (nq + 2)), cv4)


def _attn_ctx_kernel(sink_ref, q_ref, kc_ref, vc_ref, o_ref):
    kvh = pl.program_id(1)
    lc = q_ref.shape[0]
    q2 = jnp.concatenate([q_ref[:, 0:LANES], q_ref[:, LANES:2 * LANES]], axis=0)
    upper_rows = lax.broadcasted_iota(jnp.int32, (2 * lc, 1), 0) >= lc
    lo = lax.broadcasted_iota(jnp.int32, (1, LANES), 1) < HEAD_DIM
    outs = []
    for par in range(2):
        ks = slice(LANES * par, LANES * (par + 1))
        base = ATT_HEADS // ATT_KV_HEADS * kvh + par
        sink_col = jnp.where(upper_rows, sink_ref[base + 2], sink_ref[base]) * LOG2E
        cols = _scores(q2, [kc_ref[:, ks]], [(None,) * (lc // LANES)])
        outs.append(_softmax_pv(cols, [vc_ref[:, ks]], sink_col))
    o = jnp.where(lo, outs[0], outs[1])
    o_ref[:, 0:LANES] = o[0:lc].astype(o_ref.dtype)
    o_ref[:, LANES:2 * LANES] = o[lc:2 * lc].astype(o_ref.dtype)


def _attention_ctx(cq, ck4, cv4, sink):
    b, lc, _ = cq.shape
    gw = 2 * LANES
    spec = pl.BlockSpec((None, lc, gw), lambda bi, h: (bi, 0, h))
    return pl.pallas_call(
        _attn_ctx_kernel,
        out_shape=jax.ShapeDtypeStruct((b, lc, ATT_WIDTH), BF16),
        grid=(b, ATT_KV_HEADS),
        in_specs=[pl.BlockSpec(memory_space=pltpu.SMEM), spec, spec, spec],
        out_specs=spec,
        compiler_params=_params("parallel", "parallel"),
        name="context_attention",
    )(sink, cq, ck4, cv4)


def _dft_consts(n):
    n1, n2 = DFT_N1, n // DFT_N1
    ang = lambda a, m: 2.0 * np.pi * np.outer(np.arange(a), np.arange(a)) / m
    c = HEAD_DIM
    cc, sc = np.cos(ang(c, c)) / np.sqrt(c), np.sin(ang(c, c)) / np.sqrt(c)
    eye = np.eye(FNET_GROUPS)
    w_chan = np.concatenate([np.kron(eye, cc), -np.kron(eye, sc)], axis=1)
    c1, s1 = np.cos(ang(n1, n1)) / np.sqrt(n1), np.sin(ang(n1, n1)) / np.sqrt(n1)
    m1 = np.block([[c1, s1], [-s1, c1]])
    kk = np.arange(n1)[:, None, None] + n1 * np.arange(n2)[None, :, None]
    ph = 2.0 * np.pi * (kk * np.arange(n2)[None, None, :] % n) / n
    m3 = np.concatenate([np.cos(ph), np.sin(ph)], axis=2) / np.sqrt(n2)
    return tuple(jnp.asarray(a, F32) for a in (w_chan, m1, m3))


def _dft_ctx_consts(lc):
    ang = lambda a, m: 2.0 * np.pi * np.outer(np.arange(a), np.arange(a)) / m
    c = HEAD_DIM
    cc, sc = np.cos(ang(c, c)) / np.sqrt(c), np.sin(ang(c, c)) / np.sqrt(c)
    eye = np.eye(FNET_GROUPS)
    w_chan = np.concatenate([np.kron(eye, cc), -np.kron(eye, sc)], axis=1)
    cl, sl = np.cos(ang(lc, lc)) / np.sqrt(lc), np.sin(ang(lc, lc)) / np.sqrt(lc)
    return jnp.asarray(w_chan, F32), jnp.asarray(np.concatenate([cl, sl], axis=1), F32)


def _regroup_rows(x, outer):
    r, c = x.shape
    return jnp.swapaxes(x.reshape(outer, r // outer, c), 0, 1).reshape(r, c)


def _fourier_kernel(u_ref, wc_ref, m1_ref, m3_ref, wf_ref, y_ref, p_ref):
    w, t = FNET_WIDTH, DFT_STEP
    s = pl.program_id(1)
    n_a = p_ref.shape[1]
    n2 = n_a * t

    @pl.when(s < n_a)
    def _():
        u = u_ref[...].reshape(DFT_N1 * t, w).astype(BF16)
        z = _regroup_rows(_dot(u, wc_ref[...]), DFT_N1)
        a_re, a_im = [], []
        for tt in range(t):
            zt = z[DFT_N1 * tt:DFT_N1 * (tt + 1)]
            zs = jnp.concatenate([zt[:, 0:w], zt[:, w:2 * w]], axis=0).astype(BF16)
            a = _dot(m1_ref[...], zs)
            a_re.append(a[0:DFT_N1])
            a_im.append(a[DFT_N1:2 * DFT_N1])
        for part, rows in enumerate((a_re, a_im)):
            a = _regroup_rows(jnp.concatenate(rows, axis=0), t)
            p_ref[part, s] = a.reshape(DFT_N1, t, w)

    @pl.when(s >= n_a)
    def _():
        k1_0 = (s - n_a) * t
        ys = []
        for kk in range(t):
            parts = [p_ref[part, :, k1_0 + kk].reshape(n2, w) for part in range(2)]
            rhs = jnp.concatenate(parts, axis=0).astype(BF16)
            ys.append(_dot(m3_ref[kk].astype(BF16), rhs).astype(BF16))
        out = _dot(jnp.concatenate(ys, axis=0), wf_ref[...])
        y_ref[...] = _regroup_rows(out, t).reshape(n2, t, w)


def _fourier(fu, wf_bd, consts):
    b, n, w = fu.shape
    n1, n2, t = DFT_N1, n // DFT_N1, DFT_STEP
    n_a, n_b = n2 // t, n1 // t
    w_chan, m1, m3 = consts
    y = pl.pallas_call(
        _fourier_kernel,
        out_shape=jax.ShapeDtypeStruct((b, n2, n1, w), F32),
        grid=(b, n_a + n_b),
        in_specs=[pl.BlockSpec((None, n1, t, w), lambda bi, s: (bi, 0, jnp.minimum(s, n_a - 1), 0)),
                  _full((w, 2 * w)), _full((2 * n1, 2 * n1)),
                  pl.BlockSpec((t, n2, 2 * n2), lambda bi, s: (jnp.maximum(s - n_a, 0), 0, 0)), _full((w, w))],
        out_specs=pl.BlockSpec((None, n2, t, w), lambda bi, s: (bi, 0, jnp.maximum(s - n_a, 0), 0)),
        scratch_shapes=[pltpu.VMEM((2, n_a, n1, t, w), F32)],
        compiler_params=_params("parallel", "arbitrary"),
        name="fourier_mix",
    )(fu.reshape(b, n1, n2, w), w_chan.astype(BF16), m1.astype(BF16), m3, wf_bd)
    return y.reshape(b, n, w)


def _four_ctx_kernel(u_ref, wc_ref, m_ref, wf_ref, y_ref):
    w = FNET_WIDTH
    z = _dot(u_ref[...].astype(BF16), wc_ref[...])
    zs = jnp.concatenate([z[:, 0:w], z[:, w:2 * w]], axis=0).astype(BF16)
    y = _dot(m_ref[...], zs)
    y_ref[...] = _dot(y.astype(BF16), wf_ref[...]).astype(y_ref.dtype)


def _fourier_ctx(cfu, wf_bd, consts):
    b, lc, w = cfu.shape
    w_chan, m = consts[0].astype(BF16), consts[1].astype(BF16)
    return pl.pallas_call(
        _four_ctx_kernel,
        out_shape=jax.ShapeDtypeStruct((b, lc, w), BF16),
        grid=(b,),
        in_specs=[pl.BlockSpec((None, lc, w), lambda bi: (bi, 0, 0)), _full((w, 2 * w)), _full((lc, 2 * lc)),
                  _full((w, w))],
        out_specs=pl.BlockSpec((None, lc, w), lambda bi: (bi, 0, 0)),
        compiler_params=_params("parallel"),
        name="fourier_context",
    )(cfu, w_chan, m, wf_bd)


def _pair_stack(x, lo):
    zero = jnp.zeros_like(x)
    return jnp.concatenate([jnp.where(lo, x, zero), jnp.where(lo, zero, x)], axis=0)


def _gla_kernel(qf_ref, kf_ref, vf_ref, laf_ref, qb_ref, kb_ref, vb_ref, lab_ref, s0_ref,
                of_ref, ob_ref, sfin_ref, stf_ref, stb_ref):
    i = pl.program_id(1)
    c = GLA_CHUNK
    nchunk = qf_ref.shape[0] // c

    @pl.when(i == 0)
    def _():
        stf_ref[...] = s0_ref[0]
        stb_ref[...] = s0_ref[1]

    r64 = lax.broadcasted_iota(jnp.int32, (c, c), 0)
    c64 = lax.broadcasted_iota(jnp.int32, (c, c), 1)
    tri = ((c64 <= r64).astype(BF16), (c64 >= r64).astype(BF16))
    at = lax.broadcasted_iota(jnp.int32, (c, LANES), 0)
    as_ = lax.broadcasted_iota(jnp.int32, (c, LANES), 1) & (c - 1)
    att_mask = (as_ <= at, as_ >= at)
    lo = lax.broadcasted_iota(jnp.int32, (1, LANES), 1) < HEAD_DIM
    br = lax.broadcasted_iota(jnp.int32, (LANES, LANES), 0) < HEAD_DIM
    bc = lax.broadcasted_iota(jnp.int32, (LANES, LANES), 1) < HEAD_DIM
    bd_mask = br == bc
    pairs = [slice(LANES * p, LANES * (p + 1)) for p in range(GLA_WIDTH // LANES)]
    in_refs = ((qf_ref, kf_ref, vf_ref, laf_ref), (qb_ref, kb_ref, vb_ref, lab_ref))
    out_refs = (of_ref, ob_ref)
    items = [(d, step if d == 0 else nchunk - 1 - step) for step in range(nchunk) for d in range(2)]
    rows = lambda ch: slice(c * ch, c * (ch + 1))

    bcum = {}
    for d, ch in items:
        la = in_refs[d][3][rows(ch)]
        la_hi = la.astype(BF16)
        la_lo = (la - la_hi.astype(F32)).astype(BF16)
        bcum[d, ch] = _dot(tri[d], la_hi) + _dot(tri[d], la_lo)
    work = {}
    for d, ch in items:
        q_ref, k_ref, v_ref, _ = in_refs[d]
        b = bcum[d, ch]
        btot = b[0:1] if d == 1 else b[c - 1:c]
        k = k_ref[rows(ch)].astype(F32)
        q_in = (q_ref[rows(ch)].astype(F32) * jnp.exp(b)).astype(BF16)
        k_in = (k * jnp.exp(-b)).astype(BF16)
        k_out = (k * jnp.exp(btot - b)).astype(BF16)
        vb = v_ref[rows(ch)].astype(BF16)
        att = [lax.dot_general(q_in[:, p], _pair_stack(k_in[:, p], lo), NT_DIMS, preferred_element_type=F32)
               for p in pairs]
        work[d, ch] = (btot, q_in, k_out, vb, att)
    o_intra, ut = {}, {}
    for d, ch in items:
        btot, q_in, k_out, vb, att = work[d, ch]
        o_intra[d, ch] = [_dot(jnp.where(att_mask[d], a, 0.0).astype(BF16), _pair_stack(vb[:, p], lo))
                          for a, p in zip(att, pairs)]
        ut[d, ch] = [lax.dot_general(vb[:, p], k_out[:, p], TN_DIMS, preferred_element_type=F32) for p in pairs]
    st_refs = (stf_ref, stb_ref)
    st = [[st_refs[d][LANES * j:LANES * (j + 1)] for j in range(len(pairs))] for d in range(2)]
    for d, ch in items:
        btot, q_in = work[d, ch][0:2]
        outs = []
        for j, p in enumerate(pairs):
            o_inter = lax.dot_general(q_in[:, p], st[d][j].astype(BF16), NT_DIMS, preferred_element_type=F32)
            outs.append(o_intra[d, ch][j] + o_inter)
            st[d][j] = st[d][j] * jnp.exp(btot[:, p]) + jnp.where(bd_mask, ut[d, ch][j], 0.0)
        out_refs[d][rows(ch)] = jnp.concatenate(outs, axis=1).astype(out_refs[d].dtype)
    for d in range(2):
        st_refs[d][...] = jnp.concatenate(st[d], axis=0)

    @pl.when(i == pl.num_programs(1) - 1)
    def _():
        for d in range(2):
            sfin_ref[d] = jnp.concatenate(st[d], axis=0)


def _gla(gq, gk, gv, la, s0, tb):
    b, n, w = gq.shape
    nblk = n // tb
    fwd = lambda bi, i: (bi, i, 0)
    bwd = lambda bi, i: (bi, nblk - 1 - i, 0)
    bwd_la = lambda bi, i: (bi, nblk - 1 - i, 1)
    tok = lambda f: pl.BlockSpec((None, tb, w), f)
    state = pl.BlockSpec((None, 2, w, LANES), lambda bi, i: (bi, 0, 0, 0))
    return pl.pallas_call(
        _gla_kernel,
        out_shape=[jax.ShapeDtypeStruct((b, n, w), BF16), jax.ShapeDtypeStruct((b, n, w), BF16),
                   jax.ShapeDtypeStruct((b, 2, w, LANES), F32)],
        grid=(b, nblk),
        in_specs=[tok(fwd), tok(fwd), tok(fwd), tok(fwd), tok(bwd), tok(bwd), tok(bwd), tok(bwd_la), state],
        out_specs=[tok(fwd), tok(bwd), state],
        scratch_shapes=[pltpu.VMEM((w, LANES), F32), pltpu.VMEM((w, LANES), F32)],
        compiler_params=_params("parallel", "arbitrary"),
        name="gla_scan",
    )(gq, gk, gv, la, gq, gk, gv, la, s0)


def _tail_kernel(att_ref, four_ref, of_ref, ob_ref, r_ref, x_ref, mod_ref, gg_ref, bd_ref, wmix_ref, g2_ref, wi_ref,
                 wo_ref, o_ref, a_ref):
    d = D_MODEL
    mod = lambda j: mod_ref[:, j * d:(j + 1) * d]
    o = of_ref[...].astype(F32) + ob_ref[...].astype(F32)
    y = o * lax.rsqrt(_group_mean(o, bd_ref) + EPS) * gg_ref[...]
    y = y * _silu(r_ref[...].astype(F32))
    a0, a1, a2 = ATT_WIDTH, ATT_WIDTH + FNET_WIDTH, ATT_WIDTH + FNET_WIDTH + GLA_WIDTH
    mix = (_dot(att_ref[...], wmix_ref[0:a0]) + _dot(four_ref[...].astype(BF16), wmix_ref[a0:a1])
           + _dot(y.astype(BF16), wmix_ref[a1:a2]))
    x = x_ref[...] + mod(2) * mix
    ms = jnp.mean(x * x, axis=-1, keepdims=True)
    hb = ((x * lax.rsqrt(ms + EPS) * g2_ref[...]) * (1.0 + mod(4)) + mod(3)).astype(BF16)
    for c0 in range(0, FFN_HIDDEN, FFN_CHUNK):
        g = _dot(hb, wi_ref[:, c0:c0 + FFN_CHUNK])
        u = _dot(hb, wi_ref[:, FFN_HIDDEN + c0:FFN_HIDDEN + c0 + FFN_CHUNK])
        a_ref[:, c0:c0 + FFN_CHUNK] = (_silu(g) * u).astype(BF16)
    o_ref[...] = x + mod(5) * _dot(a_ref[...], wo_ref[...])


def _tail(att, four, of, ob, gr, x, mod3, mod_row, lw, tm):
    b, n, d = x.shape
    row = (lambda bi: bi) if mod_row is None else (lambda bi: mod_row)
    tok = lambda w: pl.BlockSpec((None, tm, w), lambda bi, i: (bi, i, 0))
    resident = lambda shape: _layer_block(shape, lw["layer"])
    return pl.pallas_call(
        _tail_kernel,
        out_shape=jax.ShapeDtypeStruct((b, n, d), F32),
        grid=(b, n // tm),
        in_specs=[tok(ATT_WIDTH), tok(FNET_WIDTH), tok(GLA_WIDTH), tok(GLA_WIDTH), tok(GLA_WIDTH), tok(d),
                  pl.BlockSpec((None, 1, 6 * d), lambda bi, i: (row(bi), 0, 0)),
                  _full((1, GLA_WIDTH)), _full((GLA_WIDTH, GLA_WIDTH)), resident((d, d)),
                  _full((1, d)), resident((d, 2 * FFN_HIDDEN)), resident((FFN_HIDDEN, d))],
        out_specs=tok(d),
        scratch_shapes=[pltpu.VMEM((tm, FFN_HIDDEN), BF16)],
        compiler_params=_params("parallel", "parallel"),
        name="mix_ffn",
    )(att, four, of, ob, gr, x, mod3, lw["gla_g"], lw["bd256"], lw["w_out"], lw["g2"], lw["w_ffn_in"],
      lw["w_ffn_out"])


def _rope_tables(n):
    axis_dim = HEAD_DIM // 2
    inv_freq = ROPE_BASE ** (-np.arange(0, axis_dim, 2, dtype=np.float64) / axis_dim)
    t = np.arange(n)
    ang_r = (t // GRID_W)[:, None] * inv_freq[None, :]
    ang_c = (t % GRID_W)[:, None] * inv_freq[None, :]
    cos = np.concatenate([np.cos(ang_r)] * 2 + [np.cos(ang_c)] * 2, axis=1)
    sin = np.concatenate([-np.sin(ang_r), np.sin(ang_r), -np.sin(ang_c), np.sin(ang_c)], axis=1)
    return jnp.asarray(np.tile(cos, (1, 2)), F32), jnp.asarray(np.tile(sin, (1, 2)), F32)


def _block_diag_mean(width):
    g = np.arange(width) // HEAD_DIM
    return jnp.asarray((g[:, None] == g[None, :]) / HEAD_DIM, BF16)


def _layer_weights(l, big, w_in, g_norm1, q_norm_g, k_norm_g, w_fourier, wgf, bgf, wgb, bgb, gla_norm_g, g_norm2):
    r = GLA_GATE_RANK
    w_z = jnp.zeros((D_MODEL, LANES), F32).at[:, 0:2 * r].set(w_in[l, :, MAIN_WIDTH:MAIN_WIDTH + 2 * r])
    w_gate = jnp.zeros((LANES, 2 * GLA_WIDTH), F32)
    w_gate = w_gate.at[0:r, 0:GLA_WIDTH].set(wgf[l]).at[r:2 * r, GLA_WIDTH:].set(wgb[l])
    wf_bd = jnp.zeros((FNET_WIDTH, FNET_WIDTH), F32)
    for g in range(FNET_GROUPS):
        wf_bd = wf_bd.at[HEAD_DIM * g:HEAD_DIM * (g + 1), HEAD_DIM * g:HEAD_DIM * (g + 1)].set(w_fourier[l, g])
    return {
        "g1": g_norm1[l][None, :],
        "layer": l,
        "w_main": big["w_in"],
        "w_z": w_z.astype(BF16),
        "w_gate": w_gate.astype(BF16),
        "b_gate": jnp.concatenate([bgf[l], bgb[l]])[None, :],
        "q_g": jnp.tile(q_norm_g[l], 2)[None, :],
        "k_g": jnp.tile(k_norm_g[l], 2)[None, :],
        "bd256": _block_diag_mean(GLA_WIDTH),
        "wf_bd": wf_bd.astype(BF16),
        "gla_g": jnp.tile(gla_norm_g[l], GLA_HEADS)[None, :],
        "w_out": big["w_out"],
        "g2": g_norm2[l][None, :],
        "w_ffn_in": big["w_ffn_in"],
        "w_ffn_out": big["w_ffn_out"],
    }


def kernel(x, c, ctx, c_ctx, w_mod, b_mod, g_norm1, w_in, q_norm_g, k_norm_g, attn_sink, w_fourier, gla_w_gate_f,
           gla_b_gate_f, gla_w_gate_b, gla_b_gate_b, gla_norm_g, w_out, g_norm2, w_ffn_in, w_ffn_out):
    b, n, d = x.shape
    lc = ctx.shape[1]
    depth = w_mod.shape[0]
    assert d == D_MODEL and b <= 7 and n % 512 == 0 and n % (DFT_N1 * DFT_STEP) == 0 and lc % GLA_CHUNK == 0

    cc = jnp.zeros((8, d), F32).at[0:b].set(c).at[b].set(c_ctx)
    mod = _modulation(cc, w_mod, b_mod)
    rope_tabs = _rope_tables(n)
    dft = _dft_consts(n)
    dft_ctx = _dft_ctx_consts(lc)
    tm_in, tm_tail, tb_lat = 1024, 512, 1024
    flat = lambda t: t.reshape(1, b * lc, t.shape[-1])
    unflat = lambda t: t.reshape(b, lc, t.shape[-1])
    xc = ctx
    big = {"w_in": w_in.astype(BF16), "w_out": w_out.astype(BF16), "w_ffn_in": w_ffn_in.astype(BF16),
           "w_ffn_out": w_ffn_out.astype(BF16)}
    for l in range(depth):
        need_ctx = l < depth - 1
        lw = _layer_weights(l, big, w_in, g_norm1, q_norm_g, k_norm_g, w_fourier, gla_w_gate_f, gla_b_gate_f,
                            gla_w_gate_b, gla_b_gate_b, gla_norm_g, g_norm2)
        mod3 = mod[l].reshape(8, 1, 6 * d)
        sink = attn_sink[l]
        cq, ck4, cv4, cfu, cgq, cgk, cgv, cgr, cla = map(unflat, _inproj(flat(xc), mod3, b, lw, None, b * lc))
        q, k4, v4, fu, gq, gk, gv, gr, la = _inproj(x, mod3, None, lw, rope_tabs, tm_in)
        att = _attention(q, k4, v4, ck4, cv4, sink)
        four = _fourier(fu, lw["wf_bd"], dft)
        s_zero = jnp.zeros((b, 2, GLA_WIDTH, LANES), F32)
        ocf, ocb, s_ctx = _gla(cgq, cgk, cgv, cla, s_zero, lc)
        olf, olb, _ = _gla(gq, gk, gv, la, s_ctx, tb_lat)
        x = _tail(att, four, olf, olb, gr, x, mod3, None, lw, tm_tail)
        if need_ctx:
            att_c = _attention_ctx(cq, ck4, cv4, sink)
            four_c = _fourier_ctx(cfu, lw["wf_bd"], dft_ctx)
            xc = unflat(_tail(*map(flat, (att_c, four_c, ocf, ocb, cgr, xc)), mod3, b, lw, b * lc))
    return x
```

```python
import functools

import numpy as np
import jax
import jax.numpy as jnp
from jax import lax
from jax.experimental import pallas as pl
from jax.experimental.pallas import tpu as pltpu

F32 = jnp.float32
BF16 = jnp.bfloat16

D_MODEL = 1024
HEAD_DIM = 64
GRID_W = 64
ROPE_BASE = 10000.0
ATT_HEADS = 8
ATT_KV_HEADS = 2
ATT_WIDTH = ATT_HEADS * HEAD_DIM
KV_WIDTH = ATT_KV_HEADS * HEAD_DIM
ATT_BLOCK = 128
ATT_QBLOCKS = 4
ATT_SCALE = HEAD_DIM ** -0.5
LOG2E = 1.4426950408889634
NEG_INF = -1e30
FNET_GROUPS = 4
FNET_WIDTH = FNET_GROUPS * HEAD_DIM
GLA_HEADS = 4
GLA_WIDTH = GLA_HEADS * HEAD_DIM
GLA_GATE_RANK = 16
GLA_TAU = 16.0
GLA_CHUNK = 64
GLA_SCALE = HEAD_DIM ** -0.5
MAIN_WIDTH = ATT_WIDTH + 2 * KV_WIDTH + FNET_WIDTH + 4 * GLA_WIDTH
FFN_HIDDEN = 2816
FFN_CHUNK = 256
INPROJ_SUB = 512
TAIL_SUB = 512
EPS = 1e-6
LANES = 128
DFT_N1 = 128
DFT_STEP = 16
VMEM_LIMIT = 56 * 1024 * 1024

NT_DIMS = (((1,), (1,)), ((), ()))
TN_DIMS = (((0,), (0,)), ((), ()))


def _params(*sem):
    return pltpu.CompilerParams(dimension_semantics=sem, vmem_limit_bytes=VMEM_LIMIT)


def _dot(a, b):
    return jnp.dot(a, b, preferred_element_type=F32)


def _silu(x):
    return x / (1.0 + jnp.exp(-x))


def _layer_block(shape, layer):
    nd = len(shape)
    return pl.BlockSpec((None,) + tuple(shape), lambda *_: (layer,) + (0,) * nd, pipeline_mode=pl.Buffered(1))


def _full(shape):
    nd = len(shape)
    return pl.BlockSpec(shape, lambda *_: (0,) * nd)


def _group_mean(t, bd_ref):
    return _dot((t * t).astype(BF16), bd_ref[...])


def _mod_kernel(c_ref, w_ref, b_ref, o_ref):
    s = _silu(c_ref[...]).astype(BF16)
    o_ref[...] = _dot(s, w_ref[...].astype(BF16)) + b_ref[...]


def _modulation(cc, w_mod, b_mod):
    depth, d, width = w_mod.shape
    tn = 1536
    return pl.pallas_call(
        _mod_kernel,
        out_shape=jax.ShapeDtypeStruct((depth, 8, width), F32),
        grid=(depth, width // tn),
        in_specs=[
            _full((8, d)),
            pl.BlockSpec((None, d, tn), lambda l, j: (l, 0, j)),
            pl.BlockSpec((None, 1, tn), lambda l, j: (l, 0, j)),
        ],
        out_specs=pl.BlockSpec((None, 8, tn), lambda l, j: (l, 0, j)),
        compiler_params=_params("parallel", "parallel"),
        name="modulation",
    )(cc, w_mod, b_mod.reshape(depth, 1, width))


def _inproj_kernel(*refs, rope):
    if rope:
        (x_ref, mod_ref, g1_ref, wm_ref, wz_ref, wg_ref, bg_ref, qg_ref, kg_ref, bd_ref, cos_ref, sin_ref,
         q_ref, k4_ref, v4_ref, fu_ref, gq_ref, gk_ref, gv_ref, gr_ref, la_ref) = refs
    else:
        (x_ref, mod_ref, g1_ref, wm_ref, wz_ref, wg_ref, bg_ref, qg_ref, kg_ref, bd_ref,
         q_ref, k4_ref, v4_ref, fu_ref, gq_ref, gk_ref, gv_ref, gr_ref, la_ref) = refs
    d = D_MODEL
    mod = mod_ref[...]
    sh, sc = mod[:, 0:d], mod[:, d:2 * d]
    sub = min(x_ref.shape[0], INPROJ_SUB)
    subs = [slice(sub * j, sub * (j + 1)) for j in range(x_ref.shape[0] // sub)]

    def normed(rs):
        x = x_ref[rs]
        ms = jnp.mean(x * x, axis=-1, keepdims=True)
        return ((x * lax.rsqrt(ms + EPS) * g1_ref[...]) * (1.0 + sc) + sh).astype(BF16)

    lane = lax.broadcasted_iota(jnp.int32, (1, LANES), 1)
    lo = lane < HEAD_DIM
    second_half = (lane & 16) != 0

    def head_norm(t, g_ref):
        wdt = t.shape[1]
        ms = _dot((t * t).astype(BF16), bd_ref[0:wdt, 0:wdt])
        g = g_ref[...] if wdt == LANES else jnp.concatenate([g_ref[...]] * (wdt // LANES), axis=1)
        return t * lax.rsqrt(ms + EPS) * g

    def rotary(t, rs):
        if not rope:
            return t
        partner = jnp.where(second_half, pltpu.roll(t, 16, 1), pltpu.roll(t, LANES - 16, 1))
        return t * cos_ref[rs] + partner * sin_ref[rs]

    def spread(t, out_ref, rs, idle):
        tr = pltpu.roll(t, HEAD_DIM, 1)
        fill = jnp.full_like(t, idle)
        out_ref[rs, 0:128] = jnp.where(lo, t, fill).astype(out_ref.dtype)
        out_ref[rs, 128:256] = jnp.where(lo, fill, tr).astype(out_ref.dtype)
        out_ref[rs, 256:384] = jnp.where(lo, tr, fill).astype(out_ref.dtype)
        out_ref[rs, 384:512] = jnp.where(lo, fill, t).astype(out_ref.dtype)

    assert (ATT_WIDTH, 2 * KV_WIDTH + FNET_WIDTH, 2 * GLA_WIDTH) == (4 * LANES,) * 3

    def project(rs, hb):
        group = lambda g: _dot(hb, wm_ref[:, 4 * LANES * g:4 * LANES * (g + 1)])
        gz = _dot(hb, wz_ref[...])
        a_q = group(0)
        z = _dot(gz.astype(BF16), wg_ref[...]) + bg_ref[...]
        log_sig = jnp.minimum(z, 0.0) - jnp.log(1.0 + jnp.exp(-jnp.abs(z)))
        la_ref[rs] = log_sig * (1.0 / GLA_TAU)
        a_kvf = group(1)
        for j2 in range(ATT_WIDTH // (2 * LANES)):
            qn = head_norm(a_q[:, 2 * LANES * j2:2 * LANES * (j2 + 1)], qg_ref)
            for j in range(2):
                t = rotary(qn[:, LANES * j:LANES * (j + 1)], rs) * (ATT_SCALE * LOG2E)
                q_ref[rs, LANES * (2 * j2 + j):LANES * (2 * j2 + j + 1)] = t.astype(q_ref.dtype)
        a_qk = group(2)
        spread(rotary(head_norm(a_kvf[:, 0:KV_WIDTH], kg_ref), rs), k4_ref, rs, 0.0)
        spread(a_kvf[:, KV_WIDTH:2 * KV_WIDTH], v4_ref, rs, 1.0)
        fu_ref[rs] = a_kvf[:, 2 * KV_WIDTH:].astype(fu_ref.dtype)
        a_vr = group(3)
        gq_ref[rs] = (a_qk[:, 0:GLA_WIDTH] * GLA_SCALE).astype(gq_ref.dtype)
        gk_ref[rs] = a_qk[:, GLA_WIDTH:].astype(gk_ref.dtype)
        gv_ref[rs] = a_vr[:, 0:GLA_WIDTH].astype(gv_ref.dtype)
        gr_ref[rs] = a_vr[:, GLA_WIDTH:].astype(gr_ref.dtype)

    hbs = [normed(rs) for rs in subs]
    for rs, hb in zip(subs, hbs):
        project(rs, hb)


def _inproj(x, mod3, mod_row, lw, rope_tabs, tm):
    b, n, d = x.shape
    rope = rope_tabs is not None
    row = (lambda bi: bi) if mod_row is None else (lambda bi: mod_row)
    tok = lambda w: pl.BlockSpec((None, tm, w), lambda bi, i: (bi, i, 0))
    in_specs = [
        tok(d),
        pl.BlockSpec((None, 1, 6 * d), lambda bi, i: (row(bi), 0, 0)),
        _full((1, d)), _layer_block((d, MAIN_WIDTH), lw["layer"]), _full((d, LANES)), _full((LANES, 2 * GLA_WIDTH)),
        _full((1, 2 * GLA_WIDTH)), _full((1, LANES)), _full((1, LANES)), _full((2 * LANES, 2 * LANES)),
    ]
    args = [x, mod3, lw["g1"], lw["w_main"], lw["w_z"], lw["w_gate"], lw["b_gate"], lw["q_g"], lw["k_g"], lw["bd256"]]
    if rope:
        in_specs += [pl.BlockSpec((tm, LANES), lambda bi, i: (i, 0))] * 2
        args += list(rope_tabs)
    sds = lambda w, dt: jax.ShapeDtypeStruct((b, n, w), dt)
    out_shape = [sds(ATT_WIDTH, BF16), sds(4 * LANES, BF16), sds(4 * LANES, BF16), sds(FNET_WIDTH, F32),
                 sds(GLA_WIDTH, BF16), sds(GLA_WIDTH, BF16), sds(GLA_WIDTH, BF16), sds(GLA_WIDTH, BF16),
                 sds(2 * GLA_WIDTH, F32)]
    out_specs = [tok(s.shape[-1]) for s in out_shape]
    return pl.pallas_call(
        functools.partial(_inproj_kernel, rope=rope),
        out_shape=out_shape, grid=(b, n // tm), in_specs=in_specs, out_specs=out_specs,
        compiler_params=_params("parallel", "parallel"),
        name="inproj_rope" if rope else "inproj_ctx",
    )(*args)


def _scores(q2, key_tiles, masks):
    cols = []
    for k, tile_masks in zip(key_tiles, masks):
        s = lax.dot_general(q2, k, NT_DIMS, preferred_element_type=F32)
        for j, mk in enumerate(tile_masks):
            c = s[:, LANES * j:LANES * (j + 1)]
            cols.append(c if mk is None else jnp.where(mk, c, NEG_INF))
    return cols


def _softmax_pv(cols, val_tiles, sink_col):
    mx = cols[0]
    for c in cols[1:]:
        mx = jnp.maximum(mx, c)
    m = jnp.maximum(jnp.max(mx, axis=-1, keepdims=True), sink_col)
    p = jnp.concatenate([jnp.exp2(c - m).astype(BF16) for c in cols], axis=1)
    pv = _dot(p, jnp.concatenate(val_tiles, axis=0))
    den = pltpu.roll(pv, HEAD_DIM, 1) + jnp.exp2(sink_col - m)
    return pv / den


def _attn_kernel(*refs):
    nq = ATT_QBLOCKS
    sink_ref, q_ref = refs[0:2]
    k_refs, kc_ref = refs[2:nq + 4], refs[nq + 4]
    v_refs, vc_ref = refs[nq + 5:2 * nq + 7], refs[2 * nq + 7]
    o_ref = refs[2 * nq + 8]
    i = pl.program_id(1)
    last = pl.num_programs(1) - 1
    blk = ATT_BLOCK
    group = ATT_HEADS // ATT_KV_HEADS
    row = lax.broadcasted_iota(jnp.int32, (2 * blk, blk), 0) & (blk - 1)
    col = lax.broadcasted_iota(jnp.int32, (2 * blk, blk), 1)
    band_l, band_r = col >= row, col <= row
    mask_l = [jnp.logical_and(band_l, i > 0) if s == 0 else band_l for s in range(nq)]
    mask_r = [jnp.logical_and(band_r, i < last) if s == nq - 1 else band_r for s in range(nq)]
    upper_rows = lax.broadcasted_iota(jnp.int32, (2 * blk, 1), 0) >= blk
    lo = lax.broadcasted_iota(jnp.int32, (1, LANES), 1) < HEAD_DIM

    def tiles(blocks, c, s, ks):
        l, m_, r = blocks[s:s + 3]
        return [jnp.concatenate([l[:, ks], m_[:, ks]], axis=0), jnp.concatenate([r[:, ks], c[0:blk, ks]], axis=0),
                c[blk:, ks]]

    slots = [(s, kvh, par) for s in range(nq) for kvh in range(ATT_KV_HEADS) for par in range(2)]
    lane_slice = lambda kvh, par: slice(2 * LANES * kvh + LANES * par, 2 * LANES * kvh + LANES * (par + 1))
    cols = {}
    for s, kvh, par in slots:
        qs, rs = 2 * LANES * kvh, slice(blk * s, blk * (s + 1))
        q2 = jnp.concatenate([q_ref[rs, qs:qs + LANES], q_ref[rs, qs + LANES:qs + 2 * LANES]], axis=0)
        cols[s, kvh, par] = _scores(q2, tiles(k_refs, kc_ref, s, lane_slice(kvh, par)),
                                    [(mask_l[s], None), (mask_r[s], None), (None,)])
    outs = {}
    for s, kvh, par in slots:
        base = group * kvh + par
        sink_col = jnp.where(upper_rows, sink_ref[base + 2], sink_ref[base]) * LOG2E
        outs[s, kvh, par] = _softmax_pv(cols[s, kvh, par], tiles(v_refs, vc_ref, s, lane_slice(kvh, par)), sink_col)
    for s in range(nq):
        for kvh in range(ATT_KV_HEADS):
            qs, r0 = 2 * LANES * kvh, blk * s
            o = jnp.where(lo, outs[s, kvh, 0], outs[s, kvh, 1])
            o_ref[r0:r0 + blk, qs:qs + LANES] = o[0:blk].astype(o_ref.dtype)
            o_ref[r0:r0 + blk, qs + LANES:qs + 2 * LANES] = o[blk:2 * blk].astype(o_ref.dtype)


def _attention(q, k4, v4, ck4, cv4, sink):
    b, n, _ = q.shape
    lc = ck4.shape[1]
    nq = ATT_QBLOCKS
    nb = n // ATT_BLOCK
    assert lc == 2 * ATT_BLOCK and nb % nq == 0
    qspec = pl.BlockSpec((None, nq * ATT_BLOCK, 4 * LANES), lambda bi, i: (bi, i, 0))
    kv = [pl.BlockSpec((None, ATT_BLOCK, 4 * LANES),
                       functools.partial(lambda bi, i, j: (bi, jnp.clip(nq * i + j - 1, 0, nb - 1), 0), j=j))
          for j in range(nq + 2)]
    ctx = pl.BlockSpec((None, lc, 4 * LANES), lambda bi, i: (bi, 0, 0))
    return pl.pallas_call(
        _attn_kernel,
        out_shape=jax.ShapeDtypeStruct((b, n, ATT_WIDTH), BF16),
        grid=(b, nb // nq),
        in_specs=[pl.BlockSpec(memory_space=pltpu.SMEM), qspec] + kv + [ctx] + kv + [ctx],
        out_specs=qspec,
        compiler_params=_params("parallel", "parallel"),
        name="window_attention",
    )(sink, q, *([k4] * (nq + 2)), ck4, *([v4] * (nq + 2)), cv4)


def _attn_ctx_kernel(sink_ref, q_ref, kc_ref, vc_ref, o_ref):
    kvh = pl.program_id(1)
    lc = q_ref.shape[0]
    q2 = jnp.concatenate([q_ref[:, 0:LANES], q_ref[:, LANES:2 * LANES]], axis=0)
    upper_rows = lax.broadcasted_iota(jnp.int32, (2 * lc, 1), 0) >= lc
    lo = lax.broadcasted_iota(jnp.int32, (1, LANES), 1) < HEAD_DIM
    outs = []
    for par in range(2):
        ks = slice(LANES * par, LANES * (par + 1))
        base = ATT_HEADS // ATT_KV_HEADS * kvh + par
        sink_col = jnp.where(upper_rows, sink_ref[base + 2], sink_ref[base]) * LOG2E
        cols = _scores(q2, [kc_ref[:, ks]], [(None,) * (lc // LANES)])
        outs.append(_softmax_pv(cols, [vc_ref[:, ks]], sink_col))
    o = jnp.where(lo, outs[0], outs[1])
    o_ref[:, 0:LANES] = o[0:lc].astype(o_ref.dtype)
    o_ref[:, LANES:2 * LANES] = o[lc:2 * lc].astype(o_ref.dtype)


def _attention_ctx(cq, ck4, cv4, sink):
    b, lc, _ = cq.shape
    gw = 2 * LANES
    spec = pl.BlockSpec((None, lc, gw), lambda bi, h: (bi, 0, h))
    return pl.pallas_call(
        _attn_ctx_kernel,
        out_shape=jax.ShapeDtypeStruct((b, lc, ATT_WIDTH), BF16),
        grid=(b, ATT_KV_HEADS),
        in_specs=[pl.BlockSpec(memory_space=pltpu.SMEM), spec, spec, spec],
        out_specs=spec,
        compiler_params=_params("parallel", "parallel"),
        name="context_attention",
    )(sink, cq, ck4, cv4)


def _dft_consts(n):
    n1, n2 = DFT_N1, n // DFT_N1
    ang = lambda a, m: 2.0 * np.pi * np.outer(np.arange(a), np.arange(a)) / m
    c = HEAD_DIM
    cc, sc = np.cos(ang(c, c)) / np.sqrt(c), np.sin(ang(c, c)) / np.sqrt(c)
    eye = np.eye(FNET_GROUPS)
    w_chan = np.concatenate([np.kron(eye, cc), -np.kron(eye, sc)], axis=1)
    c1, s1 = np.cos(ang(n1, n1)) / np.sqrt(n1), np.sin(ang(n1, n1)) / np.sqrt(n1)
    m1 = np.block([[c1, s1], [-s1, c1]])
    kk = np.arange(n1)[:, None, None] + n1 * np.arange(n2)[None, :, None]
    ph = 2.0 * np.pi * (kk * np.arange(n2)[None, None, :] % n) / n
    m3 = np.concatenate([np.cos(ph), np.sin(ph)], axis=2) / np.sqrt(n2)
    return tuple(jnp.asarray(a, F32) for a in (w_chan, m1, m3))


def _dft_ctx_consts(lc):
    ang = lambda a, m: 2.0 * np.pi * np.outer(np.arange(a), np.arange(a)) / m
    c = HEAD_DIM
    cc, sc = np.cos(ang(c, c)) / np.sqrt(c), np.sin(ang(c, c)) / np.sqrt(c)
    eye = np.eye(FNET_GROUPS)
    w_chan = np.concatenate([np.kron(eye, cc), -np.kron(eye, sc)], axis=1)
    cl, sl = np.cos(ang(lc, lc)) / np.sqrt(lc), np.sin(ang(lc, lc)) / np.sqrt(lc)
    return jnp.asarray(w_chan, F32), jnp.asarray(np.concatenate([cl, sl], axis=1), F32)


def _regroup_rows(x, outer):
    r, c = x.shape
    return jnp.swapaxes(x.reshape(outer, r // outer, c), 0, 1).reshape(r, c)


def _fourier_kernel(u_ref, wc_ref, m1_ref, m3_ref, wf_ref, y_ref, p_ref):
    w, t = FNET_WIDTH, DFT_STEP
    s = pl.program_id(1)
    n_a = p_ref.shape[1]
    n2 = n_a * t

    @pl.when(s < n_a)
    def _():
        u = u_ref[...].reshape(DFT_N1 * t, w).astype(BF16)
        z = _regroup_rows(_dot(u, wc_ref[...]), DFT_N1)
        a_re, a_im = [], []
        for tt in range(t):
            zt = z[DFT_N1 * tt:DFT_N1 * (tt + 1)]
            zs = jnp.concatenate([zt[:, 0:w], zt[:, w:2 * w]], axis=0).astype(BF16)
            a = _dot(m1_ref[...], zs)
            a_re.append(a[0:DFT_N1])
            a_im.append(a[DFT_N1:2 * DFT_N1])
        for part, rows in enumerate((a_re, a_im)):
            a = _regroup_rows(jnp.concatenate(rows, axis=0), t)
            p_ref[part, s] = a.reshape(DFT_N1, t, w)

    @pl.when(s >= n_a)
    def _():
        k1_0 = (s - n_a) * t
        ys = []
        for kk in range(t):
            parts = [p_ref[part, :, k1_0 + kk].reshape(n2, w) for part in range(2)]
            rhs = jnp.concatenate(parts, axis=0).astype(BF16)
            ys.append(_dot(m3_ref[kk].astype(BF16), rhs).astype(BF16))
        out = _dot(jnp.concatenate(ys, axis=0), wf_ref[...])
        y_ref[...] = _regroup_rows(out, t).reshape(n2, t, w)


def _fourier(fu, wf_bd, consts):
    b, n, w = fu.shape
    n1, n2, t = DFT_N1, n // DFT_N1, DFT_STEP
    n_a, n_b = n2 // t, n1 // t
    w_chan, m1, m3 = consts
    y = pl.pallas_call(
        _fourier_kernel,
        out_shape=jax.ShapeDtypeStruct((b, n2, n1, w), F32),
        grid=(b, n_a + n_b),
        in_specs=[pl.BlockSpec((None, n1, t, w), lambda bi, s: (bi, 0, jnp.minimum(s, n_a - 1), 0)),
                  _full((w, 2 * w)), _full((2 * n1, 2 * n1)),
                  pl.BlockSpec((t, n2, 2 * n2), lambda bi, s: (jnp.maximum(s - n_a, 0), 0, 0)), _full((w, w))],
        out_specs=pl.BlockSpec((None, n2, t, w), lambda bi, s: (bi, 0, jnp.maximum(s - n_a, 0), 0)),
        scratch_shapes=[pltpu.VMEM((2, n_a, n1, t, w), F32)],
        compiler_params=_params("parallel", "arbitrary"),
        name="fourier_mix",
    )(fu.reshape(b, n1, n2, w), w_chan.astype(BF16), m1.astype(BF16), m3, wf_bd)
    return y.reshape(b, n, w)


def _four_ctx_kernel(u_ref, wc_ref, m_ref, wf_ref, y_ref):
    w = FNET_WIDTH
    z = _dot(u_ref[...].astype(BF16), wc_ref[...])
    zs = jnp.concatenate([z[:, 0:w], z[:, w:2 * w]], axis=0).astype(BF16)
    y = _dot(m_ref[...], zs)
    y_ref[...] = _dot(y.astype(BF16), wf_ref[...]).astype(y_ref.dtype)


def _fourier_ctx(cfu, wf_bd, consts):
    b, lc, w = cfu.shape
    w_chan, m = consts[0].astype(BF16), consts[1].astype(BF16)
    return pl.pallas_call(
        _four_ctx_kernel,
        out_shape=jax.ShapeDtypeStruct((b, lc, w), BF16),
        grid=(b,),
        in_specs=[pl.BlockSpec((None, lc, w), lambda bi: (bi, 0, 0)), _full((w, 2 * w)), _full((lc, 2 * lc)),
                  _full((w, w))],
        out_specs=pl.BlockSpec((None, lc, w), lambda bi: (bi, 0, 0)),
        compiler_params=_params("parallel"),
        name="fourier_context",
    )(cfu, w_chan, m, wf_bd)


def _pair_stack(x, lo):
    zero = jnp.zeros_like(x)
    return jnp.concatenate([jnp.where(lo, x, zero), jnp.where(lo, zero, x)], axis=0)


def _gla_kernel(qf_ref, kf_ref, vf_ref, laf_ref, qb_ref, kb_ref, vb_ref, lab_ref, s0_ref,
                of_ref, ob_ref, sfin_ref, stf_ref, stb_ref):
    i = pl.program_id(1)
    c = GLA_CHUNK
    nchunk = qf_ref.shape[0] // c

    @pl.when(i == 0)
    def _():
        stf_ref[...] = s0_ref[0]
        stb_ref[...] = s0_ref[1]

    r64 = lax.broadcasted_iota(jnp.int32, (c, c), 0)
    c64 = lax.broadcasted_iota(jnp.int32, (c, c), 1)
    tri = ((c64 <= r64).astype(BF16), (c64 >= r64).astype(BF16))
    at = lax.broadcasted_iota(jnp.int32, (c, LANES), 0)
    as_ = lax.broadcasted_iota(jnp.int32, (c, LANES), 1) & (c - 1)
    att_mask = (as_ <= at, as_ >= at)
    lo = lax.broadcasted_iota(jnp.int32, (1, LANES), 1) < HEAD_DIM
    br = lax.broadcasted_iota(jnp.int32, (LANES, LANES), 0) < HEAD_DIM
    bc = lax.broadcasted_iota(jnp.int32, (LANES, LANES), 1) < HEAD_DIM
    bd_mask = br == bc
    pairs = [slice(LANES * p, LANES * (p + 1)) for p in range(GLA_WIDTH // LANES)]
    in_refs = ((qf_ref, kf_ref, vf_ref, laf_ref), (qb_ref, kb_ref, vb_ref, lab_ref))
    out_refs = (of_ref, ob_ref)
    items = [(d, step if d == 0 else nchunk - 1 - step) for step in range(nchunk) for d in range(2)]
    rows = lambda ch: slice(c * ch, c * (ch + 1))

    bcum = {}
    for d, ch in items:
        la = in_refs[d][3][rows(ch)]
        la_hi = la.astype(BF16)
        la_lo = (la - la_hi.astype(F32)).astype(BF16)
        bcum[d, ch] = _dot(tri[d], la_hi) + _dot(tri[d], la_lo)
    work = {}
    for d, ch in items:
        q_ref, k_ref, v_ref, _ = in_refs[d]
        b = bcum[d, ch]
        btot = b[0:1] if d == 1 else b[c - 1:c]
        k = k_ref[rows(ch)].astype(F32)
        q_in = (q_ref[rows(ch)].astype(F32) * jnp.exp(b)).astype(BF16)
        k_in = (k * jnp.exp(-b)).astype(BF16)
        k_out = (k * jnp.exp(btot - b)).astype(BF16)
        vb = v_ref[rows(ch)].astype(BF16)
        att = [lax.dot_general(q_in[:, p], _pair_stack(k_in[:, p], lo), NT_DIMS, preferred_element_type=F32)
               for p in pairs]
        work[d, ch] = (btot, q_in, k_out, vb, att)
    o_intra, ut = {}, {}
    for d, ch in items:
        btot, q_in, k_out, vb, att = work[d, ch]
        o_intra[d, ch] = [_dot(jnp.where(att_mask[d], a, 0.0).astype(BF16), _pair_stack(vb[:, p], lo))
                          for a, p in zip(att, pairs)]
        ut[d, ch] = [lax.dot_general(vb[:, p], k_out[:, p], TN_DIMS, preferred_element_type=F32) for p in pairs]
    st_refs = (stf_ref, stb_ref)
    st = [[st_refs[d][LANES * j:LANES * (j + 1)] for j in range(len(pairs))] for d in range(2)]
    for d, ch in items:
        btot, q_in = work[d, ch][0:2]
        outs = []
        for j, p in enumerate(pairs):
            o_inter = lax.dot_general(q_in[:, p], st[d][j].astype(BF16), NT_DIMS, preferred_element_type=F32)
            outs.append(o_intra[d, ch][j] + o_inter)
            st[d][j] = st[d][j] * jnp.exp(btot[:, p]) + jnp.where(bd_mask, ut[d, ch][j], 0.0)
        out_refs[d][rows(ch)] = jnp.concatenate(outs, axis=1).astype(out_refs[d].dtype)
    for d in range(2):
        st_refs[d][...] = jnp.concatenate(st[d], axis=0)

    @pl.when(i == pl.num_programs(1) - 1)
    def _():
        for d in range(2):
            sfin_ref[d] = jnp.concatenate(st[d], axis=0)


def _gla(gq, gk, gv, la, s0, tb):
    b, n, w = gq.shape
    nblk = n // tb
    fwd = lambda bi, i: (bi, i, 0)
    bwd = lambda bi, i: (bi, nblk - 1 - i, 0)
    bwd_la = lambda bi, i: (bi, nblk - 1 - i, 1)
    tok = lambda f: pl.BlockSpec((None, tb, w), f)
    state = pl.BlockSpec((None, 2, w, LANES), lambda bi, i: (bi, 0, 0, 0))
    return pl.pallas_call(
        _gla_kernel,
        out_shape=[jax.ShapeDtypeStruct((b, n, w), BF16), jax.ShapeDtypeStruct((b, n, w), BF16),
                   jax.ShapeDtypeStruct((b, 2, w, LANES), F32)],
        grid=(b, nblk),
        in_specs=[tok(fwd), tok(fwd), tok(fwd), tok(fwd), tok(bwd), tok(bwd), tok(bwd), tok(bwd_la), state],
        out_specs=[tok(fwd), tok(bwd), state],
        scratch_shapes=[pltpu.VMEM((w, LANES), F32), pltpu.VMEM((w, LANES), F32)],
        compiler_params=_params("parallel", "arbitrary"),
        name="gla_scan",
    )(gq, gk, gv, la, gq, gk, gv, la, s0)


def _tail_kernel(att_ref, four_ref, of_ref, ob_ref, r_ref, x_ref, mod_ref, gg_ref, bd_ref, wmix_ref, g2_ref, wi_ref,
                 wo_ref, o_ref, a_ref):
    d = D_MODEL
    mod = lambda j: mod_ref[:, j * d:(j + 1) * d]
    a0, a1, a2 = ATT_WIDTH, ATT_WIDTH + FNET_WIDTH, ATT_WIDTH + FNET_WIDTH + GLA_WIDTH
    sub = min(x_ref.shape[0], TAIL_SUB)
    subs = [slice(sub * j, sub * (j + 1)) for j in range(x_ref.shape[0] // sub)]

    def mixed(rs):
        o = of_ref[rs].astype(F32) + ob_ref[rs].astype(F32)
        y = o * lax.rsqrt(_group_mean(o, bd_ref) + EPS) * gg_ref[...]
        y = y * _silu(r_ref[rs].astype(F32))
        mix = (_dot(att_ref[rs], wmix_ref[0:a0]) + _dot(four_ref[rs].astype(BF16), wmix_ref[a0:a1])
               + _dot(y.astype(BF16), wmix_ref[a1:a2]))
        x = x_ref[rs] + mod(2) * mix
        ms = jnp.mean(x * x, axis=-1, keepdims=True)
        return x, ((x * lax.rsqrt(ms + EPS) * g2_ref[...]) * (1.0 + mod(4)) + mod(3)).astype(BF16)

    pre = [mixed(rs) for rs in subs]
    for rs, (x, hb) in zip(subs, pre):
        for c0 in range(0, FFN_HIDDEN, FFN_CHUNK):
            g = _dot(hb, wi_ref[:, c0:c0 + FFN_CHUNK])
            u = _dot(hb, wi_ref[:, FFN_HIDDEN + c0:FFN_HIDDEN + c0 + FFN_CHUNK])
            a_ref[rs, c0:c0 + FFN_CHUNK] = (_silu(g) * u).astype(BF16)
        o_ref[rs] = x + mod(5) * _dot(a_ref[rs], wo_ref[...])


def _tail(att, four, of, ob, gr, x, mod3, mod_row, lw, tm):
    b, n, d = x.shape
    row = (lambda bi: bi) if mod_row is None else (lambda bi: mod_row)
    tok = lambda w: pl.BlockSpec((None, tm, w), lambda bi, i: (bi, i, 0))
    resident = lambda shape: _layer_block(shape, lw["layer"])
    return pl.pallas_call(
        _tail_kernel,
        out_shape=jax.ShapeDtypeStruct((b, n, d), F32),
        grid=(b, n // tm),
        in_specs=[tok(ATT_WIDTH), tok(FNET_WIDTH), tok(GLA_WIDTH), tok(GLA_WIDTH), tok(GLA_WIDTH), tok(d),
                  pl.BlockSpec((None, 1, 6 * d), lambda bi, i: (row(bi), 0, 0)),
                  _full((1, GLA_WIDTH)), _full((GLA_WIDTH, GLA_WIDTH)), resident((d, d)),
                  _full((1, d)), resident((d, 2 * FFN_HIDDEN)), resident((FFN_HIDDEN, d))],
        out_specs=tok(d),
        scratch_shapes=[pltpu.VMEM((tm, FFN_HIDDEN), BF16)],
        compiler_params=_params("parallel", "parallel"),
        name="mix_ffn",
    )(att, four, of, ob, gr, x, mod3, lw["gla_g"], lw["bd256"], lw["w_out"], lw["g2"], lw["w_ffn_in"],
      lw["w_ffn_out"])


def _rope_tables(n):
    axis_dim = HEAD_DIM // 2
    inv_freq = ROPE_BASE ** (-np.arange(0, axis_dim, 2, dtype=np.float64) / axis_dim)
    t = np.arange(n)
    ang_r = (t // GRID_W)[:, None] * inv_freq[None, :]
    ang_c = (t % GRID_W)[:, None] * inv_freq[None, :]
    cos = np.concatenate([np.cos(ang_r)] * 2 + [np.cos(ang_c)] * 2, axis=1)
    sin = np.concatenate([-np.sin(ang_r), np.sin(ang_r), -np.sin(ang_c), np.sin(ang_c)], axis=1)
    return jnp.asarray(np.tile(cos, (1, 2)), F32), jnp.asarray(np.tile(sin, (1, 2)), F32)


def _block_diag_mean(width):
    g = np.arange(width) // HEAD_DIM
    return jnp.asarray((g[:, None] == g[None, :]) / HEAD_DIM, BF16)


def _layer_weights(l, big, w_in, g_norm1, q_norm_g, k_norm_g, w_fourier, wgf, bgf, wgb, bgb, gla_norm_g, g_norm2):
    r = GLA_GATE_RANK
    w_z = jnp.zeros((D_MODEL, LANES), F32).at[:, 0:2 * r].set(w_in[l, :, MAIN_WIDTH:MAIN_WIDTH + 2 * r])
    w_gate = jnp.zeros((LANES, 2 * GLA_WIDTH), F32)
    w_gate = w_gate.at[0:r, 0:GLA_WIDTH].set(wgf[l]).at[r:2 * r, GLA_WIDTH:].set(wgb[l])
    wf_bd = jnp.zeros((FNET_WIDTH, FNET_WIDTH), F32)
    for g in range(FNET_GROUPS):
        wf_bd = wf_bd.at[HEAD_DIM * g:HEAD_DIM * (g + 1), HEAD_DIM * g:HEAD_DIM * (g + 1)].set(w_fourier[l, g])
    return {
        "g1": g_norm1[l][None, :],
        "layer": l,
        "w_main": big["w_in"],
        "w_z": w_z.astype(BF16),
        "w_gate": w_gate.astype(BF16),
        "b_gate": jnp.concatenate([bgf[l], bgb[l]])[None, :],
        "q_g": jnp.tile(q_norm_g[l], 2)[None, :],
        "k_g": jnp.tile(k_norm_g[l], 2)[None, :],
        "bd256": _block_diag_mean(GLA_WIDTH),
        "wf_bd": wf_bd.astype(BF16),
        "gla_g": jnp.tile(gla_norm_g[l], GLA_HEADS)[None, :],
        "w_out": big["w_out"],
        "g2": g_norm2[l][None, :],
        "w_ffn_in": big["w_ffn_in"],
        "w_ffn_out": big["w_ffn_out"],
    }


def kernel(x, c, ctx, c_ctx, w_mod, b_mod, g_norm1, w_in, q_norm_g, k_norm_g, attn_sink, w_fourier, gla_w_gate_f,
           gla_b_gate_f, gla_w_gate_b, gla_b_gate_b, gla_norm_g, w_out, g_norm2, w_ffn_in, w_ffn_out):
    b, n, d = x.shape
    lc = ctx.shape[1]
    depth = w_mod.shape[0]
    assert d == D_MODEL and b <= 7 and n % 512 == 0 and n % (DFT_N1 * DFT_STEP) == 0 and lc % GLA_CHUNK == 0

    cc = jnp.zeros((8, d), F32).at[0:b].set(c).at[b].set(c_ctx)
    mod = _modulation(cc, w_mod, b_mod)
    rope_tabs = _rope_tables(n)
    dft = _dft_consts(n)
    dft_ctx = _dft_ctx_consts(lc)
    tm_in, tm_tail, tb_lat = 1024, 1024, 1024
    flat = lambda t: t.reshape(1, b * lc, t.shape[-1])
    unflat = lambda t: t.reshape(b, lc, t.shape[-1])
    xc = ctx
    big = {"w_in": w_in.astype(BF16), "w_out": w_out.astype(BF16), "w_ffn_in": w_ffn_in.astype(BF16),
           "w_ffn_out": w_ffn_out.astype(BF16)}
    for l in range(depth):
        need_ctx = l < depth - 1
        lw = _layer_weights(l, big, w_in, g_norm1, q_norm_g, k_norm_g, w_fourier, gla_w_gate_f, gla_b_gate_f,
                            gla_w_gate_b, gla_b_gate_b, gla_norm_g, g_norm2)
        mod3 = mod[l].reshape(8, 1, 6 * d)
        sink = attn_sink[l]
        cq, ck4, cv4, cfu, cgq, cgk, cgv, cgr, cla = map(unflat, _inproj(flat(xc), mod3, b, lw, None, b * lc))
        q, k4, v4, fu, gq, gk, gv, gr, la = _inproj(x, mod3, None, lw, rope_tabs, tm_in)
        att = _attention(q, k4, v4, ck4, cv4, sink)
        four = _fourier(fu, lw["wf_bd"], dft)
        s_zero = jnp.zeros((b, 2, GLA_WIDTH, LANES), F32)
        ocf, ocb, s_ctx = _gla(cgq, cgk, cgv, cla, s_zero, lc)
        olf, olb, _ = _gla(gq, gk, gv, la, s_ctx, tb_lat)
        x = _tail(att, four, olf, olb, gr, x, mod3, None, lw, tm_tail)
        if need_ctx:
            att_c = _attention_ctx(cq, ck4, cv4, sink)
            four_c = _fourier_ctx(cfu, lw["wf_bd"], dft_ctx)
            xc = unflat(_tail(*map(flat, (att_c, four_c, ocf, ocb, cgr, xc)), mod3, b, lw, b * lc))
    return x
```

```python
import functools

import numpy as np
import jax
import jax.numpy as jnp
from jax import lax
from jax.experimental import pallas as pl
from jax.experimental.pallas import tpu as pltpu

F32 = jnp.float32
BF16 = jnp.bfloat16

D_MODEL = 1024
HEAD_DIM = 64
GRID_W = 64
ROPE_BASE = 10000.0
ATT_HEADS = 8
ATT_KV_HEADS = 2
ATT_WIDTH = ATT_HEADS * HEAD_DIM
KV_WIDTH = ATT_KV_HEADS * HEAD_DIM
ATT_BLOCK = 128
ATT_QBLOCKS = 4
ATT_SCALE = HEAD_DIM ** -0.5
LOG2E = 1.4426950408889634
NEG_INF = -1e30
FNET_GROUPS = 4
FNET_WIDTH = FNET_GROUPS * HEAD_DIM
GLA_HEADS = 4
GLA_WIDTH = GLA_HEADS * HEAD_DIM
GLA_GATE_RANK = 16
GLA_TAU = 16.0
GLA_CHUNK = 64
GLA_SCALE = HEAD_DIM ** -0.5
MAIN_WIDTH = ATT_WIDTH + 2 * KV_WIDTH + FNET_WIDTH + 4 * GLA_WIDTH
FFN_HIDDEN = 2816
FFN_CHUNK = 256
INPROJ_SUB = 512
TAIL_SUB = 512
GLA_GROUP = 16
EPS = 1e-6
LANES = 128
DFT_N1 = 128
DFT_STEP = 16
VMEM_LIMIT = 56 * 1024 * 1024

NT_DIMS = (((1,), (1,)), ((), ()))
TN_DIMS = (((0,), (0,)), ((), ()))


def _params(*sem):
    return pltpu.CompilerParams(dimension_semantics=sem, vmem_limit_bytes=VMEM_LIMIT)


def _dot(a, b):
    return jnp.dot(a, b, preferred_element_type=F32)


def _silu(x):
    return x / (1.0 + jnp.exp(-x))


def _layer_block(shape, layer):
    nd = len(shape)
    return pl.BlockSpec((None,) + tuple(shape), lambda *_: (layer,) + (0,) * nd, pipeline_mode=pl.Buffered(1))


def _full(shape):
    nd = len(shape)
    return pl.BlockSpec(shape, lambda *_: (0,) * nd)


def _group_mean(t, bd_ref):
    return _dot((t * t).astype(BF16), bd_ref[...])


def _mod_kernel(c_ref, w_ref, b_ref, o_ref):
    s = _silu(c_ref[...]).astype(BF16)
    o_ref[...] = _dot(s, w_ref[...].astype(BF16)) + b_ref[...]


def _modulation(cc, w_mod, b_mod):
    depth, d, width = w_mod.shape
    tn = 1536
    return pl.pallas_call(
        _mod_kernel,
        out_shape=jax.ShapeDtypeStruct((depth, 8, width), F32),
        grid=(depth, width // tn),
        in_specs=[
            _full((8, d)),
            pl.BlockSpec((None, d, tn), lambda l, j: (l, 0, j)),
            pl.BlockSpec((None, 1, tn), lambda l, j: (l, 0, j)),
        ],
        out_specs=pl.BlockSpec((None, 8, tn), lambda l, j: (l, 0, j)),
        compiler_params=_params("parallel", "parallel"),
        name="modulation",
    )(cc, w_mod, b_mod.reshape(depth, 1, width))


def _inproj_kernel(*refs, rope):
    if rope:
        (x_ref, mod_ref, g1_ref, wm_ref, wz_ref, wg_ref, bg_ref, qg_ref, kg_ref, bd_ref, cos_ref, sin_ref,
         q_ref, k4_ref, v4_ref, fu_ref, gq_ref, gk_ref, gv_ref, gr_ref, la_ref) = refs
    else:
        (x_ref, mod_ref, g1_ref, wm_ref, wz_ref, wg_ref, bg_ref, qg_ref, kg_ref, bd_ref,
         q_ref, k4_ref, v4_ref, fu_ref, gq_ref, gk_ref, gv_ref, gr_ref, la_ref) = refs
    d = D_MODEL
    mod = mod_ref[...]
    sh, sc = mod[:, 0:d], mod[:, d:2 * d]
    sub = min(x_ref.shape[0], INPROJ_SUB)
    subs = [slice(sub * j, sub * (j + 1)) for j in range(x_ref.shape[0] // sub)]

    def normed(rs):
        x = x_ref[rs]
        ms = jnp.mean(x * x, axis=-1, keepdims=True)
        return ((x * lax.rsqrt(ms + EPS) * g1_ref[...]) * (1.0 + sc) + sh).astype(BF16)

    lane = lax.broadcasted_iota(jnp.int32, (1, LANES), 1)
    lo = lane < HEAD_DIM
    second_half = (lane & 16) != 0

    def head_norm(t, g_ref):
        wdt = t.shape[1]
        ms = _dot((t * t).astype(BF16), bd_ref[0:wdt, 0:wdt])
        g = g_ref[...] if wdt == LANES else jnp.concatenate([g_ref[...]] * (wdt // LANES), axis=1)
        return t * lax.rsqrt(ms + EPS) * g

    def rotary(t, rs):
        if not rope:
            return t
        partner = jnp.where(second_half, pltpu.roll(t, 16, 1), pltpu.roll(t, LANES - 16, 1))
        return t * cos_ref[rs] + partner * sin_ref[rs]

    def spread(t, out_ref, rs, idle):
        tr = pltpu.roll(t, HEAD_DIM, 1)
        fill = jnp.full_like(t, idle)
        out_ref[rs, 0:128] = jnp.where(lo, t, fill).astype(out_ref.dtype)
        out_ref[rs, 128:256] = jnp.where(lo, fill, tr).astype(out_ref.dtype)
        out_ref[rs, 256:384] = jnp.where(lo, tr, fill).astype(out_ref.dtype)
        out_ref[rs, 384:512] = jnp.where(lo, fill, t).astype(out_ref.dtype)

    assert (ATT_WIDTH, 2 * KV_WIDTH + FNET_WIDTH, 2 * GLA_WIDTH) == (4 * LANES,) * 3

    def project(rs, hb):
        group = lambda g: _dot(hb, wm_ref[:, 4 * LANES * g:4 * LANES * (g + 1)])
        gz = _dot(hb, wz_ref[...])
        a_q = group(0)
        z = _dot(gz.astype(BF16), wg_ref[...]) + bg_ref[...]
        log_sig = jnp.minimum(z, 0.0) - jnp.log(1.0 + jnp.exp(-jnp.abs(z)))
        la_ref[rs] = log_sig * (LOG2E / GLA_TAU)
        a_kvf = group(1)
        for j2 in range(ATT_WIDTH // (2 * LANES)):
            qn = head_norm(a_q[:, 2 * LANES * j2:2 * LANES * (j2 + 1)], qg_ref)
            for j in range(2):
                t = rotary(qn[:, LANES * j:LANES * (j + 1)], rs) * (ATT_SCALE * LOG2E)
                q_ref[rs, LANES * (2 * j2 + j):LANES * (2 * j2 + j + 1)] = t.astype(q_ref.dtype)
        a_qk = group(2)
        spread(rotary(head_norm(a_kvf[:, 0:KV_WIDTH], kg_ref), rs), k4_ref, rs, 0.0)
        spread(a_kvf[:, KV_WIDTH:2 * KV_WIDTH], v4_ref, rs, 1.0)
        fu_ref[rs] = a_kvf[:, 2 * KV_WIDTH:].astype(fu_ref.dtype)
        a_vr = group(3)
        gq_ref[rs] = (a_qk[:, 0:GLA_WIDTH] * GLA_SCALE).astype(gq_ref.dtype)
        gk_ref[rs] = a_qk[:, GLA_WIDTH:].astype(gk_ref.dtype)
        gv_ref[rs] = a_vr[:, 0:GLA_WIDTH].astype(gv_ref.dtype)
        gr_ref[rs] = a_vr[:, GLA_WIDTH:].astype(gr_ref.dtype)

    hbs = [normed(rs) for rs in subs]
    for rs, hb in zip(subs, hbs):
        project(rs, hb)


def _inproj(x, mod3, mod_row, lw, rope_tabs, tm):
    b, n, d = x.shape
    rope = rope_tabs is not None
    row = (lambda bi: bi) if mod_row is None else (lambda bi: mod_row)
    tok = lambda w: pl.BlockSpec((None, tm, w), lambda bi, i: (bi, i, 0))
    in_specs = [
        tok(d),
        pl.BlockSpec((None, 1, 6 * d), lambda bi, i: (row(bi), 0, 0)),
        _full((1, d)), _layer_block((d, MAIN_WIDTH), lw["layer"]), _full((d, LANES)), _full((LANES, 2 * GLA_WIDTH)),
        _full((1, 2 * GLA_WIDTH)), _full((1, LANES)), _full((1, LANES)), _full((2 * LANES, 2 * LANES)),
    ]
    args = [x, mod3, lw["g1"], lw["w_main"], lw["w_z"], lw["w_gate"], lw["b_gate"], lw["q_g"], lw["k_g"], lw["bd256"]]
    if rope:
        in_specs += [pl.BlockSpec((tm, LANES), lambda bi, i: (i, 0))] * 2
        args += list(rope_tabs)
    sds = lambda w, dt: jax.ShapeDtypeStruct((b, n, w), dt)
    out_shape = [sds(ATT_WIDTH, BF16), sds(4 * LANES, BF16), sds(4 * LANES, BF16), sds(FNET_WIDTH, F32),
                 sds(GLA_WIDTH, BF16), sds(GLA_WIDTH, BF16), sds(GLA_WIDTH, BF16), sds(GLA_WIDTH, BF16),
                 sds(2 * GLA_WIDTH, F32)]
    out_specs = [tok(s.shape[-1]) for s in out_shape]
    return pl.pallas_call(
        functools.partial(_inproj_kernel, rope=rope),
        out_shape=out_shape, grid=(b, n // tm), in_specs=in_specs, out_specs=out_specs,
        compiler_params=_params("parallel", "parallel"),
        name="inproj_rope" if rope else "inproj_ctx",
    )(*args)


def _scores(q2, key_tiles, masks):
    cols = []
    for k, tile_masks in zip(key_tiles, masks):
        s = lax.dot_general(q2, k, NT_DIMS, preferred_element_type=F32)
        for j, mk in enumerate(tile_masks):
            c = s[:, LANES * j:LANES * (j + 1)]
            cols.append(c if mk is None else jnp.where(mk, c, NEG_INF))
    return cols


def _softmax_pv(cols, val_tiles, sink_col):
    mx = cols[0]
    for c in cols[1:]:
        mx = jnp.maximum(mx, c)
    m = jnp.maximum(jnp.max(mx, axis=-1, keepdims=True), sink_col)
    p = jnp.concatenate([jnp.exp2(c - m).astype(BF16) for c in cols], axis=1)
    pv = _dot(p, jnp.concatenate(val_tiles, axis=0))
    den = pltpu.roll(pv, HEAD_DIM, 1) + jnp.exp2(sink_col - m)
    return pv / den


def _attn_kernel(*refs):
    nq = ATT_QBLOCKS
    sink_ref, q_ref = refs[0:2]
    k_refs, kc_ref = refs[2:nq + 4], refs[nq + 4]
    v_refs, vc_ref = refs[nq + 5:2 * nq + 7], refs[2 * nq + 7]
    o_ref = refs[2 * nq + 8]
    i = pl.program_id(1)
    last = pl.num_programs(1) - 1
    blk = ATT_BLOCK
    group = ATT_HEADS // ATT_KV_HEADS
    row = lax.broadcasted_iota(jnp.int32, (2 * blk, blk), 0) & (blk - 1)
    col = lax.broadcasted_iota(jnp.int32, (2 * blk, blk), 1)
    band_l, band_r = col >= row, col <= row
    mask_l = [jnp.logical_and(band_l, i > 0) if s == 0 else band_l for s in range(nq)]
    mask_r = [jnp.logical_and(band_r, i < last) if s == nq - 1 else band_r for s in range(nq)]
    upper_rows = lax.broadcasted_iota(jnp.int32, (2 * blk, 1), 0) >= blk
    lo = lax.broadcasted_iota(jnp.int32, (1, LANES), 1) < HEAD_DIM

    def tiles(blocks, c, s, ks):
        l, m_, r = blocks[s:s + 3]
        return [jnp.concatenate([l[:, ks], m_[:, ks]], axis=0), jnp.concatenate([r[:, ks], c[0:blk, ks]], axis=0),
                c[blk:, ks]]

    slots = [(s, kvh, par) for s in range(nq) for kvh in range(ATT_KV_HEADS) for par in range(2)]
    lane_slice = lambda kvh, par: slice(2 * LANES * kvh + LANES * par, 2 * LANES * kvh + LANES * (par + 1))
    cols = {}
    for s, kvh, par in slots:
        qs, rs = 2 * LANES * kvh, slice(blk * s, blk * (s + 1))
        q2 = jnp.concatenate([q_ref[rs, qs:qs + LANES], q_ref[rs, qs + LANES:qs + 2 * LANES]], axis=0)
        cols[s, kvh, par] = _scores(q2, tiles(k_refs, kc_ref, s, lane_slice(kvh, par)),
                                    [(mask_l[s], None), (mask_r[s], None), (None,)])
    outs = {}
    for s, kvh, par in slots:
        base = group * kvh + par
        sink_col = jnp.where(upper_rows, sink_ref[base + 2], sink_ref[base]) * LOG2E
        outs[s, kvh, par] = _softmax_pv(cols[s, kvh, par], tiles(v_refs, vc_ref, s, lane_slice(kvh, par)), sink_col)
    for s in range(nq):
        for kvh in range(ATT_KV_HEADS):
            qs, r0 = 2 * LANES * kvh, blk * s
            o = jnp.where(lo, outs[s, kvh, 0], outs[s, kvh, 1])
            o_ref[r0:r0 + blk, qs:qs + LANES] = o[0:blk].astype(o_ref.dtype)
            o_ref[r0:r0 + blk, qs + LANES:qs + 2 * LANES] = o[blk:2 * blk].astype(o_ref.dtype)


def _attention(q, k4, v4, ck4, cv4, sink):
    b, n, _ = q.shape
    lc = ck4.shape[1]
    nq = ATT_QBLOCKS
    nb = n // ATT_BLOCK
    assert lc == 2 * ATT_BLOCK and nb % nq == 0
    qspec = pl.BlockSpec((None, nq * ATT_BLOCK, 4 * LANES), lambda bi, i: (bi, i, 0))
    kv = [pl.BlockSpec((None, ATT_BLOCK, 4 * LANES),
                       functools.partial(lambda bi, i, j: (bi, jnp.clip(nq * i + j - 1, 0, nb - 1), 0), j=j))
          for j in range(nq + 2)]
    ctx = pl.BlockSpec((None, lc, 4 * LANES), lambda bi, i: (bi, 0, 0))
    return pl.pallas_call(
        _attn_kernel,
        out_shape=jax.ShapeDtypeStruct((b, n, ATT_WIDTH), BF16),
        grid=(b, nb // nq),
        in_specs=[pl.BlockSpec(memory_space=pltpu.SMEM), qspec] + kv + [ctx] + kv + [ctx],
        out_specs=qspec,
        compiler_params=_params("parallel", "parallel"),
        name="window_attention",
    )(sink, q, *([k4] * (nq + 2)), ck4, *([v4] * (nq + 2)), cv4)


def _attn_ctx_kernel(sink_ref, q_ref, kc_ref, vc_ref, o_ref):
    kvh = pl.program_id(1)
    lc = q_ref.shape[0]
    q2 = jnp.concatenate([q_ref[:, 0:LANES], q_ref[:, LANES:2 * LANES]], axis=0)
    upper_rows = lax.broadcasted_iota(jnp.int32, (2 * lc, 1), 0) >= lc
    lo = lax.broadcasted_iota(jnp.int32, (1, LANES), 1) < HEAD_DIM
    outs = []
    for par in range(2):
        ks = slice(LANES * par, LANES * (par + 1))
        base = ATT_HEADS // ATT_KV_HEADS * kvh + par
        sink_col = jnp.where(upper_rows, sink_ref[base + 2], sink_ref[base]) * LOG2E
        cols = _scores(q2, [kc_ref[:, ks]], [(None,) * (lc // LANES)])
        outs.append(_softmax_pv(cols, [vc_ref[:, ks]], sink_col))
    o = jnp.where(lo, outs[0], outs[1])
    o_ref[:, 0:LANES] = o[0:lc].astype(o_ref.dtype)
    o_ref[:, LANES:2 * LANES] = o[lc:2 * lc].astype(o_ref.dtype)


def _attention_ctx(cq, ck4, cv4, sink):
    b, lc, _ = cq.shape
    gw = 2 * LANES
    spec = pl.BlockSpec((None, lc, gw), lambda bi, h: (bi, 0, h))
    return pl.pallas_call(
        _attn_ctx_kernel,
        out_shape=jax.ShapeDtypeStruct((b, lc, ATT_WIDTH), BF16),
        grid=(b, ATT_KV_HEADS),
        in_specs=[pl.BlockSpec(memory_space=pltpu.SMEM), spec, spec, spec],
        out_specs=spec,
        compiler_params=_params("parallel", "parallel"),
        name="context_attention",
    )(sink, cq, ck4, cv4)


def _dft_consts(n):
    n1, n2 = DFT_N1, n // DFT_N1
    ang = lambda a, m: 2.0 * np.pi * np.outer(np.arange(a), np.arange(a)) / m
    c = HEAD_DIM
    cc, sc = np.cos(ang(c, c)) / np.sqrt(c), np.sin(ang(c, c)) / np.sqrt(c)
    eye = np.eye(FNET_GROUPS)
    w_chan = np.concatenate([np.kron(eye, cc), -np.kron(eye, sc)], axis=1)
    c1, s1 = np.cos(ang(n1, n1)) / np.sqrt(n1), np.sin(ang(n1, n1)) / np.sqrt(n1)
    m1 = np.block([[c1, s1], [-s1, c1]])
    kk = np.arange(n1)[:, None, None] + n1 * np.arange(n2)[None, :, None]
    ph = 2.0 * np.pi * (kk * np.arange(n2)[None, None, :] % n) / n
    m3 = np.concatenate([np.cos(ph), np.sin(ph)], axis=2) / np.sqrt(n2)
    return tuple(jnp.asarray(a, F32) for a in (w_chan, m1, m3))


def _dft_ctx_consts(lc):
    ang = lambda a, m: 2.0 * np.pi * np.outer(np.arange(a), np.arange(a)) / m
    c = HEAD_DIM
    cc, sc = np.cos(ang(c, c)) / np.sqrt(c), np.sin(ang(c, c)) / np.sqrt(c)
    eye = np.eye(FNET_GROUPS)
    w_chan = np.concatenate([np.kron(eye, cc), -np.kron(eye, sc)], axis=1)
    cl, sl = np.cos(ang(lc, lc)) / np.sqrt(lc), np.sin(ang(lc, lc)) / np.sqrt(lc)
    return jnp.asarray(w_chan, F32), jnp.asarray(np.concatenate([cl, sl], axis=1), F32)


def _regroup_rows(x, outer):
    r, c = x.shape
    return jnp.swapaxes(x.reshape(outer, r // outer, c), 0, 1).reshape(r, c)


def _fourier_kernel(u_ref, wc_ref, m1_ref, m3_ref, wf_ref, y_ref, p_ref):
    w, t = FNET_WIDTH, DFT_STEP
    s = pl.program_id(1)
    n_a = p_ref.shape[1]
    n2 = n_a * t

    @pl.when(s < n_a)
    def _():
        u = u_ref[...].reshape(DFT_N1 * t, w).astype(BF16)
        z = _regroup_rows(_dot(u, wc_ref[...]), DFT_N1)
        a_re, a_im = [], []
        for tt in range(t):
            zt = z[DFT_N1 * tt:DFT_N1 * (tt + 1)]
            zs = jnp.concatenate([zt[:, 0:w], zt[:, w:2 * w]], axis=0).astype(BF16)
            a = _dot(m1_ref[...], zs)
            a_re.append(a[0:DFT_N1])
            a_im.append(a[DFT_N1:2 * DFT_N1])
        for part, rows in enumerate((a_re, a_im)):
            a = _regroup_rows(jnp.concatenate(rows, axis=0), t)
            p_ref[part, s] = a.reshape(DFT_N1, t, w)

    @pl.when(s >= n_a)
    def _():
        k1_0 = (s - n_a) * t
        ys = []
        for kk in range(t):
            parts = [p_ref[part, :, k1_0 + kk].reshape(n2, w) for part in range(2)]
            rhs = jnp.concatenate(parts, axis=0).astype(BF16)
            ys.append(_dot(m3_ref[kk].astype(BF16), rhs).astype(BF16))
        out = _dot(jnp.concatenate(ys, axis=0), wf_ref[...])
        y_ref[...] = _regroup_rows(out, t).reshape(n2, t, w)


def _fourier(fu, wf_bd, consts):
    b, n, w = fu.shape
    n1, n2, t = DFT_N1, n // DFT_N1, DFT_STEP
    n_a, n_b = n2 // t, n1 // t
    w_chan, m1, m3 = consts
    y = pl.pallas_call(
        _fourier_kernel,
        out_shape=jax.ShapeDtypeStruct((b, n2, n1, w), F32),
        grid=(b, n_a + n_b),
        in_specs=[pl.BlockSpec((None, n1, t, w), lambda bi, s: (bi, 0, jnp.minimum(s, n_a - 1), 0)),
                  _full((w, 2 * w)), _full((2 * n1, 2 * n1)),
                  pl.BlockSpec((t, n2, 2 * n2), lambda bi, s: (jnp.maximum(s - n_a, 0), 0, 0)), _full((w, w))],
        out_specs=pl.BlockSpec((None, n2, t, w), lambda bi, s: (bi, 0, jnp.maximum(s - n_a, 0), 0)),
        scratch_shapes=[pltpu.VMEM((2, n_a, n1, t, w), F32)],
        compiler_params=_params("parallel", "arbitrary"),
        name="fourier_mix",
    )(fu.reshape(b, n1, n2, w), w_chan.astype(BF16), m1.astype(BF16), m3, wf_bd)
    return y.reshape(b, n, w)


def _four_ctx_kernel(u_ref, wc_ref, m_ref, wf_ref, y_ref):
    w = FNET_WIDTH
    z = _dot(u_ref[...].astype(BF16), wc_ref[...])
    zs = jnp.concatenate([z[:, 0:w], z[:, w:2 * w]], axis=0).astype(BF16)
    y = _dot(m_ref[...], zs)
    y_ref[...] = _dot(y.astype(BF16), wf_ref[...]).astype(y_ref.dtype)


def _fourier_ctx(cfu, wf_bd, consts):
    b, lc, w = cfu.shape
    w_chan, m = consts[0].astype(BF16), consts[1].astype(BF16)
    return pl.pallas_call(
        _four_ctx_kernel,
        out_shape=jax.ShapeDtypeStruct((b, lc, w), BF16),
        grid=(b,),
        in_specs=[pl.BlockSpec((None, lc, w), lambda bi: (bi, 0, 0)), _full((w, 2 * w)), _full((lc, 2 * lc)),
                  _full((w, w))],
        out_specs=pl.BlockSpec((None, lc, w), lambda bi: (bi, 0, 0)),
        compiler_params=_params("parallel"),
        name="fourier_context",
    )(cfu, w_chan, m, wf_bd)


def _pair_stack(x, lo):
    zero = jnp.zeros_like(x)
    return jnp.concatenate([jnp.where(lo, x, zero), jnp.where(lo, zero, x)], axis=0)


def _gla_kernel(qf_ref, kf_ref, vf_ref, laf_ref, qb_ref, kb_ref, vb_ref, lab_ref, s0_ref,
                of_ref, ob_ref, sfin_ref, stf_ref, stb_ref):
    i = pl.program_id(1)
    c = GLA_CHUNK
    nchunk = qf_ref.shape[0] // c

    @pl.when(i == 0)
    def _():
        stf_ref[...] = s0_ref[0]
        stb_ref[...] = s0_ref[1]

    r64 = lax.broadcasted_iota(jnp.int32, (c, c), 0)
    c64 = lax.broadcasted_iota(jnp.int32, (c, c), 1)
    tri = ((c64 <= r64).astype(BF16), (c64 >= r64).astype(BF16))
    at = lax.broadcasted_iota(jnp.int32, (c, LANES), 0)
    as_ = lax.broadcasted_iota(jnp.int32, (c, LANES), 1) & (c - 1)
    att_mask = (as_ <= at, as_ >= at)
    lo = lax.broadcasted_iota(jnp.int32, (1, LANES), 1) < HEAD_DIM
    br = lax.broadcasted_iota(jnp.int32, (LANES, LANES), 0) < HEAD_DIM
    bc = lax.broadcasted_iota(jnp.int32, (LANES, LANES), 1) < HEAD_DIM
    bd_mask = br == bc
    pairs = [slice(LANES * p, LANES * (p + 1)) for p in range(GLA_WIDTH // LANES)]
    in_refs = ((qf_ref, kf_ref, vf_ref, laf_ref), (qb_ref, kb_ref, vb_ref, lab_ref))
    out_refs = (of_ref, ob_ref)
    items = [(d, step if d == 0 else nchunk - 1 - step) for step in range(nchunk) for d in range(2)]
    rows = lambda ch: slice(c * ch, c * (ch + 1))

    bcum, work, o_intra, ut, decay = {}, {}, {}, {}, {}
    st_refs = (stf_ref, stb_ref)
    st = [[st_refs[d][LANES * j:LANES * (j + 1)] for j in range(len(pairs))] for d in range(2)]

    def gate_sums(d, ch):
        la = in_refs[d][3][rows(ch)]
        la_hi = la.astype(BF16)
        la_lo = (la - la_hi.astype(F32)).astype(BF16)
        bcum[d, ch] = _dot(jnp.concatenate([tri[d], tri[d]], axis=1), jnp.concatenate([la_hi, la_lo], axis=0))

    def scores(d, ch):
        q_ref, k_ref, v_ref, _ = in_refs[d]
        b = bcum[d, ch]
        btot = b[0:1] if d == 1 else b[c - 1:c]
        k = k_ref[rows(ch)].astype(F32)
        q_in = (q_ref[rows(ch)].astype(F32) * jnp.exp2(b)).astype(BF16)
        k_in = (k * jnp.exp2(-b)).astype(BF16)
        k_out = (k * jnp.exp2(btot - b)).astype(BF16)
        vb = v_ref[rows(ch)].astype(BF16)
        att = [lax.dot_general(q_in[:, p], _pair_stack(k_in[:, p], lo), NT_DIMS, preferred_element_type=F32)
               for p in pairs]
        work[d, ch] = (btot, q_in, k_out, vb, att)

    def intra(d, ch):
        btot, q_in, k_out, vb, att = work[d, ch]
        o_intra[d, ch] = [_dot(jnp.where(att_mask[d], a, 0.0).astype(BF16), _pair_stack(vb[:, p], lo))
                          for a, p in zip(att, pairs)]
        ut[d, ch] = [lax.dot_general(k_out[:, p], vb[:, p], TN_DIMS, preferred_element_type=F32) for p in pairs]
        decay[d, ch] = [jnp.transpose(jnp.exp2(btot[:, p])) for p in pairs]

    def carry(d, ch):
        btot, q_in = work[d, ch][0:2]
        outs = []
        for j, p in enumerate(pairs):
            outs.append(o_intra[d, ch][j] + _dot(q_in[:, p], st[d][j].astype(BF16)))
            st[d][j] = st[d][j] * decay[d, ch][j] + jnp.where(bd_mask, ut[d, ch][j], 0.0)
        out_refs[d][rows(ch)] = jnp.concatenate(outs, axis=1).astype(out_refs[d].dtype)

    groups = [items[g:g + GLA_GROUP] for g in range(0, len(items), GLA_GROUP)]
    for it in groups[0]:
        gate_sums(*it)
    for it in groups[0]:
        scores(*it)
    for g, grp in enumerate(groups):
        for it in grp:
            intra(*it)
        nxt = groups[g + 1] if g + 1 < len(groups) else []
        for it in nxt:
            gate_sums(*it)
        for k_ in range(len(grp)):
            carry(*grp[k_])
            if k_ < len(nxt):
                scores(*nxt[k_])
    for d in range(2):
        st_refs[d][...] = jnp.concatenate(st[d], axis=0)

    @pl.when(i == pl.num_programs(1) - 1)
    def _():
        for d in range(2):
            sfin_ref[d] = jnp.concatenate(st[d], axis=0)


def _gla(gq, gk, gv, la, s0, tb):
    b, n, w = gq.shape
    nblk = n // tb
    fwd = lambda bi, i: (bi, i, 0)
    bwd = lambda bi, i: (bi, nblk - 1 - i, 0)
    bwd_la = lambda bi, i: (bi, nblk - 1 - i, 1)
    tok = lambda f: pl.BlockSpec((None, tb, w), f)
    state = pl.BlockSpec((None, 2, w, LANES), lambda bi, i: (bi, 0, 0, 0))
    return pl.pallas_call(
        _gla_kernel,
        out_shape=[jax.ShapeDtypeStruct((b, n, w), BF16), jax.ShapeDtypeStruct((b, n, w), BF16),
                   jax.ShapeDtypeStruct((b, 2, w, LANES), F32)],
        grid=(b, nblk),
        in_specs=[tok(fwd), tok(fwd), tok(fwd), tok(fwd), tok(bwd), tok(bwd), tok(bwd), tok(bwd_la), state],
        out_specs=[tok(fwd), tok(bwd), state],
        scratch_shapes=[pltpu.VMEM((w, LANES), F32), pltpu.VMEM((w, LANES), F32)],
        compiler_params=_params("parallel", "arbitrary"),
        name="gla_scan",
    )(gq, gk, gv, la, gq, gk, gv, la, s0)


def _tail_kernel(att_ref, four_ref, of_ref, ob_ref, r_ref, x_ref, mod_ref, gg_ref, bd_ref, wmix_ref, g2_ref, wi_ref,
                 wo_ref, o_ref, a_ref):
    d = D_MODEL
    mod = lambda j: mod_ref[:, j * d:(j + 1) * d]
    a0, a1, a2 = ATT_WIDTH, ATT_WIDTH + FNET_WIDTH, ATT_WIDTH + FNET_WIDTH + GLA_WIDTH
    sub = min(x_ref.shape[0], TAIL_SUB)
    subs = [slice(sub * j, sub * (j + 1)) for j in range(x_ref.shape[0] // sub)]

    def mixed(rs):
        o = of_ref[rs].astype(F32) + ob_ref[rs].astype(F32)
        y = o * lax.rsqrt(_group_mean(o, bd_ref) + EPS) * gg_ref[...]
        y = y * _silu(r_ref[rs].astype(F32))
        mix = (_dot(att_ref[rs], wmix_ref[0:a0]) + _dot(four_ref[rs].astype(BF16), wmix_ref[a0:a1])
               + _dot(y.astype(BF16), wmix_ref[a1:a2]))
        x = x_ref[rs] + mod(2) * mix
        ms = jnp.mean(x * x, axis=-1, keepdims=True)
        return x, ((x * lax.rsqrt(ms + EPS) * g2_ref[...]) * (1.0 + mod(4)) + mod(3)).astype(BF16)

    pre = [mixed(rs) for rs in subs]
    for rs, (x, hb) in zip(subs, pre):
        for c0 in range(0, FFN_HIDDEN, FFN_CHUNK):
            g = _dot(hb, wi_ref[:, c0:c0 + FFN_CHUNK])
            u = _dot(hb, wi_ref[:, FFN_HIDDEN + c0:FFN_HIDDEN + c0 + FFN_CHUNK])
            a_ref[rs, c0:c0 + FFN_CHUNK] = (_silu(g) * u).astype(BF16)
        o_ref[rs] = x + mod(5) * _dot(a_ref[rs], wo_ref[...])


def _tail(att, four, of, ob, gr, x, mod3, mod_row, lw, tm):
    b, n, d = x.shape
    row = (lambda bi: bi) if mod_row is None else (lambda bi: mod_row)
    tok = lambda w: pl.BlockSpec((None, tm, w), lambda bi, i: (bi, i, 0))
    resident = lambda shape: _layer_block(shape, lw["layer"])
    return pl.pallas_call(
        _tail_kernel,
        out_shape=jax.ShapeDtypeStruct((b, n, d), F32),
        grid=(b, n // tm),
        in_specs=[tok(ATT_WIDTH), tok(FNET_WIDTH), tok(GLA_WIDTH), tok(GLA_WIDTH), tok(GLA_WIDTH), tok(d),
                  pl.BlockSpec((None, 1, 6 * d), lambda bi, i: (row(bi), 0, 0)),
                  _full((1, GLA_WIDTH)), _full((GLA_WIDTH, GLA_WIDTH)), resident((d, d)),
                  _full((1, d)), resident((d, 2 * FFN_HIDDEN)), resident((FFN_HIDDEN, d))],
        out_specs=tok(d),
        scratch_shapes=[pltpu.VMEM((tm, FFN_HIDDEN), BF16)],
        compiler_params=_params("parallel", "parallel"),
        name="mix_ffn",
    )(att, four, of, ob, gr, x, mod3, lw["gla_g"], lw["bd256"], lw["w_out"], lw["g2"], lw["w_ffn_in"],
      lw["w_ffn_out"])


def _rope_tables(n):
    axis_dim = HEAD_DIM // 2
    inv_freq = ROPE_BASE ** (-np.arange(0, axis_dim, 2, dtype=np.float64) / axis_dim)
    t = np.arange(n)
    ang_r = (t // GRID_W)[:, None] * inv_freq[None, :]
    ang_c = (t % GRID_W)[:, None] * inv_freq[None, :]
    cos = np.concatenate([np.cos(ang_r)] * 2 + [np.cos(ang_c)] * 2, axis=1)
    sin = np.concatenate([-np.sin(ang_r), np.sin(ang_r), -np.sin(ang_c), np.sin(ang_c)], axis=1)
    return jnp.asarray(np.tile(cos, (1, 2)), F32), jnp.asarray(np.tile(sin, (1, 2)), F32)


def _block_diag_mean(width):
    g = np.arange(width) // HEAD_DIM
    return jnp.asarray((g[:, None] == g[None, :]) / HEAD_DIM, BF16)


def _layer_weights(l, big, w_in, g_norm1, q_norm_g, k_norm_g, w_fourier, wgf, bgf, wgb, bgb, gla_norm_g, g_norm2):
    r = GLA_GATE_RANK
    w_z = jnp.zeros((D_MODEL, LANES), F32).at[:, 0:2 * r].set(w_in[l, :, MAIN_WIDTH:MAIN_WIDTH + 2 * r])
    w_gate = jnp.zeros((LANES, 2 * GLA_WIDTH), F32)
    w_gate = w_gate.at[0:r, 0:GLA_WIDTH].set(wgf[l]).at[r:2 * r, GLA_WIDTH:].set(wgb[l])
    wf_bd = jnp.zeros((FNET_WIDTH, FNET_WIDTH), F32)
    for g in range(FNET_GROUPS):
        wf_bd = wf_bd.at[HEAD_DIM * g:HEAD_DIM * (g + 1), HEAD_DIM * g:HEAD_DIM * (g + 1)].set(w_fourier[l, g])
    return {
        "g1": g_norm1[l][None, :],
        "layer": l,
        "w_main": big["w_in"],
        "w_z": w_z.astype(BF16),
        "w_gate": w_gate.astype(BF16),
        "b_gate": jnp.concatenate([bgf[l], bgb[l]])[None, :],
        "q_g": jnp.tile(q_norm_g[l], 2)[None, :],
        "k_g": jnp.tile(k_norm_g[l], 2)[None, :],
        "bd256": _block_diag_mean(GLA_WIDTH),
        "wf_bd": wf_bd.astype(BF16),
        "gla_g": jnp.tile(gla_norm_g[l], GLA_HEADS)[None, :],
        "w_out": big["w_out"],
        "g2": g_norm2[l][None, :],
        "w_ffn_in": big["w_ffn_in"],
        "w_ffn_out": big["w_ffn_out"],
    }


def kernel(x, c, ctx, c_ctx, w_mod, b_mod, g_norm1, w_in, q_norm_g, k_norm_g, attn_sink, w_fourier, gla_w_gate_f,
           gla_b_gate_f, gla_w_gate_b, gla_b_gate_b, gla_norm_g, w_out, g_norm2, w_ffn_in, w_ffn_out):
    b, n, d = x.shape
    lc = ctx.shape[1]
    depth = w_mod.shape[0]
    assert d == D_MODEL and b <= 7 and n % 512 == 0 and n % (DFT_N1 * DFT_STEP) == 0 and lc % GLA_CHUNK == 0

    cc = jnp.zeros((8, d), F32).at[0:b].set(c).at[b].set(c_ctx)
    mod = _modulation(cc, w_mod, b_mod)
    rope_tabs = _rope_tables(n)
    dft = _dft_consts(n)
    dft_ctx = _dft_ctx_consts(lc)
    tm_in, tm_tail, tb_lat = 1024, 1024, 1024
    flat = lambda t: t.reshape(1, b * lc, t.shape[-1])
    unflat = lambda t: t.reshape(b, lc, t.shape[-1])
    xc = ctx
    big = {"w_in": w_in.astype(BF16), "w_out": w_out.astype(BF16), "w_ffn_in": w_ffn_in.astype(BF16),
           "w_ffn_out": w_ffn_out.astype(BF16)}
    for l in range(depth):
        need_ctx = l < depth - 1
        lw = _layer_weights(l, big, w_in, g_norm1, q_norm_g, k_norm_g, w_fourier, gla_w_gate_f, gla_b_gate_f,
                            gla_w_gate_b, gla_b_gate_b, gla_norm_g, g_norm2)
        mod3 = mod[l].reshape(8, 1, 6 * d)
        sink = attn_sink[l]
        cq, ck4, cv4, cfu, cgq, cgk, cgv, cgr, cla = map(unflat, _inproj(flat(xc), mod3, b, lw, None, b * lc))
        q, k4, v4, fu, gq, gk, gv, gr, la = _inproj(x, mod3, None, lw, rope_tabs, tm_in)
        att = _attention(q, k4, v4, ck4, cv4, sink)
        four = _fourier(fu, lw["wf_bd"], dft)
        s_zero = jnp.zeros((b, 2, GLA_WIDTH, LANES), F32)
        ocf, ocb, s_ctx = _gla(cgq, cgk, cgv, cla, s_zero, lc)
        olf, olb, _ = _gla(gq, gk, gv, la, s_ctx, tb_lat)
        x = _tail(att, four, olf, olb, gr, x, mod3, None, lw, tm_tail)
        if need_ctx:
            att_c = _attention_ctx(cq, ck4, cv4, sink)
            four_c = _fourier_ctx(cfu, lw["wf_bd"], dft_ctx)
            xc = unflat(_tail(*map(flat, (att_c, four_c, ocf, ocb, cgr, xc)), mod3, b, lw, b * lc))
    return x
```

```python
import functools

import numpy as np
import jax
import jax.numpy as jnp
from jax import lax
from jax.experimental import pallas as pl
from jax.experimental.pallas import tpu as pltpu

F32 = jnp.float32
BF16 = jnp.bfloat16

D_MODEL = 1024
HEAD_DIM = 64
GRID_W = 64
ROPE_BASE = 10000.0
ATT_HEADS = 8
ATT_KV_HEADS = 2
ATT_WIDTH = ATT_HEADS * HEAD_DIM
KV_WIDTH = ATT_KV_HEADS * HEAD_DIM
ATT_BLOCK = 128
ATT_QBLOCKS = 8
ATT_SCALE = HEAD_DIM ** -0.5
LOG2E = 1.4426950408889634
NEG_INF = -1e30
FNET_GROUPS = 4
FNET_WIDTH = FNET_GROUPS * HEAD_DIM
GLA_HEADS = 4
GLA_WIDTH = GLA_HEADS * HEAD_DIM
GLA_GATE_RANK = 16
GLA_TAU = 16.0
GLA_CHUNK = 64
GLA_SCALE = HEAD_DIM ** -0.5
MAIN_WIDTH = ATT_WIDTH + 2 * KV_WIDTH + FNET_WIDTH + 4 * GLA_WIDTH
FFN_HIDDEN = 2816
FFN_CHUNK = 256
INPROJ_SUB = 512
TAIL_SUB = 512
GLA_GROUP = 16
EPS = 1e-6
LANES = 128
DFT_N1 = 128
DFT_STEP = 16
VMEM_LIMIT = 56 * 1024 * 1024

NT_DIMS = (((1,), (1,)), ((), ()))
TN_DIMS = (((0,), (0,)), ((), ()))


def _params(*sem):
    return pltpu.CompilerParams(dimension_semantics=sem, vmem_limit_bytes=VMEM_LIMIT)


def _dot(a, b):
    return jnp.dot(a, b, preferred_element_type=F32)


def _silu(x):
    return x / (1.0 + jnp.exp(-x))


def _layer_block(shape, layer):
    nd = len(shape)
    return pl.BlockSpec((None,) + tuple(shape), lambda *_: (layer,) + (0,) * nd, pipeline_mode=pl.Buffered(1))


def _full(shape):
    nd = len(shape)
    return pl.BlockSpec(shape, lambda *_: (0,) * nd)


def _group_mean(t, bd_ref):
    return _dot((t * t).astype(BF16), bd_ref[...])


def _mod_kernel(c_ref, w_ref, b_ref, o_ref):
    s = _silu(c_ref[...]).astype(BF16)
    o_ref[...] = _dot(s, w_ref[...].astype(BF16)) + b_ref[...]


def _modulation(cc, w_mod, b_mod):
    depth, d, width = w_mod.shape
    tn = 1536
    return pl.pallas_call(
        _mod_kernel,
        out_shape=jax.ShapeDtypeStruct((depth, 8, width), F32),
        grid=(depth, width // tn),
        in_specs=[
            _full((8, d)),
            pl.BlockSpec((None, d, tn), lambda l, j: (l, 0, j)),
            pl.BlockSpec((None, 1, tn), lambda l, j: (l, 0, j)),
        ],
        out_specs=pl.BlockSpec((None, 8, tn), lambda l, j: (l, 0, j)),
        compiler_params=_params("parallel", "parallel"),
        name="modulation",
    )(cc, w_mod, b_mod.reshape(depth, 1, width))


def _inproj_kernel(*refs, rope):
    if rope:
        (x_ref, mod_ref, g1_ref, wm_ref, wz_ref, wg_ref, bg_ref, qg_ref, kg_ref, bd_ref, cos_ref, sin_ref,
         q_ref, k4_ref, v4_ref, fu_ref, gqkv_ref, gr_ref, la_ref) = refs
    else:
        (x_ref, mod_ref, g1_ref, wm_ref, wz_ref, wg_ref, bg_ref, qg_ref, kg_ref, bd_ref,
         q_ref, k4_ref, v4_ref, fu_ref, gqkv_ref, gr_ref, la_ref) = refs
    d = D_MODEL
    mod = mod_ref[...]
    sh, sc = mod[:, 0:d], mod[:, d:2 * d]
    sub = min(x_ref.shape[0], INPROJ_SUB)
    subs = [slice(sub * j, sub * (j + 1)) for j in range(x_ref.shape[0] // sub)]

    def normed(rs):
        x = x_ref[rs]
        ms = jnp.mean(x * x, axis=-1, keepdims=True)
        return ((x * lax.rsqrt(ms + EPS) * g1_ref[...]) * (1.0 + sc) + sh).astype(BF16)

    lane = lax.broadcasted_iota(jnp.int32, (1, LANES), 1)
    lo = lane < HEAD_DIM
    second_half = (lane & 16) != 0

    def head_norm(t, g_ref):
        wdt = t.shape[1]
        ms = _dot((t * t).astype(BF16), bd_ref[0:wdt, 0:wdt])
        g = g_ref[...] if wdt == LANES else jnp.concatenate([g_ref[...]] * (wdt // LANES), axis=1)
        return t * lax.rsqrt(ms + EPS) * g

    def rotary(t, rs):
        if not rope:
            return t
        partner = jnp.where(second_half, pltpu.roll(t, 16, 1), pltpu.roll(t, LANES - 16, 1))
        return t * cos_ref[rs] + partner * sin_ref[rs]

    def spread(t, out_ref, rs, idle):
        tr = pltpu.roll(t, HEAD_DIM, 1)
        fill = jnp.full_like(t, idle)
        out_ref[rs, 0:128] = jnp.where(lo, t, fill).astype(out_ref.dtype)
        out_ref[rs, 128:256] = jnp.where(lo, fill, tr).astype(out_ref.dtype)
        out_ref[rs, 256:384] = jnp.where(lo, tr, fill).astype(out_ref.dtype)
        out_ref[rs, 384:512] = jnp.where(lo, fill, t).astype(out_ref.dtype)

    assert (ATT_WIDTH, 2 * KV_WIDTH + FNET_WIDTH, 2 * GLA_WIDTH) == (4 * LANES,) * 3

    def project(rs, hb):
        group = lambda g: _dot(hb, wm_ref[:, 4 * LANES * g:4 * LANES * (g + 1)])
        gz = _dot(hb, wz_ref[...])
        a_q = group(0)
        z = _dot(gz.astype(BF16), wg_ref[...]) + bg_ref[...]
        log_sig = jnp.minimum(z, 0.0) - jnp.log(1.0 + jnp.exp(-jnp.abs(z)))
        la_ref[rs] = log_sig * (LOG2E / GLA_TAU)
        a_kvf = group(1)
        for j2 in range(ATT_WIDTH // (2 * LANES)):
            qn = head_norm(a_q[:, 2 * LANES * j2:2 * LANES * (j2 + 1)], qg_ref)
            for j in range(2):
                t = rotary(qn[:, LANES * j:LANES * (j + 1)], rs) * (ATT_SCALE * LOG2E)
                q_ref[rs, LANES * (2 * j2 + j):LANES * (2 * j2 + j + 1)] = t.astype(q_ref.dtype)
        a_qk = group(2)
        spread(rotary(head_norm(a_kvf[:, 0:KV_WIDTH], kg_ref), rs), k4_ref, rs, 0.0)
        spread(a_kvf[:, KV_WIDTH:2 * KV_WIDTH], v4_ref, rs, 1.0)
        fu_ref[rs] = a_kvf[:, 2 * KV_WIDTH:].astype(fu_ref.dtype)
        a_vr = group(3)
        gqkv_ref[rs, 0:GLA_WIDTH] = (a_qk[:, 0:GLA_WIDTH] * GLA_SCALE).astype(gqkv_ref.dtype)
        gqkv_ref[rs, GLA_WIDTH:2 * GLA_WIDTH] = a_qk[:, GLA_WIDTH:].astype(gqkv_ref.dtype)
        gqkv_ref[rs, 2 * GLA_WIDTH:3 * GLA_WIDTH] = a_vr[:, 0:GLA_WIDTH].astype(gqkv_ref.dtype)
        gr_ref[rs] = a_vr[:, GLA_WIDTH:].astype(gr_ref.dtype)

    hbs = [normed(rs) for rs in subs]
    for rs, hb in zip(subs, hbs):
        project(rs, hb)


def _inproj(x, mod3, mod_row, lw, rope_tabs, tm):
    b, n, d = x.shape
    rope = rope_tabs is not None
    row = (lambda bi: bi) if mod_row is None else (lambda bi: mod_row)
    tok = lambda w: pl.BlockSpec((None, tm, w), lambda bi, i: (bi, i, 0))
    in_specs = [
        tok(d),
        pl.BlockSpec((None, 1, 6 * d), lambda bi, i: (row(bi), 0, 0)),
        _full((1, d)), _layer_block((d, MAIN_WIDTH), lw["layer"]), _full((d, LANES)), _full((LANES, 2 * GLA_WIDTH)),
        _full((1, 2 * GLA_WIDTH)), _full((1, LANES)), _full((1, LANES)), _full((2 * LANES, 2 * LANES)),
    ]
    args = [x, mod3, lw["g1"], lw["w_main"], lw["w_z"], lw["w_gate"], lw["b_gate"], lw["q_g"], lw["k_g"], lw["bd256"]]
    if rope:
        in_specs += [pl.BlockSpec((tm, LANES), lambda bi, i: (i, 0))] * 2
        args += list(rope_tabs)
    sds = lambda w, dt: jax.ShapeDtypeStruct((b, n, w), dt)
    out_shape = [sds(ATT_WIDTH, BF16), sds(4 * LANES, BF16), sds(4 * LANES, BF16), sds(FNET_WIDTH, F32),
                 sds(3 * GLA_WIDTH, BF16), sds(GLA_WIDTH, BF16), sds(2 * GLA_WIDTH, F32)]
    out_specs = [tok(s.shape[-1]) for s in out_shape]
    return pl.pallas_call(
        functools.partial(_inproj_kernel, rope=rope),
        out_shape=out_shape, grid=(b, n // tm), in_specs=in_specs, out_specs=out_specs,
        compiler_params=_params("parallel", "parallel"),
        name="inproj_rope" if rope else "inproj_ctx",
    )(*args)


def _scores(q2, key_tiles, masks):
    cols = []
    for k, tile_masks in zip(key_tiles, masks):
        s = lax.dot_general(q2, k, NT_DIMS, preferred_element_type=F32)
        for j, mk in enumerate(tile_masks):
            c = s[:, LANES * j:LANES * (j + 1)]
            cols.append(c if mk is None else jnp.where(mk, c, NEG_INF))
    return cols


def _softmax_pv(cols, val_tiles, sink_col):
    mx = cols[0]
    for c in cols[1:]:
        mx = jnp.maximum(mx, c)
    m = jnp.maximum(jnp.max(mx, axis=-1, keepdims=True), sink_col)
    p = jnp.concatenate([jnp.exp2(c - m).astype(BF16) for c in cols], axis=1)
    pv = _dot(p, jnp.concatenate(val_tiles, axis=0))
    den = pltpu.roll(pv, HEAD_DIM, 1) + jnp.exp2(sink_col - m)
    return pv / den


def _attn_kernel(*refs):
    nq = ATT_QBLOCKS
    sink_ref, q_ref = refs[0:2]
    k_refs, kc_ref = refs[2:nq + 4], refs[nq + 4]
    v_refs, vc_ref = refs[nq + 5:2 * nq + 7], refs[2 * nq + 7]
    o_ref = refs[2 * nq + 8]
    i = pl.program_id(1)
    last = pl.num_programs(1) - 1
    blk = ATT_BLOCK
    group = ATT_HEADS // ATT_KV_HEADS
    row = lax.broadcasted_iota(jnp.int32, (2 * blk, blk), 0) & (blk - 1)
    col = lax.broadcasted_iota(jnp.int32, (2 * blk, blk), 1)
    band_l, band_r = col >= row, col <= row
    mask_l = [jnp.logical_and(band_l, i > 0) if s == 0 else band_l for s in range(nq)]
    mask_r = [jnp.logical_and(band_r, i < last) if s == nq - 1 else band_r for s in range(nq)]
    upper_rows = lax.broadcasted_iota(jnp.int32, (2 * blk, 1), 0) >= blk
    lo = lax.broadcasted_iota(jnp.int32, (1, LANES), 1) < HEAD_DIM

    def tiles(blocks, c, s, ks):
        l, m_, r = blocks[s:s + 3]
        return [jnp.concatenate([l[:, ks], m_[:, ks]], axis=0), jnp.concatenate([r[:, ks], c[0:blk, ks]], axis=0),
                c[blk:, ks]]

    slots = [(s, kvh, par) for s in range(nq) for kvh in range(ATT_KV_HEADS) for par in range(2)]
    lane_slice = lambda kvh, par: slice(2 * LANES * kvh + LANES * par, 2 * LANES * kvh + LANES * (par + 1))
    cols = {}
    for s, kvh, par in slots:
        qs, rs = 2 * LANES * kvh, slice(blk * s, blk * (s + 1))
        q2 = jnp.concatenate([q_ref[rs, qs:qs + LANES], q_ref[rs, qs + LANES:qs + 2 * LANES]], axis=0)
        cols[s, kvh, par] = _scores(q2, tiles(k_refs, kc_ref, s, lane_slice(kvh, par)),
                                    [(mask_l[s], None), (mask_r[s], None), (None,)])
    outs = {}
    for s, kvh, par in slots:
        base = group * kvh + par
        sink_col = jnp.where(upper_rows, sink_ref[base + 2], sink_ref[base]) * LOG2E
        outs[s, kvh, par] = _softmax_pv(cols[s, kvh, par], tiles(v_refs, vc_ref, s, lane_slice(kvh, par)), sink_col)
    for s in range(nq):
        for kvh in range(ATT_KV_HEADS):
            qs, r0 = 2 * LANES * kvh, blk * s
            o = jnp.where(lo, outs[s, kvh, 0], outs[s, kvh, 1])
            o_ref[r0:r0 + blk, qs:qs + LANES] = o[0:blk].astype(o_ref.dtype)
            o_ref[r0:r0 + blk, qs + LANES:qs + 2 * LANES] = o[blk:2 * blk].astype(o_ref.dtype)


def _attention(q, k4, v4, ck4, cv4, sink):
    b, n, _ = q.shape
    lc = ck4.shape[1]
    nq = ATT_QBLOCKS
    nb = n // ATT_BLOCK
    assert lc == 2 * ATT_BLOCK and nb % nq == 0
    qspec = pl.BlockSpec((None, nq * ATT_BLOCK, 4 * LANES), lambda bi, i: (bi, i, 0))
    kv = [pl.BlockSpec((None, ATT_BLOCK, 4 * LANES),
                       functools.partial(lambda bi, i, j: (bi, jnp.clip(nq * i + j - 1, 0, nb - 1), 0), j=j))
          for j in range(nq + 2)]
    ctx = pl.BlockSpec((None, lc, 4 * LANES), lambda bi, i: (bi, 0, 0))
    return pl.pallas_call(
        _attn_kernel,
        out_shape=jax.ShapeDtypeStruct((b, n, ATT_WIDTH), BF16),
        grid=(b, nb // nq),
        in_specs=[pl.BlockSpec(memory_space=pltpu.SMEM), qspec] + kv + [ctx] + kv + [ctx],
        out_specs=qspec,
        compiler_params=_params("parallel", "parallel"),
        name="window_attention",
    )(sink, q, *([k4] * (nq + 2)), ck4, *([v4] * (nq + 2)), cv4)


def _attn_ctx_kernel(sink_ref, q_ref, kc_ref, vc_ref, o_ref):
    kvh = pl.program_id(1)
    lc = q_ref.shape[0]
    q2 = jnp.concatenate([q_ref[:, 0:LANES], q_ref[:, LANES:2 * LANES]], axis=0)
    upper_rows = lax.broadcasted_iota(jnp.int32, (2 * lc, 1), 0) >= lc
    lo = lax.broadcasted_iota(jnp.int32, (1, LANES), 1) < HEAD_DIM
    outs = []
    for par in range(2):
        ks = slice(LANES * par, LANES * (par + 1))
        base = ATT_HEADS // ATT_KV_HEADS * kvh + par
        sink_col = jnp.where(upper_rows, sink_ref[base + 2], sink_ref[base]) * LOG2E
        cols = _scores(q2, [kc_ref[:, ks]], [(None,) * (lc // LANES)])
        outs.append(_softmax_pv(cols, [vc_ref[:, ks]], sink_col))
    o = jnp.where(lo, outs[0], outs[1])
    o_ref[:, 0:LANES] = o[0:lc].astype(o_ref.dtype)
    o_ref[:, LANES:2 * LANES] = o[lc:2 * lc].astype(o_ref.dtype)


def _attention_ctx(cq, ck4, cv4, sink):
    b, lc, _ = cq.shape
    gw = 2 * LANES
    spec = pl.BlockSpec((None, lc, gw), lambda bi, h: (bi, 0, h))
    return pl.pallas_call(
        _attn_ctx_kernel,
        out_shape=jax.ShapeDtypeStruct((b, lc, ATT_WIDTH), BF16),
        grid=(b, ATT_KV_HEADS),
        in_specs=[pl.BlockSpec(memory_space=pltpu.SMEM), spec, spec, spec],
        out_specs=spec,
        compiler_params=_params("parallel", "parallel"),
        name="context_attention",
    )(sink, cq, ck4, cv4)


def _dft_consts(n):
    n1, n2 = DFT_N1, n // DFT_N1
    ang = lambda a, m: 2.0 * np.pi * np.outer(np.arange(a), np.arange(a)) / m
    c = HEAD_DIM
    cc, sc = np.cos(ang(c, c)) / np.sqrt(c), np.sin(ang(c, c)) / np.sqrt(c)
    eye = np.eye(FNET_GROUPS)
    w_chan = np.concatenate([np.kron(eye, cc), -np.kron(eye, sc)], axis=1)
    c1, s1 = np.cos(ang(n1, n1)) / np.sqrt(n1), np.sin(ang(n1, n1)) / np.sqrt(n1)
    m1 = np.block([[c1, s1], [-s1, c1]])
    kk = np.arange(n1)[:, None, None] + n1 * np.arange(n2)[None, :, None]
    ph = 2.0 * np.pi * (kk * np.arange(n2)[None, None, :] % n) / n
    m3 = np.concatenate([np.cos(ph), np.sin(ph)], axis=2) / np.sqrt(n2)
    return tuple(jnp.asarray(a, F32) for a in (w_chan, m1, m3))


def _dft_ctx_consts(lc):
    ang = lambda a, m: 2.0 * np.pi * np.outer(np.arange(a), np.arange(a)) / m
    c = HEAD_DIM
    cc, sc = np.cos(ang(c, c)) / np.sqrt(c), np.sin(ang(c, c)) / np.sqrt(c)
    eye = np.eye(FNET_GROUPS)
    w_chan = np.concatenate([np.kron(eye, cc), -np.kron(eye, sc)], axis=1)
    cl, sl = np.cos(ang(lc, lc)) / np.sqrt(lc), np.sin(ang(lc, lc)) / np.sqrt(lc)
    return jnp.asarray(w_chan, F32), jnp.asarray(np.concatenate([cl, sl], axis=1), F32)


def _regroup_rows(x, outer):
    r, c = x.shape
    return jnp.swapaxes(x.reshape(outer, r // outer, c), 0, 1).reshape(r, c)


def _fourier_kernel(u_ref, wc_ref, m1_ref, m3_ref, wf_ref, y_ref, p_ref):
    w, t = FNET_WIDTH, DFT_STEP
    s = pl.program_id(1)
    n_a = p_ref.shape[1]
    n2 = n_a * t

    @pl.when(s < n_a)
    def _():
        u = u_ref[...].reshape(DFT_N1 * t, w).astype(BF16)
        z = _regroup_rows(_dot(u, wc_ref[...]), DFT_N1)
        a_re, a_im = [], []
        for tt in range(t):
            zt = z[DFT_N1 * tt:DFT_N1 * (tt + 1)]
            zs = jnp.concatenate([zt[:, 0:w], zt[:, w:2 * w]], axis=0).astype(BF16)
            a = _dot(m1_ref[...], zs)
            a_re.append(a[0:DFT_N1])
            a_im.append(a[DFT_N1:2 * DFT_N1])
        for part, rows in enumerate((a_re, a_im)):
            a = _regroup_rows(jnp.concatenate(rows, axis=0), t)
            p_ref[part, s] = a.reshape(DFT_N1, t, w)

    @pl.when(s >= n_a)
    def _():
        k1_0 = (s - n_a) * t
        ys = []
        for kk in range(t):
            parts = [p_ref[part, :, k1_0 + kk].reshape(n2, w) for part in range(2)]
            rhs = jnp.concatenate(parts, axis=0).astype(BF16)
            ys.append(_dot(m3_ref[kk].astype(BF16), rhs).astype(BF16))
        out = _dot(jnp.concatenate(ys, axis=0), wf_ref[...])
        y_ref[...] = _regroup_rows(out, t).reshape(n2, t, w)


def _fourier(fu, wf_bd, consts):
    b, n, w = fu.shape
    n1, n2, t = DFT_N1, n // DFT_N1, DFT_STEP
    n_a, n_b = n2 // t, n1 // t
    w_chan, m1, m3 = consts
    y = pl.pallas_call(
        _fourier_kernel,
        out_shape=jax.ShapeDtypeStruct((b, n2, n1, w), F32),
        grid=(b, n_a + n_b),
        in_specs=[pl.BlockSpec((None, n1, t, w), lambda bi, s: (bi, 0, jnp.minimum(s, n_a - 1), 0)),
                  _full((w, 2 * w)), _full((2 * n1, 2 * n1)),
                  pl.BlockSpec((t, n2, 2 * n2), lambda bi, s: (jnp.maximum(s - n_a, 0), 0, 0)), _full((w, w))],
        out_specs=pl.BlockSpec((None, n2, t, w), lambda bi, s: (bi, 0, jnp.maximum(s - n_a, 0), 0)),
        scratch_shapes=[pltpu.VMEM((2, n_a, n1, t, w), F32)],
        compiler_params=_params("parallel", "arbitrary"),
        name="fourier_mix",
    )(fu.reshape(b, n1, n2, w), w_chan.astype(BF16), m1.astype(BF16), m3, wf_bd)
    return y.reshape(b, n, w)


def _four_ctx_kernel(u_ref, wc_ref, m_ref, wf_ref, y_ref):
    w = FNET_WIDTH
    z = _dot(u_ref[...].astype(BF16), wc_ref[...])
    zs = jnp.concatenate([z[:, 0:w], z[:, w:2 * w]], axis=0).astype(BF16)
    y = _dot(m_ref[...], zs)
    y_ref[...] = _dot(y.astype(BF16), wf_ref[...]).astype(y_ref.dtype)


def _fourier_ctx(cfu, wf_bd, consts):
    b, lc, w = cfu.shape
    w_chan, m = consts[0].astype(BF16), consts[1].astype(BF16)
    return pl.pallas_call(
        _four_ctx_kernel,
        out_shape=jax.ShapeDtypeStruct((b, lc, w), BF16),
        grid=(b,),
        in_specs=[pl.BlockSpec((None, lc, w), lambda bi: (bi, 0, 0)), _full((w, 2 * w)), _full((lc, 2 * lc)),
                  _full((w, w))],
        out_specs=pl.BlockSpec((None, lc, w), lambda bi: (bi, 0, 0)),
        compiler_params=_params("parallel"),
        name="fourier_context",
    )(cfu, w_chan, m, wf_bd)


def _pair_stack(x, lo):
    zero = jnp.zeros_like(x)
    return jnp.concatenate([jnp.where(lo, x, zero), jnp.where(lo, zero, x)], axis=0)


def _gla_kernel(qkvf_ref, laf_ref, qkvb_ref, lab_ref, s0_ref, of_ref, ob_ref, sfin_ref, stf_ref, stb_ref):
    i = pl.program_id(1)
    c = GLA_CHUNK
    nchunk = qkvf_ref.shape[0] // c

    @pl.when(i == 0)
    def _():
        stf_ref[...] = s0_ref[0]
        stb_ref[...] = s0_ref[1]

    r64 = lax.broadcasted_iota(jnp.int32, (c, c), 0)
    c64 = lax.broadcasted_iota(jnp.int32, (c, c), 1)
    tri = ((c64 <= r64).astype(BF16), (c64 >= r64).astype(BF16))
    at = lax.broadcasted_iota(jnp.int32, (c, LANES), 0)
    as_ = lax.broadcasted_iota(jnp.int32, (c, LANES), 1) & (c - 1)
    att_mask = (as_ <= at, as_ >= at)
    lo = lax.broadcasted_iota(jnp.int32, (1, LANES), 1) < HEAD_DIM
    br = lax.broadcasted_iota(jnp.int32, (LANES, LANES), 0) < HEAD_DIM
    bc = lax.broadcasted_iota(jnp.int32, (LANES, LANES), 1) < HEAD_DIM
    bd_mask = br == bc
    pairs = [slice(LANES * p, LANES * (p + 1)) for p in range(GLA_WIDTH // LANES)]
    in_refs = ((qkvf_ref, laf_ref), (qkvb_ref, lab_ref))
    out_refs = (of_ref, ob_ref)
    items = [(d, step if d == 0 else nchunk - 1 - step) for step in range(nchunk) for d in range(2)]
    rows = lambda ch: slice(c * ch, c * (ch + 1))

    bcum, work, o_intra, ut, decay = {}, {}, {}, {}, {}
    st_refs = (stf_ref, stb_ref)
    st = [[st_refs[d][LANES * j:LANES * (j + 1)] for j in range(len(pairs))] for d in range(2)]

    def gate_sums(d, ch):
        la = in_refs[d][1][rows(ch)]
        la_hi = la.astype(BF16)
        la_lo = (la - la_hi.astype(F32)).astype(BF16)
        bcum[d, ch] = _dot(jnp.concatenate([tri[d], tri[d]], axis=1), jnp.concatenate([la_hi, la_lo], axis=0))

    def scores(d, ch):
        qkv_ref = in_refs[d][0]
        w = GLA_WIDTH
        b = bcum[d, ch]
        btot = b[0:1] if d == 1 else b[c - 1:c]
        k = qkv_ref[rows(ch), w:2 * w].astype(F32)
        q_in = (qkv_ref[rows(ch), 0:w].astype(F32) * jnp.exp2(b)).astype(BF16)
        k_in = (k * jnp.exp2(-b)).astype(BF16)
        k_out = (k * jnp.exp2(btot - b)).astype(BF16)
        vb = qkv_ref[rows(ch), 2 * w:3 * w]
        att = [lax.dot_general(q_in[:, p], _pair_stack(k_in[:, p], lo), NT_DIMS, preferred_element_type=F32)
               for p in pairs]
        work[d, ch] = (btot, q_in, k_out, vb, att)

    def intra(d, ch):
        btot, q_in, k_out, vb, att = work[d, ch]
        o_intra[d, ch] = [_dot(jnp.where(att_mask[d], a, 0.0).astype(BF16), _pair_stack(vb[:, p], lo))
                          for a, p in zip(att, pairs)]
        ut[d, ch] = [lax.dot_general(k_out[:, p], vb[:, p], TN_DIMS, preferred_element_type=F32) for p in pairs]
        decay[d, ch] = [jnp.transpose(jnp.exp2(btot[:, p])) for p in pairs]

    def carry(d, ch):
        btot, q_in = work[d, ch][0:2]
        outs = []
        for j, p in enumerate(pairs):
            outs.append(o_intra[d, ch][j] + _dot(q_in[:, p], st[d][j].astype(BF16)))
            st[d][j] = st[d][j] * decay[d, ch][j] + jnp.where(bd_mask, ut[d, ch][j], 0.0)
        out_refs[d][rows(ch)] = jnp.concatenate(outs, axis=1).astype(out_refs[d].dtype)

    groups = [items[g:g + GLA_GROUP] for g in range(0, len(items), GLA_GROUP)]
    for it in groups[0]:
        gate_sums(*it)
    for it in groups[0]:
        scores(*it)
    for g, grp in enumerate(groups):
        for it in grp:
            intra(*it)
        nxt = groups[g + 1] if g + 1 < len(groups) else []
        for it in nxt:
            gate_sums(*it)
        for k_ in range(len(grp)):
            carry(*grp[k_])
            if k_ < len(nxt):
                scores(*nxt[k_])
    for d in range(2):
        st_refs[d][...] = jnp.concatenate(st[d], axis=0)

    @pl.when(i == pl.num_programs(1) - 1)
    def _():
        for d in range(2):
            sfin_ref[d] = jnp.concatenate(st[d], axis=0)


def _gla(gqkv, la, s0, tb):
    b, n, _ = gqkv.shape
    w = GLA_WIDTH
    nblk = n // tb
    fwd = lambda bi, i: (bi, i, 0)
    bwd = lambda bi, i: (bi, nblk - 1 - i, 0)
    bwd_la = lambda bi, i: (bi, nblk - 1 - i, 1)
    tok = lambda f: pl.BlockSpec((None, tb, w), f)
    qkv = lambda f: pl.BlockSpec((None, tb, 3 * w), f)
    state = pl.BlockSpec((None, 2, w, LANES), lambda bi, i: (bi, 0, 0, 0))
    return pl.pallas_call(
        _gla_kernel,
        out_shape=[jax.ShapeDtypeStruct((b, n, w), BF16), jax.ShapeDtypeStruct((b, n, w), BF16),
                   jax.ShapeDtypeStruct((b, 2, w, LANES), F32)],
        grid=(b, nblk),
        in_specs=[qkv(fwd), tok(fwd), qkv(bwd), tok(bwd_la), state],
        out_specs=[tok(fwd), tok(bwd), state],
        scratch_shapes=[pltpu.VMEM((w, LANES), F32), pltpu.VMEM((w, LANES), F32)],
        compiler_params=_params("parallel", "arbitrary"),
        name="gla_scan",
    )(gqkv, la, gqkv, la, s0)


def _tail_kernel(att_ref, four_ref, of_ref, ob_ref, r_ref, x_ref, mod_ref, gg_ref, bd_ref, wmix_ref, g2_ref, wi_ref,
                 wo_ref, o_ref, a_ref):
    d = D_MODEL
    mod = lambda j: mod_ref[:, j * d:(j + 1) * d]
    a0, a1, a2 = ATT_WIDTH, ATT_WIDTH + FNET_WIDTH, ATT_WIDTH + FNET_WIDTH + GLA_WIDTH
    sub = min(x_ref.shape[0], TAIL_SUB)
    subs = [slice(sub * j, sub * (j + 1)) for j in range(x_ref.shape[0] // sub)]

    def mixed(rs):
        o = of_ref[rs].astype(F32) + ob_ref[rs].astype(F32)
        y = o * lax.rsqrt(_group_mean(o, bd_ref) + EPS) * gg_ref[...]
        y = y * _silu(r_ref[rs].astype(F32))
        mix = (_dot(att_ref[rs], wmix_ref[0:a0]) + _dot(four_ref[rs].astype(BF16), wmix_ref[a0:a1])
               + _dot(y.astype(BF16), wmix_ref[a1:a2]))
        x = x_ref[rs] + mod(2) * mix
        ms = jnp.mean(x * x, axis=-1, keepdims=True)
        return x, ((x * lax.rsqrt(ms + EPS) * g2_ref[...]) * (1.0 + mod(4)) + mod(3)).astype(BF16)

    pre = [mixed(rs) for rs in subs]
    for rs, (x, hb) in zip(subs, pre):
        for c0 in range(0, FFN_HIDDEN, FFN_CHUNK):
            g = _dot(hb, wi_ref[:, c0:c0 + FFN_CHUNK])
            u = _dot(hb, wi_ref[:, FFN_HIDDEN + c0:FFN_HIDDEN + c0 + FFN_CHUNK])
            a_ref[rs, c0:c0 + FFN_CHUNK] = (_silu(g) * u).astype(BF16)
        o_ref[rs] = x + mod(5) * _dot(a_ref[rs], wo_ref[...])


def _tail(att, four, of, ob, gr, x, mod3, mod_row, lw, tm):
    b, n, d = x.shape
    row = (lambda bi: bi) if mod_row is None else (lambda bi: mod_row)
    tok = lambda w: pl.BlockSpec((None, tm, w), lambda bi, i: (bi, i, 0))
    resident = lambda shape: _layer_block(shape, lw["layer"])
    return pl.pallas_call(
        _tail_kernel,
        out_shape=jax.ShapeDtypeStruct((b, n, d), F32),
        grid=(b, n // tm),
        in_specs=[tok(ATT_WIDTH), tok(FNET_WIDTH), tok(GLA_WIDTH), tok(GLA_WIDTH), tok(GLA_WIDTH), tok(d),
                  pl.BlockSpec((None, 1, 6 * d), lambda bi, i: (row(bi), 0, 0)),
                  _full((1, GLA_WIDTH)), _full((GLA_WIDTH, GLA_WIDTH)), resident((d, d)),
                  _full((1, d)), resident((d, 2 * FFN_HIDDEN)), resident((FFN_HIDDEN, d))],
        out_specs=tok(d),
        scratch_shapes=[pltpu.VMEM((tm, FFN_HIDDEN), BF16)],
        compiler_params=_params("parallel", "parallel"),
        name="mix_ffn",
    )(att, four, of, ob, gr, x, mod3, lw["gla_g"], lw["bd256"], lw["w_out"], lw["g2"], lw["w_ffn_in"],
      lw["w_ffn_out"])


def _rope_tables(n):
    axis_dim = HEAD_DIM // 2
    inv_freq = ROPE_BASE ** (-np.arange(0, axis_dim, 2, dtype=np.float64) / axis_dim)
    t = np.arange(n)
    ang_r = (t // GRID_W)[:, None] * inv_freq[None, :]
    ang_c = (t % GRID_W)[:, None] * inv_freq[None, :]
    cos = np.concatenate([np.cos(ang_r)] * 2 + [np.cos(ang_c)] * 2, axis=1)
    sin = np.concatenate([-np.sin(ang_r), np.sin(ang_r), -np.sin(ang_c), np.sin(ang_c)], axis=1)
    return jnp.asarray(np.tile(cos, (1, 2)), F32), jnp.asarray(np.tile(sin, (1, 2)), F32)


def _block_diag_mean(width):
    g = np.arange(width) // HEAD_DIM
    return jnp.asarray((g[:, None] == g[None, :]) / HEAD_DIM, BF16)


def _layer_weights(l, big, w_in, g_norm1, q_norm_g, k_norm_g, w_fourier, wgf, bgf, wgb, bgb, gla_norm_g, g_norm2):
    r = GLA_GATE_RANK
    w_z = jnp.zeros((D_MODEL, LANES), F32).at[:, 0:2 * r].set(w_in[l, :, MAIN_WIDTH:MAIN_WIDTH + 2 * r])
    w_gate = jnp.zeros((LANES, 2 * GLA_WIDTH), F32)
    w_gate = w_gate.at[0:r, 0:GLA_WIDTH].set(wgf[l]).at[r:2 * r, GLA_WIDTH:].set(wgb[l])
    wf_bd = jnp.zeros((FNET_WIDTH, FNET_WIDTH), F32)
    for g in range(FNET_GROUPS):
        wf_bd = wf_bd.at[HEAD_DIM * g:HEAD_DIM * (g + 1), HEAD_DIM * g:HEAD_DIM * (g + 1)].set(w_fourier[l, g])
    return {
        "g1": g_norm1[l][None, :],
        "layer": l,
        "w_main": big["w_in"],
        "w_z": w_z.astype(BF16),
        "w_gate": w_gate.astype(BF16),
        "b_gate": jnp.concatenate([bgf[l], bgb[l]])[None, :],
        "q_g": jnp.tile(q_norm_g[l], 2)[None, :],
        "k_g": jnp.tile(k_norm_g[l], 2)[None, :],
        "bd256": _block_diag_mean(GLA_WIDTH),
        "wf_bd": wf_bd.astype(BF16),
        "gla_g": jnp.tile(gla_norm_g[l], GLA_HEADS)[None, :],
        "w_out": big["w_out"],
        "g2": g_norm2[l][None, :],
        "w_ffn_in": big["w_ffn_in"],
        "w_ffn_out": big["w_ffn_out"],
    }


def kernel(x, c, ctx, c_ctx, w_mod, b_mod, g_norm1, w_in, q_norm_g, k_norm_g, attn_sink, w_fourier, gla_w_gate_f,
           gla_b_gate_f, gla_w_gate_b, gla_b_gate_b, gla_norm_g, w_out, g_norm2, w_ffn_in, w_ffn_out):
    b, n, d = x.shape
    lc = ctx.shape[1]
    depth = w_mod.shape[0]
    assert d == D_MODEL and b <= 7 and n % 512 == 0 and n % (DFT_N1 * DFT_STEP) == 0 and lc % GLA_CHUNK == 0

    cc = jnp.zeros((8, d), F32).at[0:b].set(c).at[b].set(c_ctx)
    mod = _modulation(cc, w_mod, b_mod)
    rope_tabs = _rope_tables(n)
    dft = _dft_consts(n)
    dft_ctx = _dft_ctx_consts(lc)
    tm_in, tm_tail, tb_lat = 1024, 1024, 1024
    flat = lambda t: t.reshape(1, b * lc, t.shape[-1])
    unflat = lambda t: t.reshape(b, lc, t.shape[-1])
    xc = ctx
    big = {"w_in": w_in.astype(BF16), "w_out": w_out.astype(BF16), "w_ffn_in": w_ffn_in.astype(BF16),
           "w_ffn_out": w_ffn_out.astype(BF16)}
    for l in range(depth):
        need_ctx = l < depth - 1
        lw = _layer_weights(l, big, w_in, g_norm1, q_norm_g, k_norm_g, w_fourier, gla_w_gate_f, gla_b_gate_f,
                            gla_w_gate_b, gla_b_gate_b, gla_norm_g, g_norm2)
        mod3 = mod[l].reshape(8, 1, 6 * d)
        sink = attn_sink[l]
        cq, ck4, cv4, cfu, cgqkv, cgr, cla = map(unflat, _inproj(flat(xc), mod3, b, lw, None, b * lc))
        q, k4, v4, fu, gqkv, gr, la = _inproj(x, mod3, None, lw, rope_tabs, tm_in)
        att = _attention(q, k4, v4, ck4, cv4, sink)
        four = _fourier(fu, lw["wf_bd"], dft)
        s_zero = jnp.zeros((b, 2, GLA_WIDTH, LANES), F32)
        ocf, ocb, s_ctx = _gla(cgqkv, cla, s_zero, lc)
        olf, olb, _ = _gla(gqkv, la, s_ctx, tb_lat)
        x = _tail(att, four, olf, olb, gr, x, mod3, None, lw, tm_tail)
        if need_ctx:
            att_c = _attention_ctx(cq, ck4, cv4, sink)
            four_c = _fourier_ctx(cfu, lw["wf_bd"], dft_ctx)
            xc = unflat(_tail(*map(flat, (att_c, four_c, ocf, ocb, cgr, xc)), mod3, b, lw, b * lc))
    return x
```

```python
import functools

import numpy as np
import jax
import jax.numpy as jnp
from jax import lax
from jax.experimental import pallas as pl
from jax.experimental.pallas import tpu as pltpu

F32 = jnp.float32
BF16 = jnp.bfloat16

D_MODEL = 1024
HEAD_DIM = 64
GRID_W = 64
ROPE_BASE = 10000.0
ATT_HEADS = 8
ATT_KV_HEADS = 2
ATT_WIDTH = ATT_HEADS * HEAD_DIM
KV_WIDTH = ATT_KV_HEADS * HEAD_DIM
ATT_BLOCK = 128
ATT_QBLOCKS = 8
ATT_SCALE = HEAD_DIM ** -0.5
LOG2E = 1.4426950408889634
NEG_INF = -1e30
FNET_GROUPS = 4
FNET_WIDTH = FNET_GROUPS * HEAD_DIM
GLA_HEADS = 4
GLA_WIDTH = GLA_HEADS * HEAD_DIM
GLA_GATE_RANK = 16
GLA_TAU = 16.0
GLA_CHUNK = 64
GLA_SCALE = HEAD_DIM ** -0.5
MAIN_WIDTH = ATT_WIDTH + 2 * KV_WIDTH + FNET_WIDTH + 4 * GLA_WIDTH
FFN_HIDDEN = 2816
FFN_CHUNK = 256
INPROJ_SUB = 512
TAIL_SUB = 512
FU_BLOCK = 2
GLA_GROUP = 16
EPS = 1e-6
LANES = 128
DFT_N1 = 128
DFT_STEP = 16
VMEM_LIMIT = 56 * 1024 * 1024

NT_DIMS = (((1,), (1,)), ((), ()))
TN_DIMS = (((0,), (0,)), ((), ()))


def _params(*sem):
    return pltpu.CompilerParams(dimension_semantics=sem, vmem_limit_bytes=VMEM_LIMIT)


def _dot(a, b):
    return jnp.dot(a, b, preferred_element_type=F32)


def _silu(x):
    return x / (1.0 + jnp.exp(-x))


def _layer_block(shape, layer):
    nd = len(shape)
    return pl.BlockSpec((None,) + tuple(shape), lambda *_: (layer,) + (0,) * nd, pipeline_mode=pl.Buffered(1))


def _full(shape):
    nd = len(shape)
    return pl.BlockSpec(shape, lambda *_: (0,) * nd)


def _group_mean(t, bd_ref):
    return _dot((t * t).astype(BF16), bd_ref[...])


def _mod_kernel(c_ref, w_ref, b_ref, o_ref):
    s = _silu(c_ref[...]).astype(BF16)
    o_ref[...] = _dot(s, w_ref[...].astype(BF16)) + b_ref[...]


def _modulation(cc, w_mod, b_mod):
    depth, d, width = w_mod.shape
    tn = 1536
    return pl.pallas_call(
        _mod_kernel,
        out_shape=jax.ShapeDtypeStruct((depth, 8, width), F32),
        grid=(depth, width // tn),
        in_specs=[
            _full((8, d)),
            pl.BlockSpec((None, d, tn), lambda l, j: (l, 0, j)),
            pl.BlockSpec((None, 1, tn), lambda l, j: (l, 0, j)),
        ],
        out_specs=pl.BlockSpec((None, 8, tn), lambda l, j: (l, 0, j)),
        compiler_params=_params("parallel", "parallel"),
        name="modulation",
    )(cc, w_mod, b_mod.reshape(depth, 1, width))


def _inproj_kernel(*refs, rope):
    if rope:
        (x_ref, mod_ref, g1_ref, wm_ref, wz_ref, wg_ref, bg_ref, qg_ref, kg_ref, bd_ref, cos_ref, sin_ref,
         q_ref, kv4_ref, g4_ref, lf_ref) = refs
    else:
        (x_ref, mod_ref, g1_ref, wm_ref, wz_ref, wg_ref, bg_ref, qg_ref, kg_ref, bd_ref,
         q_ref, kv4_ref, g4_ref, lf_ref) = refs
    d = D_MODEL
    mod = mod_ref[...]
    sh, sc = mod[:, 0:d], mod[:, d:2 * d]
    sub = min(x_ref.shape[0], INPROJ_SUB)
    subs = [slice(sub * j, sub * (j + 1)) for j in range(x_ref.shape[0] // sub)]

    def normed(rs):
        x = x_ref[rs]
        ms = jnp.mean(x * x, axis=-1, keepdims=True)
        return ((x * lax.rsqrt(ms + EPS) * g1_ref[...]) * (1.0 + sc) + sh).astype(BF16)

    lane = lax.broadcasted_iota(jnp.int32, (1, LANES), 1)
    lo = lane < HEAD_DIM
    second_half = (lane & 16) != 0

    def head_norm(t, g_ref):
        wdt = t.shape[1]
        ms = _dot((t * t).astype(BF16), bd_ref[0:wdt, 0:wdt])
        g = g_ref[...] if wdt == LANES else jnp.concatenate([g_ref[...]] * (wdt // LANES), axis=1)
        return t * lax.rsqrt(ms + EPS) * g

    def rotary(t, rs):
        if not rope:
            return t
        partner = jnp.where(second_half, pltpu.roll(t, 16, 1), pltpu.roll(t, LANES - 16, 1))
        return t * cos_ref[rs] + partner * sin_ref[rs]

    def spread(t, c0, rs, idle):
        tr = pltpu.roll(t, HEAD_DIM, 1)
        fill = jnp.full_like(t, idle)
        tiles = (jnp.where(lo, t, fill), jnp.where(lo, fill, tr), jnp.where(lo, tr, fill), jnp.where(lo, fill, t))
        for j, tile in enumerate(tiles):
            kv4_ref[rs, c0 + LANES * j:c0 + LANES * (j + 1)] = tile.astype(kv4_ref.dtype)

    assert (ATT_WIDTH, 2 * KV_WIDTH + FNET_WIDTH, 2 * GLA_WIDTH) == (4 * LANES,) * 3

    def project(rs, hb):
        group = lambda g: _dot(hb, wm_ref[:, 4 * LANES * g:4 * LANES * (g + 1)])
        gz = _dot(hb, wz_ref[...])
        a_q = group(0)
        z = _dot(gz.astype(BF16), wg_ref[...]) + bg_ref[...]
        log_sig = jnp.minimum(z, 0.0) - jnp.log(1.0 + jnp.exp(-jnp.abs(z)))
        lf_ref[rs, 0:2 * GLA_WIDTH] = log_sig * (LOG2E / GLA_TAU)
        a_kvf = group(1)
        for j2 in range(ATT_WIDTH // (2 * LANES)):
            qn = head_norm(a_q[:, 2 * LANES * j2:2 * LANES * (j2 + 1)], qg_ref)
            for j in range(2):
                t = rotary(qn[:, LANES * j:LANES * (j + 1)], rs) * (ATT_SCALE * LOG2E)
                q_ref[rs, LANES * (2 * j2 + j):LANES * (2 * j2 + j + 1)] = t.astype(q_ref.dtype)
        a_qk = group(2)
        spread(rotary(head_norm(a_kvf[:, 0:KV_WIDTH], kg_ref), rs), 0, rs, 0.0)
        spread(a_kvf[:, KV_WIDTH:2 * KV_WIDTH], 4 * LANES, rs, 1.0)
        lf_ref[rs, 2 * GLA_WIDTH:] = a_kvf[:, 2 * KV_WIDTH:]
        a_vr = group(3)
        g4_ref[rs, 0:GLA_WIDTH] = (a_qk[:, 0:GLA_WIDTH] * GLA_SCALE).astype(g4_ref.dtype)
        g4_ref[rs, GLA_WIDTH:2 * GLA_WIDTH] = a_qk[:, GLA_WIDTH:].astype(g4_ref.dtype)
        g4_ref[rs, 2 * GLA_WIDTH:4 * GLA_WIDTH] = a_vr.astype(g4_ref.dtype)

    hbs = [normed(rs) for rs in subs]
    for rs, hb in zip(subs, hbs):
        project(rs, hb)


def _inproj(x, mod3, mod_row, lw, rope_tabs, tm):
    b, n, d = x.shape
    rope = rope_tabs is not None
    row = (lambda bi: bi) if mod_row is None else (lambda bi: mod_row)
    tok = lambda w: pl.BlockSpec((None, tm, w), lambda bi, i: (bi, i, 0))
    in_specs = [
        tok(d),
        pl.BlockSpec((None, 1, 6 * d), lambda bi, i: (row(bi), 0, 0)),
        _full((1, d)), _layer_block((d, MAIN_WIDTH), lw["layer"]), _full((d, LANES)), _full((LANES, 2 * GLA_WIDTH)),
        _full((1, 2 * GLA_WIDTH)), _full((1, LANES)), _full((1, LANES)), _full((2 * LANES, 2 * LANES)),
    ]
    args = [x, mod3, lw["g1"], lw["w_main"], lw["w_z"], lw["w_gate"], lw["b_gate"], lw["q_g"], lw["k_g"], lw["bd256"]]
    if rope:
        in_specs += [pl.BlockSpec((tm, LANES), lambda bi, i: (i, 0))] * 2
        args += list(rope_tabs)
    sds = lambda w, dt: jax.ShapeDtypeStruct((b, n, w), dt)
    out_shape = [sds(ATT_WIDTH, BF16), sds(8 * LANES, BF16), sds(4 * GLA_WIDTH, BF16),
                 sds(2 * GLA_WIDTH + FNET_WIDTH, F32)]
    out_specs = [tok(s.shape[-1]) for s in out_shape]
    return pl.pallas_call(
        functools.partial(_inproj_kernel, rope=rope),
        out_shape=out_shape, grid=(b, n // tm), in_specs=in_specs, out_specs=out_specs,
        compiler_params=_params("parallel", "parallel"),
        name="inproj_rope" if rope else "inproj_ctx",
    )(*args)


def _scores(q2, key_tiles, masks):
    cols = []
    for k, tile_masks in zip(key_tiles, masks):
        s = lax.dot_general(q2, k, NT_DIMS, preferred_element_type=F32)
        for j, mk in enumerate(tile_masks):
            c = s[:, LANES * j:LANES * (j + 1)]
            cols.append(c if mk is None else jnp.where(mk, c, NEG_INF))
    return cols


def _softmax_pv(cols, val_tiles, sink_col):
    mx = cols[0]
    for c in cols[1:]:
        mx = jnp.maximum(mx, c)
    m = jnp.maximum(jnp.max(mx, axis=-1, keepdims=True), sink_col)
    p = jnp.concatenate([jnp.exp2(c - m).astype(BF16) for c in cols], axis=1)
    pv = _dot(p, jnp.concatenate(val_tiles, axis=0))
    den = pltpu.roll(pv, HEAD_DIM, 1) + jnp.exp2(sink_col - m)
    return pv / den


def _attn_kernel(*refs):
    nq = ATT_QBLOCKS
    sink_ref, q_ref = refs[0:2]
    kv_refs, kvc_ref, o_ref = refs[2:nq + 4], refs[nq + 4], refs[nq + 5]
    v_off = 4 * LANES
    i = pl.program_id(1)
    last = pl.num_programs(1) - 1
    blk = ATT_BLOCK
    group = ATT_HEADS // ATT_KV_HEADS
    row = lax.broadcasted_iota(jnp.int32, (2 * blk, blk), 0) & (blk - 1)
    col = lax.broadcasted_iota(jnp.int32, (2 * blk, blk), 1)
    band_l, band_r = col >= row, col <= row
    mask_l = [jnp.logical_and(band_l, i > 0) if s == 0 else band_l for s in range(nq)]
    mask_r = [jnp.logical_and(band_r, i < last) if s == nq - 1 else band_r for s in range(nq)]
    upper_rows = lax.broadcasted_iota(jnp.int32, (2 * blk, 1), 0) >= blk
    lo = lax.broadcasted_iota(jnp.int32, (1, LANES), 1) < HEAD_DIM

    def tiles(s, ks):
        l, m_, r = kv_refs[s:s + 3]
        return [jnp.concatenate([l[:, ks], m_[:, ks]], axis=0),
                jnp.concatenate([r[:, ks], kvc_ref[0:blk, ks]], axis=0), kvc_ref[blk:, ks]]

    slots = [(s, kvh, par) for s in range(nq) for kvh in range(ATT_KV_HEADS) for par in range(2)]
    lane_slice = lambda kvh, par, off=0: slice(off + 2 * LANES * kvh + LANES * par,
                                               off + 2 * LANES * kvh + LANES * (par + 1))
    cols = {}
    for s, kvh, par in slots:
        qs, rs = 2 * LANES * kvh, slice(blk * s, blk * (s + 1))
        q2 = jnp.concatenate([q_ref[rs, qs:qs + LANES], q_ref[rs, qs + LANES:qs + 2 * LANES]], axis=0)
        cols[s, kvh, par] = _scores(q2, tiles(s, lane_slice(kvh, par)),
                                    [(mask_l[s], None), (mask_r[s], None), (None,)])
    outs = {}
    for s, kvh, par in slots:
        base = group * kvh + par
        sink_col = jnp.where(upper_rows, sink_ref[base + 2], sink_ref[base]) * LOG2E
        outs[s, kvh, par] = _softmax_pv(cols[s, kvh, par], tiles(s, lane_slice(kvh, par, v_off)), sink_col)
    for s in range(nq):
        for kvh in range(ATT_KV_HEADS):
            qs, r0 = 2 * LANES * kvh, blk * s
            o = jnp.where(lo, outs[s, kvh, 0], outs[s, kvh, 1])
            o_ref[r0:r0 + blk, qs:qs + LANES] = o[0:blk].astype(o_ref.dtype)
            o_ref[r0:r0 + blk, qs + LANES:qs + 2 * LANES] = o[blk:2 * blk].astype(o_ref.dtype)


def _attention(q, kv4, ckv4, sink):
    b, n, _ = q.shape
    lc = ckv4.shape[1]
    nq = ATT_QBLOCKS
    nb = n // ATT_BLOCK
    assert lc == 2 * ATT_BLOCK and nb % nq == 0
    qspec = pl.BlockSpec((None, nq * ATT_BLOCK, 4 * LANES), lambda bi, i: (bi, i, 0))
    kv = [pl.BlockSpec((None, ATT_BLOCK, 8 * LANES),
                       functools.partial(lambda bi, i, j: (bi, jnp.clip(nq * i + j - 1, 0, nb - 1), 0), j=j))
          for j in range(nq + 2)]
    ctx = pl.BlockSpec((None, lc, 8 * LANES), lambda bi, i: (bi, 0, 0))
    return pl.pallas_call(
        _attn_kernel,
        out_shape=jax.ShapeDtypeStruct((b, n, ATT_WIDTH), BF16),
        grid=(b, nb // nq),
        in_specs=[pl.BlockSpec(memory_space=pltpu.SMEM), qspec] + kv + [ctx],
        out_specs=qspec,
        compiler_params=_params("parallel", "parallel"),
        name="window_attention",
    )(sink, q, *([kv4] * (nq + 2)), ckv4)


def _attn_ctx_kernel(sink_ref, q_ref, kc_ref, vc_ref, o_ref):
    kvh = pl.program_id(1)
    lc = q_ref.shape[0]
    q2 = jnp.concatenate([q_ref[:, 0:LANES], q_ref[:, LANES:2 * LANES]], axis=0)
    upper_rows = lax.broadcasted_iota(jnp.int32, (2 * lc, 1), 0) >= lc
    lo = lax.broadcasted_iota(jnp.int32, (1, LANES), 1) < HEAD_DIM
    outs = []
    for par in range(2):
        ks = slice(LANES * par, LANES * (par + 1))
        base = ATT_HEADS // ATT_KV_HEADS * kvh + par
        sink_col = jnp.where(upper_rows, sink_ref[base + 2], sink_ref[base]) * LOG2E
        cols = _scores(q2, [kc_ref[:, ks]], [(None,) * (lc // LANES)])
        outs.append(_softmax_pv(cols, [vc_ref[:, ks]], sink_col))
    o = jnp.where(lo, outs[0], outs[1])
    o_ref[:, 0:LANES] = o[0:lc].astype(o_ref.dtype)
    o_ref[:, LANES:2 * LANES] = o[lc:2 * lc].astype(o_ref.dtype)


def _attention_ctx(cq, ckv4, sink):
    b, lc, _ = cq.shape
    gw = 2 * LANES
    spec = pl.BlockSpec((None, lc, gw), lambda bi, h: (bi, 0, h))
    vspec = pl.BlockSpec((None, lc, gw), lambda bi, h: (bi, 0, ATT_KV_HEADS + h))
    return pl.pallas_call(
        _attn_ctx_kernel,
        out_shape=jax.ShapeDtypeStruct((b, lc, ATT_WIDTH), BF16),
        grid=(b, ATT_KV_HEADS),
        in_specs=[pl.BlockSpec(memory_space=pltpu.SMEM), spec, spec, vspec],
        out_specs=spec,
        compiler_params=_params("parallel", "parallel"),
        name="context_attention",
    )(sink, cq, ckv4, ckv4)


def _dft_consts(n):
    n1, n2 = DFT_N1, n // DFT_N1
    ang = lambda a, m: 2.0 * np.pi * np.outer(np.arange(a), np.arange(a)) / m
    c = HEAD_DIM
    cc, sc = np.cos(ang(c, c)) / np.sqrt(c), np.sin(ang(c, c)) / np.sqrt(c)
    eye = np.eye(FNET_GROUPS)
    w_chan = np.concatenate([np.kron(eye, cc), -np.kron(eye, sc)], axis=1)
    c1, s1 = np.cos(ang(n1, n1)) / np.sqrt(n1), np.sin(ang(n1, n1)) / np.sqrt(n1)
    m1 = np.block([[c1, s1], [-s1, c1]])
    kk = np.arange(n1)[:, None, None] + n1 * np.arange(n2)[None, :, None]
    ph = 2.0 * np.pi * (kk * np.arange(n2)[None, None, :] % n) / n
    m3 = np.concatenate([np.cos(ph), np.sin(ph)], axis=2) / np.sqrt(n2)
    return tuple(jnp.asarray(a, F32) for a in (w_chan, m1, m3))


def _dft_ctx_consts(lc):
    ang = lambda a, m: 2.0 * np.pi * np.outer(np.arange(a), np.arange(a)) / m
    c = HEAD_DIM
    cc, sc = np.cos(ang(c, c)) / np.sqrt(c), np.sin(ang(c, c)) / np.sqrt(c)
    eye = np.eye(FNET_GROUPS)
    w_chan = np.concatenate([np.kron(eye, cc), -np.kron(eye, sc)], axis=1)
    cl, sl = np.cos(ang(lc, lc)) / np.sqrt(lc), np.sin(ang(lc, lc)) / np.sqrt(lc)
    return jnp.asarray(w_chan, F32), jnp.asarray(np.concatenate([cl, sl], axis=1), F32)


def _regroup_rows(x, outer):
    r, c = x.shape
    return jnp.swapaxes(x.reshape(outer, r // outer, c), 0, 1).reshape(r, c)


def _fourier_kernel(u_ref, wc_ref, m1_ref, m3_ref, wf_ref, y_ref, p_ref):
    w, t = FNET_WIDTH, DFT_STEP
    s = pl.program_id(1)
    n_a = p_ref.shape[1]
    n2 = n_a * t

    @pl.when(s < n_a)
    def _():
        u = u_ref[...].reshape(DFT_N1 * t, w).astype(BF16)
        z = _regroup_rows(_dot(u, wc_ref[...]), DFT_N1)
        a_re, a_im = [], []
        for tt in range(t):
            zt = z[DFT_N1 * tt:DFT_N1 * (tt + 1)]
            zs = jnp.concatenate([zt[:, 0:w], zt[:, w:2 * w]], axis=0).astype(BF16)
            a = _dot(m1_ref[...], zs)
            a_re.append(a[0:DFT_N1])
            a_im.append(a[DFT_N1:2 * DFT_N1])
        for part, rows in enumerate((a_re, a_im)):
            a = _regroup_rows(jnp.concatenate(rows, axis=0), t)
            p_ref[part, s] = a.reshape(DFT_N1, t, w)

    @pl.when(s >= n_a)
    def _():
        k1_0 = (s - n_a) * t
        ys = []
        for kk in range(t):
            parts = [p_ref[part, :, k1_0 + kk].reshape(n2, w) for part in range(2)]
            rhs = jnp.concatenate(parts, axis=0).astype(BF16)
            ys.append(_dot(m3_ref[kk].astype(BF16), rhs).astype(BF16))
        out = _dot(jnp.concatenate(ys, axis=0), wf_ref[...])
        y_ref[...] = _regroup_rows(out, t).reshape(n2, t, w)


def _fourier(lf, wf_bd, consts):
    b, n, wl = lf.shape
    w = FNET_WIDTH
    n1, n2, t = DFT_N1, n // DFT_N1, DFT_STEP
    n_a, n_b = n2 // t, n1 // t
    w_chan, m1, m3 = consts
    y = pl.pallas_call(
        _fourier_kernel,
        out_shape=jax.ShapeDtypeStruct((b, n2, n1, w), F32),
        grid=(b, n_a + n_b),
        in_specs=[pl.BlockSpec((None, n1, t, w), lambda bi, s: (bi, 0, jnp.minimum(s, n_a - 1), FU_BLOCK)),
                  _full((w, 2 * w)), _full((2 * n1, 2 * n1)),
                  pl.BlockSpec((t, n2, 2 * n2), lambda bi, s: (jnp.maximum(s - n_a, 0), 0, 0)), _full((w, w))],
        out_specs=pl.BlockSpec((None, n2, t, w), lambda bi, s: (bi, 0, jnp.maximum(s - n_a, 0), 0)),
        scratch_shapes=[pltpu.VMEM((2, n_a, n1, t, w), F32)],
        compiler_params=_params("parallel", "arbitrary"),
        name="fourier_mix",
    )(lf.reshape(b, n1, n2, wl), w_chan.astype(BF16), m1.astype(BF16), m3, wf_bd)
    return y.reshape(b, n, w)


def _four_ctx_kernel(u_ref, wc_ref, m_ref, wf_ref, y_ref):
    w = FNET_WIDTH
    z = _dot(u_ref[...].astype(BF16), wc_ref[...])
    zs = jnp.concatenate([z[:, 0:w], z[:, w:2 * w]], axis=0).astype(BF16)
    y = _dot(m_ref[...], zs)
    y_ref[...] = _dot(y.astype(BF16), wf_ref[...]).astype(y_ref.dtype)


def _fourier_ctx(clf, wf_bd, consts):
    b, lc, _ = clf.shape
    w = FNET_WIDTH
    w_chan, m = consts[0].astype(BF16), consts[1].astype(BF16)
    return pl.pallas_call(
        _four_ctx_kernel,
        out_shape=jax.ShapeDtypeStruct((b, lc, w), BF16),
        grid=(b,),
        in_specs=[pl.BlockSpec((None, lc, w), lambda bi: (bi, 0, FU_BLOCK)), _full((w, 2 * w)), _full((lc, 2 * lc)),
                  _full((w, w))],
        out_specs=pl.BlockSpec((None, lc, w), lambda bi: (bi, 0, 0)),
        compiler_params=_params("parallel"),
        name="fourier_context",
    )(clf, w_chan, m, wf_bd)


def _pair_stack(x, lo):
    zero = jnp.zeros_like(x)
    return jnp.concatenate([jnp.where(lo, x, zero), jnp.where(lo, zero, x)], axis=0)


def _gla_kernel(qkvf_ref, laf_ref, qkvb_ref, lab_ref, s0_ref, of_ref, ob_ref, sfin_ref, stf_ref, stb_ref):
    i = pl.program_id(1)
    c = GLA_CHUNK
    nchunk = qkvf_ref.shape[0] // c

    @pl.when(i == 0)
    def _():
        stf_ref[...] = s0_ref[0]
        stb_ref[...] = s0_ref[1]

    r64 = lax.broadcasted_iota(jnp.int32, (c, c), 0)
    c64 = lax.broadcasted_iota(jnp.int32, (c, c), 1)
    tri = ((c64 <= r64).astype(BF16), (c64 >= r64).astype(BF16))
    at = lax.broadcasted_iota(jnp.int32, (c, LANES), 0)
    as_ = lax.broadcasted_iota(jnp.int32, (c, LANES), 1) & (c - 1)
    att_mask = (as_ <= at, as_ >= at)
    lo = lax.broadcasted_iota(jnp.int32, (1, LANES), 1) < HEAD_DIM
    br = lax.broadcasted_iota(jnp.int32, (LANES, LANES), 0) < HEAD_DIM
    bc = lax.broadcasted_iota(jnp.int32, (LANES, LANES), 1) < HEAD_DIM
    bd_mask = br == bc
    pairs = [slice(LANES * p, LANES * (p + 1)) for p in range(GLA_WIDTH // LANES)]
    in_refs = ((qkvf_ref, laf_ref), (qkvb_ref, lab_ref))
    out_refs = (of_ref, ob_ref)
    items = [(d, step if d == 0 else nchunk - 1 - step) for step in range(nchunk) for d in range(2)]
    rows = lambda ch: slice(c * ch, c * (ch + 1))

    bcum, work, o_intra, ut, decay = {}, {}, {}, {}, {}
    st_refs = (stf_ref, stb_ref)
    st = [[st_refs[d][LANES * j:LANES * (j + 1)] for j in range(len(pairs))] for d in range(2)]

    def gate_sums(d, ch):
        la = in_refs[d][1][rows(ch)]
        la_hi = la.astype(BF16)
        la_lo = (la - la_hi.astype(F32)).astype(BF16)
        bcum[d, ch] = _dot(jnp.concatenate([tri[d], tri[d]], axis=1), jnp.concatenate([la_hi, la_lo], axis=0))

    def scores(d, ch):
        qkv_ref = in_refs[d][0]
        w = GLA_WIDTH
        b = bcum[d, ch]
        btot = b[0:1] if d == 1 else b[c - 1:c]
        k = qkv_ref[rows(ch), w:2 * w].astype(F32)
        q_in = (qkv_ref[rows(ch), 0:w].astype(F32) * jnp.exp2(b)).astype(BF16)
        k_in = (k * jnp.exp2(-b)).astype(BF16)
        k_out = (k * jnp.exp2(btot - b)).astype(BF16)
        vb = qkv_ref[rows(ch), 2 * w:3 * w]
        att = [lax.dot_general(q_in[:, p], _pair_stack(k_in[:, p], lo), NT_DIMS, preferred_element_type=F32)
               for p in pairs]
        work[d, ch] = (btot, q_in, k_out, vb, att)

    def intra(d, ch):
        btot, q_in, k_out, vb, att = work[d, ch]
        o_intra[d, ch] = [_dot(jnp.where(att_mask[d], a, 0.0).astype(BF16), _pair_stack(vb[:, p], lo))
                          for a, p in zip(att, pairs)]
        ut[d, ch] = [lax.dot_general(k_out[:, p], vb[:, p], TN_DIMS, preferred_element_type=F32) for p in pairs]
        decay[d, ch] = [jnp.transpose(jnp.exp2(btot[:, p])) for p in pairs]

    def carry(d, ch):
        btot, q_in = work[d, ch][0:2]
        outs = []
        for j, p in enumerate(pairs):
            outs.append(o_intra[d, ch][j] + _dot(q_in[:, p], st[d][j].astype(BF16)))
            st[d][j] = st[d][j] * decay[d, ch][j] + jnp.where(bd_mask, ut[d, ch][j], 0.0)
        out_refs[d][rows(ch)] = jnp.concatenate(outs, axis=1).astype(out_refs[d].dtype)

    groups = [items[g:g + GLA_GROUP] for g in range(0, len(items), GLA_GROUP)]
    for it in groups[0]:
        gate_sums(*it)
    for it in groups[0]:
        scores(*it)
    for g, grp in enumerate(groups):
        for it in grp:
            intra(*it)
        nxt = groups[g + 1] if g + 1 < len(groups) else []
        for it in nxt:
            gate_sums(*it)
        for k_ in range(len(grp)):
            carry(*grp[k_])
            if k_ < len(nxt):
                scores(*nxt[k_])
    for d in range(2):
        st_refs[d][...] = jnp.concatenate(st[d], axis=0)

    @pl.when(i == pl.num_programs(1) - 1)
    def _():
        for d in range(2):
            sfin_ref[d] = jnp.concatenate(st[d], axis=0)


def _gla(g4, lf, s0, tb):
    b, n, _ = g4.shape
    w = GLA_WIDTH
    nblk = n // tb
    fwd = lambda bi, i: (bi, i, 0)
    bwd = lambda bi, i: (bi, nblk - 1 - i, 0)
    bwd_la = lambda bi, i: (bi, nblk - 1 - i, 1)
    tok = lambda f: pl.BlockSpec((None, tb, w), f)
    qkv = lambda f: pl.BlockSpec((None, tb, 3 * w), f)
    state = pl.BlockSpec((None, 2, w, LANES), lambda bi, i: (bi, 0, 0, 0))
    return pl.pallas_call(
        _gla_kernel,
        out_shape=[jax.ShapeDtypeStruct((b, n, w), BF16), jax.ShapeDtypeStruct((b, n, w), BF16),
                   jax.ShapeDtypeStruct((b, 2, w, LANES), F32)],
        grid=(b, nblk),
        in_specs=[qkv(fwd), tok(fwd), qkv(bwd), tok(bwd_la), state],
        out_specs=[tok(fwd), tok(bwd), state],
        scratch_shapes=[pltpu.VMEM((w, LANES), F32), pltpu.VMEM((w, LANES), F32)],
        compiler_params=_params("parallel", "arbitrary"),
        name="gla_scan",
    )(g4, lf, g4, lf, s0)


def _tail_kernel(att_ref, four_ref, of_ref, ob_ref, r_ref, x_ref, mod_ref, gg_ref, bd_ref, wmix_ref, g2_ref, wi_ref,
                 wo_ref, o_ref, a_ref):
    d = D_MODEL
    mod = lambda j: mod_ref[:, j * d:(j + 1) * d]
    a0, a1, a2 = ATT_WIDTH, ATT_WIDTH + FNET_WIDTH, ATT_WIDTH + FNET_WIDTH + GLA_WIDTH
    sub = min(x_ref.shape[0], TAIL_SUB)
    subs = [slice(sub * j, sub * (j + 1)) for j in range(x_ref.shape[0] // sub)]

    def mixed(rs):
        o = of_ref[rs].astype(F32) + ob_ref[rs].astype(F32)
        y = o * lax.rsqrt(_group_mean(o, bd_ref) + EPS) * gg_ref[...]
        y = y * _silu(r_ref[rs].astype(F32))
        mix = (_dot(att_ref[rs], wmix_ref[0:a0]) + _dot(four_ref[rs].astype(BF16), wmix_ref[a0:a1])
               + _dot(y.astype(BF16), wmix_ref[a1:a2]))
        x = x_ref[rs] + mod(2) * mix
        ms = jnp.mean(x * x, axis=-1, keepdims=True)
        return x, ((x * lax.rsqrt(ms + EPS) * g2_ref[...]) * (1.0 + mod(4)) + mod(3)).astype(BF16)

    pre = [mixed(rs) for rs in subs]
    for rs, (x, hb) in zip(subs, pre):
        for c0 in range(0, FFN_HIDDEN, FFN_CHUNK):
            g = _dot(hb, wi_ref[:, c0:c0 + FFN_CHUNK])
            u = _dot(hb, wi_ref[:, FFN_HIDDEN + c0:FFN_HIDDEN + c0 + FFN_CHUNK])
            a_ref[rs, c0:c0 + FFN_CHUNK] = (_silu(g) * u).astype(BF16)
        o_ref[rs] = x + mod(5) * _dot(a_ref[rs], wo_ref[...])


def _tail(att, four, of, ob, g4, x, mod3, mod_row, lw, tm):
    b, n, d = x.shape
    row = (lambda bi: bi) if mod_row is None else (lambda bi: mod_row)
    tok = lambda w: pl.BlockSpec((None, tm, w), lambda bi, i: (bi, i, 0))
    gate = pl.BlockSpec((None, tm, GLA_WIDTH), lambda bi, i: (bi, i, 3))
    resident = lambda shape: _layer_block(shape, lw["layer"])
    return pl.pallas_call(
        _tail_kernel,
        out_shape=jax.ShapeDtypeStruct((b, n, d), F32),
        grid=(b, n // tm),
        in_specs=[tok(ATT_WIDTH), tok(FNET_WIDTH), tok(GLA_WIDTH), tok(GLA_WIDTH), gate, tok(d),
                  pl.BlockSpec((None, 1, 6 * d), lambda bi, i: (row(bi), 0, 0)),
                  _full((1, GLA_WIDTH)), _full((GLA_WIDTH, GLA_WIDTH)), resident((d, d)),
                  _full((1, d)), resident((d, 2 * FFN_HIDDEN)), resident((FFN_HIDDEN, d))],
        out_specs=tok(d),
        scratch_shapes=[pltpu.VMEM((tm, FFN_HIDDEN), BF16)],
        compiler_params=_params("parallel", "parallel"),
        name="mix_ffn",
    )(att, four, of, ob, g4, x, mod3, lw["gla_g"], lw["bd256"], lw["w_out"], lw["g2"], lw["w_ffn_in"],
      lw["w_ffn_out"])


def _rope_tables(n):
    axis_dim = HEAD_DIM // 2
    inv_freq = ROPE_BASE ** (-np.arange(0, axis_dim, 2, dtype=np.float64) / axis_dim)
    t = np.arange(n)
    ang_r = (t // GRID_W)[:, None] * inv_freq[None, :]
    ang_c = (t % GRID_W)[:, None] * inv_freq[None, :]
    cos = np.concatenate([np.cos(ang_r)] * 2 + [np.cos(ang_c)] * 2, axis=1)
    sin = np.concatenate([-np.sin(ang_r), np.sin(ang_r), -np.sin(ang_c), np.sin(ang_c)], axis=1)
    return jnp.asarray(np.tile(cos, (1, 2)), F32), jnp.asarray(np.tile(sin, (1, 2)), F32)


def _block_diag_mean(width):
    g = np.arange(width) // HEAD_DIM
    return jnp.asarray((g[:, None] == g[None, :]) / HEAD_DIM, BF16)


def _layer_weights(l, big, w_in, g_norm1, q_norm_g, k_norm_g, w_fourier, wgf, bgf, wgb, bgb, gla_norm_g, g_norm2):
    r = GLA_GATE_RANK
    w_z = jnp.zeros((D_MODEL, LANES), F32).at[:, 0:2 * r].set(w_in[l, :, MAIN_WIDTH:MAIN_WIDTH + 2 * r])
    w_gate = jnp.zeros((LANES, 2 * GLA_WIDTH), F32)
    w_gate = w_gate.at[0:r, 0:GLA_WIDTH].set(wgf[l]).at[r:2 * r, GLA_WIDTH:].set(wgb[l])
    wf_bd = jnp.zeros((FNET_WIDTH, FNET_WIDTH), F32)
    for g in range(FNET_GROUPS):
        wf_bd = wf_bd.at[HEAD_DIM * g:HEAD_DIM * (g + 1), HEAD_DIM * g:HEAD_DIM * (g + 1)].set(w_fourier[l, g])
    return {
        "g1": g_norm1[l][None, :],
        "layer": l,
        "w_main": big["w_in"],
        "w_z": w_z.astype(BF16),
        "w_gate": w_gate.astype(BF16),
        "b_gate": jnp.concatenate([bgf[l], bgb[l]])[None, :],
        "q_g": jnp.tile(q_norm_g[l], 2)[None, :],
        "k_g": jnp.tile(k_norm_g[l], 2)[None, :],
        "bd256": _block_diag_mean(GLA_WIDTH),
        "wf_bd": wf_bd.astype(BF16),
        "gla_g": jnp.tile(gla_norm_g[l], GLA_HEADS)[None, :],
        "w_out": big["w_out"],
        "g2": g_norm2[l][None, :],
        "w_ffn_in": big["w_ffn_in"],
        "w_ffn_out": big["w_ffn_out"],
    }


def kernel(x, c, ctx, c_ctx, w_mod, b_mod, g_norm1, w_in, q_norm_g, k_norm_g, attn_sink, w_fourier, gla_w_gate_f,
           gla_b_gate_f, gla_w_gate_b, gla_b_gate_b, gla_norm_g, w_out, g_norm2, w_ffn_in, w_ffn_out):
    b, n, d = x.shape
    lc = ctx.shape[1]
    depth = w_mod.shape[0]
    assert d == D_MODEL and b <= 7 and n % 512 == 0 and n % (DFT_N1 * DFT_STEP) == 0 and lc % GLA_CHUNK == 0

    cc = jnp.zeros((8, d), F32).at[0:b].set(c).at[b].set(c_ctx)
    mod = _modulation(cc, w_mod, b_mod)
    rope_tabs = _rope_tables(n)
    dft = _dft_consts(n)
    dft_ctx = _dft_ctx_consts(lc)
    tm_in, tm_tail, tb_lat = 1024, 1024, 1024
    flat = lambda t: t.reshape(1, b * lc, t.shape[-1])
    unflat = lambda t: t.reshape(b, lc, t.shape[-1])
    xc = ctx
    big = {"w_in": w_in.astype(BF16), "w_out": w_out.astype(BF16), "w_ffn_in": w_ffn_in.astype(BF16),
           "w_ffn_out": w_ffn_out.astype(BF16)}
    for l in range(depth):
        need_ctx = l < depth - 1
        lw = _layer_weights(l, big, w_in, g_norm1, q_norm_g, k_norm_g, w_fourier, gla_w_gate_f, gla_b_gate_f,
                            gla_w_gate_b, gla_b_gate_b, gla_norm_g, g_norm2)
        mod3 = mod[l].reshape(8, 1, 6 * d)
        sink = attn_sink[l]
        cq, ckv4, cg4, clf = map(unflat, _inproj(flat(xc), mod3, b, lw, None, b * lc))
        q, kv4, g4, lf = _inproj(x, mod3, None, lw, rope_tabs, tm_in)
        att = _attention(q, kv4, ckv4, sink)
        four = _fourier(lf, lw["wf_bd"], dft)
        s_zero = jnp.zeros((b, 2, GLA_WIDTH, LANES), F32)
        ocf, ocb, s_ctx = _gla(cg4, clf, s_zero, lc)
        olf, olb, _ = _gla(g4, lf, s_ctx, tb_lat)
        x = _tail(att, four, olf, olb, g4, x, mod3, None, lw, tm_tail)
        if need_ctx:
            att_c = _attention_ctx(cq, ckv4, sink)
            four_c = _fourier_ctx(clf, lw["wf_bd"], dft_ctx)
            xc = unflat(_tail(*map(flat, (att_c, four_c, ocf, ocb, cg4, xc)), mod3, b, lw, b * lc))
    return x
```

```python
import functools

import numpy as np
import jax
import jax.numpy as jnp
from jax import lax
from jax.experimental import pallas as pl
from jax.experimental.pallas import tpu as pltpu

F32 = jnp.float32
BF16 = jnp.bfloat16

D_MODEL = 1024
HEAD_DIM = 64
GRID_W = 64
ROPE_BASE = 10000.0
ATT_HEADS = 8
ATT_KV_HEADS = 2
ATT_WIDTH = ATT_HEADS * HEAD_DIM
KV_WIDTH = ATT_KV_HEADS * HEAD_DIM
ATT_BLOCK = 128
ATT_QBLOCKS = 8
ATT_SCALE = HEAD_DIM ** -0.5
LOG2E = 1.4426950408889634
NEG_INF = -1e30
FNET_GROUPS = 4
FNET_WIDTH = FNET_GROUPS * HEAD_DIM
GLA_HEADS = 4
GLA_WIDTH = GLA_HEADS * HEAD_DIM
GLA_GATE_RANK = 16
GLA_TAU = 16.0
GLA_CHUNK = 64
GLA_SCALE = HEAD_DIM ** -0.5
MAIN_WIDTH = ATT_WIDTH + 2 * KV_WIDTH + FNET_WIDTH + 4 * GLA_WIDTH
FFN_HIDDEN = 2816
FFN_CHUNK = 256
INPROJ_SUB = 512
TAIL_SUB = 512
FU_BLOCK = 2
GLA_GROUP = 16
EPS = 1e-6
LANES = 128
MOD_ROWS = 8
DFT_N1 = 128
DFT_STEP = 16
VMEM_LIMIT = 56 * 1024 * 1024

NT_DIMS = (((1,), (1,)), ((), ()))
TN_DIMS = (((0,), (0,)), ((), ()))


def _params(*sem):
    return pltpu.CompilerParams(dimension_semantics=sem, vmem_limit_bytes=VMEM_LIMIT)


def _dot(a, b):
    return jnp.dot(a, b, preferred_element_type=F32)


def _silu(x):
    return x / (1.0 + jnp.exp(-x))


def _layer_block(shape, layer):
    nd = len(shape)
    return pl.BlockSpec((None,) + tuple(shape), lambda *_: (layer,) + (0,) * nd, pipeline_mode=pl.Buffered(1))


def _full(shape):
    nd = len(shape)
    return pl.BlockSpec(shape, lambda *_: (0,) * nd)


def _group_mean(t, bd_ref):
    return _dot((t * t).astype(BF16), bd_ref[...])


def _mod_kernel(c_ref, w_ref, b_ref, o_ref):
    s = _silu(c_ref[...]).astype(BF16)
    o_ref[...] = _dot(s, w_ref[...].astype(BF16)) + b_ref[...]


def _modulation(cc, w_mod, b_mod):
    depth, d, width = w_mod.shape
    tn = 1536
    return pl.pallas_call(
        _mod_kernel,
        out_shape=jax.ShapeDtypeStruct((depth, MOD_ROWS, width), F32),
        grid=(depth, width // tn),
        in_specs=[
            _full((MOD_ROWS, d)),
            pl.BlockSpec((None, d, tn), lambda l, j: (l, 0, j)),
            pl.BlockSpec((None, 1, tn), lambda l, j: (l, 0, j)),
        ],
        out_specs=pl.BlockSpec((None, MOD_ROWS, tn), lambda l, j: (l, 0, j)),
        compiler_params=_params("parallel", "parallel"),
        name="modulation",
    )(cc, w_mod, b_mod.reshape(depth, 1, width))


def _inproj_kernel(*refs, rope):
    if rope:
        (x_ref, mod_ref, g1_ref, wm_ref, wz_ref, wg_ref, bg_ref, qg_ref, kg_ref, bd_ref, cos_ref, sin_ref,
         q_ref, kv4_ref, g4_ref, lf_ref) = refs
    else:
        (x_ref, mod_ref, g1_ref, wm_ref, wz_ref, wg_ref, bg_ref, qg_ref, kg_ref, bd_ref,
         q_ref, kv4_ref, g4_ref, lf_ref) = refs
    d = D_MODEL
    mod = mod_ref[...]
    sh, sc = mod[:, 0:d], mod[:, d:2 * d]
    sub = min(x_ref.shape[0], INPROJ_SUB)
    subs = [slice(sub * j, sub * (j + 1)) for j in range(x_ref.shape[0] // sub)]

    def normed(rs):
        x = x_ref[rs]
        ms = jnp.mean(x * x, axis=-1, keepdims=True)
        return ((x * lax.rsqrt(ms + EPS) * g1_ref[...]) * (1.0 + sc) + sh).astype(BF16)

    lane = lax.broadcasted_iota(jnp.int32, (1, LANES), 1)
    lo = lane < HEAD_DIM
    second_half = (lane & 16) != 0

    def head_norm(t, g_ref):
        wdt = t.shape[1]
        ms = _dot((t * t).astype(BF16), bd_ref[0:wdt, 0:wdt])
        g = g_ref[...] if wdt == LANES else jnp.concatenate([g_ref[...]] * (wdt // LANES), axis=1)
        return t * lax.rsqrt(ms + EPS) * g

    def rotary(t, rs):
        if not rope:
            return t
        partner = jnp.where(second_half, pltpu.roll(t, 16, 1), pltpu.roll(t, LANES - 16, 1))
        return t * cos_ref[rs] + partner * sin_ref[rs]

    def spread(t, c0, rs, idle):
        tr = pltpu.roll(t, HEAD_DIM, 1)
        fill = jnp.full_like(t, idle)
        tiles = (jnp.where(lo, t, fill), jnp.where(lo, fill, tr), jnp.where(lo, tr, fill), jnp.where(lo, fill, t))
        for j, tile in enumerate(tiles):
            kv4_ref[rs, c0 + LANES * j:c0 + LANES * (j + 1)] = tile.astype(kv4_ref.dtype)

    assert (ATT_WIDTH, 2 * KV_WIDTH + FNET_WIDTH, 2 * GLA_WIDTH) == (4 * LANES,) * 3

    def project(rs, hb):
        group = lambda g: _dot(hb, wm_ref[:, 4 * LANES * g:4 * LANES * (g + 1)])
        gz = _dot(hb, wz_ref[...])
        a_q = group(0)
        z = _dot(gz.astype(BF16), wg_ref[...]) + bg_ref[...]
        log_sig = jnp.minimum(z, 0.0) - jnp.log(1.0 + jnp.exp(-jnp.abs(z)))
        lf_ref[rs, 0:2 * GLA_WIDTH] = log_sig * (LOG2E / GLA_TAU)
        a_kvf = group(1)
        for j2 in range(ATT_WIDTH // (2 * LANES)):
            qn = head_norm(a_q[:, 2 * LANES * j2:2 * LANES * (j2 + 1)], qg_ref)
            for j in range(2):
                t = rotary(qn[:, LANES * j:LANES * (j + 1)], rs) * (ATT_SCALE * LOG2E)
                q_ref[rs, LANES * (2 * j2 + j):LANES * (2 * j2 + j + 1)] = t.astype(q_ref.dtype)
        a_qk = group(2)
        spread(rotary(head_norm(a_kvf[:, 0:KV_WIDTH], kg_ref), rs), 0, rs, 0.0)
        spread(a_kvf[:, KV_WIDTH:2 * KV_WIDTH], 4 * LANES, rs, 1.0)
        lf_ref[rs, 2 * GLA_WIDTH:] = a_kvf[:, 2 * KV_WIDTH:]
        a_vr = group(3)
        g4_ref[rs, 0:GLA_WIDTH] = (a_qk[:, 0:GLA_WIDTH] * GLA_SCALE).astype(g4_ref.dtype)
        g4_ref[rs, GLA_WIDTH:2 * GLA_WIDTH] = a_qk[:, GLA_WIDTH:].astype(g4_ref.dtype)
        g4_ref[rs, 2 * GLA_WIDTH:4 * GLA_WIDTH] = a_vr.astype(g4_ref.dtype)

    hbs = [normed(rs) for rs in subs]
    for rs, hb in zip(subs, hbs):
        project(rs, hb)


def _inproj(x, mod3, mod_row, lw, rope_tabs, tm):
    b, n, d = x.shape
    rope = rope_tabs is not None
    row = lambda bi: MOD_ROWS * lw["layer"] + (bi if mod_row is None else mod_row)
    tok = lambda w: pl.BlockSpec((None, tm, w), lambda bi, i: (bi, i, 0))
    lb = lambda shape: _layer_block(shape, lw["layer"])
    in_specs = [
        tok(d),
        pl.BlockSpec((None, 1, 6 * d), lambda bi, i: (row(bi), 0, 0)),
        lb((1, d)), lb((d, MAIN_WIDTH)), lb((d, LANES)), lb((LANES, 2 * GLA_WIDTH)),
        lb((1, 2 * GLA_WIDTH)), lb((1, LANES)), lb((1, LANES)), _full((2 * LANES, 2 * LANES)),
    ]
    args = [x, mod3, lw["g1"], lw["w_main"], lw["w_z"], lw["w_gate"], lw["b_gate"], lw["q_g"], lw["k_g"], lw["bd256"]]
    if rope:
        in_specs += [pl.BlockSpec((tm, LANES), lambda bi, i: (i, 0))] * 2
        args += list(rope_tabs)
    sds = lambda w, dt: jax.ShapeDtypeStruct((b, n, w), dt)
    out_shape = [sds(ATT_WIDTH, BF16), sds(8 * LANES, BF16), sds(4 * GLA_WIDTH, BF16),
                 sds(2 * GLA_WIDTH + FNET_WIDTH, F32)]
    out_specs = [tok(s.shape[-1]) for s in out_shape]
    return pl.pallas_call(
        functools.partial(_inproj_kernel, rope=rope),
        out_shape=out_shape, grid=(b, n // tm), in_specs=in_specs, out_specs=out_specs,
        compiler_params=_params("parallel", "parallel"),
        name="inproj_rope" if rope else "inproj_ctx",
    )(*args)


def _scores(q2, key_tiles, masks):
    cols = []
    for k, tile_masks in zip(key_tiles, masks):
        s = lax.dot_general(q2, k, NT_DIMS, preferred_element_type=F32)
        for j, mk in enumerate(tile_masks):
            c = s[:, LANES * j:LANES * (j + 1)]
            cols.append(c if mk is None else jnp.where(mk, c, NEG_INF))
    return cols


def _softmax_pv(cols, val_tiles, sink_col):
    mx = cols[0]
    for c in cols[1:]:
        mx = jnp.maximum(mx, c)
    m = jnp.maximum(jnp.max(mx, axis=-1, keepdims=True), sink_col)
    p = jnp.concatenate([jnp.exp2(c - m).astype(BF16) for c in cols], axis=1)
    pv = _dot(p, jnp.concatenate(val_tiles, axis=0))
    den = pltpu.roll(pv, HEAD_DIM, 1) + jnp.exp2(sink_col - m)
    return pv / den


def _attn_kernel(*refs):
    nq = ATT_QBLOCKS
    sink_ref, q_ref = refs[0:2]
    kv_refs, kvc_ref, o_ref = refs[2:nq + 4], refs[nq + 4], refs[nq + 5]
    v_off = 4 * LANES
    i = pl.program_id(1)
    last = pl.num_programs(1) - 1
    blk = ATT_BLOCK
    group = ATT_HEADS // ATT_KV_HEADS
    row = lax.broadcasted_iota(jnp.int32, (2 * blk, blk), 0) & (blk - 1)
    col = lax.broadcasted_iota(jnp.int32, (2 * blk, blk), 1)
    band_l, band_r = col >= row, col <= row
    mask_l = [jnp.logical_and(band_l, i > 0) if s == 0 else band_l for s in range(nq)]
    mask_r = [jnp.logical_and(band_r, i < last) if s == nq - 1 else band_r for s in range(nq)]
    upper_rows = lax.broadcasted_iota(jnp.int32, (2 * blk, 1), 0) >= blk
    lo = lax.broadcasted_iota(jnp.int32, (1, LANES), 1) < HEAD_DIM

    def tiles(s, ks):
        l, m_, r = kv_refs[s:s + 3]
        return [jnp.concatenate([l[:, ks], m_[:, ks]], axis=0),
                jnp.concatenate([r[:, ks], kvc_ref[0:blk, ks]], axis=0), kvc_ref[blk:, ks]]

    slots = [(s, kvh, par) for s in range(nq) for kvh in range(ATT_KV_HEADS) for par in range(2)]
    lane_slice = lambda kvh, par, off=0: slice(off + 2 * LANES * kvh + LANES * par,
                                               off + 2 * LANES * kvh + LANES * (par + 1))
    cols = {}
    for s, kvh, par in slots:
        qs, rs = 2 * LANES * kvh, slice(blk * s, blk * (s + 1))
        q2 = jnp.concatenate([q_ref[rs, qs:qs + LANES], q_ref[rs, qs + LANES:qs + 2 * LANES]], axis=0)
        cols[s, kvh, par] = _scores(q2, tiles(s, lane_slice(kvh, par)),
                                    [(mask_l[s], None), (mask_r[s], None), (None,)])
    outs = {}
    for s, kvh, par in slots:
        base = group * kvh + par
        sink_col = jnp.where(upper_rows, sink_ref[base + 2], sink_ref[base]) * LOG2E
        outs[s, kvh, par] = _softmax_pv(cols[s, kvh, par], tiles(s, lane_slice(kvh, par, v_off)), sink_col)
    for s in range(nq):
        for kvh in range(ATT_KV_HEADS):
            qs, r0 = 2 * LANES * kvh, blk * s
            o = jnp.where(lo, outs[s, kvh, 0], outs[s, kvh, 1])
            o_ref[r0:r0 + blk, qs:qs + LANES] = o[0:blk].astype(o_ref.dtype)
            o_ref[r0:r0 + blk, qs + LANES:qs + 2 * LANES] = o[blk:2 * blk].astype(o_ref.dtype)


def _attention(q, kv4, ckv4, sink):
    b, n, _ = q.shape
    lc = ckv4.shape[1]
    nq = ATT_QBLOCKS
    nb = n // ATT_BLOCK
    assert lc == 2 * ATT_BLOCK and nb % nq == 0
    qspec = pl.BlockSpec((None, nq * ATT_BLOCK, 4 * LANES), lambda bi, i: (bi, i, 0))
    kv = [pl.BlockSpec((None, ATT_BLOCK, 8 * LANES),
                       functools.partial(lambda bi, i, j: (bi, jnp.clip(nq * i + j - 1, 0, nb - 1), 0), j=j))
          for j in range(nq + 2)]
    ctx = pl.BlockSpec((None, lc, 8 * LANES), lambda bi, i: (bi, 0, 0))
    return pl.pallas_call(
        _attn_kernel,
        out_shape=jax.ShapeDtypeStruct((b, n, ATT_WIDTH), BF16),
        grid=(b, nb // nq),
        in_specs=[pl.BlockSpec(memory_space=pltpu.SMEM), qspec] + kv + [ctx],
        out_specs=qspec,
        compiler_params=_params("parallel", "parallel"),
        name="window_attention",
    )(sink, q, *([kv4] * (nq + 2)), ckv4)


def _attn_ctx_kernel(sink_ref, q_ref, kc_ref, vc_ref, o_ref):
    kvh = pl.program_id(1)
    lc = q_ref.shape[0]
    q2 = jnp.concatenate([q_ref[:, 0:LANES], q_ref[:, LANES:2 * LANES]], axis=0)
    upper_rows = lax.broadcasted_iota(jnp.int32, (2 * lc, 1), 0) >= lc
    lo = lax.broadcasted_iota(jnp.int32, (1, LANES), 1) < HEAD_DIM
    outs = []
    for par in range(2):
        ks = slice(LANES * par, LANES * (par + 1))
        base = ATT_HEADS // ATT_KV_HEADS * kvh + par
        sink_col = jnp.where(upper_rows, sink_ref[base + 2], sink_ref[base]) * LOG2E
        cols = _scores(q2, [kc_ref[:, ks]], [(None,) * (lc // LANES)])
        outs.append(_softmax_pv(cols, [vc_ref[:, ks]], sink_col))
    o = jnp.where(lo, outs[0], outs[1])
    o_ref[:, 0:LANES] = o[0:lc].astype(o_ref.dtype)
    o_ref[:, LANES:2 * LANES] = o[lc:2 * lc].astype(o_ref.dtype)


def _attention_ctx(cq, ckv4, sink):
    b, lc, _ = cq.shape
    gw = 2 * LANES
    spec = pl.BlockSpec((None, lc, gw), lambda bi, h: (bi, 0, h))
    vspec = pl.BlockSpec((None, lc, gw), lambda bi, h: (bi, 0, ATT_KV_HEADS + h))
    return pl.pallas_call(
        _attn_ctx_kernel,
        out_shape=jax.ShapeDtypeStruct((b, lc, ATT_WIDTH), BF16),
        grid=(b, ATT_KV_HEADS),
        in_specs=[pl.BlockSpec(memory_space=pltpu.SMEM), spec, spec, vspec],
        out_specs=spec,
        compiler_params=_params("parallel", "parallel"),
        name="context_attention",
    )(sink, cq, ckv4, ckv4)


def _dft_consts(n):
    n1, n2 = DFT_N1, n // DFT_N1
    ang = lambda a, m: 2.0 * np.pi * np.outer(np.arange(a), np.arange(a)) / m
    c = HEAD_DIM
    cc, sc = np.cos(ang(c, c)) / np.sqrt(c), np.sin(ang(c, c)) / np.sqrt(c)
    eye = np.eye(FNET_GROUPS)
    w_chan = np.concatenate([np.kron(eye, cc), -np.kron(eye, sc)], axis=1)
    c1, s1 = np.cos(ang(n1, n1)) / np.sqrt(n1), np.sin(ang(n1, n1)) / np.sqrt(n1)
    m1 = np.block([[c1, s1], [-s1, c1]])
    kk = np.arange(n1)[:, None, None] + n1 * np.arange(n2)[None, :, None]
    ph = 2.0 * np.pi * (kk * np.arange(n2)[None, None, :] % n) / n
    m3 = np.concatenate([np.cos(ph), np.sin(ph)], axis=2) / np.sqrt(n2)
    return tuple(jnp.asarray(a, F32) for a in (w_chan, m1, m3))


def _dft_ctx_consts(lc):
    ang = lambda a, m: 2.0 * np.pi * np.outer(np.arange(a), np.arange(a)) / m
    c = HEAD_DIM
    cc, sc = np.cos(ang(c, c)) / np.sqrt(c), np.sin(ang(c, c)) / np.sqrt(c)
    eye = np.eye(FNET_GROUPS)
    w_chan = np.concatenate([np.kron(eye, cc), -np.kron(eye, sc)], axis=1)
    cl, sl = np.cos(ang(lc, lc)) / np.sqrt(lc), np.sin(ang(lc, lc)) / np.sqrt(lc)
    return jnp.asarray(w_chan, F32), jnp.asarray(np.concatenate([cl, sl], axis=1), F32)


def _regroup_rows(x, outer):
    r, c = x.shape
    return jnp.swapaxes(x.reshape(outer, r // outer, c), 0, 1).reshape(r, c)


def _fourier_kernel(u_ref, wc_ref, m1_ref, m3_ref, wf_ref, y_ref, p_ref):
    w, t = FNET_WIDTH, DFT_STEP
    s = pl.program_id(1)
    n_a = p_ref.shape[1]
    n2 = n_a * t

    @pl.when(s < n_a)
    def _():
        u = u_ref[...].reshape(DFT_N1 * t, w).astype(BF16)
        z = _regroup_rows(_dot(u, wc_ref[...]), DFT_N1)
        a_re, a_im = [], []
        for tt in range(t):
            zt = z[DFT_N1 * tt:DFT_N1 * (tt + 1)]
            zs = jnp.concatenate([zt[:, 0:w], zt[:, w:2 * w]], axis=0).astype(BF16)
            a = _dot(m1_ref[...], zs)
            a_re.append(a[0:DFT_N1])
            a_im.append(a[DFT_N1:2 * DFT_N1])
        for part, rows in enumerate((a_re, a_im)):
            a = _regroup_rows(jnp.concatenate(rows, axis=0), t)
            p_ref[part, s] = a.reshape(DFT_N1, t, w)

    @pl.when(s >= n_a)
    def _():
        k1_0 = (s - n_a) * t
        ys = []
        for kk in range(t):
            parts = [p_ref[part, :, k1_0 + kk].reshape(n2, w) for part in range(2)]
            rhs = jnp.concatenate(parts, axis=0).astype(BF16)
            ys.append(_dot(m3_ref[kk].astype(BF16), rhs).astype(BF16))
        out = _dot(jnp.concatenate(ys, axis=0), wf_ref[...])
        y_ref[...] = _regroup_rows(out, t).reshape(n2, t, w)


def _fourier(lf, wf_bd, layer, consts):
    b, n, wl = lf.shape
    w = FNET_WIDTH
    n1, n2, t = DFT_N1, n // DFT_N1, DFT_STEP
    n_a, n_b = n2 // t, n1 // t
    w_chan, m1, m3 = consts
    y = pl.pallas_call(
        _fourier_kernel,
        out_shape=jax.ShapeDtypeStruct((b, n2, n1, w), F32),
        grid=(b, n_a + n_b),
        in_specs=[pl.BlockSpec((None, n1, t, w), lambda bi, s: (bi, 0, jnp.minimum(s, n_a - 1), FU_BLOCK)),
                  _full((w, 2 * w)), _full((2 * n1, 2 * n1)),
                  pl.BlockSpec((t, n2, 2 * n2), lambda bi, s: (jnp.maximum(s - n_a, 0), 0, 0)),
                  _layer_block((w, w), layer)],
        out_specs=pl.BlockSpec((None, n2, t, w), lambda bi, s: (bi, 0, jnp.maximum(s - n_a, 0), 0)),
        scratch_shapes=[pltpu.VMEM((2, n_a, n1, t, w), F32)],
        compiler_params=_params("parallel", "arbitrary"),
        name="fourier_mix",
    )(lf.reshape(b, n1, n2, wl), w_chan.astype(BF16), m1.astype(BF16), m3, wf_bd)
    return y.reshape(b, n, w)


def _four_ctx_kernel(u_ref, wc_ref, m_ref, wf_ref, y_ref):
    w = FNET_WIDTH
    z = _dot(u_ref[...].astype(BF16), wc_ref[...])
    zs = jnp.concatenate([z[:, 0:w], z[:, w:2 * w]], axis=0).astype(BF16)
    y = _dot(m_ref[...], zs)
    y_ref[...] = _dot(y.astype(BF16), wf_ref[...]).astype(y_ref.dtype)


def _fourier_ctx(clf, wf_bd, layer, consts):
    b, lc, _ = clf.shape
    w = FNET_WIDTH
    w_chan, m = consts[0].astype(BF16), consts[1].astype(BF16)
    return pl.pallas_call(
        _four_ctx_kernel,
        out_shape=jax.ShapeDtypeStruct((b, lc, w), BF16),
        grid=(b,),
        in_specs=[pl.BlockSpec((None, lc, w), lambda bi: (bi, 0, FU_BLOCK)), _full((w, 2 * w)), _full((lc, 2 * lc)),
                  _layer_block((w, w), layer)],
        out_specs=pl.BlockSpec((None, lc, w), lambda bi: (bi, 0, 0)),
        compiler_params=_params("parallel"),
        name="fourier_context",
    )(clf, w_chan, m, wf_bd)


def _pair_stack(x, lo):
    zero = jnp.zeros_like(x)
    return jnp.concatenate([jnp.where(lo, x, zero), jnp.where(lo, zero, x)], axis=0)


def _gla_kernel(qkvf_ref, laf_ref, qkvb_ref, lab_ref, s0_ref, of_ref, ob_ref, sfin_ref, stf_ref, stb_ref):
    i = pl.program_id(1)
    c = GLA_CHUNK
    nchunk = qkvf_ref.shape[0] // c

    @pl.when(i == 0)
    def _():
        stf_ref[...] = s0_ref[0]
        stb_ref[...] = s0_ref[1]

    r64 = lax.broadcasted_iota(jnp.int32, (c, c), 0)
    c64 = lax.broadcasted_iota(jnp.int32, (c, c), 1)
    tri = ((c64 <= r64).astype(BF16), (c64 >= r64).astype(BF16))
    at = lax.broadcasted_iota(jnp.int32, (c, LANES), 0)
    as_ = lax.broadcasted_iota(jnp.int32, (c, LANES), 1) & (c - 1)
    att_mask = (as_ <= at, as_ >= at)
    lo = lax.broadcasted_iota(jnp.int32, (1, LANES), 1) < HEAD_DIM
    br = lax.broadcasted_iota(jnp.int32, (LANES, LANES), 0) < HEAD_DIM
    bc = lax.broadcasted_iota(jnp.int32, (LANES, LANES), 1) < HEAD_DIM
    bd_mask = br == bc
    pairs = [slice(LANES * p, LANES * (p + 1)) for p in range(GLA_WIDTH // LANES)]
    in_refs = ((qkvf_ref, laf_ref), (qkvb_ref, lab_ref))
    out_refs = (of_ref, ob_ref)
    items = [(d, step if d == 0 else nchunk - 1 - step) for step in range(nchunk) for d in range(2)]
    rows = lambda ch: slice(c * ch, c * (ch + 1))

    bcum, work, o_intra, ut, decay = {}, {}, {}, {}, {}
    st_refs = (stf_ref, stb_ref)
    st = [[st_refs[d][LANES * j:LANES * (j + 1)] for j in range(len(pairs))] for d in range(2)]

    def gate_sums(d, ch):
        la = in_refs[d][1][rows(ch)]
        la_hi = la.astype(BF16)
        la_lo = (la - la_hi.astype(F32)).astype(BF16)
        bcum[d, ch] = _dot(jnp.concatenate([tri[d], tri[d]], axis=1), jnp.concatenate([la_hi, la_lo], axis=0))

    def scores(d, ch):
        qkv_ref = in_refs[d][0]
        w = GLA_WIDTH
        b = bcum[d, ch]
        btot = b[0:1] if d == 1 else b[c - 1:c]
        k = qkv_ref[rows(ch), w:2 * w].astype(F32)
        q_in = (qkv_ref[rows(ch), 0:w].astype(F32) * jnp.exp2(b)).astype(BF16)
        k_in = (k * jnp.exp2(-b)).astype(BF16)
        k_out = (k * jnp.exp2(btot - b)).astype(BF16)
        vb = qkv_ref[rows(ch), 2 * w:3 * w]
        att = [lax.dot_general(q_in[:, p], _pair_stack(k_in[:, p], lo), NT_DIMS, preferred_element_type=F32)
               for p in pairs]
        work[d, ch] = (btot, q_in, k_out, vb, att)

    def intra(d, ch):
        btot, q_in, k_out, vb, att = work[d, ch]
        o_intra[d, ch] = [_dot(jnp.where(att_mask[d], a, 0.0).astype(BF16), _pair_stack(vb[:, p], lo))
                          for a, p in zip(att, pairs)]
        ut[d, ch] = [lax.dot_general(k_out[:, p], vb[:, p], TN_DIMS, preferred_element_type=F32) for p in pairs]
        decay[d, ch] = [jnp.transpose(jnp.exp2(btot[:, p])) for p in pairs]

    def carry(d, ch):
        btot, q_in = work[d, ch][0:2]
        outs = []
        for j, p in enumerate(pairs):
            outs.append(o_intra[d, ch][j] + _dot(q_in[:, p], st[d][j].astype(BF16)))
            st[d][j] = st[d][j] * decay[d, ch][j] + jnp.where(bd_mask, ut[d, ch][j], 0.0)
        out_refs[d][rows(ch)] = jnp.concatenate(outs, axis=1).astype(out_refs[d].dtype)

    groups = [items[g:g + GLA_GROUP] for g in range(0, len(items), GLA_GROUP)]
    for it in groups[0]:
        gate_sums(*it)
    for it in groups[0]:
        scores(*it)
    for g, grp in enumerate(groups):
        for it in grp:
            intra(*it)
        nxt = groups[g + 1] if g + 1 < len(groups) else []
        for it in nxt:
            gate_sums(*it)
        for k_ in range(len(grp)):
            carry(*grp[k_])
            if k_ < len(nxt):
                scores(*nxt[k_])
    for d in range(2):
        st_refs[d][...] = jnp.concatenate(st[d], axis=0)

    @pl.when(i == pl.num_programs(1) - 1)
    def _():
        for d in range(2):
            sfin_ref[d] = jnp.concatenate(st[d], axis=0)


def _gla(g4, lf, s0, tb):
    b, n, _ = g4.shape
    w = GLA_WIDTH
    nblk = n // tb
    fwd = lambda bi, i: (bi, i, 0)
    bwd = lambda bi, i: (bi, nblk - 1 - i, 0)
    bwd_la = lambda bi, i: (bi, nblk - 1 - i, 1)
    tok = lambda f: pl.BlockSpec((None, tb, w), f)
    qkv = lambda f: pl.BlockSpec((None, tb, 3 * w), f)
    state = pl.BlockSpec((None, 2, w, LANES), lambda bi, i: (bi, 0, 0, 0))
    return pl.pallas_call(
        _gla_kernel,
        out_shape=[jax.ShapeDtypeStruct((b, n, w), BF16), jax.ShapeDtypeStruct((b, n, w), BF16),
                   jax.ShapeDtypeStruct((b, 2, w, LANES), F32)],
        grid=(b, nblk),
        in_specs=[qkv(fwd), tok(fwd), qkv(bwd), tok(bwd_la), state],
        out_specs=[tok(fwd), tok(bwd), state],
        scratch_shapes=[pltpu.VMEM((w, LANES), F32), pltpu.VMEM((w, LANES), F32)],
        compiler_params=_params("parallel", "arbitrary"),
        name="gla_scan",
    )(g4, lf, g4, lf, s0)


def _tail_kernel(att_ref, four_ref, of_ref, ob_ref, r_ref, x_ref, mod_ref, gg_ref, bd_ref, wmix_ref, g2_ref, wi_ref,
                 wo_ref, o_ref, a_ref):
    d = D_MODEL
    mod = lambda j: mod_ref[:, j * d:(j + 1) * d]
    a0, a1, a2 = ATT_WIDTH, ATT_WIDTH + FNET_WIDTH, ATT_WIDTH + FNET_WIDTH + GLA_WIDTH
    sub = min(x_ref.shape[0], TAIL_SUB)
    subs = [slice(sub * j, sub * (j + 1)) for j in range(x_ref.shape[0] // sub)]

    def mixed(rs):
        o = of_ref[rs].astype(F32) + ob_ref[rs].astype(F32)
        y = o * lax.rsqrt(_group_mean(o, bd_ref) + EPS) * gg_ref[...]
        y = y * _silu(r_ref[rs].astype(F32))
        mix = (_dot(att_ref[rs], wmix_ref[0:a0]) + _dot(four_ref[rs].astype(BF16), wmix_ref[a0:a1])
               + _dot(y.astype(BF16), wmix_ref[a1:a2]))
        x = x_ref[rs] + mod(2) * mix
        ms = jnp.mean(x * x, axis=-1, keepdims=True)
        return x, ((x * lax.rsqrt(ms + EPS) * g2_ref[...]) * (1.0 + mod(4)) + mod(3)).astype(BF16)

    pre = [mixed(rs) for rs in subs]
    for rs, (x, hb) in zip(subs, pre):
        for c0 in range(0, FFN_HIDDEN, FFN_CHUNK):
            g = _dot(hb, wi_ref[:, c0:c0 + FFN_CHUNK])
            u = _dot(hb, wi_ref[:, FFN_HIDDEN + c0:FFN_HIDDEN + c0 + FFN_CHUNK])
            a_ref[rs, c0:c0 + FFN_CHUNK] = (_silu(g) * u).astype(BF16)
        o_ref[rs] = x + mod(5) * _dot(a_ref[rs], wo_ref[...])


def _tail(att, four, of, ob, g4, x, mod3, mod_row, lw, tm):
    b, n, d = x.shape
    row = lambda bi: MOD_ROWS * lw["layer"] + (bi if mod_row is None else mod_row)
    tok = lambda w: pl.BlockSpec((None, tm, w), lambda bi, i: (bi, i, 0))
    gate = pl.BlockSpec((None, tm, GLA_WIDTH), lambda bi, i: (bi, i, 3))
    resident = lambda shape: _layer_block(shape, lw["layer"])
    return pl.pallas_call(
        _tail_kernel,
        out_shape=jax.ShapeDtypeStruct((b, n, d), F32),
        grid=(b, n // tm),
        in_specs=[tok(ATT_WIDTH), tok(FNET_WIDTH), tok(GLA_WIDTH), tok(GLA_WIDTH), gate, tok(d),
                  pl.BlockSpec((None, 1, 6 * d), lambda bi, i: (row(bi), 0, 0)),
                  resident((1, GLA_WIDTH)), _full((GLA_WIDTH, GLA_WIDTH)), resident((d, d)),
                  resident((1, d)), resident((d, 2 * FFN_HIDDEN)), resident((FFN_HIDDEN, d))],
        out_specs=tok(d),
        scratch_shapes=[pltpu.VMEM((tm, FFN_HIDDEN), BF16)],
        compiler_params=_params("parallel", "parallel"),
        name="mix_ffn",
    )(att, four, of, ob, g4, x, mod3, lw["gla_g"], lw["bd256"], lw["w_out"], lw["g2"], lw["w_ffn_in"],
      lw["w_ffn_out"])


def _rope_tables(n):
    axis_dim = HEAD_DIM // 2
    inv_freq = ROPE_BASE ** (-np.arange(0, axis_dim, 2, dtype=np.float64) / axis_dim)
    t = np.arange(n)
    ang_r = (t // GRID_W)[:, None] * inv_freq[None, :]
    ang_c = (t % GRID_W)[:, None] * inv_freq[None, :]
    cos = np.concatenate([np.cos(ang_r)] * 2 + [np.cos(ang_c)] * 2, axis=1)
    sin = np.concatenate([-np.sin(ang_r), np.sin(ang_r), -np.sin(ang_c), np.sin(ang_c)], axis=1)
    return jnp.asarray(np.tile(cos, (1, 2)), F32), jnp.asarray(np.tile(sin, (1, 2)), F32)


def _block_diag_mean(width):
    g = np.arange(width) // HEAD_DIM
    return jnp.asarray((g[:, None] == g[None, :]) / HEAD_DIM, BF16)


def _prepare_weights(w_in, g_norm1, q_norm_g, k_norm_g, w_fourier, wgf, bgf, wgb, bgb, gla_norm_g, w_out, g_norm2,
                     w_ffn_in, w_ffn_out):
    depth = w_in.shape[0]
    r = GLA_GATE_RANK
    row = lambda t: t[:, None, :]
    zr = jnp.zeros((depth, r, GLA_WIDTH), F32)
    w_gate = jnp.concatenate([jnp.concatenate([wgf, zr], axis=2), jnp.concatenate([zr, wgb], axis=2)], axis=1)
    eye = jnp.eye(FNET_GROUPS, dtype=F32)
    wf_bd = jnp.einsum("lgij,gh->lgihj", w_fourier, eye).reshape(depth, FNET_WIDTH, FNET_WIDTH)
    return {
        "g1": row(g_norm1),
        "w_main": w_in.astype(BF16),
        "w_z": jnp.pad(w_in[:, :, MAIN_WIDTH:], ((0, 0), (0, 0), (0, LANES - 2 * r))).astype(BF16),
        "w_gate": jnp.pad(w_gate, ((0, 0), (0, LANES - 2 * r), (0, 0))).astype(BF16),
        "b_gate": row(jnp.concatenate([bgf, bgb], axis=1)),
        "q_g": row(jnp.tile(q_norm_g, (1, 2))),
        "k_g": row(jnp.tile(k_norm_g, (1, 2))),
        "bd256": _block_diag_mean(GLA_WIDTH),
        "wf_bd": wf_bd.astype(BF16),
        "gla_g": row(jnp.tile(gla_norm_g, (1, GLA_HEADS))),
        "w_out": w_out.astype(BF16),
        "g2": row(g_norm2),
        "w_ffn_in": w_ffn_in.astype(BF16),
        "w_ffn_out": w_ffn_out.astype(BF16),
    }


def kernel(x, c, ctx, c_ctx, w_mod, b_mod, g_norm1, w_in, q_norm_g, k_norm_g, attn_sink, w_fourier, gla_w_gate_f,
           gla_b_gate_f, gla_w_gate_b, gla_b_gate_b, gla_norm_g, w_out, g_norm2, w_ffn_in, w_ffn_out):
    b, n, d = x.shape
    lc = ctx.shape[1]
    depth = w_mod.shape[0]
    assert d == D_MODEL and b < MOD_ROWS and n % 1024 == 0 and n % (DFT_N1 * DFT_STEP) == 0 and lc % GLA_CHUNK == 0

    cc = jnp.concatenate([c, c_ctx[None, :], jnp.zeros((MOD_ROWS - b - 1, d), F32)], axis=0)
    mod3 = _modulation(cc, w_mod, b_mod).reshape(depth * MOD_ROWS, 1, 6 * d)
    rope_tabs = _rope_tables(n)
    dft = _dft_consts(n)
    dft_ctx = _dft_ctx_consts(lc)
    tm_in, tm_tail, tb_lat = 1024, 1024, 1024
    flat = lambda t: t.reshape(1, b * lc, t.shape[-1])
    unflat = lambda t: t.reshape(b, lc, t.shape[-1])
    xc = ctx
    weights = _prepare_weights(w_in, g_norm1, q_norm_g, k_norm_g, w_fourier, gla_w_gate_f, gla_b_gate_f,
                               gla_w_gate_b, gla_b_gate_b, gla_norm_g, w_out, g_norm2, w_ffn_in, w_ffn_out)
    s_zero = jnp.zeros((b, 2, GLA_WIDTH, LANES), F32)
    for l in range(depth):
        need_ctx = l < depth - 1
        lw = dict(weights, layer=l)
        sink = attn_sink[l]
        cq, ckv4, cg4, clf = map(unflat, _inproj(flat(xc), mod3, b, lw, None, b * lc))
        q, kv4, g4, lf = _inproj(x, mod3, None, lw, rope_tabs, tm_in)
        att = _attention(q, kv4, ckv4, sink)
        four = _fourier(lf, lw["wf_bd"], l, dft)
        ocf, ocb, s_ctx = _gla(cg4, clf, s_zero, lc)
        olf, olb, _ = _gla(g4, lf, s_ctx, tb_lat)
        x = _tail(att, four, olf, olb, g4, x, mod3, None, lw, tm_tail)
        if need_ctx:
            att_c = _attention_ctx(cq, ckv4, sink)
            four_c = _fourier_ctx(clf, lw["wf_bd"], l, dft_ctx)
            xc = unflat(_tail(*map(flat, (att_c, four_c, ocf, ocb, cg4, xc)), mod3, b, lw, b * lc))
    return x
```

```python
import functools

import numpy as np
import jax
import jax.numpy as jnp
from jax import lax
from jax.experimental import pallas as pl
from jax.experimental.pallas import tpu as pltpu

F32 = jnp.float32
BF16 = jnp.bfloat16

D_MODEL = 1024
HEAD_DIM = 64
GRID_W = 64
ROPE_BASE = 10000.0
ATT_HEADS = 8
ATT_KV_HEADS = 2
ATT_WIDTH = ATT_HEADS * HEAD_DIM
KV_WIDTH = ATT_KV_HEADS * HEAD_DIM
ATT_BLOCK = 128
ATT_QBLOCKS = 8
ATT_SCALE = HEAD_DIM ** -0.5
LOG2E = 1.4426950408889634
NEG_INF = -1e30
FNET_GROUPS = 4
FNET_WIDTH = FNET_GROUPS * HEAD_DIM
GLA_HEADS = 4
GLA_WIDTH = GLA_HEADS * HEAD_DIM
GLA_GATE_RANK = 16
GLA_TAU = 16.0
GLA_CHUNK = 64
GLA_SCALE = HEAD_DIM ** -0.5
MAIN_WIDTH = ATT_WIDTH + 2 * KV_WIDTH + FNET_WIDTH + 4 * GLA_WIDTH
FFN_HIDDEN = 2816
FFN_CHUNK = 256
INPROJ_SUB = 512
TAIL_SUB = 512
FU_BLOCK = 2
GLA_GROUP = 16
TOKEN_BLOCK = 1024
EPS = 1e-6
LANES = 128
MOD_ROWS = 8
DFT_N1 = 128
DFT_STEP = 16
VMEM_LIMIT = 56 * 1024 * 1024

NT_DIMS = (((1,), (1,)), ((), ()))
TN_DIMS = (((0,), (0,)), ((), ()))


def _params(*sem):
    return pltpu.CompilerParams(dimension_semantics=sem, vmem_limit_bytes=VMEM_LIMIT)


def _dot(a, b):
    return jnp.dot(a, b, preferred_element_type=F32)


def _silu(x):
    return x / (1.0 + jnp.exp(-x))


def _layer_block(shape, layer):
    nd = len(shape)
    return pl.BlockSpec((None,) + tuple(shape), lambda *_: (layer,) + (0,) * nd, pipeline_mode=pl.Buffered(1))


def _full(shape):
    nd = len(shape)
    return pl.BlockSpec(shape, lambda *_: (0,) * nd)


def _group_mean(t, bd_ref):
    return _dot((t * t).astype(BF16), bd_ref[...])


def _mod_kernel(c_ref, w_ref, b_ref, o_ref):
    s = _silu(c_ref[...]).astype(BF16)
    o_ref[...] = _dot(s, w_ref[...].astype(BF16)) + b_ref[...]


def _modulation(cc, w_mod, b_mod):
    depth, d, width = w_mod.shape
    tn = 1536
    return pl.pallas_call(
        _mod_kernel,
        out_shape=jax.ShapeDtypeStruct((depth, MOD_ROWS, width), F32),
        grid=(depth, width // tn),
        in_specs=[
            _full((MOD_ROWS, d)),
            pl.BlockSpec((None, d, tn), lambda l, j: (l, 0, j)),
            pl.BlockSpec((None, 1, tn), lambda l, j: (l, 0, j)),
        ],
        out_specs=pl.BlockSpec((None, MOD_ROWS, tn), lambda l, j: (l, 0, j)),
        compiler_params=_params("parallel", "parallel"),
        name="modulation",
    )(cc, w_mod, b_mod.reshape(depth, 1, width))


def _inproj_kernel(*refs, rope):
    if rope:
        (x_ref, mod_ref, g1_ref, wm_ref, wz_ref, wg_ref, bg_ref, qg_ref, kg_ref, bd_ref, cos_ref, sin_ref,
         q_ref, kv4_ref, g4_ref, lf_ref) = refs
    else:
        (x_ref, mod_ref, g1_ref, wm_ref, wz_ref, wg_ref, bg_ref, qg_ref, kg_ref, bd_ref,
         q_ref, kv4_ref, g4_ref, lf_ref) = refs
    d = D_MODEL
    mod = mod_ref[...]
    sh, sc = mod[:, 0:d], mod[:, d:2 * d]
    sub = min(x_ref.shape[0], INPROJ_SUB)
    subs = [slice(sub * j, sub * (j + 1)) for j in range(x_ref.shape[0] // sub)]

    def normed(rs):
        x = x_ref[rs]
        ms = jnp.mean(x * x, axis=-1, keepdims=True)
        return ((x * lax.rsqrt(ms + EPS) * g1_ref[...]) * (1.0 + sc) + sh).astype(BF16)

    lane = lax.broadcasted_iota(jnp.int32, (1, LANES), 1)
    lo = lane < HEAD_DIM
    second_half = (lane & 16) != 0

    def head_norm(t, g_ref):
        wdt = t.shape[1]
        ms = _dot((t * t).astype(BF16), bd_ref[0:wdt, 0:wdt])
        g = g_ref[...] if wdt == LANES else jnp.concatenate([g_ref[...]] * (wdt // LANES), axis=1)
        return t * lax.rsqrt(ms + EPS) * g

    def rotary(t, rs):
        if not rope:
            return t
        partner = jnp.where(second_half, pltpu.roll(t, 16, 1), pltpu.roll(t, LANES - 16, 1))
        return t * cos_ref[rs] + partner * sin_ref[rs]

    def spread(t, c0, rs, idle):
        tr = pltpu.roll(t, HEAD_DIM, 1)
        fill = jnp.full_like(t, idle)
        tiles = (jnp.where(lo, t, fill), jnp.where(lo, fill, tr), jnp.where(lo, tr, fill), jnp.where(lo, fill, t))
        for j, tile in enumerate(tiles):
            kv4_ref[rs, c0 + LANES * j:c0 + LANES * (j + 1)] = tile.astype(kv4_ref.dtype)

    assert (ATT_WIDTH, 2 * KV_WIDTH + FNET_WIDTH, 2 * GLA_WIDTH) == (4 * LANES,) * 3

    def project(rs, hb):
        group = lambda g: _dot(hb, wm_ref[:, 4 * LANES * g:4 * LANES * (g + 1)])
        gz = _dot(hb, wz_ref[...])
        a_q = group(0)
        z = _dot(gz.astype(BF16), wg_ref[...]) + bg_ref[...]
        log_sig = jnp.minimum(z, 0.0) - jnp.log(1.0 + jnp.exp(-jnp.abs(z)))
        lf_ref[rs, 0:2 * GLA_WIDTH] = log_sig * (LOG2E / GLA_TAU)
        a_kvf = group(1)
        for j2 in range(ATT_WIDTH // (2 * LANES)):
            qn = head_norm(a_q[:, 2 * LANES * j2:2 * LANES * (j2 + 1)], qg_ref)
            for j in range(2):
                t = rotary(qn[:, LANES * j:LANES * (j + 1)], rs) * (ATT_SCALE * LOG2E)
                q_ref[rs, LANES * (2 * j2 + j):LANES * (2 * j2 + j + 1)] = t.astype(q_ref.dtype)
        a_qk = group(2)
        spread(rotary(head_norm(a_kvf[:, 0:KV_WIDTH], kg_ref), rs), 0, rs, 0.0)
        spread(a_kvf[:, KV_WIDTH:2 * KV_WIDTH], 4 * LANES, rs, 1.0)
        lf_ref[rs, 2 * GLA_WIDTH:] = a_kvf[:, 2 * KV_WIDTH:]
        a_vr = group(3)
        g4_ref[rs, 0:GLA_WIDTH] = (a_qk[:, 0:GLA_WIDTH] * GLA_SCALE).astype(g4_ref.dtype)
        g4_ref[rs, GLA_WIDTH:2 * GLA_WIDTH] = a_qk[:, GLA_WIDTH:].astype(g4_ref.dtype)
        g4_ref[rs, 2 * GLA_WIDTH:4 * GLA_WIDTH] = a_vr.astype(g4_ref.dtype)

    hbs = [normed(rs) for rs in subs]
    for rs, hb in zip(subs, hbs):
        project(rs, hb)


def _inproj(x, mod3, mod_row, lw, rope_tabs, tm):
    b, n, d = x.shape
    rope = rope_tabs is not None
    row = lambda bi: MOD_ROWS * lw["layer"] + (bi if mod_row is None else mod_row)
    tok = lambda w: pl.BlockSpec((None, tm, w), lambda bi, i: (bi, i, 0))
    lb = lambda shape: _layer_block(shape, lw["layer"])
    in_specs = [
        tok(d),
        pl.BlockSpec((None, 1, 6 * d), lambda bi, i: (row(bi), 0, 0)),
        lb((1, d)), lb((d, MAIN_WIDTH)), lb((d, LANES)), lb((LANES, 2 * GLA_WIDTH)),
        lb((1, 2 * GLA_WIDTH)), lb((1, LANES)), lb((1, LANES)), _full((2 * LANES, 2 * LANES)),
    ]
    args = [x, mod3, lw["g1"], lw["w_main"], lw["w_z"], lw["w_gate"], lw["b_gate"], lw["q_g"], lw["k_g"], lw["bd256"]]
    if rope:
        in_specs += [pl.BlockSpec((tm, LANES), lambda bi, i: (i, 0))] * 2
        args += list(rope_tabs)
    sds = lambda w, dt: jax.ShapeDtypeStruct((b, n, w), dt)
    out_shape = [sds(ATT_WIDTH, BF16), sds(8 * LANES, BF16), sds(4 * GLA_WIDTH, BF16),
                 sds(2 * GLA_WIDTH + FNET_WIDTH, F32)]
    out_specs = [tok(s.shape[-1]) for s in out_shape]
    return pl.pallas_call(
        functools.partial(_inproj_kernel, rope=rope),
        out_shape=out_shape, grid=(b, n // tm), in_specs=in_specs, out_specs=out_specs,
        compiler_params=_params("parallel", "parallel"),
        name="inproj_rope" if rope else "inproj_ctx",
    )(*args)


def _scores(q2, key_tiles, masks):
    cols = []
    for k, tile_masks in zip(key_tiles, masks):
        s = lax.dot_general(q2, k, NT_DIMS, preferred_element_type=F32)
        for j, mk in enumerate(tile_masks):
            c = s[:, LANES * j:LANES * (j + 1)]
            cols.append(c if mk is None else jnp.where(mk, c, NEG_INF))
    return cols


def _softmax_pv(cols, val_tiles, sink_col):
    mx = cols[0]
    for c in cols[1:]:
        mx = jnp.maximum(mx, c)
    m = jnp.maximum(jnp.max(mx, axis=-1, keepdims=True), sink_col)
    p = jnp.concatenate([jnp.exp2(c - m).astype(BF16) for c in cols], axis=1)
    pv = _dot(p, jnp.concatenate(val_tiles, axis=0))
    den = pltpu.roll(pv, HEAD_DIM, 1) + jnp.exp2(sink_col - m)
    return pv / den


def _attn_kernel(*refs):
    nq = ATT_QBLOCKS
    sink_ref, q_ref = refs[0:2]
    kv_refs, kvc_ref, o_ref = refs[2:nq + 4], refs[nq + 4], refs[nq + 5]
    v_off = 4 * LANES
    i = pl.program_id(1)
    last = pl.num_programs(1) - 1
    blk = ATT_BLOCK
    group = ATT_HEADS // ATT_KV_HEADS
    row = lax.broadcasted_iota(jnp.int32, (2 * blk, blk), 0) & (blk - 1)
    col = lax.broadcasted_iota(jnp.int32, (2 * blk, blk), 1)
    band_l, band_r = col >= row, col <= row
    mask_l = [jnp.logical_and(band_l, i > 0) if s == 0 else band_l for s in range(nq)]
    mask_r = [jnp.logical_and(band_r, i < last) if s == nq - 1 else band_r for s in range(nq)]
    upper_rows = lax.broadcasted_iota(jnp.int32, (2 * blk, 1), 0) >= blk
    lo = lax.broadcasted_iota(jnp.int32, (1, LANES), 1) < HEAD_DIM

    def tiles(s, ks):
        l, m_, r = kv_refs[s:s + 3]
        return [jnp.concatenate([l[:, ks], m_[:, ks]], axis=0),
                jnp.concatenate([r[:, ks], kvc_ref[0:blk, ks]], axis=0), kvc_ref[blk:, ks]]

    slots = [(s, kvh, par) for s in range(nq) for kvh in range(ATT_KV_HEADS) for par in range(2)]
    lane_slice = lambda kvh, par, off=0: slice(off + 2 * LANES * kvh + LANES * par,
                                               off + 2 * LANES * kvh + LANES * (par + 1))
    cols = {}
    for s, kvh, par in slots:
        qs, rs = 2 * LANES * kvh, slice(blk * s, blk * (s + 1))
        q2 = jnp.concatenate([q_ref[rs, qs:qs + LANES], q_ref[rs, qs + LANES:qs + 2 * LANES]], axis=0)
        cols[s, kvh, par] = _scores(q2, tiles(s, lane_slice(kvh, par)),
                                    [(mask_l[s], None), (mask_r[s], None), (None,)])
    outs = {}
    for s, kvh, par in slots:
        base = group * kvh + par
        sink_col = jnp.where(upper_rows, sink_ref[base + 2], sink_ref[base]) * LOG2E
        outs[s, kvh, par] = _softmax_pv(cols[s, kvh, par], tiles(s, lane_slice(kvh, par, v_off)), sink_col)
    for s in range(nq):
        for kvh in range(ATT_KV_HEADS):
            qs, r0 = 2 * LANES * kvh, blk * s
            o = jnp.where(lo, outs[s, kvh, 0], outs[s, kvh, 1])
            o_ref[r0:r0 + blk, qs:qs + LANES] = o[0:blk].astype(o_ref.dtype)
            o_ref[r0:r0 + blk, qs + LANES:qs + 2 * LANES] = o[blk:2 * blk].astype(o_ref.dtype)


def _attention(q, kv4, ckv4, sink):
    b, n, _ = q.shape
    lc = ckv4.shape[1]
    nq = ATT_QBLOCKS
    nb = n // ATT_BLOCK
    assert lc == 2 * ATT_BLOCK and nb % nq == 0
    qspec = pl.BlockSpec((None, nq * ATT_BLOCK, 4 * LANES), lambda bi, i: (bi, i, 0))
    kv = [pl.BlockSpec((None, ATT_BLOCK, 8 * LANES),
                       functools.partial(lambda bi, i, j: (bi, jnp.clip(nq * i + j - 1, 0, nb - 1), 0), j=j))
          for j in range(nq + 2)]
    ctx = pl.BlockSpec((None, lc, 8 * LANES), lambda bi, i: (bi, 0, 0))
    return pl.pallas_call(
        _attn_kernel,
        out_shape=jax.ShapeDtypeStruct((b, n, ATT_WIDTH), BF16),
        grid=(b, nb // nq),
        in_specs=[pl.BlockSpec(memory_space=pltpu.SMEM), qspec] + kv + [ctx],
        out_specs=qspec,
        compiler_params=_params("parallel", "parallel"),
        name="window_attention",
    )(sink, q, *([kv4] * (nq + 2)), ckv4)


def _attn_ctx_kernel(sink_ref, q_ref, kc_ref, vc_ref, o_ref):
    kvh = pl.program_id(1)
    lc = q_ref.shape[0]
    q2 = jnp.concatenate([q_ref[:, 0:LANES], q_ref[:, LANES:2 * LANES]], axis=0)
    upper_rows = lax.broadcasted_iota(jnp.int32, (2 * lc, 1), 0) >= lc
    lo = lax.broadcasted_iota(jnp.int32, (1, LANES), 1) < HEAD_DIM
    outs = []
    for par in range(2):
        ks = slice(LANES * par, LANES * (par + 1))
        base = ATT_HEADS // ATT_KV_HEADS * kvh + par
        sink_col = jnp.where(upper_rows, sink_ref[base + 2], sink_ref[base]) * LOG2E
        cols = _scores(q2, [kc_ref[:, ks]], [(None,) * (lc // LANES)])
        outs.append(_softmax_pv(cols, [vc_ref[:, ks]], sink_col))
    o = jnp.where(lo, outs[0], outs[1])
    o_ref[:, 0:LANES] = o[0:lc].astype(o_ref.dtype)
    o_ref[:, LANES:2 * LANES] = o[lc:2 * lc].astype(o_ref.dtype)


def _attention_ctx(cq, ckv4, sink):
    b, lc, _ = cq.shape
    gw = 2 * LANES
    spec = pl.BlockSpec((None, lc, gw), lambda bi, h: (bi, 0, h))
    vspec = pl.BlockSpec((None, lc, gw), lambda bi, h: (bi, 0, ATT_KV_HEADS + h))
    return pl.pallas_call(
        _attn_ctx_kernel,
        out_shape=jax.ShapeDtypeStruct((b, lc, ATT_WIDTH), BF16),
        grid=(b, ATT_KV_HEADS),
        in_specs=[pl.BlockSpec(memory_space=pltpu.SMEM), spec, spec, vspec],
        out_specs=spec,
        compiler_params=_params("parallel", "parallel"),
        name="context_attention",
    )(sink, cq, ckv4, ckv4)


def _dft_consts(n):
    n1, n2 = DFT_N1, n // DFT_N1
    ang = lambda a, m: 2.0 * np.pi * np.outer(np.arange(a), np.arange(a)) / m
    c = HEAD_DIM
    cc, sc = np.cos(ang(c, c)) / np.sqrt(c), np.sin(ang(c, c)) / np.sqrt(c)
    eye = np.eye(FNET_GROUPS)
    w_chan = np.concatenate([np.kron(eye, cc), -np.kron(eye, sc)], axis=1)
    c1, s1 = np.cos(ang(n1, n1)) / np.sqrt(n1), np.sin(ang(n1, n1)) / np.sqrt(n1)
    m1 = np.block([[c1, s1], [-s1, c1]])
    kk = np.arange(n1)[:, None, None] + n1 * np.arange(n2)[None, :, None]
    ph = 2.0 * np.pi * (kk * np.arange(n2)[None, None, :] % n) / n
    m3 = np.concatenate([np.cos(ph), np.sin(ph)], axis=2) / np.sqrt(n2)
    return tuple(jnp.asarray(a, F32) for a in (w_chan, m1, m3))


def _dft_ctx_consts(lc):
    ang = lambda a, m: 2.0 * np.pi * np.outer(np.arange(a), np.arange(a)) / m
    c = HEAD_DIM
    cc, sc = np.cos(ang(c, c)) / np.sqrt(c), np.sin(ang(c, c)) / np.sqrt(c)
    eye = np.eye(FNET_GROUPS)
    w_chan = np.concatenate([np.kron(eye, cc), -np.kron(eye, sc)], axis=1)
    cl, sl = np.cos(ang(lc, lc)) / np.sqrt(lc), np.sin(ang(lc, lc)) / np.sqrt(lc)
    return jnp.asarray(w_chan, F32), jnp.asarray(np.concatenate([cl, sl], axis=1), F32)


def _regroup_rows(x, outer):
    r, c = x.shape
    return jnp.swapaxes(x.reshape(outer, r // outer, c), 0, 1).reshape(r, c)


def _fourier_kernel(u_ref, wc_ref, m1_ref, m3_ref, wf_ref, y_ref, p_ref):
    w, t = FNET_WIDTH, DFT_STEP
    s = pl.program_id(1)
    n_a = p_ref.shape[1]
    n2 = n_a * t

    @pl.when(s < n_a)
    def _():
        u = u_ref[...].reshape(DFT_N1 * t, w).astype(BF16)
        z = _regroup_rows(_dot(u, wc_ref[...]), DFT_N1)
        a_re, a_im = [], []
        for tt in range(t):
            zt = z[DFT_N1 * tt:DFT_N1 * (tt + 1)]
            zs = jnp.concatenate([zt[:, 0:w], zt[:, w:2 * w]], axis=0).astype(BF16)
            a = _dot(m1_ref[...], zs)
            a_re.append(a[0:DFT_N1])
            a_im.append(a[DFT_N1:2 * DFT_N1])
        for part, rows in enumerate((a_re, a_im)):
            a = _regroup_rows(jnp.concatenate(rows, axis=0), t)
            p_ref[part, s] = a.reshape(DFT_N1, t, w)

    @pl.when(s >= n_a)
    def _():
        k1_0 = (s - n_a) * t
        ys = []
        for kk in range(t):
            parts = [p_ref[part, :, k1_0 + kk].reshape(n2, w) for part in range(2)]
            rhs = jnp.concatenate(parts, axis=0).astype(BF16)
            ys.append(_dot(m3_ref[kk].astype(BF16), rhs).astype(BF16))
        out = _dot(jnp.concatenate(ys, axis=0), wf_ref[...])
        y_ref[...] = _regroup_rows(out, t).reshape(n2, t, w)


def _fourier(lf, wf_bd, layer, consts):
    b, n, wl = lf.shape
    w = FNET_WIDTH
    n1, n2, t = DFT_N1, n // DFT_N1, DFT_STEP
    n_a, n_b = n2 // t, n1 // t
    w_chan, m1, m3 = consts
    y = pl.pallas_call(
        _fourier_kernel,
        out_shape=jax.ShapeDtypeStruct((b, n2, n1, w), F32),
        grid=(b, n_a + n_b),
        in_specs=[pl.BlockSpec((None, n1, t, w), lambda bi, s: (bi, 0, jnp.minimum(s, n_a - 1), FU_BLOCK)),
                  _full((w, 2 * w)), _full((2 * n1, 2 * n1)),
                  pl.BlockSpec((t, n2, 2 * n2), lambda bi, s: (jnp.maximum(s - n_a, 0), 0, 0)),
                  _layer_block((w, w), layer)],
        out_specs=pl.BlockSpec((None, n2, t, w), lambda bi, s: (bi, 0, jnp.maximum(s - n_a, 0), 0)),
        scratch_shapes=[pltpu.VMEM((2, n_a, n1, t, w), F32)],
        compiler_params=_params("parallel", "arbitrary"),
        name="fourier_mix",
    )(lf.reshape(b, n1, n2, wl), w_chan.astype(BF16), m1.astype(BF16), m3, wf_bd)
    return y.reshape(b, n, w)


def _four_ctx_kernel(u_ref, wc_ref, m_ref, wf_ref, y_ref):
    w = FNET_WIDTH
    z = _dot(u_ref[...].astype(BF16), wc_ref[...])
    zs = jnp.concatenate([z[:, 0:w], z[:, w:2 * w]], axis=0).astype(BF16)
    y = _dot(m_ref[...], zs)
    y_ref[...] = _dot(y.astype(BF16), wf_ref[...]).astype(y_ref.dtype)


def _fourier_ctx(clf, wf_bd, layer, consts):
    b, lc, _ = clf.shape
    w = FNET_WIDTH
    w_chan, m = consts[0].astype(BF16), consts[1].astype(BF16)
    return pl.pallas_call(
        _four_ctx_kernel,
        out_shape=jax.ShapeDtypeStruct((b, lc, w), BF16),
        grid=(b,),
        in_specs=[pl.BlockSpec((None, lc, w), lambda bi: (bi, 0, FU_BLOCK)), _full((w, 2 * w)), _full((lc, 2 * lc)),
                  _layer_block((w, w), layer)],
        out_specs=pl.BlockSpec((None, lc, w), lambda bi: (bi, 0, 0)),
        compiler_params=_params("parallel"),
        name="fourier_context",
    )(clf, w_chan, m, wf_bd)


def _pair_stack(x, lo):
    zero = jnp.zeros_like(x)
    return jnp.concatenate([jnp.where(lo, x, zero), jnp.where(lo, zero, x)], axis=0)


def _gla_kernel(qkvf_ref, laf_ref, qkvb_ref, lab_ref, s0_ref, of_ref, ob_ref, sfin_ref, stf_ref, stb_ref):
    i = pl.program_id(1)
    c = GLA_CHUNK
    nchunk = qkvf_ref.shape[0] // c

    @pl.when(i == 0)
    def _():
        stf_ref[...] = s0_ref[0]
        stb_ref[...] = s0_ref[1]

    r64 = lax.broadcasted_iota(jnp.int32, (c, c), 0)
    c64 = lax.broadcasted_iota(jnp.int32, (c, c), 1)
    tri = ((c64 <= r64).astype(BF16), (c64 >= r64).astype(BF16))
    at = lax.broadcasted_iota(jnp.int32, (c, LANES), 0)
    as_ = lax.broadcasted_iota(jnp.int32, (c, LANES), 1) & (c - 1)
    att_mask = (as_ <= at, as_ >= at)
    lo = lax.broadcasted_iota(jnp.int32, (1, LANES), 1) < HEAD_DIM
    br = lax.broadcasted_iota(jnp.int32, (LANES, LANES), 0) < HEAD_DIM
    bc = lax.broadcasted_iota(jnp.int32, (LANES, LANES), 1) < HEAD_DIM
    bd_mask = br == bc
    pairs = [slice(LANES * p, LANES * (p + 1)) for p in range(GLA_WIDTH // LANES)]
    in_refs = ((qkvf_ref, laf_ref), (qkvb_ref, lab_ref))
    out_refs = (of_ref, ob_ref)
    items = [(d, step if d == 0 else nchunk - 1 - step) for step in range(nchunk) for d in range(2)]
    rows = lambda ch: slice(c * ch, c * (ch + 1))

    bcum, work, o_intra, ut, decay = {}, {}, {}, {}, {}
    st_refs = (stf_ref, stb_ref)
    st = [[st_refs[d][LANES * j:LANES * (j + 1)] for j in range(len(pairs))] for d in range(2)]

    def gate_sums(d, ch):
        la = in_refs[d][1][rows(ch)]
        la_hi = la.astype(BF16)
        la_lo = (la - la_hi.astype(F32)).astype(BF16)
        bcum[d, ch] = _dot(jnp.concatenate([tri[d], tri[d]], axis=1), jnp.concatenate([la_hi, la_lo], axis=0))

    def scores(d, ch):
        qkv_ref = in_refs[d][0]
        w = GLA_WIDTH
        b = bcum[d, ch]
        btot = b[0:1] if d == 1 else b[c - 1:c]
        k = qkv_ref[rows(ch), w:2 * w].astype(F32)
        q_in = (qkv_ref[rows(ch), 0:w].astype(F32) * jnp.exp2(b)).astype(BF16)
        k_in = (k * jnp.exp2(-b)).astype(BF16)
        k_out = (k * jnp.exp2(btot - b)).astype(BF16)
        vb = qkv_ref[rows(ch), 2 * w:3 * w]
        att = [lax.dot_general(q_in[:, p], _pair_stack(k_in[:, p], lo), NT_DIMS, preferred_element_type=F32)
               for p in pairs]
        work[d, ch] = (btot, q_in, k_out, vb, att)

    def intra(d, ch):
        btot, q_in, k_out, vb, att = work[d, ch]
        o_intra[d, ch] = [_dot(jnp.where(att_mask[d], a, 0.0).astype(BF16), _pair_stack(vb[:, p], lo))
                          for a, p in zip(att, pairs)]
        ut[d, ch] = [lax.dot_general(k_out[:, p], vb[:, p], TN_DIMS, preferred_element_type=F32) for p in pairs]
        decay[d, ch] = [jnp.transpose(jnp.exp2(btot[:, p])) for p in pairs]

    def carry(d, ch):
        btot, q_in = work[d, ch][0:2]
        outs = []
        for j, p in enumerate(pairs):
            outs.append(o_intra[d, ch][j] + _dot(q_in[:, p], st[d][j].astype(BF16)))
            st[d][j] = st[d][j] * decay[d, ch][j] + jnp.where(bd_mask, ut[d, ch][j], 0.0)
        out_refs[d][rows(ch)] = jnp.concatenate(outs, axis=1).astype(out_refs[d].dtype)

    groups = [items[g:g + GLA_GROUP] for g in range(0, len(items), GLA_GROUP)]
    for it in groups[0]:
        gate_sums(*it)
    for it in groups[0]:
        scores(*it)
    for g, grp in enumerate(groups):
        for it in grp:
            intra(*it)
        nxt = groups[g + 1] if g + 1 < len(groups) else []
        for it in nxt:
            gate_sums(*it)
        for k_ in range(len(grp)):
            carry(*grp[k_])
            if k_ < len(nxt):
                scores(*nxt[k_])
    for d in range(2):
        st_refs[d][...] = jnp.concatenate(st[d], axis=0)

    @pl.when(i == pl.num_programs(1) - 1)
    def _():
        for d in range(2):
            sfin_ref[d] = jnp.concatenate(st[d], axis=0)


def _gla(g4, lf, s0, tb):
    b, n, _ = g4.shape
    w = GLA_WIDTH
    nblk = n // tb
    fwd = lambda bi, i: (bi, i, 0)
    bwd = lambda bi, i: (bi, nblk - 1 - i, 0)
    bwd_la = lambda bi, i: (bi, nblk - 1 - i, 1)
    tok = lambda f: pl.BlockSpec((None, tb, w), f)
    qkv = lambda f: pl.BlockSpec((None, tb, 3 * w), f)
    state = pl.BlockSpec((None, 2, w, LANES), lambda bi, i: (bi, 0, 0, 0))
    return pl.pallas_call(
        _gla_kernel,
        out_shape=[jax.ShapeDtypeStruct((b, n, w), BF16), jax.ShapeDtypeStruct((b, n, w), BF16),
                   jax.ShapeDtypeStruct((b, 2, w, LANES), F32)],
        grid=(b, nblk),
        in_specs=[qkv(fwd), tok(fwd), qkv(bwd), tok(bwd_la), state],
        out_specs=[tok(fwd), tok(bwd), state],
        scratch_shapes=[pltpu.VMEM((w, LANES), F32), pltpu.VMEM((w, LANES), F32)],
        compiler_params=_params("parallel", "arbitrary"),
        name="gla_scan",
    )(g4, lf, g4, lf, s0)


def _tail_kernel(att_ref, four_ref, of_ref, ob_ref, r_ref, x_ref, mod_ref, gg_ref, bd_ref, wmix_ref, g2_ref, wi_ref,
                 wo_ref, o_ref, a_ref):
    d = D_MODEL
    mod = lambda j: mod_ref[:, j * d:(j + 1) * d]
    a0, a1, a2 = ATT_WIDTH, ATT_WIDTH + FNET_WIDTH, ATT_WIDTH + FNET_WIDTH + GLA_WIDTH
    sub = min(x_ref.shape[0], TAIL_SUB)
    subs = [slice(sub * j, sub * (j + 1)) for j in range(x_ref.shape[0] // sub)]

    def mixed(rs):
        o = of_ref[rs].astype(F32) + ob_ref[rs].astype(F32)
        y = o * lax.rsqrt(_group_mean(o, bd_ref) + EPS) * gg_ref[...]
        y = y * _silu(r_ref[rs].astype(F32))
        mix = (_dot(att_ref[rs], wmix_ref[0:a0]) + _dot(four_ref[rs].astype(BF16), wmix_ref[a0:a1])
               + _dot(y.astype(BF16), wmix_ref[a1:a2]))
        x = x_ref[rs] + mod(2) * mix
        ms = jnp.mean(x * x, axis=-1, keepdims=True)
        return x, ((x * lax.rsqrt(ms + EPS) * g2_ref[...]) * (1.0 + mod(4)) + mod(3)).astype(BF16)

    pre = [mixed(rs) for rs in subs]
    for rs, (x, hb) in zip(subs, pre):
        for c0 in range(0, FFN_HIDDEN, FFN_CHUNK):
            g = _dot(hb, wi_ref[:, c0:c0 + FFN_CHUNK])
            u = _dot(hb, wi_ref[:, FFN_HIDDEN + c0:FFN_HIDDEN + c0 + FFN_CHUNK])
            a_ref[rs, c0:c0 + FFN_CHUNK] = (_silu(g) * u).astype(BF16)
        o_ref[rs] = x + mod(5) * _dot(a_ref[rs], wo_ref[...])


def _tail(att, four, of, ob, g4, x, mod3, mod_row, lw, tm):
    b, n, d = x.shape
    row = lambda bi: MOD_ROWS * lw["layer"] + (bi if mod_row is None else mod_row)
    tok = lambda w: pl.BlockSpec((None, tm, w), lambda bi, i: (bi, i, 0))
    gate = pl.BlockSpec((None, tm, GLA_WIDTH), lambda bi, i: (bi, i, 3))
    resident = lambda shape: _layer_block(shape, lw["layer"])
    return pl.pallas_call(
        _tail_kernel,
        out_shape=jax.ShapeDtypeStruct((b, n, d), F32),
        grid=(b, n // tm),
        in_specs=[tok(ATT_WIDTH), tok(FNET_WIDTH), tok(GLA_WIDTH), tok(GLA_WIDTH), gate, tok(d),
                  pl.BlockSpec((None, 1, 6 * d), lambda bi, i: (row(bi), 0, 0)),
                  resident((1, GLA_WIDTH)), _full((GLA_WIDTH, GLA_WIDTH)), resident((d, d)),
                  resident((1, d)), resident((d, 2 * FFN_HIDDEN)), resident((FFN_HIDDEN, d))],
        out_specs=tok(d),
        scratch_shapes=[pltpu.VMEM((tm, FFN_HIDDEN), BF16)],
        compiler_params=_params("parallel", "parallel"),
        name="mix_ffn",
    )(att, four, of, ob, g4, x, mod3, lw["gla_g"], lw["bd256"], lw["w_out"], lw["g2"], lw["w_ffn_in"],
      lw["w_ffn_out"])


def _rope_tables(n):
    axis_dim = HEAD_DIM // 2
    inv_freq = ROPE_BASE ** (-np.arange(0, axis_dim, 2, dtype=np.float64) / axis_dim)
    t = np.arange(n)
    ang_r = (t // GRID_W)[:, None] * inv_freq[None, :]
    ang_c = (t % GRID_W)[:, None] * inv_freq[None, :]
    cos = np.concatenate([np.cos(ang_r)] * 2 + [np.cos(ang_c)] * 2, axis=1)
    sin = np.concatenate([-np.sin(ang_r), np.sin(ang_r), -np.sin(ang_c), np.sin(ang_c)], axis=1)
    return jnp.asarray(np.tile(cos, (1, 2)), F32), jnp.asarray(np.tile(sin, (1, 2)), F32)


def _block_diag_mean(width):
    g = np.arange(width) // HEAD_DIM
    return jnp.asarray((g[:, None] == g[None, :]) / HEAD_DIM, BF16)


def _prepare_weights(w_in, g_norm1, q_norm_g, k_norm_g, w_fourier, wgf, bgf, wgb, bgb, gla_norm_g, w_out, g_norm2,
                     w_ffn_in, w_ffn_out):
    depth = w_in.shape[0]
    r = GLA_GATE_RANK
    row = lambda t: t[:, None, :]
    zr = jnp.zeros((depth, r, GLA_WIDTH), F32)
    w_gate = jnp.concatenate([jnp.concatenate([wgf, zr], axis=2), jnp.concatenate([zr, wgb], axis=2)], axis=1)
    eye = jnp.eye(FNET_GROUPS, dtype=F32)
    wf_bd = jnp.einsum("lgij,gh->lgihj", w_fourier, eye).reshape(depth, FNET_WIDTH, FNET_WIDTH)
    return {
        "g1": row(g_norm1),
        "w_main": w_in.astype(BF16),
        "w_z": jnp.pad(w_in[:, :, MAIN_WIDTH:], ((0, 0), (0, 0), (0, LANES - 2 * r))).astype(BF16),
        "w_gate": jnp.pad(w_gate, ((0, 0), (0, LANES - 2 * r), (0, 0))).astype(BF16),
        "b_gate": row(jnp.concatenate([bgf, bgb], axis=1)),
        "q_g": row(jnp.tile(q_norm_g, (1, 2))),
        "k_g": row(jnp.tile(k_norm_g, (1, 2))),
        "bd256": _block_diag_mean(GLA_WIDTH),
        "wf_bd": wf_bd.astype(BF16),
        "gla_g": row(jnp.tile(gla_norm_g, (1, GLA_HEADS))),
        "w_out": w_out.astype(BF16),
        "g2": row(g_norm2),
        "w_ffn_in": w_ffn_in.astype(BF16),
        "w_ffn_out": w_ffn_out.astype(BF16),
    }


def kernel(x, c, ctx, c_ctx, w_mod, b_mod, g_norm1, w_in, q_norm_g, k_norm_g, attn_sink, w_fourier, gla_w_gate_f,
           gla_b_gate_f, gla_w_gate_b, gla_b_gate_b, gla_norm_g, w_out, g_norm2, w_ffn_in, w_ffn_out):
    b, n, d = x.shape
    lc = ctx.shape[1]
    depth = w_mod.shape[0]
    assert d == D_MODEL and b < MOD_ROWS and n % TOKEN_BLOCK == 0 and n % (DFT_N1 * DFT_STEP) == 0 and lc % GLA_CHUNK == 0

    cc = jnp.concatenate([c, c_ctx[None, :], jnp.zeros((MOD_ROWS - b - 1, d), F32)], axis=0)
    mod3 = _modulation(cc, w_mod, b_mod).reshape(depth * MOD_ROWS, 1, 6 * d)
    rope_tabs = _rope_tables(n)
    dft = _dft_consts(n)
    dft_ctx = _dft_ctx_consts(lc)
    flat = lambda t: t.reshape(1, b * lc, t.shape[-1])
    unflat = lambda t: t.reshape(b, lc, t.shape[-1])
    xc = ctx
    weights = _prepare_weights(w_in, g_norm1, q_norm_g, k_norm_g, w_fourier, gla_w_gate_f, gla_b_gate_f,
                               gla_w_gate_b, gla_b_gate_b, gla_norm_g, w_out, g_norm2, w_ffn_in, w_ffn_out)
    s_zero = jnp.zeros((b, 2, GLA_WIDTH, LANES), F32)
    for l in range(depth):
        need_ctx = l < depth - 1
        lw = dict(weights, layer=l)
        sink = attn_sink[l]
        cq, ckv4, cg4, clf = map(unflat, _inproj(flat(xc), mod3, b, lw, None, b * lc))
        q, kv4, g4, lf = _inproj(x, mod3, None, lw, rope_tabs, TOKEN_BLOCK)
        att = _attention(q, kv4, ckv4, sink)
        four = _fourier(lf, lw["wf_bd"], l, dft)
        ocf, ocb, s_ctx = _gla(cg4, clf, s_zero, lc)
        olf, olb, _ = _gla(g4, lf, s_ctx, TOKEN_BLOCK)
        x = _tail(att, four, olf, olb, g4, x, mod3, None, lw, TOKEN_BLOCK)
        if need_ctx:
            att_c = _attention_ctx(cq, ckv4, sink)
            four_c = _fourier_ctx(clf, lw["wf_bd"], l, dft_ctx)
            xc = unflat(_tail(*map(flat, (att_c, four_c, ocf, ocb, cg4, xc)), mod3, b, lw, b * lc))
    return x
```

```python
import functools

import numpy as np
import jax
import jax.numpy as jnp
from jax import lax
from jax.experimental import pallas as pl
from jax.experimental.pallas import tpu as pltpu

F32 = jnp.float32
BF16 = jnp.bfloat16

D_MODEL = 1024
HEAD_DIM = 64
GRID_W = 64
ROPE_BASE = 10000.0
ATT_HEADS = 8
ATT_KV_HEADS = 2
ATT_WIDTH = ATT_HEADS * HEAD_DIM
KV_WIDTH = ATT_KV_HEADS * HEAD_DIM
ATT_BLOCK = 128
ATT_QBLOCKS = 8
ATT_SCALE = HEAD_DIM ** -0.5
LOG2E = 1.4426950408889634
NEG_INF = -1e30
FNET_GROUPS = 4
FNET_WIDTH = FNET_GROUPS * HEAD_DIM
GLA_HEADS = 4
GLA_WIDTH = GLA_HEADS * HEAD_DIM
GLA_GATE_RANK = 16
GLA_TAU = 16.0
GLA_CHUNK = 64
GLA_SCALE = HEAD_DIM ** -0.5
MAIN_WIDTH = ATT_WIDTH + 2 * KV_WIDTH + FNET_WIDTH + 4 * GLA_WIDTH
FFN_HIDDEN = 2816
FFN_CHUNK = 256
INPROJ_SUB = 512
TAIL_SUB = 512
FU_BLOCK = 2
GLA_GROUP = 16
TOKEN_BLOCK = 1024
EPS = 1e-6
LANES = 128
MOD_ROWS = 8
DFT_N1 = 128
DFT_STEP = 16
VMEM_LIMIT = 56 * 1024 * 1024
TAIL_VMEM_LIMIT = 61 * 1024 * 1024

NT_DIMS = (((1,), (1,)), ((), ()))
TN_DIMS = (((0,), (0,)), ((), ()))


def _params(*sem, vmem_limit=VMEM_LIMIT):
    return pltpu.CompilerParams(dimension_semantics=sem, vmem_limit_bytes=vmem_limit)


def _dot(a, b):
    return jnp.dot(a, b, preferred_element_type=F32)


def _silu(x):
    return x / (1.0 + jnp.exp(-x))


def _layer_block(shape, layer):
    nd = len(shape)
    return pl.BlockSpec((None,) + tuple(shape), lambda *_: (layer,) + (0,) * nd, pipeline_mode=pl.Buffered(1))


def _full(shape):
    nd = len(shape)
    return pl.BlockSpec(shape, lambda *_: (0,) * nd)


def _group_mean(t, bd_ref):
    return _dot((t * t).astype(BF16), bd_ref[...])


def _mod_kernel(c_ref, w_ref, b_ref, o_ref):
    s = _silu(c_ref[...]).astype(BF16)
    o_ref[...] = _dot(s, w_ref[...].astype(BF16)) + b_ref[...]


def _modulation(cc, w_mod, b_mod):
    depth, d, width = w_mod.shape
    tn = 1536
    return pl.pallas_call(
        _mod_kernel,
        out_shape=jax.ShapeDtypeStruct((depth, MOD_ROWS, width), F32),
        grid=(depth, width // tn),
        in_specs=[
            _full((MOD_ROWS, d)),
            pl.BlockSpec((None, d, tn), lambda l, j: (l, 0, j)),
            pl.BlockSpec((None, 1, tn), lambda l, j: (l, 0, j)),
        ],
        out_specs=pl.BlockSpec((None, MOD_ROWS, tn), lambda l, j: (l, 0, j)),
        compiler_params=_params("parallel", "parallel"),
        name="modulation",
    )(cc, w_mod, b_mod.reshape(depth, 1, width))


def _inproj_kernel(*refs, rope):
    if rope:
        (x_ref, mod_ref, g1_ref, wm_ref, wz_ref, wg_ref, bg_ref, qg_ref, kg_ref, bd_ref, cos_ref, sin_ref,
         q_ref, kv4_ref, g4_ref, lf_ref) = refs
    else:
        (x_ref, mod_ref, g1_ref, wm_ref, wz_ref, wg_ref, bg_ref, qg_ref, kg_ref, bd_ref,
         q_ref, kv4_ref, g4_ref, lf_ref) = refs
    d = D_MODEL
    mod = mod_ref[...]
    sh, sc = mod[:, 0:d], mod[:, d:2 * d]
    sub = min(x_ref.shape[0], INPROJ_SUB)
    subs = [slice(sub * j, sub * (j + 1)) for j in range(x_ref.shape[0] // sub)]

    def normed(rs):
        x = x_ref[rs]
        ms = jnp.mean(x * x, axis=-1, keepdims=True)
        return ((x * lax.rsqrt(ms + EPS) * g1_ref[...]) * (1.0 + sc) + sh).astype(BF16)

    lane = lax.broadcasted_iota(jnp.int32, (1, LANES), 1)
    lo = lane < HEAD_DIM
    second_half = (lane & 16) != 0

    def head_norm(t, g_ref):
        wdt = t.shape[1]
        ms = _dot((t * t).astype(BF16), bd_ref[0:wdt, 0:wdt])
        g = g_ref[...] if wdt == LANES else jnp.concatenate([g_ref[...]] * (wdt // LANES), axis=1)
        return t * lax.rsqrt(ms + EPS) * g

    def rotary(t, rs):
        if not rope:
            return t
        partner = jnp.where(second_half, pltpu.roll(t, 16, 1), pltpu.roll(t, LANES - 16, 1))
        return t * cos_ref[rs] + partner * sin_ref[rs]

    def spread(t, c0, rs, idle):
        tr = pltpu.roll(t, HEAD_DIM, 1)
        fill = jnp.full_like(t, idle)
        tiles = (jnp.where(lo, t, fill), jnp.where(lo, fill, tr), jnp.where(lo, tr, fill), jnp.where(lo, fill, t))
        for j, tile in enumerate(tiles):
            kv4_ref[rs, c0 + LANES * j:c0 + LANES * (j + 1)] = tile.astype(kv4_ref.dtype)

    assert (ATT_WIDTH, 2 * KV_WIDTH + FNET_WIDTH, 2 * GLA_WIDTH) == (4 * LANES,) * 3

    def project(rs, hb):
        group = lambda g: _dot(hb, wm_ref[:, 4 * LANES * g:4 * LANES * (g + 1)])
        gz = _dot(hb, wz_ref[...])
        a_q = group(0)
        z = _dot(gz.astype(BF16), wg_ref[...]) + bg_ref[...]
        log_sig = jnp.minimum(z, 0.0) - jnp.log(1.0 + jnp.exp(-jnp.abs(z)))
        lf_ref[rs, 0:2 * GLA_WIDTH] = log_sig * (LOG2E / GLA_TAU)
        a_kvf = group(1)
        for j2 in range(ATT_WIDTH // (2 * LANES)):
            qn = head_norm(a_q[:, 2 * LANES * j2:2 * LANES * (j2 + 1)], qg_ref)
            for j in range(2):
                t = rotary(qn[:, LANES * j:LANES * (j + 1)], rs) * (ATT_SCALE * LOG2E)
                q_ref[rs, LANES * (2 * j2 + j):LANES * (2 * j2 + j + 1)] = t.astype(q_ref.dtype)
        a_qk = group(2)
        spread(rotary(head_norm(a_kvf[:, 0:KV_WIDTH], kg_ref), rs), 0, rs, 0.0)
        spread(a_kvf[:, KV_WIDTH:2 * KV_WIDTH], 4 * LANES, rs, 1.0)
        lf_ref[rs, 2 * GLA_WIDTH:] = a_kvf[:, 2 * KV_WIDTH:]
        a_vr = group(3)
        g4_ref[rs, 0:GLA_WIDTH] = (a_qk[:, 0:GLA_WIDTH] * GLA_SCALE).astype(g4_ref.dtype)
        g4_ref[rs, GLA_WIDTH:2 * GLA_WIDTH] = a_qk[:, GLA_WIDTH:].astype(g4_ref.dtype)
        g4_ref[rs, 2 * GLA_WIDTH:4 * GLA_WIDTH] = a_vr.astype(g4_ref.dtype)

    hbs = [normed(rs) for rs in subs]
    for rs, hb in zip(subs, hbs):
        project(rs, hb)


def _inproj(x, mod3, mod_row, lw, rope_tabs, tm):
    b, n, d = x.shape
    rope = rope_tabs is not None
    row = lambda bi: MOD_ROWS * lw["layer"] + (bi if mod_row is None else mod_row)
    tok = lambda w: pl.BlockSpec((None, tm, w), lambda bi, i: (bi, i, 0))
    lb = lambda shape: _layer_block(shape, lw["layer"])
    in_specs = [
        tok(d),
        pl.BlockSpec((None, 1, 6 * d), lambda bi, i: (row(bi), 0, 0)),
        lb((1, d)), lb((d, MAIN_WIDTH)), lb((d, LANES)), lb((LANES, 2 * GLA_WIDTH)),
        lb((1, 2 * GLA_WIDTH)), lb((1, LANES)), lb((1, LANES)), _full((2 * LANES, 2 * LANES)),
    ]
    args = [x, mod3, lw["g1"], lw["w_main"], lw["w_z"], lw["w_gate"], lw["b_gate"], lw["q_g"], lw["k_g"], lw["bd256"]]
    if rope:
        in_specs += [pl.BlockSpec((tm, LANES), lambda bi, i: (i, 0))] * 2
        args += list(rope_tabs)
    sds = lambda w, dt: jax.ShapeDtypeStruct((b, n, w), dt)
    out_shape = [sds(ATT_WIDTH, BF16), sds(8 * LANES, BF16), sds(4 * GLA_WIDTH, BF16),
                 sds(2 * GLA_WIDTH + FNET_WIDTH, F32)]
    out_specs = [tok(s.shape[-1]) for s in out_shape]
    return pl.pallas_call(
        functools.partial(_inproj_kernel, rope=rope),
        out_shape=out_shape, grid=(b, n // tm), in_specs=in_specs, out_specs=out_specs,
        compiler_params=_params("parallel", "parallel"),
        name="inproj_rope" if rope else "inproj_ctx",
    )(*args)


def _scores(q2, key_tiles, masks):
    cols = []
    for k, tile_masks in zip(key_tiles, masks):
        s = lax.dot_general(q2, k, NT_DIMS, preferred_element_type=F32)
        for j, mk in enumerate(tile_masks):
            c = s[:, LANES * j:LANES * (j + 1)]
            cols.append(c if mk is None else jnp.where(mk, c, NEG_INF))
    return cols


def _softmax_pv(cols, val_tiles, sink_col):
    mx = cols[0]
    for c in cols[1:]:
        mx = jnp.maximum(mx, c)
    m = jnp.maximum(jnp.max(mx, axis=-1, keepdims=True), sink_col)
    p = jnp.concatenate([jnp.exp2(c - m).astype(BF16) for c in cols], axis=1)
    pv = _dot(p, jnp.concatenate(val_tiles, axis=0))
    den = pltpu.roll(pv, HEAD_DIM, 1) + jnp.exp2(sink_col - m)
    return pv / den


def _attn_kernel(*refs):
    nq = ATT_QBLOCKS
    sink_ref, q_ref = refs[0:2]
    kv_refs, kvc_ref, o_ref = refs[2:nq + 4], refs[nq + 4], refs[nq + 5]
    v_off = 4 * LANES
    i = pl.program_id(1)
    last = pl.num_programs(1) - 1
    blk = ATT_BLOCK
    group = ATT_HEADS // ATT_KV_HEADS
    row = lax.broadcasted_iota(jnp.int32, (2 * blk, blk), 0) & (blk - 1)
    col = lax.broadcasted_iota(jnp.int32, (2 * blk, blk), 1)
    band_l, band_r = col >= row, col <= row
    mask_l = [jnp.logical_and(band_l, i > 0) if s == 0 else band_l for s in range(nq)]
    mask_r = [jnp.logical_and(band_r, i < last) if s == nq - 1 else band_r for s in range(nq)]
    upper_rows = lax.broadcasted_iota(jnp.int32, (2 * blk, 1), 0) >= blk
    lo = lax.broadcasted_iota(jnp.int32, (1, LANES), 1) < HEAD_DIM

    def tiles(s, ks):
        l, m_, r = kv_refs[s:s + 3]
        return [jnp.concatenate([l[:, ks], m_[:, ks]], axis=0),
                jnp.concatenate([r[:, ks], kvc_ref[0:blk, ks]], axis=0), kvc_ref[blk:, ks]]

    slots = [(s, kvh, par) for s in range(nq) for kvh in range(ATT_KV_HEADS) for par in range(2)]
    lane_slice = lambda kvh, par, off=0: slice(off + 2 * LANES * kvh + LANES * par,
                                               off + 2 * LANES * kvh + LANES * (par + 1))
    cols = {}
    for s, kvh, par in slots:
        qs, rs = 2 * LANES * kvh, slice(blk * s, blk * (s + 1))
        q2 = jnp.concatenate([q_ref[rs, qs:qs + LANES], q_ref[rs, qs + LANES:qs + 2 * LANES]], axis=0)
        cols[s, kvh, par] = _scores(q2, tiles(s, lane_slice(kvh, par)),
                                    [(mask_l[s], None), (mask_r[s], None), (None,)])
    outs = {}
    for s, kvh, par in slots:
        base = group * kvh + par
        sink_col = jnp.where(upper_rows, sink_ref[base + 2], sink_ref[base]) * LOG2E
        outs[s, kvh, par] = _softmax_pv(cols[s, kvh, par], tiles(s, lane_slice(kvh, par, v_off)), sink_col)
    for s in range(nq):
        for kvh in range(ATT_KV_HEADS):
            qs, r0 = 2 * LANES * kvh, blk * s
            o = jnp.where(lo, outs[s, kvh, 0], outs[s, kvh, 1])
            o_ref[r0:r0 + blk, qs:qs + LANES] = o[0:blk].astype(o_ref.dtype)
            o_ref[r0:r0 + blk, qs + LANES:qs + 2 * LANES] = o[blk:2 * blk].astype(o_ref.dtype)


def _attention(q, kv4, ckv4, sink):
    b, n, _ = q.shape
    lc = ckv4.shape[1]
    nq = ATT_QBLOCKS
    nb = n // ATT_BLOCK
    assert lc == 2 * ATT_BLOCK and nb % nq == 0
    qspec = pl.BlockSpec((None, nq * ATT_BLOCK, 4 * LANES), lambda bi, i: (bi, i, 0))
    kv = [pl.BlockSpec((None, ATT_BLOCK, 8 * LANES),
                       functools.partial(lambda bi, i, j: (bi, jnp.clip(nq * i + j - 1, 0, nb - 1), 0), j=j))
          for j in range(nq + 2)]
    ctx = pl.BlockSpec((None, lc, 8 * LANES), lambda bi, i: (bi, 0, 0))
    return pl.pallas_call(
        _attn_kernel,
        out_shape=jax.ShapeDtypeStruct((b, n, ATT_WIDTH), BF16),
        grid=(b, nb // nq),
        in_specs=[pl.BlockSpec(memory_space=pltpu.SMEM), qspec] + kv + [ctx],
        out_specs=qspec,
        compiler_params=_params("parallel", "parallel"),
        name="window_attention",
    )(sink, q, *([kv4] * (nq + 2)), ckv4)


def _attn_ctx_kernel(sink_ref, q_ref, kc_ref, vc_ref, o_ref):
    kvh = pl.program_id(1)
    lc = q_ref.shape[0]
    q2 = jnp.concatenate([q_ref[:, 0:LANES], q_ref[:, LANES:2 * LANES]], axis=0)
    upper_rows = lax.broadcasted_iota(jnp.int32, (2 * lc, 1), 0) >= lc
    lo = lax.broadcasted_iota(jnp.int32, (1, LANES), 1) < HEAD_DIM
    outs = []
    for par in range(2):
        ks = slice(LANES * par, LANES * (par + 1))
        base = ATT_HEADS // ATT_KV_HEADS * kvh + par
        sink_col = jnp.where(upper_rows, sink_ref[base + 2], sink_ref[base]) * LOG2E
        cols = _scores(q2, [kc_ref[:, ks]], [(None,) * (lc // LANES)])
        outs.append(_softmax_pv(cols, [vc_ref[:, ks]], sink_col))
    o = jnp.where(lo, outs[0], outs[1])
    o_ref[:, 0:LANES] = o[0:lc].astype(o_ref.dtype)
    o_ref[:, LANES:2 * LANES] = o[lc:2 * lc].astype(o_ref.dtype)


def _attention_ctx(cq, ckv4, sink):
    b, lc, _ = cq.shape
    gw = 2 * LANES
    spec = pl.BlockSpec((None, lc, gw), lambda bi, h: (bi, 0, h))
    vspec = pl.BlockSpec((None, lc, gw), lambda bi, h: (bi, 0, ATT_KV_HEADS + h))
    return pl.pallas_call(
        _attn_ctx_kernel,
        out_shape=jax.ShapeDtypeStruct((b, lc, ATT_WIDTH), BF16),
        grid=(b, ATT_KV_HEADS),
        in_specs=[pl.BlockSpec(memory_space=pltpu.SMEM), spec, spec, vspec],
        out_specs=spec,
        compiler_params=_params("parallel", "parallel"),
        name="context_attention",
    )(sink, cq, ckv4, ckv4)


def _dft_consts(n):
    n1, n2 = DFT_N1, n // DFT_N1
    ang = lambda a, m: 2.0 * np.pi * np.outer(np.arange(a), np.arange(a)) / m
    c = HEAD_DIM
    cc, sc = np.cos(ang(c, c)) / np.sqrt(c), np.sin(ang(c, c)) / np.sqrt(c)
    eye = np.eye(FNET_GROUPS)
    w_chan = np.concatenate([np.kron(eye, cc), -np.kron(eye, sc)], axis=1)
    c1, s1 = np.cos(ang(n1, n1)) / np.sqrt(n1), np.sin(ang(n1, n1)) / np.sqrt(n1)
    m1 = np.block([[c1, s1], [-s1, c1]])
    kk = np.arange(n1)[:, None, None] + n1 * np.arange(n2)[None, :, None]
    ph = 2.0 * np.pi * (kk * np.arange(n2)[None, None, :] % n) / n
    m3 = np.concatenate([np.cos(ph), np.sin(ph)], axis=2) / np.sqrt(n2)
    return tuple(jnp.asarray(a, F32) for a in (w_chan, m1, m3))


def _dft_ctx_consts(lc):
    ang = lambda a, m: 2.0 * np.pi * np.outer(np.arange(a), np.arange(a)) / m
    c = HEAD_DIM
    cc, sc = np.cos(ang(c, c)) / np.sqrt(c), np.sin(ang(c, c)) / np.sqrt(c)
    eye = np.eye(FNET_GROUPS)
    w_chan = np.concatenate([np.kron(eye, cc), -np.kron(eye, sc)], axis=1)
    cl, sl = np.cos(ang(lc, lc)) / np.sqrt(lc), np.sin(ang(lc, lc)) / np.sqrt(lc)
    return jnp.asarray(w_chan, F32), jnp.asarray(np.concatenate([cl, sl], axis=1), F32)


def _regroup_rows(x, outer):
    r, c = x.shape
    return jnp.swapaxes(x.reshape(outer, r // outer, c), 0, 1).reshape(r, c)


def _fourier_kernel(u_ref, wc_ref, m1_ref, m3_ref, wf_ref, y_ref, p_ref):
    w, t = FNET_WIDTH, DFT_STEP
    s = pl.program_id(1)
    n_a = p_ref.shape[1]
    n2 = n_a * t

    @pl.when(s < n_a)
    def _():
        u = u_ref[...].reshape(DFT_N1 * t, w).astype(BF16)
        z = _regroup_rows(_dot(u, wc_ref[...]), DFT_N1)
        a_re, a_im = [], []
        for tt in range(t):
            zt = z[DFT_N1 * tt:DFT_N1 * (tt + 1)]
            zs = jnp.concatenate([zt[:, 0:w], zt[:, w:2 * w]], axis=0).astype(BF16)
            a = _dot(m1_ref[...], zs)
            a_re.append(a[0:DFT_N1])
            a_im.append(a[DFT_N1:2 * DFT_N1])
        for part, rows in enumerate((a_re, a_im)):
            a = _regroup_rows(jnp.concatenate(rows, axis=0), t)
            p_ref[part, s] = a.reshape(DFT_N1, t, w)

    @pl.when(s >= n_a)
    def _():
        k1_0 = (s - n_a) * t
        ys = []
        for kk in range(t):
            parts = [p_ref[part, :, k1_0 + kk].reshape(n2, w) for part in range(2)]
            rhs = jnp.concatenate(parts, axis=0).astype(BF16)
            ys.append(_dot(m3_ref[kk].astype(BF16), rhs).astype(BF16))
        out = _dot(jnp.concatenate(ys, axis=0), wf_ref[...])
        y_ref[...] = _regroup_rows(out, t).reshape(n2, t, w)


def _fourier(lf, wf_bd, layer, consts):
    b, n, wl = lf.shape
    w = FNET_WIDTH
    n1, n2, t = DFT_N1, n // DFT_N1, DFT_STEP
    n_a, n_b = n2 // t, n1 // t
    w_chan, m1, m3 = consts
    y = pl.pallas_call(
        _fourier_kernel,
        out_shape=jax.ShapeDtypeStruct((b, n2, n1, w), F32),
        grid=(b, n_a + n_b),
        in_specs=[pl.BlockSpec((None, n1, t, w), lambda bi, s: (bi, 0, jnp.minimum(s, n_a - 1), FU_BLOCK)),
                  _full((w, 2 * w)), _full((2 * n1, 2 * n1)),
                  pl.BlockSpec((t, n2, 2 * n2), lambda bi, s: (jnp.maximum(s - n_a, 0), 0, 0)),
                  _layer_block((w, w), layer)],
        out_specs=pl.BlockSpec((None, n2, t, w), lambda bi, s: (bi, 0, jnp.maximum(s - n_a, 0), 0)),
        scratch_shapes=[pltpu.VMEM((2, n_a, n1, t, w), F32)],
        compiler_params=_params("parallel", "arbitrary"),
        name="fourier_mix",
    )(lf.reshape(b, n1, n2, wl), w_chan.astype(BF16), m1.astype(BF16), m3, wf_bd)
    return y.reshape(b, n, w)


def _four_ctx_kernel(u_ref, wc_ref, m_ref, wf_ref, y_ref):
    w = FNET_WIDTH
    z = _dot(u_ref[...].astype(BF16), wc_ref[...])
    zs = jnp.concatenate([z[:, 0:w], z[:, w:2 * w]], axis=0).astype(BF16)
    y = _dot(m_ref[...], zs)
    y_ref[...] = _dot(y.astype(BF16), wf_ref[...]).astype(y_ref.dtype)


def _fourier_ctx(clf, wf_bd, layer, consts):
    b, lc, _ = clf.shape
    w = FNET_WIDTH
    w_chan, m = consts[0].astype(BF16), consts[1].astype(BF16)
    return pl.pallas_call(
        _four_ctx_kernel,
        out_shape=jax.ShapeDtypeStruct((b, lc, w), BF16),
        grid=(b,),
        in_specs=[pl.BlockSpec((None, lc, w), lambda bi: (bi, 0, FU_BLOCK)), _full((w, 2 * w)), _full((lc, 2 * lc)),
                  _layer_block((w, w), layer)],
        out_specs=pl.BlockSpec((None, lc, w), lambda bi: (bi, 0, 0)),
        compiler_params=_params("parallel"),
        name="fourier_context",
    )(clf, w_chan, m, wf_bd)


def _pair_stack(x, lo):
    zero = jnp.zeros_like(x)
    return jnp.concatenate([jnp.where(lo, x, zero), jnp.where(lo, zero, x)], axis=0)


def _gla_kernel(qkvf_ref, laf_ref, qkvb_ref, lab_ref, s0_ref, of_ref, ob_ref, sfin_ref, stf_ref, stb_ref):
    i = pl.program_id(1)
    c = GLA_CHUNK
    nchunk = qkvf_ref.shape[0] // c

    @pl.when(i == 0)
    def _():
        stf_ref[...] = s0_ref[0]
        stb_ref[...] = s0_ref[1]

    r64 = lax.broadcasted_iota(jnp.int32, (c, c), 0)
    c64 = lax.broadcasted_iota(jnp.int32, (c, c), 1)
    tri = ((c64 <= r64).astype(BF16), (c64 >= r64).astype(BF16))
    at = lax.broadcasted_iota(jnp.int32, (c, LANES), 0)
    as_ = lax.broadcasted_iota(jnp.int32, (c, LANES), 1) & (c - 1)
    att_mask = (as_ <= at, as_ >= at)
    lo = lax.broadcasted_iota(jnp.int32, (1, LANES), 1) < HEAD_DIM
    br = lax.broadcasted_iota(jnp.int32, (LANES, LANES), 0) < HEAD_DIM
    bc = lax.broadcasted_iota(jnp.int32, (LANES, LANES), 1) < HEAD_DIM
    bd_mask = br == bc
    pairs = [slice(LANES * p, LANES * (p + 1)) for p in range(GLA_WIDTH // LANES)]
    in_refs = ((qkvf_ref, laf_ref), (qkvb_ref, lab_ref))
    out_refs = (of_ref, ob_ref)
    items = [(d, step if d == 0 else nchunk - 1 - step) for step in range(nchunk) for d in range(2)]
    rows = lambda ch: slice(c * ch, c * (ch + 1))

    bcum, work, o_intra, ut, decay = {}, {}, {}, {}, {}
    st_refs = (stf_ref, stb_ref)
    st = [[st_refs[d][LANES * j:LANES * (j + 1)] for j in range(len(pairs))] for d in range(2)]

    def gate_sums(d, ch):
        la = in_refs[d][1][rows(ch)]
        la_hi = la.astype(BF16)
        la_lo = (la - la_hi.astype(F32)).astype(BF16)
        bcum[d, ch] = _dot(jnp.concatenate([tri[d], tri[d]], axis=1), jnp.concatenate([la_hi, la_lo], axis=0))

    def scores(d, ch):
        qkv_ref = in_refs[d][0]
        w = GLA_WIDTH
        b = bcum[d, ch]
        btot = b[0:1] if d == 1 else b[c - 1:c]
        k = qkv_ref[rows(ch), w:2 * w].astype(F32)
        q_in = (qkv_ref[rows(ch), 0:w].astype(F32) * jnp.exp2(b)).astype(BF16)
        k_in = (k * jnp.exp2(-b)).astype(BF16)
        k_out = (k * jnp.exp2(btot - b)).astype(BF16)
        vb = qkv_ref[rows(ch), 2 * w:3 * w]
        att = [lax.dot_general(q_in[:, p], _pair_stack(k_in[:, p], lo), NT_DIMS, preferred_element_type=F32)
               for p in pairs]
        work[d, ch] = (btot, q_in, k_out, vb, att)

    def intra(d, ch):
        btot, q_in, k_out, vb, att = work[d, ch]
        o_intra[d, ch] = [_dot(jnp.where(att_mask[d], a, 0.0).astype(BF16), _pair_stack(vb[:, p], lo))
                          for a, p in zip(att, pairs)]
        ut[d, ch] = [lax.dot_general(k_out[:, p], vb[:, p], TN_DIMS, preferred_element_type=F32) for p in pairs]
        decay[d, ch] = [jnp.transpose(jnp.exp2(btot[:, p])) for p in pairs]

    def carry(d, ch):
        btot, q_in = work[d, ch][0:2]
        outs = []
        for j, p in enumerate(pairs):
            outs.append(o_intra[d, ch][j] + _dot(q_in[:, p], st[d][j].astype(BF16)))
            st[d][j] = st[d][j] * decay[d, ch][j] + jnp.where(bd_mask, ut[d, ch][j], 0.0)
        out_refs[d][rows(ch)] = jnp.concatenate(outs, axis=1).astype(out_refs[d].dtype)

    groups = [items[g:g + GLA_GROUP] for g in range(0, len(items), GLA_GROUP)]
    for it in groups[0]:
        gate_sums(*it)
    for it in groups[0]:
        scores(*it)
    for g, grp in enumerate(groups):
        for it in grp:
            intra(*it)
        nxt = groups[g + 1] if g + 1 < len(groups) else []
        for it in nxt:
            gate_sums(*it)
        for k_ in range(len(grp)):
            carry(*grp[k_])
            if k_ < len(nxt):
                scores(*nxt[k_])
    for d in range(2):
        st_refs[d][...] = jnp.concatenate(st[d], axis=0)

    @pl.when(i == pl.num_programs(1) - 1)
    def _():
        for d in range(2):
            sfin_ref[d] = jnp.concatenate(st[d], axis=0)


def _gla(g4, lf, s0, tb):
    b, n, _ = g4.shape
    w = GLA_WIDTH
    nblk = n // tb
    fwd = lambda bi, i: (bi, i, 0)
    bwd = lambda bi, i: (bi, nblk - 1 - i, 0)
    bwd_la = lambda bi, i: (bi, nblk - 1 - i, 1)
    tok = lambda f: pl.BlockSpec((None, tb, w), f)
    qkv = lambda f: pl.BlockSpec((None, tb, 3 * w), f)
    state = pl.BlockSpec((None, 2, w, LANES), lambda bi, i: (bi, 0, 0, 0))
    return pl.pallas_call(
        _gla_kernel,
        out_shape=[jax.ShapeDtypeStruct((b, n, w), BF16), jax.ShapeDtypeStruct((b, n, w), BF16),
                   jax.ShapeDtypeStruct((b, 2, w, LANES), F32)],
        grid=(b, nblk),
        in_specs=[qkv(fwd), tok(fwd), qkv(bwd), tok(bwd_la), state],
        out_specs=[tok(fwd), tok(bwd), state],
        scratch_shapes=[pltpu.VMEM((w, LANES), F32), pltpu.VMEM((w, LANES), F32)],
        compiler_params=_params("parallel", "arbitrary"),
        name="gla_scan",
    )(g4, lf, g4, lf, s0)


def _tail_kernel(*refs, ncast):
    (att_ref, four_ref, of_ref, ob_ref, r_ref, x_ref, mod_ref, gg_ref, bd_ref, wmix_ref, g2_ref, wi_ref,
     wo_ref) = refs[0:13]
    o_ref, a_ref = refs[13 + ncast], refs[14 + 2 * ncast]
    for src, dst in zip(refs[13:13 + ncast], refs[14 + ncast:14 + 2 * ncast]):
        dst[...] = src[...].astype(dst.dtype)
    d = D_MODEL
    mod = lambda j: mod_ref[:, j * d:(j + 1) * d]
    a0, a1, a2 = ATT_WIDTH, ATT_WIDTH + FNET_WIDTH, ATT_WIDTH + FNET_WIDTH + GLA_WIDTH
    sub = min(x_ref.shape[0], TAIL_SUB)
    subs = [slice(sub * j, sub * (j + 1)) for j in range(x_ref.shape[0] // sub)]

    def mixed(rs):
        o = of_ref[rs].astype(F32) + ob_ref[rs].astype(F32)
        y = o * lax.rsqrt(_group_mean(o, bd_ref) + EPS) * gg_ref[...]
        y = y * _silu(r_ref[rs].astype(F32))
        mix = (_dot(att_ref[rs], wmix_ref[0:a0]) + _dot(four_ref[rs].astype(BF16), wmix_ref[a0:a1])
               + _dot(y.astype(BF16), wmix_ref[a1:a2]))
        x = x_ref[rs] + mod(2) * mix
        ms = jnp.mean(x * x, axis=-1, keepdims=True)
        return x, ((x * lax.rsqrt(ms + EPS) * g2_ref[...]) * (1.0 + mod(4)) + mod(3)).astype(BF16)

    pre = [mixed(rs) for rs in subs]
    for rs, (x, hb) in zip(subs, pre):
        for c0 in range(0, FFN_HIDDEN, FFN_CHUNK):
            g = _dot(hb, wi_ref[:, c0:c0 + FFN_CHUNK])
            u = _dot(hb, wi_ref[:, FFN_HIDDEN + c0:FFN_HIDDEN + c0 + FFN_CHUNK])
            a_ref[rs, c0:c0 + FFN_CHUNK] = (_silu(g) * u).astype(BF16)
        o_ref[rs] = x + mod(5) * _dot(a_ref[rs], wo_ref[...])


def _tail(att, four, of, ob, g4, x, mod3, mod_row, lw, tm, cast_next=()):
    b, n, d = x.shape
    steps = n // tm
    row = lambda bi: MOD_ROWS * lw["layer"] + (bi if mod_row is None else mod_row)
    tok = lambda w: pl.BlockSpec((None, tm, w), lambda bi, i: (bi, i, 0))
    gate = pl.BlockSpec((None, tm, GLA_WIDTH), lambda bi, i: (bi, i, 3))
    small = lambda shape: _layer_block(shape, lw["layer"])
    big = lambda shape: _layer_block(shape, lw["big_layer"])
    slab = lambda wgt, layer: pl.BlockSpec((None, wgt.shape[1] // (b * steps), wgt.shape[2]),
                                           lambda bi, i: (layer, bi * steps + i, 0))
    assert all(wgt.shape[1] % (b * steps * 16) == 0 for wgt in cast_next)
    out_shape = [jax.ShapeDtypeStruct((b, n, d), F32)]
    out_shape += [jax.ShapeDtypeStruct((1,) + wgt.shape[1:], BF16) for wgt in cast_next]
    res = pl.pallas_call(
        functools.partial(_tail_kernel, ncast=len(cast_next)),
        out_shape=out_shape,
        grid=(b, steps),
        in_specs=[tok(ATT_WIDTH), tok(FNET_WIDTH), tok(GLA_WIDTH), tok(GLA_WIDTH), gate, tok(d),
                  pl.BlockSpec((None, 1, 6 * d), lambda bi, i: (row(bi), 0, 0)),
                  small((1, GLA_WIDTH)), _full((GLA_WIDTH, GLA_WIDTH)), big((d, d)),
                  small((1, d)), big((d, 2 * FFN_HIDDEN)), big((FFN_HIDDEN, d))]
        + [slab(wgt, lw["layer"] + 1) for wgt in cast_next],
        out_specs=[tok(d)] + [slab(wgt, 0) for wgt in cast_next],
        scratch_shapes=[pltpu.VMEM((tm, FFN_HIDDEN), BF16)],
        compiler_params=_params("arbitrary", "arbitrary", vmem_limit=TAIL_VMEM_LIMIT),
        name="mix_ffn",
    )(att, four, of, ob, g4, x, mod3, lw["gla_g"], lw["bd256"], lw["w_out"], lw["g2"], lw["w_ffn_in"],
      lw["w_ffn_out"], *cast_next)
    return res[0], res[1:]


def _rope_tables(n):
    axis_dim = HEAD_DIM // 2
    inv_freq = ROPE_BASE ** (-np.arange(0, axis_dim, 2, dtype=np.float64) / axis_dim)
    t = np.arange(n)
    ang_r = (t // GRID_W)[:, None] * inv_freq[None, :]
    ang_c = (t % GRID_W)[:, None] * inv_freq[None, :]
    cos = np.concatenate([np.cos(ang_r)] * 2 + [np.cos(ang_c)] * 2, axis=1)
    sin = np.concatenate([-np.sin(ang_r), np.sin(ang_r), -np.sin(ang_c), np.sin(ang_c)], axis=1)
    return jnp.asarray(np.tile(cos, (1, 2)), F32), jnp.asarray(np.tile(sin, (1, 2)), F32)


def _block_diag_mean(width):
    g = np.arange(width) // HEAD_DIM
    return jnp.asarray((g[:, None] == g[None, :]) / HEAD_DIM, BF16)


def _prepare_weights(w_in, g_norm1, q_norm_g, k_norm_g, w_fourier, wgf, bgf, wgb, bgb, gla_norm_g, w_out, g_norm2,
                     w_ffn_in, w_ffn_out):
    depth = w_in.shape[0]
    r = GLA_GATE_RANK
    row = lambda t: t[:, None, :]
    zr = jnp.zeros((depth, r, GLA_WIDTH), F32)
    w_gate = jnp.concatenate([jnp.concatenate([wgf, zr], axis=2), jnp.concatenate([zr, wgb], axis=2)], axis=1)
    eye = jnp.eye(FNET_GROUPS, dtype=F32)
    wf_bd = jnp.einsum("lgij,gh->lgihj", w_fourier, eye).reshape(depth, FNET_WIDTH, FNET_WIDTH)
    return {
        "g1": row(g_norm1),
        "w_main": w_in.astype(BF16),
        "w_z": jnp.pad(w_in[:, :, MAIN_WIDTH:], ((0, 0), (0, 0), (0, LANES - 2 * r))).astype(BF16),
        "w_gate": jnp.pad(w_gate, ((0, 0), (0, LANES - 2 * r), (0, 0))).astype(BF16),
        "b_gate": row(jnp.concatenate([bgf, bgb], axis=1)),
        "q_g": row(jnp.tile(q_norm_g, (1, 2))),
        "k_g": row(jnp.tile(k_norm_g, (1, 2))),
        "bd256": _block_diag_mean(GLA_WIDTH),
        "wf_bd": wf_bd.astype(BF16),
        "gla_g": row(jnp.tile(gla_norm_g, (1, GLA_HEADS))),
        "w_out": w_out[0:1].astype(BF16),
        "g2": row(g_norm2),
        "w_ffn_in": w_ffn_in[0:1].astype(BF16),
        "w_ffn_out": w_ffn_out[0:1].astype(BF16),
    }


def kernel(x, c, ctx, c_ctx, w_mod, b_mod, g_norm1, w_in, q_norm_g, k_norm_g, attn_sink, w_fourier, gla_w_gate_f,
           gla_b_gate_f, gla_w_gate_b, gla_b_gate_b, gla_norm_g, w_out, g_norm2, w_ffn_in, w_ffn_out):
    b, n, d = x.shape
    lc = ctx.shape[1]
    depth = w_mod.shape[0]
    assert d == D_MODEL and b < MOD_ROWS and n % TOKEN_BLOCK == 0 and n % (DFT_N1 * DFT_STEP) == 0 and lc % GLA_CHUNK == 0

    cc = jnp.concatenate([c, c_ctx[None, :], jnp.zeros((MOD_ROWS - b - 1, d), F32)], axis=0)
    mod3 = _modulation(cc, w_mod, b_mod).reshape(depth * MOD_ROWS, 1, 6 * d)
    rope_tabs = _rope_tables(n)
    dft = _dft_consts(n)
    dft_ctx = _dft_ctx_consts(lc)
    flat = lambda t: t.reshape(1, b * lc, t.shape[-1])
    unflat = lambda t: t.reshape(b, lc, t.shape[-1])
    xc = ctx
    weights = _prepare_weights(w_in, g_norm1, q_norm_g, k_norm_g, w_fourier, gla_w_gate_f, gla_b_gate_f,
                               gla_w_gate_b, gla_b_gate_b, gla_norm_g, w_out, g_norm2, w_ffn_in, w_ffn_out)
    s_zero = jnp.zeros((b, 2, GLA_WIDTH, LANES), F32)
    tail_weights = ("w_out", "w_ffn_in", "w_ffn_out")
    for l in range(depth):
        need_ctx = l < depth - 1
        lw = dict(weights, layer=l, big_layer=0)
        sink = attn_sink[l]
        cq, ckv4, cg4, clf = map(unflat, _inproj(flat(xc), mod3, b, lw, None, b * lc))
        q, kv4, g4, lf = _inproj(x, mod3, None, lw, rope_tabs, TOKEN_BLOCK)
        att = _attention(q, kv4, ckv4, sink)
        four = _fourier(lf, lw["wf_bd"], l, dft)
        ocf, ocb, s_ctx = _gla(cg4, clf, s_zero, lc)
        olf, olb, _ = _gla(g4, lf, s_ctx, TOKEN_BLOCK)
        cast_next = (w_out, w_ffn_in, w_ffn_out) if l + 1 < depth else ()
        x, next_tail = _tail(att, four, olf, olb, g4, x, mod3, None, lw, TOKEN_BLOCK, cast_next)
        if need_ctx:
            att_c = _attention_ctx(cq, ckv4, sink)
            four_c = _fourier_ctx(clf, lw["wf_bd"], l, dft_ctx)
            xc = unflat(_tail(*map(flat, (att_c, four_c, ocf, ocb, cg4, xc)), mod3, b, lw, b * lc)[0])
        weights = dict(weights, **dict(zip(tail_weights, next_tail)))
    return x
```

```python
import functools

import numpy as np
import jax
import jax.numpy as jnp
from jax import lax
from jax.experimental import pallas as pl
from jax.experimental.pallas import tpu as pltpu

F32 = jnp.float32
BF16 = jnp.bfloat16

D_MODEL = 1024
HEAD_DIM = 64
GRID_W = 64
ROPE_BASE = 10000.0
ATT_HEADS = 8
ATT_KV_HEADS = 2
ATT_WIDTH = ATT_HEADS * HEAD_DIM
KV_WIDTH = ATT_KV_HEADS * HEAD_DIM
ATT_BLOCK = 128
ATT_QBLOCKS = 8
ATT_SCALE = HEAD_DIM ** -0.5
LOG2E = 1.4426950408889634
NEG_INF = -1e30
FNET_GROUPS = 4
FNET_WIDTH = FNET_GROUPS * HEAD_DIM
GLA_HEADS = 4
GLA_WIDTH = GLA_HEADS * HEAD_DIM
GLA_GATE_RANK = 16
GLA_TAU = 16.0
GLA_CHUNK = 64
GLA_SCALE = HEAD_DIM ** -0.5
MAIN_WIDTH = ATT_WIDTH + 2 * KV_WIDTH + FNET_WIDTH + 4 * GLA_WIDTH
FFN_HIDDEN = 2816
FFN_CHUNK = 256
INPROJ_SUB = 512
TAIL_SUB = 512
FU_BLOCK = 2
GLA_GROUP = 16
TOKEN_BLOCK = 1024
EPS = 1e-6
LANES = 128
MOD_ROWS = 8
DFT_N1 = 128
DFT_STEP = 16
VMEM_LIMIT = 56 * 1024 * 1024
TAIL_VMEM_LIMIT = 61 * 1024 * 1024

NT_DIMS = (((1,), (1,)), ((), ()))
TN_DIMS = (((0,), (0,)), ((), ()))


def _params(*sem, vmem_limit=VMEM_LIMIT):
    return pltpu.CompilerParams(dimension_semantics=sem, vmem_limit_bytes=vmem_limit)


def _dot(a, b):
    return jnp.dot(a, b, preferred_element_type=F32)


def _silu(x):
    return x / (1.0 + jnp.exp(-x))


def _layer_block(shape, layer):
    nd = len(shape)
    return pl.BlockSpec((None,) + tuple(shape), lambda *_: (layer,) + (0,) * nd, pipeline_mode=pl.Buffered(1))


def _slab_cast_specs(weights, layer, grid):
    steps = grid[0] * grid[1]
    assert all(w.shape[1] % (16 * steps) == 0 for w in weights)
    spec = lambda w, lyr: pl.BlockSpec((None, w.shape[1] // steps, w.shape[2]),
                                       lambda i, j: (lyr, i * grid[1] + j, 0))
    return ([spec(w, layer) for w in weights], [spec(w, 0) for w in weights],
            [jax.ShapeDtypeStruct((1,) + w.shape[1:], BF16) for w in weights])


def _full(shape):
    nd = len(shape)
    return pl.BlockSpec(shape, lambda *_: (0,) * nd)


def _group_mean(t, bd_ref):
    return _dot((t * t).astype(BF16), bd_ref[...])


def _mod_kernel(c_ref, w_ref, b_ref, o_ref):
    s = _silu(c_ref[...]).astype(BF16)
    o_ref[...] = _dot(s, w_ref[...].astype(BF16)) + b_ref[...]


def _modulation(cc, w_mod, b_mod):
    depth, d, width = w_mod.shape
    tn = 1536
    return pl.pallas_call(
        _mod_kernel,
        out_shape=jax.ShapeDtypeStruct((depth, MOD_ROWS, width), F32),
        grid=(depth, width // tn),
        in_specs=[
            _full((MOD_ROWS, d)),
            pl.BlockSpec((None, d, tn), lambda l, j: (l, 0, j)),
            pl.BlockSpec((None, 1, tn), lambda l, j: (l, 0, j)),
        ],
        out_specs=pl.BlockSpec((None, MOD_ROWS, tn), lambda l, j: (l, 0, j)),
        compiler_params=_params("parallel", "parallel"),
        name="modulation",
    )(cc, w_mod, b_mod.reshape(depth, 1, width))


def _inproj_kernel(*refs, rope):
    if rope:
        (x_ref, mod_ref, g1_ref, wm_ref, wz_ref, wg_ref, bg_ref, qg_ref, kg_ref, bd_ref, cos_ref, sin_ref,
         q_ref, kv4_ref, g4_ref, lf_ref) = refs
    else:
        (x_ref, mod_ref, g1_ref, wm_ref, wz_ref, wg_ref, bg_ref, qg_ref, kg_ref, bd_ref,
         q_ref, kv4_ref, g4_ref, lf_ref) = refs
    d = D_MODEL
    mod = mod_ref[...]
    sh, sc = mod[:, 0:d], mod[:, d:2 * d]
    sub = min(x_ref.shape[0], INPROJ_SUB)
    subs = [slice(sub * j, sub * (j + 1)) for j in range(x_ref.shape[0] // sub)]

    def normed(rs):
        x = x_ref[rs]
        ms = jnp.mean(x * x, axis=-1, keepdims=True)
        return ((x * lax.rsqrt(ms + EPS) * g1_ref[...]) * (1.0 + sc) + sh).astype(BF16)

    lane = lax.broadcasted_iota(jnp.int32, (1, LANES), 1)
    lo = lane < HEAD_DIM
    second_half = (lane & 16) != 0

    def head_norm(t, g_ref):
        wdt = t.shape[1]
        ms = _dot((t * t).astype(BF16), bd_ref[0:wdt, 0:wdt])
        g = g_ref[...] if wdt == LANES else jnp.concatenate([g_ref[...]] * (wdt // LANES), axis=1)
        return t * lax.rsqrt(ms + EPS) * g

    def rotary(t, rs):
        if not rope:
            return t
        partner = jnp.where(second_half, pltpu.roll(t, 16, 1), pltpu.roll(t, LANES - 16, 1))
        return t * cos_ref[rs] + partner * sin_ref[rs]

    def spread(t, c0, rs, idle):
        tr = pltpu.roll(t, HEAD_DIM, 1)
        fill = jnp.full_like(t, idle)
        tiles = (jnp.where(lo, t, fill), jnp.where(lo, fill, tr), jnp.where(lo, tr, fill), jnp.where(lo, fill, t))
        for j, tile in enumerate(tiles):
            kv4_ref[rs, c0 + LANES * j:c0 + LANES * (j + 1)] = tile.astype(kv4_ref.dtype)

    assert (ATT_WIDTH, 2 * KV_WIDTH + FNET_WIDTH, 2 * GLA_WIDTH) == (4 * LANES,) * 3

    def project(rs, hb):
        group = lambda g: _dot(hb, wm_ref[:, 4 * LANES * g:4 * LANES * (g + 1)])
        gz = _dot(hb, wz_ref[...])
        a_q = group(0)
        z = _dot(gz.astype(BF16), wg_ref[...]) + bg_ref[...]
        log_sig = jnp.minimum(z, 0.0) - jnp.log(1.0 + jnp.exp(-jnp.abs(z)))
        lf_ref[rs, 0:2 * GLA_WIDTH] = log_sig * (LOG2E / GLA_TAU)
        a_kvf = group(1)
        for j2 in range(ATT_WIDTH // (2 * LANES)):
            qn = head_norm(a_q[:, 2 * LANES * j2:2 * LANES * (j2 + 1)], qg_ref)
            for j in range(2):
                t = rotary(qn[:, LANES * j:LANES * (j + 1)], rs) * (ATT_SCALE * LOG2E)
                q_ref[rs, LANES * (2 * j2 + j):LANES * (2 * j2 + j + 1)] = t.astype(q_ref.dtype)
        a_qk = group(2)
        spread(rotary(head_norm(a_kvf[:, 0:KV_WIDTH], kg_ref), rs), 0, rs, 0.0)
        spread(a_kvf[:, KV_WIDTH:2 * KV_WIDTH], 4 * LANES, rs, 1.0)
        lf_ref[rs, 2 * GLA_WIDTH:] = a_kvf[:, 2 * KV_WIDTH:]
        a_vr = group(3)
        g4_ref[rs, 0:GLA_WIDTH] = (a_qk[:, 0:GLA_WIDTH] * GLA_SCALE).astype(g4_ref.dtype)
        g4_ref[rs, GLA_WIDTH:2 * GLA_WIDTH] = a_qk[:, GLA_WIDTH:].astype(g4_ref.dtype)
        g4_ref[rs, 2 * GLA_WIDTH:4 * GLA_WIDTH] = a_vr.astype(g4_ref.dtype)

    hbs = [normed(rs) for rs in subs]
    for rs, hb in zip(subs, hbs):
        project(rs, hb)


def _inproj(x, mod3, mod_row, lw, rope_tabs, tm):
    b, n, d = x.shape
    rope = rope_tabs is not None
    row = lambda bi: MOD_ROWS * lw["layer"] + (bi if mod_row is None else mod_row)
    tok = lambda w: pl.BlockSpec((None, tm, w), lambda bi, i: (bi, i, 0))
    lb = lambda shape: _layer_block(shape, lw["layer"])
    in_specs = [
        tok(d),
        pl.BlockSpec((None, 1, 6 * d), lambda bi, i: (row(bi), 0, 0)),
        lb((1, d)), lb((d, MAIN_WIDTH)), lb((d, LANES)), lb((LANES, 2 * GLA_WIDTH)),
        lb((1, 2 * GLA_WIDTH)), lb((1, LANES)), lb((1, LANES)), _full((2 * LANES, 2 * LANES)),
    ]
    args = [x, mod3, lw["g1"], lw["w_main"], lw["w_z"], lw["w_gate"], lw["b_gate"], lw["q_g"], lw["k_g"], lw["bd256"]]
    if rope:
        in_specs += [pl.BlockSpec((tm, LANES), lambda bi, i: (i, 0))] * 2
        args += list(rope_tabs)
    sds = lambda w, dt: jax.ShapeDtypeStruct((b, n, w), dt)
    out_shape = [sds(ATT_WIDTH, BF16), sds(8 * LANES, BF16), sds(4 * GLA_WIDTH, BF16),
                 sds(2 * GLA_WIDTH + FNET_WIDTH, F32)]
    out_specs = [tok(s.shape[-1]) for s in out_shape]
    return pl.pallas_call(
        functools.partial(_inproj_kernel, rope=rope),
        out_shape=out_shape, grid=(b, n // tm), in_specs=in_specs, out_specs=out_specs,
        compiler_params=_params("parallel", "parallel"),
        name="inproj_rope" if rope else "inproj_ctx",
    )(*args)


def _scores(q2, key_tiles, masks):
    cols = []
    for k, tile_masks in zip(key_tiles, masks):
        s = lax.dot_general(q2, k, NT_DIMS, preferred_element_type=F32)
        for j, mk in enumerate(tile_masks):
            c = s[:, LANES * j:LANES * (j + 1)]
            cols.append(c if mk is None else jnp.where(mk, c, NEG_INF))
    return cols


def _softmax_pv(cols, val_tiles, sink_col):
    mx = cols[0]
    for c in cols[1:]:
        mx = jnp.maximum(mx, c)
    m = jnp.maximum(jnp.max(mx, axis=-1, keepdims=True), sink_col)
    p = jnp.concatenate([jnp.exp2(c - m).astype(BF16) for c in cols], axis=1)
    pv = _dot(p, jnp.concatenate(val_tiles, axis=0))
    den = pltpu.roll(pv, HEAD_DIM, 1) + jnp.exp2(sink_col - m)
    return pv / den


def _attn_kernel(*refs):
    nq = ATT_QBLOCKS
    sink_ref, q_ref = refs[0:2]
    kv_refs, kvc_ref, o_ref = refs[2:nq + 4], refs[nq + 4], refs[nq + 5]
    v_off = 4 * LANES
    i = pl.program_id(1)
    last = pl.num_programs(1) - 1
    blk = ATT_BLOCK
    group = ATT_HEADS // ATT_KV_HEADS
    row = lax.broadcasted_iota(jnp.int32, (2 * blk, blk), 0) & (blk - 1)
    col = lax.broadcasted_iota(jnp.int32, (2 * blk, blk), 1)
    band_l, band_r = col >= row, col <= row
    mask_l = [jnp.logical_and(band_l, i > 0) if s == 0 else band_l for s in range(nq)]
    mask_r = [jnp.logical_and(band_r, i < last) if s == nq - 1 else band_r for s in range(nq)]
    upper_rows = lax.broadcasted_iota(jnp.int32, (2 * blk, 1), 0) >= blk
    lo = lax.broadcasted_iota(jnp.int32, (1, LANES), 1) < HEAD_DIM

    def tiles(s, ks):
        l, m_, r = kv_refs[s:s + 3]
        return [jnp.concatenate([l[:, ks], m_[:, ks]], axis=0),
                jnp.concatenate([r[:, ks], kvc_ref[0:blk, ks]], axis=0), kvc_ref[blk:, ks]]

    slots = [(s, kvh, par) for s in range(nq) for kvh in range(ATT_KV_HEADS) for par in range(2)]
    lane_slice = lambda kvh, par, off=0: slice(off + 2 * LANES * kvh + LANES * par,
                                               off + 2 * LANES * kvh + LANES * (par + 1))
    cols = {}
    for s, kvh, par in slots:
        qs, rs = 2 * LANES * kvh, slice(blk * s, blk * (s + 1))
        q2 = jnp.concatenate([q_ref[rs, qs:qs + LANES], q_ref[rs, qs + LANES:qs + 2 * LANES]], axis=0)
        cols[s, kvh, par] = _scores(q2, tiles(s, lane_slice(kvh, par)),
                                    [(mask_l[s], None), (mask_r[s], None), (None,)])
    outs = {}
    for s, kvh, par in slots:
        base = group * kvh + par
        sink_col = jnp.where(upper_rows, sink_ref[base + 2], sink_ref[base]) * LOG2E
        outs[s, kvh, par] = _softmax_pv(cols[s, kvh, par], tiles(s, lane_slice(kvh, par, v_off)), sink_col)
    for s in range(nq):
        for kvh in range(ATT_KV_HEADS):
            qs, r0 = 2 * LANES * kvh, blk * s
            o = jnp.where(lo, outs[s, kvh, 0], outs[s, kvh, 1])
            o_ref[r0:r0 + blk, qs:qs + LANES] = o[0:blk].astype(o_ref.dtype)
            o_ref[r0:r0 + blk, qs + LANES:qs + 2 * LANES] = o[blk:2 * blk].astype(o_ref.dtype)


def _attention(q, kv4, ckv4, sink):
    b, n, _ = q.shape
    lc = ckv4.shape[1]
    nq = ATT_QBLOCKS
    nb = n // ATT_BLOCK
    assert lc == 2 * ATT_BLOCK and nb % nq == 0
    qspec = pl.BlockSpec((None, nq * ATT_BLOCK, 4 * LANES), lambda bi, i: (bi, i, 0))
    kv = [pl.BlockSpec((None, ATT_BLOCK, 8 * LANES),
                       functools.partial(lambda bi, i, j: (bi, jnp.clip(nq * i + j - 1, 0, nb - 1), 0), j=j))
          for j in range(nq + 2)]
    ctx = pl.BlockSpec((None, lc, 8 * LANES), lambda bi, i: (bi, 0, 0))
    return pl.pallas_call(
        _attn_kernel,
        out_shape=jax.ShapeDtypeStruct((b, n, ATT_WIDTH), BF16),
        grid=(b, nb // nq),
        in_specs=[pl.BlockSpec(memory_space=pltpu.SMEM), qspec] + kv + [ctx],
        out_specs=qspec,
        compiler_params=_params("parallel", "parallel"),
        name="window_attention",
    )(sink, q, *([kv4] * (nq + 2)), ckv4)


def _attn_ctx_kernel(sink_ref, q_ref, kc_ref, vc_ref, o_ref):
    kvh = pl.program_id(1)
    lc = q_ref.shape[0]
    q2 = jnp.concatenate([q_ref[:, 0:LANES], q_ref[:, LANES:2 * LANES]], axis=0)
    upper_rows = lax.broadcasted_iota(jnp.int32, (2 * lc, 1), 0) >= lc
    lo = lax.broadcasted_iota(jnp.int32, (1, LANES), 1) < HEAD_DIM
    outs = []
    for par in range(2):
        ks = slice(LANES * par, LANES * (par + 1))
        base = ATT_HEADS // ATT_KV_HEADS * kvh + par
        sink_col = jnp.where(upper_rows, sink_ref[base + 2], sink_ref[base]) * LOG2E
        cols = _scores(q2, [kc_ref[:, ks]], [(None,) * (lc // LANES)])
        outs.append(_softmax_pv(cols, [vc_ref[:, ks]], sink_col))
    o = jnp.where(lo, outs[0], outs[1])
    o_ref[:, 0:LANES] = o[0:lc].astype(o_ref.dtype)
    o_ref[:, LANES:2 * LANES] = o[lc:2 * lc].astype(o_ref.dtype)


def _attention_ctx(cq, ckv4, sink):
    b, lc, _ = cq.shape
    gw = 2 * LANES
    spec = pl.BlockSpec((None, lc, gw), lambda bi, h: (bi, 0, h))
    vspec = pl.BlockSpec((None, lc, gw), lambda bi, h: (bi, 0, ATT_KV_HEADS + h))
    return pl.pallas_call(
        _attn_ctx_kernel,
        out_shape=jax.ShapeDtypeStruct((b, lc, ATT_WIDTH), BF16),
        grid=(b, ATT_KV_HEADS),
        in_specs=[pl.BlockSpec(memory_space=pltpu.SMEM), spec, spec, vspec],
        out_specs=spec,
        compiler_params=_params("parallel", "parallel"),
        name="context_attention",
    )(sink, cq, ckv4, ckv4)


def _dft_consts(n):
    n1, n2 = DFT_N1, n // DFT_N1
    ang = lambda a, m: 2.0 * np.pi * np.outer(np.arange(a), np.arange(a)) / m
    c = HEAD_DIM
    cc, sc = np.cos(ang(c, c)) / np.sqrt(c), np.sin(ang(c, c)) / np.sqrt(c)
    eye = np.eye(FNET_GROUPS)
    w_chan = np.concatenate([np.kron(eye, cc), -np.kron(eye, sc)], axis=1)
    c1, s1 = np.cos(ang(n1, n1)) / np.sqrt(n1), np.sin(ang(n1, n1)) / np.sqrt(n1)
    m1 = np.block([[c1, s1], [-s1, c1]])
    kk = np.arange(n1)[:, None, None] + n1 * np.arange(n2)[None, :, None]
    ph = 2.0 * np.pi * (kk * np.arange(n2)[None, None, :] % n) / n
    m3 = np.concatenate([np.cos(ph), np.sin(ph)], axis=2) / np.sqrt(n2)
    return tuple(jnp.asarray(a, F32) for a in (w_chan, m1, m3))


def _dft_ctx_consts(lc):
    ang = lambda a, m: 2.0 * np.pi * np.outer(np.arange(a), np.arange(a)) / m
    c = HEAD_DIM
    cc, sc = np.cos(ang(c, c)) / np.sqrt(c), np.sin(ang(c, c)) / np.sqrt(c)
    eye = np.eye(FNET_GROUPS)
    w_chan = np.concatenate([np.kron(eye, cc), -np.kron(eye, sc)], axis=1)
    cl, sl = np.cos(ang(lc, lc)) / np.sqrt(lc), np.sin(ang(lc, lc)) / np.sqrt(lc)
    return jnp.asarray(w_chan, F32), jnp.asarray(np.concatenate([cl, sl], axis=1), F32)


def _regroup_rows(x, outer):
    r, c = x.shape
    return jnp.swapaxes(x.reshape(outer, r // outer, c), 0, 1).reshape(r, c)


def _fourier_kernel(u_ref, wc_ref, m1_ref, m3_ref, wf_ref, y_ref, p_ref):
    w, t = FNET_WIDTH, DFT_STEP
    s = pl.program_id(1)
    n_a = p_ref.shape[1]
    n2 = n_a * t

    @pl.when(s < n_a)
    def _():
        u = u_ref[...].reshape(DFT_N1 * t, w).astype(BF16)
        z = _regroup_rows(_dot(u, wc_ref[...]), DFT_N1)
        a_re, a_im = [], []
        for tt in range(t):
            zt = z[DFT_N1 * tt:DFT_N1 * (tt + 1)]
            zs = jnp.concatenate([zt[:, 0:w], zt[:, w:2 * w]], axis=0).astype(BF16)
            a = _dot(m1_ref[...], zs)
            a_re.append(a[0:DFT_N1])
            a_im.append(a[DFT_N1:2 * DFT_N1])
        for part, rows in enumerate((a_re, a_im)):
            a = _regroup_rows(jnp.concatenate(rows, axis=0), t)
            p_ref[part, s] = a.reshape(DFT_N1, t, w)

    @pl.when(s >= n_a)
    def _():
        k1_0 = (s - n_a) * t
        ys = []
        for kk in range(t):
            parts = [p_ref[part, :, k1_0 + kk].reshape(n2, w) for part in range(2)]
            rhs = jnp.concatenate(parts, axis=0).astype(BF16)
            ys.append(_dot(m3_ref[kk].astype(BF16), rhs).astype(BF16))
        out = _dot(jnp.concatenate(ys, axis=0), wf_ref[...])
        y_ref[...] = _regroup_rows(out, t).reshape(n2, t, w)


def _fourier(lf, wf_bd, layer, consts):
    b, n, wl = lf.shape
    w = FNET_WIDTH
    n1, n2, t = DFT_N1, n // DFT_N1, DFT_STEP
    n_a, n_b = n2 // t, n1 // t
    w_chan, m1, m3 = consts
    y = pl.pallas_call(
        _fourier_kernel,
        out_shape=jax.ShapeDtypeStruct((b, n2, n1, w), F32),
        grid=(b, n_a + n_b),
        in_specs=[pl.BlockSpec((None, n1, t, w), lambda bi, s: (bi, 0, jnp.minimum(s, n_a - 1), FU_BLOCK)),
                  _full((w, 2 * w)), _full((2 * n1, 2 * n1)),
                  pl.BlockSpec((t, n2, 2 * n2), lambda bi, s: (jnp.maximum(s - n_a, 0), 0, 0)),
                  _layer_block((w, w), layer)],
        out_specs=pl.BlockSpec((None, n2, t, w), lambda bi, s: (bi, 0, jnp.maximum(s - n_a, 0), 0)),
        scratch_shapes=[pltpu.VMEM((2, n_a, n1, t, w), F32)],
        compiler_params=_params("parallel", "arbitrary"),
        name="fourier_mix",
    )(lf.reshape(b, n1, n2, wl), w_chan.astype(BF16), m1.astype(BF16), m3, wf_bd)
    return y.reshape(b, n, w)


def _four_ctx_kernel(u_ref, wc_ref, m_ref, wf_ref, y_ref):
    w = FNET_WIDTH
    z = _dot(u_ref[...].astype(BF16), wc_ref[...])
    zs = jnp.concatenate([z[:, 0:w], z[:, w:2 * w]], axis=0).astype(BF16)
    y = _dot(m_ref[...], zs)
    y_ref[...] = _dot(y.astype(BF16), wf_ref[...]).astype(y_ref.dtype)


def _fourier_ctx(clf, wf_bd, layer, consts):
    b, lc, _ = clf.shape
    w = FNET_WIDTH
    w_chan, m = consts[0].astype(BF16), consts[1].astype(BF16)
    return pl.pallas_call(
        _four_ctx_kernel,
        out_shape=jax.ShapeDtypeStruct((b, lc, w), BF16),
        grid=(b,),
        in_specs=[pl.BlockSpec((None, lc, w), lambda bi: (bi, 0, FU_BLOCK)), _full((w, 2 * w)), _full((lc, 2 * lc)),
                  _layer_block((w, w), layer)],
        out_specs=pl.BlockSpec((None, lc, w), lambda bi: (bi, 0, 0)),
        compiler_params=_params("parallel"),
        name="fourier_context",
    )(clf, w_chan, m, wf_bd)


def _pair_stack(x, lo):
    zero = jnp.zeros_like(x)
    return jnp.concatenate([jnp.where(lo, x, zero), jnp.where(lo, zero, x)], axis=0)


def _gla_kernel(*refs, ncast):
    qkvf_ref, laf_ref, qkvb_ref, lab_ref, s0_ref = refs[0:5]
    of_ref, ob_ref, sfin_ref = refs[5 + ncast:8 + ncast]
    stf_ref, stb_ref = refs[8 + 2 * ncast:10 + 2 * ncast]
    for src, dst in zip(refs[5:5 + ncast], refs[8 + ncast:8 + 2 * ncast]):
        dst[...] = src[...].astype(dst.dtype)
    i = pl.program_id(1)
    c = GLA_CHUNK
    nchunk = qkvf_ref.shape[0] // c

    @pl.when(i == 0)
    def _():
        stf_ref[...] = s0_ref[0]
        stb_ref[...] = s0_ref[1]

    r64 = lax.broadcasted_iota(jnp.int32, (c, c), 0)
    c64 = lax.broadcasted_iota(jnp.int32, (c, c), 1)
    tri = ((c64 <= r64).astype(BF16), (c64 >= r64).astype(BF16))
    at = lax.broadcasted_iota(jnp.int32, (c, LANES), 0)
    as_ = lax.broadcasted_iota(jnp.int32, (c, LANES), 1) & (c - 1)
    att_mask = (as_ <= at, as_ >= at)
    lo = lax.broadcasted_iota(jnp.int32, (1, LANES), 1) < HEAD_DIM
    br = lax.broadcasted_iota(jnp.int32, (LANES, LANES), 0) < HEAD_DIM
    bc = lax.broadcasted_iota(jnp.int32, (LANES, LANES), 1) < HEAD_DIM
    bd_mask = br == bc
    pairs = [slice(LANES * p, LANES * (p + 1)) for p in range(GLA_WIDTH // LANES)]
    in_refs = ((qkvf_ref, laf_ref), (qkvb_ref, lab_ref))
    out_refs = (of_ref, ob_ref)
    items = [(d, step if d == 0 else nchunk - 1 - step) for step in range(nchunk) for d in range(2)]
    rows = lambda ch: slice(c * ch, c * (ch + 1))

    bcum, work, o_intra, ut, decay = {}, {}, {}, {}, {}
    st_refs = (stf_ref, stb_ref)
    st = [[st_refs[d][LANES * j:LANES * (j + 1)] for j in range(len(pairs))] for d in range(2)]

    def gate_sums(d, ch):
        la = in_refs[d][1][rows(ch)]
        la_hi = la.astype(BF16)
        la_lo = (la - la_hi.astype(F32)).astype(BF16)
        bcum[d, ch] = _dot(jnp.concatenate([tri[d], tri[d]], axis=1), jnp.concatenate([la_hi, la_lo], axis=0))

    def scores(d, ch):
        qkv_ref = in_refs[d][0]
        w = GLA_WIDTH
        b = bcum[d, ch]
        btot = b[0:1] if d == 1 else b[c - 1:c]
        k = qkv_ref[rows(ch), w:2 * w].astype(F32)
        q_in = (qkv_ref[rows(ch), 0:w].astype(F32) * jnp.exp2(b)).astype(BF16)
        k_in = (k * jnp.exp2(-b)).astype(BF16)
        k_out = (k * jnp.exp2(btot - b)).astype(BF16)
        vb = qkv_ref[rows(ch), 2 * w:3 * w]
        att = [lax.dot_general(q_in[:, p], _pair_stack(k_in[:, p], lo), NT_DIMS, preferred_element_type=F32)
               for p in pairs]
        work[d, ch] = (btot, q_in, k_out, vb, att)

    def intra(d, ch):
        btot, q_in, k_out, vb, att = work[d, ch]
        o_intra[d, ch] = [_dot(jnp.where(att_mask[d], a, 0.0).astype(BF16), _pair_stack(vb[:, p], lo))
                          for a, p in zip(att, pairs)]
        ut[d, ch] = [lax.dot_general(k_out[:, p], vb[:, p], TN_DIMS, preferred_element_type=F32) for p in pairs]
        decay[d, ch] = [jnp.transpose(jnp.exp2(btot[:, p])) for p in pairs]

    def carry(d, ch):
        btot, q_in = work[d, ch][0:2]
        outs = []
        for j, p in enumerate(pairs):
            outs.append(o_intra[d, ch][j] + _dot(q_in[:, p], st[d][j].astype(BF16)))
            st[d][j] = st[d][j] * decay[d, ch][j] + jnp.where(bd_mask, ut[d, ch][j], 0.0)
        out_refs[d][rows(ch)] = jnp.concatenate(outs, axis=1).astype(out_refs[d].dtype)

    groups = [items[g:g + GLA_GROUP] for g in range(0, len(items), GLA_GROUP)]
    for it in groups[0]:
        gate_sums(*it)
    for it in groups[0]:
        scores(*it)
    for g, grp in enumerate(groups):
        for it in grp:
            intra(*it)
        nxt = groups[g + 1] if g + 1 < len(groups) else []
        for it in nxt:
            gate_sums(*it)
        for k_ in range(len(grp)):
            carry(*grp[k_])
            if k_ < len(nxt):
                scores(*nxt[k_])
    for d in range(2):
        st_refs[d][...] = jnp.concatenate(st[d], axis=0)

    @pl.when(i == pl.num_programs(1) - 1)
    def _():
        for d in range(2):
            sfin_ref[d] = jnp.concatenate(st[d], axis=0)


def _gla(g4, lf, s0, tb, cast=(), cast_layer=0):
    b, n, _ = g4.shape
    w = GLA_WIDTH
    nblk = n // tb
    fwd = lambda bi, i: (bi, i, 0)
    bwd = lambda bi, i: (bi, nblk - 1 - i, 0)
    bwd_la = lambda bi, i: (bi, nblk - 1 - i, 1)
    tok = lambda f: pl.BlockSpec((None, tb, w), f)
    qkv = lambda f: pl.BlockSpec((None, tb, 3 * w), f)
    state = pl.BlockSpec((None, 2, w, LANES), lambda bi, i: (bi, 0, 0, 0))
    cast_in, cast_out, cast_shapes = _slab_cast_specs(cast, cast_layer, (b, nblk))
    res = pl.pallas_call(
        functools.partial(_gla_kernel, ncast=len(cast)),
        out_shape=[jax.ShapeDtypeStruct((b, n, w), BF16), jax.ShapeDtypeStruct((b, n, w), BF16),
                   jax.ShapeDtypeStruct((b, 2, w, LANES), F32)] + cast_shapes,
        grid=(b, nblk),
        in_specs=[qkv(fwd), tok(fwd), qkv(bwd), tok(bwd_la), state] + cast_in,
        out_specs=[tok(fwd), tok(bwd), state] + cast_out,
        scratch_shapes=[pltpu.VMEM((w, LANES), F32), pltpu.VMEM((w, LANES), F32)],
        compiler_params=_params("arbitrary", "arbitrary"),
        name="gla_scan",
    )(g4, lf, g4, lf, s0, *cast)
    return res[0], res[1], res[2], res[3:]


def _tail_kernel(*refs, ncast):
    (att_ref, four_ref, of_ref, ob_ref, r_ref, x_ref, mod_ref, gg_ref, bd_ref, wmix_ref, g2_ref, wi_ref,
     wo_ref) = refs[0:13]
    o_ref, a_ref = refs[13 + ncast], refs[14 + 2 * ncast]
    for src, dst in zip(refs[13:13 + ncast], refs[14 + ncast:14 + 2 * ncast]):
        dst[...] = src[...].astype(dst.dtype)
    d = D_MODEL
    mod = lambda j: mod_ref[:, j * d:(j + 1) * d]
    a0, a1, a2 = ATT_WIDTH, ATT_WIDTH + FNET_WIDTH, ATT_WIDTH + FNET_WIDTH + GLA_WIDTH
    sub = min(x_ref.shape[0], TAIL_SUB)
    subs = [slice(sub * j, sub * (j + 1)) for j in range(x_ref.shape[0] // sub)]

    def mixed(rs):
        o = of_ref[rs].astype(F32) + ob_ref[rs].astype(F32)
        y = o * lax.rsqrt(_group_mean(o, bd_ref) + EPS) * gg_ref[...]
        y = y * _silu(r_ref[rs].astype(F32))
        mix = (_dot(att_ref[rs], wmix_ref[0:a0]) + _dot(four_ref[rs].astype(BF16), wmix_ref[a0:a1])
               + _dot(y.astype(BF16), wmix_ref[a1:a2]))
        x = x_ref[rs] + mod(2) * mix
        ms = jnp.mean(x * x, axis=-1, keepdims=True)
        return x, ((x * lax.rsqrt(ms + EPS) * g2_ref[...]) * (1.0 + mod(4)) + mod(3)).astype(BF16)

    pre = [mixed(rs) for rs in subs]
    for rs, (x, hb) in zip(subs, pre):
        for c0 in range(0, FFN_HIDDEN, FFN_CHUNK):
            g = _dot(hb, wi_ref[:, c0:c0 + FFN_CHUNK])
            u = _dot(hb, wi_ref[:, FFN_HIDDEN + c0:FFN_HIDDEN + c0 + FFN_CHUNK])
            a_ref[rs, c0:c0 + FFN_CHUNK] = (_silu(g) * u).astype(BF16)
        o_ref[rs] = x + mod(5) * _dot(a_ref[rs], wo_ref[...])


def _tail(att, four, of, ob, g4, x, mod3, mod_row, lw, tm, cast_next=()):
    b, n, d = x.shape
    steps = n // tm
    row = lambda bi: MOD_ROWS * lw["layer"] + (bi if mod_row is None else mod_row)
    tok = lambda w: pl.BlockSpec((None, tm, w), lambda bi, i: (bi, i, 0))
    gate = pl.BlockSpec((None, tm, GLA_WIDTH), lambda bi, i: (bi, i, 3))
    small = lambda shape: _layer_block(shape, lw["layer"])
    big = lambda shape: _layer_block(shape, lw["big_layer"])
    cast_in, cast_out, cast_shapes = _slab_cast_specs(cast_next, lw["layer"] + 1, (b, steps))
    res = pl.pallas_call(
        functools.partial(_tail_kernel, ncast=len(cast_next)),
        out_shape=[jax.ShapeDtypeStruct((b, n, d), F32)] + cast_shapes,
        grid=(b, steps),
        in_specs=[tok(ATT_WIDTH), tok(FNET_WIDTH), tok(GLA_WIDTH), tok(GLA_WIDTH), gate, tok(d),
                  pl.BlockSpec((None, 1, 6 * d), lambda bi, i: (row(bi), 0, 0)),
                  small((1, GLA_WIDTH)), _full((GLA_WIDTH, GLA_WIDTH)), big((d, d)),
                  small((1, d)), big((d, 2 * FFN_HIDDEN)), big((FFN_HIDDEN, d))] + cast_in,
        out_specs=[tok(d)] + cast_out,
        scratch_shapes=[pltpu.VMEM((tm, FFN_HIDDEN), BF16)],
        compiler_params=_params("arbitrary", "arbitrary", vmem_limit=TAIL_VMEM_LIMIT),
        name="mix_ffn",
    )(att, four, of, ob, g4, x, mod3, lw["gla_g"], lw["bd256"], lw["w_out"], lw["g2"], lw["w_ffn_in"],
      lw["w_ffn_out"], *cast_next)
    return res[0], res[1:]


def _rope_tables(n):
    axis_dim = HEAD_DIM // 2
    inv_freq = ROPE_BASE ** (-np.arange(0, axis_dim, 2, dtype=np.float64) / axis_dim)
    t = np.arange(n)
    ang_r = (t // GRID_W)[:, None] * inv_freq[None, :]
    ang_c = (t % GRID_W)[:, None] * inv_freq[None, :]
    cos = np.concatenate([np.cos(ang_r)] * 2 + [np.cos(ang_c)] * 2, axis=1)
    sin = np.concatenate([-np.sin(ang_r), np.sin(ang_r), -np.sin(ang_c), np.sin(ang_c)], axis=1)
    return jnp.asarray(np.tile(cos, (1, 2)), F32), jnp.asarray(np.tile(sin, (1, 2)), F32)


def _block_diag_mean(width):
    g = np.arange(width) // HEAD_DIM
    return jnp.asarray((g[:, None] == g[None, :]) / HEAD_DIM, BF16)


def _prepare_weights(w_in, g_norm1, q_norm_g, k_norm_g, w_fourier, wgf, bgf, wgb, bgb, gla_norm_g, g_norm2):
    depth = w_in.shape[0]
    r = GLA_GATE_RANK
    row = lambda t: t[:, None, :]
    zr = jnp.zeros((depth, r, GLA_WIDTH), F32)
    w_gate = jnp.concatenate([jnp.concatenate([wgf, zr], axis=2), jnp.concatenate([zr, wgb], axis=2)], axis=1)
    eye = jnp.eye(FNET_GROUPS, dtype=F32)
    wf_bd = jnp.einsum("lgij,gh->lgihj", w_fourier, eye).reshape(depth, FNET_WIDTH, FNET_WIDTH)
    return {
        "g1": row(g_norm1),
        "w_main": w_in.astype(BF16),
        "w_z": jnp.pad(w_in[:, :, MAIN_WIDTH:], ((0, 0), (0, 0), (0, LANES - 2 * r))).astype(BF16),
        "w_gate": jnp.pad(w_gate, ((0, 0), (0, LANES - 2 * r), (0, 0))).astype(BF16),
        "b_gate": row(jnp.concatenate([bgf, bgb], axis=1)),
        "q_g": row(jnp.tile(q_norm_g, (1, 2))),
        "k_g": row(jnp.tile(k_norm_g, (1, 2))),
        "bd256": _block_diag_mean(GLA_WIDTH),
        "wf_bd": wf_bd.astype(BF16),
        "gla_g": row(jnp.tile(gla_norm_g, (1, GLA_HEADS))),
        "g2": row(g_norm2),
    }


def kernel(x, c, ctx, c_ctx, w_mod, b_mod, g_norm1, w_in, q_norm_g, k_norm_g, attn_sink, w_fourier, gla_w_gate_f,
           gla_b_gate_f, gla_w_gate_b, gla_b_gate_b, gla_norm_g, w_out, g_norm2, w_ffn_in, w_ffn_out):
    b, n, d = x.shape
    lc = ctx.shape[1]
    depth = w_mod.shape[0]
    assert d == D_MODEL and b < MOD_ROWS and n % TOKEN_BLOCK == 0 and n % (DFT_N1 * DFT_STEP) == 0 and lc % GLA_CHUNK == 0

    cc = jnp.concatenate([c, c_ctx[None, :], jnp.zeros((MOD_ROWS - b - 1, d), F32)], axis=0)
    mod3 = _modulation(cc, w_mod, b_mod).reshape(depth * MOD_ROWS, 1, 6 * d)
    rope_tabs = _rope_tables(n)
    dft = _dft_consts(n)
    dft_ctx = _dft_ctx_consts(lc)
    flat = lambda t: t.reshape(1, b * lc, t.shape[-1])
    unflat = lambda t: t.reshape(b, lc, t.shape[-1])
    xc = ctx
    weights = _prepare_weights(w_in, g_norm1, q_norm_g, k_norm_g, w_fourier, gla_w_gate_f, gla_b_gate_f,
                               gla_w_gate_b, gla_b_gate_b, gla_norm_g, g_norm2)
    s_zero = jnp.zeros((b, 2, GLA_WIDTH, LANES), F32)
    tail_names, tail_f32, tail_bf16 = ("w_out", "w_ffn_in", "w_ffn_out"), (w_out, w_ffn_in, w_ffn_out), ()
    for l in range(depth):
        need_ctx = l < depth - 1
        lw = dict(weights, layer=l, big_layer=0)
        sink = attn_sink[l]
        cq, ckv4, cg4, clf = map(unflat, _inproj(flat(xc), mod3, b, lw, None, b * lc))
        q, kv4, g4, lf = _inproj(x, mod3, None, lw, rope_tabs, TOKEN_BLOCK)
        att = _attention(q, kv4, ckv4, sink)
        four = _fourier(lf, lw["wf_bd"], l, dft)
        ocf, ocb, s_ctx, _ = _gla(cg4, clf, s_zero, lc)
        olf, olb, _, cast_now = _gla(g4, lf, s_ctx, TOKEN_BLOCK, () if tail_bf16 else tail_f32, l)
        lw.update(zip(tail_names, tail_bf16 or cast_now))
        x, tail_bf16 = _tail(att, four, olf, olb, g4, x, mod3, None, lw, TOKEN_BLOCK, tail_f32 if need_ctx else ())
        if need_ctx:
            att_c = _attention_ctx(cq, ckv4, sink)
            four_c = _fourier_ctx(clf, lw["wf_bd"], l, dft_ctx)
            xc = unflat(_tail(*map(flat, (att_c, four_c, ocf, ocb, cg4, xc)), mod3, b, lw, b * lc)[0])
    return x
```

```python
import functools

import numpy as np
import jax
import jax.numpy as jnp
from jax import lax
from jax.experimental import pallas as pl
from jax.experimental.pallas import tpu as pltpu

F32 = jnp.float32
BF16 = jnp.bfloat16

D_MODEL = 1024
HEAD_DIM = 64
GRID_W = 64
ROPE_BASE = 10000.0
ATT_HEADS = 8
ATT_KV_HEADS = 2
ATT_WIDTH = ATT_HEADS * HEAD_DIM
KV_WIDTH = ATT_KV_HEADS * HEAD_DIM
ATT_BLOCK = 128
ATT_QBLOCKS = 8
ATT_SCALE = HEAD_DIM ** -0.5
LOG2E = 1.4426950408889634
NEG_INF = -1e30
FNET_GROUPS = 4
FNET_WIDTH = FNET_GROUPS * HEAD_DIM
GLA_HEADS = 4
GLA_WIDTH = GLA_HEADS * HEAD_DIM
GLA_GATE_RANK = 16
GLA_TAU = 16.0
GLA_CHUNK = 64
GLA_SCALE = HEAD_DIM ** -0.5
MAIN_WIDTH = ATT_WIDTH + 2 * KV_WIDTH + FNET_WIDTH + 4 * GLA_WIDTH
FFN_HIDDEN = 2816
FFN_CHUNK = 256
INPROJ_SUB = 512
TAIL_SUB = 512
FU_BLOCK = 2
GLA_GROUP = 16
TOKEN_BLOCK = 1024
EPS = 1e-6
LANES = 128
MOD_ROWS = 8
DFT_N1 = 128
DFT_STEP = 16
VMEM_LIMIT = 56 * 1024 * 1024
TAIL_VMEM_LIMIT = 61 * 1024 * 1024

NT_DIMS = (((1,), (1,)), ((), ()))
TN_DIMS = (((0,), (0,)), ((), ()))


def _params(*sem, vmem_limit=VMEM_LIMIT):
    return pltpu.CompilerParams(dimension_semantics=sem, vmem_limit_bytes=vmem_limit)


def _dot(a, b):
    return jnp.dot(a, b, preferred_element_type=F32)


def _silu(x):
    return x / (1.0 + jnp.exp(-x))


def _layer_block(shape, layer):
    nd = len(shape)
    return pl.BlockSpec((None,) + tuple(shape), lambda *_: (layer,) + (0,) * nd, pipeline_mode=pl.Buffered(1))


def _slab_cast_specs(weights, layer, grid):
    steps = grid[0] * grid[1]
    assert all(w.shape[1] % (16 * steps) == 0 for w in weights)
    spec = lambda w, lyr: pl.BlockSpec((None, w.shape[1] // steps, w.shape[2]),
                                       lambda i, j: (lyr, i * grid[1] + j, 0))
    return ([spec(w, layer) for w in weights], [spec(w, 0) for w in weights],
            [jax.ShapeDtypeStruct((1,) + w.shape[1:], BF16) for w in weights])


def _full(shape):
    nd = len(shape)
    return pl.BlockSpec(shape, lambda *_: (0,) * nd)


def _group_mean(t, bd_ref):
    return _dot((t * t).astype(BF16), bd_ref[...])


def _mod_kernel(c_ref, w_ref, b_ref, o_ref):
    s = _silu(c_ref[...]).astype(BF16)
    o_ref[...] = _dot(s, w_ref[...].astype(BF16)) + b_ref[pl.ds(pl.program_id(0), 1), :]


def _modulation(cc, w_mod, b_mod):
    depth, d, width = w_mod.shape
    tn = 1536
    return pl.pallas_call(
        _mod_kernel,
        out_shape=jax.ShapeDtypeStruct((depth, MOD_ROWS, width), F32),
        grid=(depth, width // tn),
        in_specs=[
            _full((MOD_ROWS, d)),
            pl.BlockSpec((None, d, tn), lambda l, j: (l, 0, j)),
            pl.BlockSpec((depth, tn), lambda l, j: (0, j)),
        ],
        out_specs=pl.BlockSpec((None, MOD_ROWS, tn), lambda l, j: (l, 0, j)),
        compiler_params=_params("parallel", "parallel"),
        name="modulation",
    )(cc, w_mod, b_mod)


def _inproj_kernel(*refs, rope, layer, mod_row):
    if rope:
        (x_ref, mod_ref, g1_ref, wm_ref, wz_ref, wg_ref, bg_ref, qg_ref, kg_ref, bd_ref, cos_ref, sin_ref,
         q_ref, kv4_ref, g4_ref, lf_ref) = refs
    else:
        (x_ref, mod_ref, g1_ref, wm_ref, wz_ref, wg_ref, bg_ref, qg_ref, kg_ref, bd_ref,
         q_ref, kv4_ref, g4_ref, lf_ref) = refs
    d = D_MODEL
    pick = lambda ref: ref[layer:layer + 1, :]
    g1, bg, qg, kg = pick(g1_ref), pick(bg_ref), pick(qg_ref), pick(kg_ref)
    mod = mod_ref[pl.ds(pl.program_id(0) if mod_row is None else mod_row, 1), :]
    sh, sc = mod[:, 0:d], mod[:, d:2 * d]
    sub = min(x_ref.shape[0], INPROJ_SUB)
    subs = [slice(sub * j, sub * (j + 1)) for j in range(x_ref.shape[0] // sub)]

    def normed(rs):
        x = x_ref[rs]
        ms = jnp.mean(x * x, axis=-1, keepdims=True)
        return ((x * lax.rsqrt(ms + EPS) * g1) * (1.0 + sc) + sh).astype(BF16)

    lane = lax.broadcasted_iota(jnp.int32, (1, LANES), 1)
    lo = lane < HEAD_DIM
    second_half = (lane & 16) != 0

    def head_norm(t, g128):
        wdt = t.shape[1]
        ms = _dot((t * t).astype(BF16), bd_ref[0:wdt, 0:wdt])
        g = g128 if wdt == LANES else jnp.concatenate([g128] * (wdt // LANES), axis=1)
        return t * lax.rsqrt(ms + EPS) * g

    def rotary(t, rs):
        if not rope:
            return t
        partner = jnp.where(second_half, pltpu.roll(t, 16, 1), pltpu.roll(t, LANES - 16, 1))
        return t * cos_ref[rs] + partner * sin_ref[rs]

    def spread(t, c0, rs, idle):
        tr = pltpu.roll(t, HEAD_DIM, 1)
        fill = jnp.full_like(t, idle)
        tiles = (jnp.where(lo, t, fill), jnp.where(lo, fill, tr), jnp.where(lo, tr, fill), jnp.where(lo, fill, t))
        for j, tile in enumerate(tiles):
            kv4_ref[rs, c0 + LANES * j:c0 + LANES * (j + 1)] = tile.astype(kv4_ref.dtype)

    assert (ATT_WIDTH, 2 * KV_WIDTH + FNET_WIDTH, 2 * GLA_WIDTH) == (4 * LANES,) * 3

    def project(rs, hb):
        group = lambda g: _dot(hb, wm_ref[:, 4 * LANES * g:4 * LANES * (g + 1)])
        gz = _dot(hb, wz_ref[...])
        a_q = group(0)
        z = _dot(gz.astype(BF16), wg_ref[...]) + bg
        log_sig = jnp.minimum(z, 0.0) - jnp.log(1.0 + jnp.exp(-jnp.abs(z)))
        lf_ref[rs, 0:2 * GLA_WIDTH] = log_sig * (LOG2E / GLA_TAU)
        a_kvf = group(1)
        for j2 in range(ATT_WIDTH // (2 * LANES)):
            qn = head_norm(a_q[:, 2 * LANES * j2:2 * LANES * (j2 + 1)], qg)
            for j in range(2):
                t = rotary(qn[:, LANES * j:LANES * (j + 1)], rs) * (ATT_SCALE * LOG2E)
                q_ref[rs, LANES * (2 * j2 + j):LANES * (2 * j2 + j + 1)] = t.astype(q_ref.dtype)
        a_qk = group(2)
        spread(rotary(head_norm(a_kvf[:, 0:KV_WIDTH], kg), rs), 0, rs, 0.0)
        spread(a_kvf[:, KV_WIDTH:2 * KV_WIDTH], 4 * LANES, rs, 1.0)
        lf_ref[rs, 2 * GLA_WIDTH:] = a_kvf[:, 2 * KV_WIDTH:]
        a_vr = group(3)
        g4_ref[rs, 0:GLA_WIDTH] = (a_qk[:, 0:GLA_WIDTH] * GLA_SCALE).astype(g4_ref.dtype)
        g4_ref[rs, GLA_WIDTH:2 * GLA_WIDTH] = a_qk[:, GLA_WIDTH:].astype(g4_ref.dtype)
        g4_ref[rs, 2 * GLA_WIDTH:4 * GLA_WIDTH] = a_vr.astype(g4_ref.dtype)

    hbs = [normed(rs) for rs in subs]
    for rs, hb in zip(subs, hbs):
        project(rs, hb)


def _inproj(x, mod3, mod_row, lw, rope_tabs, tm):
    b, n, d = x.shape
    rope = rope_tabs is not None
    tok = lambda w: pl.BlockSpec((None, tm, w), lambda bi, i: (bi, i, 0))
    lb = lambda shape: _layer_block(shape, lw["layer"])
    vec = lambda name: _full(lw[name].shape)
    in_specs = [
        tok(d),
        lb((MOD_ROWS, 6 * d)),
        vec("g1"), lb((d, MAIN_WIDTH)), lb((d, LANES)), lb((LANES, 2 * GLA_WIDTH)),
        vec("b_gate"), vec("q_g"), vec("k_g"), _full((2 * LANES, 2 * LANES)),
    ]
    args = [x, mod3, lw["g1"], lw["w_main"], lw["w_z"], lw["w_gate"], lw["b_gate"], lw["q_g"], lw["k_g"], lw["bd256"]]
    if rope:
        in_specs += [pl.BlockSpec((tm, LANES), lambda bi, i: (i, 0))] * 2
        args += list(rope_tabs)
    sds = lambda w, dt: jax.ShapeDtypeStruct((b, n, w), dt)
    out_shape = [sds(ATT_WIDTH, BF16), sds(8 * LANES, BF16), sds(4 * GLA_WIDTH, BF16),
                 sds(2 * GLA_WIDTH + FNET_WIDTH, F32)]
    out_specs = [tok(s.shape[-1]) for s in out_shape]
    return pl.pallas_call(
        functools.partial(_inproj_kernel, rope=rope, layer=lw["layer"], mod_row=mod_row),
        out_shape=out_shape, grid=(b, n // tm), in_specs=in_specs, out_specs=out_specs,
        compiler_params=_params("parallel", "parallel"),
        name="inproj_rope" if rope else "inproj_ctx",
    )(*args)


def _scores(q2, key_tiles, masks):
    cols = []
    for k, tile_masks in zip(key_tiles, masks):
        s = lax.dot_general(q2, k, NT_DIMS, preferred_element_type=F32)
        for j, mk in enumerate(tile_masks):
            c = s[:, LANES * j:LANES * (j + 1)]
            cols.append(c if mk is None else jnp.where(mk, c, NEG_INF))
    return cols


def _softmax_pv(cols, val_tiles, sink_col):
    mx = cols[0]
    for c in cols[1:]:
        mx = jnp.maximum(mx, c)
    m = jnp.maximum(jnp.max(mx, axis=-1, keepdims=True), sink_col)
    p = jnp.concatenate([jnp.exp2(c - m).astype(BF16) for c in cols], axis=1)
    pv = _dot(p, jnp.concatenate(val_tiles, axis=0))
    den = pltpu.roll(pv, HEAD_DIM, 1) + jnp.exp2(sink_col - m)
    return pv / den


def _attn_kernel(*refs):
    nq = ATT_QBLOCKS
    sink_ref, q_ref = refs[0:2]
    kv_refs, kvc_ref, o_ref = refs[2:nq + 4], refs[nq + 4], refs[nq + 5]
    v_off = 4 * LANES
    i = pl.program_id(1)
    last = pl.num_programs(1) - 1
    blk = ATT_BLOCK
    group = ATT_HEADS // ATT_KV_HEADS
    row = lax.broadcasted_iota(jnp.int32, (2 * blk, blk), 0) & (blk - 1)
    col = lax.broadcasted_iota(jnp.int32, (2 * blk, blk), 1)
    band_l, band_r = col >= row, col <= row
    mask_l = [jnp.logical_and(band_l, i > 0) if s == 0 else band_l for s in range(nq)]
    mask_r = [jnp.logical_and(band_r, i < last) if s == nq - 1 else band_r for s in range(nq)]
    upper_rows = lax.broadcasted_iota(jnp.int32, (2 * blk, 1), 0) >= blk
    lo = lax.broadcasted_iota(jnp.int32, (1, LANES), 1) < HEAD_DIM

    def tiles(s, ks):
        l, m_, r = kv_refs[s:s + 3]
        return [jnp.concatenate([l[:, ks], m_[:, ks]], axis=0),
                jnp.concatenate([r[:, ks], kvc_ref[0:blk, ks]], axis=0), kvc_ref[blk:, ks]]

    slots = [(s, kvh, par) for s in range(nq) for kvh in range(ATT_KV_HEADS) for par in range(2)]
    lane_slice = lambda kvh, par, off=0: slice(off + 2 * LANES * kvh + LANES * par,
                                               off + 2 * LANES * kvh + LANES * (par + 1))
    cols = {}
    for s, kvh, par in slots:
        qs, rs = 2 * LANES * kvh, slice(blk * s, blk * (s + 1))
        q2 = jnp.concatenate([q_ref[rs, qs:qs + LANES], q_ref[rs, qs + LANES:qs + 2 * LANES]], axis=0)
        cols[s, kvh, par] = _scores(q2, tiles(s, lane_slice(kvh, par)),
                                    [(mask_l[s], None), (mask_r[s], None), (None,)])
    outs = {}
    for s, kvh, par in slots:
        base = group * kvh + par
        sink_col = jnp.where(upper_rows, sink_ref[base + 2], sink_ref[base]) * LOG2E
        outs[s, kvh, par] = _softmax_pv(cols[s, kvh, par], tiles(s, lane_slice(kvh, par, v_off)), sink_col)
    for s in range(nq):
        for kvh in range(ATT_KV_HEADS):
            qs, r0 = 2 * LANES * kvh, blk * s
            o = jnp.where(lo, outs[s, kvh, 0], outs[s, kvh, 1])
            o_ref[r0:r0 + blk, qs:qs + LANES] = o[0:blk].astype(o_ref.dtype)
            o_ref[r0:r0 + blk, qs + LANES:qs + 2 * LANES] = o[blk:2 * blk].astype(o_ref.dtype)


def _attention(q, kv4, ckv4, sink):
    b, n, _ = q.shape
    lc = ckv4.shape[1]
    nq = ATT_QBLOCKS
    nb = n // ATT_BLOCK
    assert lc == 2 * ATT_BLOCK and nb % nq == 0
    qspec = pl.BlockSpec((None, nq * ATT_BLOCK, 4 * LANES), lambda bi, i: (bi, i, 0))
    kv = [pl.BlockSpec((None, ATT_BLOCK, 8 * LANES),
                       functools.partial(lambda bi, i, j: (bi, jnp.clip(nq * i + j - 1, 0, nb - 1), 0), j=j))
          for j in range(nq + 2)]
    ctx = pl.BlockSpec((None, lc, 8 * LANES), lambda bi, i: (bi, 0, 0))
    return pl.pallas_call(
        _attn_kernel,
        out_shape=jax.ShapeDtypeStruct((b, n, ATT_WIDTH), BF16),
        grid=(b, nb // nq),
        in_specs=[pl.BlockSpec(memory_space=pltpu.SMEM), qspec] + kv + [ctx],
        out_specs=qspec,
        compiler_params=_params("parallel", "parallel"),
        name="window_attention",
    )(sink, q, *([kv4] * (nq + 2)), ckv4)


def _attn_ctx_kernel(sink_ref, q_ref, kc_ref, vc_ref, o_ref):
    kvh = pl.program_id(1)
    lc = q_ref.shape[0]
    q2 = jnp.concatenate([q_ref[:, 0:LANES], q_ref[:, LANES:2 * LANES]], axis=0)
    upper_rows = lax.broadcasted_iota(jnp.int32, (2 * lc, 1), 0) >= lc
    lo = lax.broadcasted_iota(jnp.int32, (1, LANES), 1) < HEAD_DIM
    outs = []
    for par in range(2):
        ks = slice(LANES * par, LANES * (par + 1))
        base = ATT_HEADS // ATT_KV_HEADS * kvh + par
        sink_col = jnp.where(upper_rows, sink_ref[base + 2], sink_ref[base]) * LOG2E
        cols = _scores(q2, [kc_ref[:, ks]], [(None,) * (lc // LANES)])
        outs.append(_softmax_pv(cols, [vc_ref[:, ks]], sink_col))
    o = jnp.where(lo, outs[0], outs[1])
    o_ref[:, 0:LANES] = o[0:lc].astype(o_ref.dtype)
    o_ref[:, LANES:2 * LANES] = o[lc:2 * lc].astype(o_ref.dtype)


def _attention_ctx(cq, ckv4, sink):
    b, lc, _ = cq.shape
    gw = 2 * LANES
    spec = pl.BlockSpec((None, lc, gw), lambda bi, h: (bi, 0, h))
    vspec = pl.BlockSpec((None, lc, gw), lambda bi, h: (bi, 0, ATT_KV_HEADS + h))
    return pl.pallas_call(
        _attn_ctx_kernel,
        out_shape=jax.ShapeDtypeStruct((b, lc, ATT_WIDTH), BF16),
        grid=(b, ATT_KV_HEADS),
        in_specs=[pl.BlockSpec(memory_space=pltpu.SMEM), spec, spec, vspec],
        out_specs=spec,
        compiler_params=_params("parallel", "parallel"),
        name="context_attention",
    )(sink, cq, ckv4, ckv4)


def _dft_consts(n):
    n1, n2 = DFT_N1, n // DFT_N1
    ang = lambda a, m: 2.0 * np.pi * np.outer(np.arange(a), np.arange(a)) / m
    c = HEAD_DIM
    cc, sc = np.cos(ang(c, c)) / np.sqrt(c), np.sin(ang(c, c)) / np.sqrt(c)
    eye = np.eye(FNET_GROUPS)
    w_chan = np.concatenate([np.kron(eye, cc), -np.kron(eye, sc)], axis=1)
    c1, s1 = np.cos(ang(n1, n1)) / np.sqrt(n1), np.sin(ang(n1, n1)) / np.sqrt(n1)
    m1 = np.block([[c1, s1], [-s1, c1]])
    kk = np.arange(n1)[:, None, None] + n1 * np.arange(n2)[None, :, None]
    ph = 2.0 * np.pi * (kk * np.arange(n2)[None, None, :] % n) / n
    m3 = np.concatenate([np.cos(ph), np.sin(ph)], axis=2) / np.sqrt(n2)
    return tuple(jnp.asarray(a, F32) for a in (w_chan, m1, m3))


def _dft_ctx_consts(lc):
    ang = lambda a, m: 2.0 * np.pi * np.outer(np.arange(a), np.arange(a)) / m
    c = HEAD_DIM
    cc, sc = np.cos(ang(c, c)) / np.sqrt(c), np.sin(ang(c, c)) / np.sqrt(c)
    eye = np.eye(FNET_GROUPS)
    w_chan = np.concatenate([np.kron(eye, cc), -np.kron(eye, sc)], axis=1)
    cl, sl = np.cos(ang(lc, lc)) / np.sqrt(lc), np.sin(ang(lc, lc)) / np.sqrt(lc)
    return jnp.asarray(w_chan, F32), jnp.asarray(np.concatenate([cl, sl], axis=1), F32)


def _regroup_rows(x, outer):
    r, c = x.shape
    return jnp.swapaxes(x.reshape(outer, r // outer, c), 0, 1).reshape(r, c)


def _fourier_kernel(u_ref, wc_ref, m1_ref, m3_ref, wf_ref, y_ref, p_ref):
    w, t = FNET_WIDTH, DFT_STEP
    s = pl.program_id(1)
    n_a = p_ref.shape[1]
    n2 = n_a * t

    @pl.when(s < n_a)
    def _():
        u = u_ref[...].reshape(DFT_N1 * t, w).astype(BF16)
        z = _regroup_rows(_dot(u, wc_ref[...]), DFT_N1)
        a_re, a_im = [], []
        for tt in range(t):
            zt = z[DFT_N1 * tt:DFT_N1 * (tt + 1)]
            zs = jnp.concatenate([zt[:, 0:w], zt[:, w:2 * w]], axis=0).astype(BF16)
            a = _dot(m1_ref[...], zs)
            a_re.append(a[0:DFT_N1])
            a_im.append(a[DFT_N1:2 * DFT_N1])
        for part, rows in enumerate((a_re, a_im)):
            a = _regroup_rows(jnp.concatenate(rows, axis=0), t)
            p_ref[part, s] = a.reshape(DFT_N1, t, w)

    @pl.when(s >= n_a)
    def _():
        k1_0 = (s - n_a) * t
        ys = []
        for kk in range(t):
            parts = [p_ref[part, :, k1_0 + kk].reshape(n2, w) for part in range(2)]
            rhs = jnp.concatenate(parts, axis=0).astype(BF16)
            ys.append(_dot(m3_ref[kk].astype(BF16), rhs).astype(BF16))
        out = _dot(jnp.concatenate(ys, axis=0), wf_ref[...])
        y_ref[...] = _regroup_rows(out, t).reshape(n2, t, w)


def _fourier(lf, wf_bd, layer, consts):
    b, n, wl = lf.shape
    w = FNET_WIDTH
    n1, n2, t = DFT_N1, n // DFT_N1, DFT_STEP
    n_a, n_b = n2 // t, n1 // t
    w_chan, m1, m3 = consts
    y = pl.pallas_call(
        _fourier_kernel,
        out_shape=jax.ShapeDtypeStruct((b, n2, n1, w), F32),
        grid=(b, n_a + n_b),
        in_specs=[pl.BlockSpec((None, n1, t, w), lambda bi, s: (bi, 0, jnp.minimum(s, n_a - 1), FU_BLOCK)),
                  _full((w, 2 * w)), _full((2 * n1, 2 * n1)),
                  pl.BlockSpec((t, n2, 2 * n2), lambda bi, s: (jnp.maximum(s - n_a, 0), 0, 0)),
                  _layer_block((w, w), layer)],
        out_specs=pl.BlockSpec((None, n2, t, w), lambda bi, s: (bi, 0, jnp.maximum(s - n_a, 0), 0)),
        scratch_shapes=[pltpu.VMEM((2, n_a, n1, t, w), F32)],
        compiler_params=_params("parallel", "arbitrary"),
        name="fourier_mix",
    )(lf.reshape(b, n1, n2, wl), w_chan.astype(BF16), m1.astype(BF16), m3, wf_bd)
    return y.reshape(b, n, w)


def _four_ctx_kernel(u_ref, wc_ref, m_ref, wf_ref, y_ref):
    w = FNET_WIDTH
    z = _dot(u_ref[...].astype(BF16), wc_ref[...])
    zs = jnp.concatenate([z[:, 0:w], z[:, w:2 * w]], axis=0).astype(BF16)
    y = _dot(m_ref[...], zs)
    y_ref[...] = _dot(y.astype(BF16), wf_ref[...]).astype(y_ref.dtype)


def _fourier_ctx(clf, wf_bd, layer, consts):
    b, lc, _ = clf.shape
    w = FNET_WIDTH
    w_chan, m = consts[0].astype(BF16), consts[1].astype(BF16)
    return pl.pallas_call(
        _four_ctx_kernel,
        out_shape=jax.ShapeDtypeStruct((b, lc, w), BF16),
        grid=(b,),
        in_specs=[pl.BlockSpec((None, lc, w), lambda bi: (bi, 0, FU_BLOCK)), _full((w, 2 * w)), _full((lc, 2 * lc)),
                  _layer_block((w, w), layer)],
        out_specs=pl.BlockSpec((None, lc, w), lambda bi: (bi, 0, 0)),
        compiler_params=_params("parallel"),
        name="fourier_context",
    )(clf, w_chan, m, wf_bd)


def _pair_stack(x, lo):
    zero = jnp.zeros_like(x)
    return jnp.concatenate([jnp.where(lo, x, zero), jnp.where(lo, zero, x)], axis=0)


def _gla_kernel(*refs, ncast):
    qkvf_ref, laf_ref, qkvb_ref, lab_ref, s0_ref = refs[0:5]
    of_ref, ob_ref, sfin_ref = refs[5 + ncast:8 + ncast]
    stf_ref, stb_ref = refs[8 + 2 * ncast:10 + 2 * ncast]
    for src, dst in zip(refs[5:5 + ncast], refs[8 + ncast:8 + 2 * ncast]):
        dst[...] = src[...].astype(dst.dtype)
    i = pl.program_id(1)
    c = GLA_CHUNK
    nchunk = qkvf_ref.shape[0] // c

    @pl.when(i == 0)
    def _():
        stf_ref[...] = s0_ref[0]
        stb_ref[...] = s0_ref[1]

    r64 = lax.broadcasted_iota(jnp.int32, (c, c), 0)
    c64 = lax.broadcasted_iota(jnp.int32, (c, c), 1)
    tri = ((c64 <= r64).astype(BF16), (c64 >= r64).astype(BF16))
    at = lax.broadcasted_iota(jnp.int32, (c, LANES), 0)
    as_ = lax.broadcasted_iota(jnp.int32, (c, LANES), 1) & (c - 1)
    att_mask = (as_ <= at, as_ >= at)
    lo = lax.broadcasted_iota(jnp.int32, (1, LANES), 1) < HEAD_DIM
    br = lax.broadcasted_iota(jnp.int32, (LANES, LANES), 0) < HEAD_DIM
    bc = lax.broadcasted_iota(jnp.int32, (LANES, LANES), 1) < HEAD_DIM
    bd_mask = br == bc
    pairs = [slice(LANES * p, LANES * (p + 1)) for p in range(GLA_WIDTH // LANES)]
    in_refs = ((qkvf_ref, laf_ref), (qkvb_ref, lab_ref))
    out_refs = (of_ref, ob_ref)
    items = [(d, step if d == 0 else nchunk - 1 - step) for step in range(nchunk) for d in range(2)]
    rows = lambda ch: slice(c * ch, c * (ch + 1))

    bcum, work, o_intra, ut, decay = {}, {}, {}, {}, {}
    st_refs = (stf_ref, stb_ref)
    st = [[st_refs[d][LANES * j:LANES * (j + 1)] for j in range(len(pairs))] for d in range(2)]

    def gate_sums(d, ch):
        la = in_refs[d][1][rows(ch)]
        la_hi = la.astype(BF16)
        la_lo = (la - la_hi.astype(F32)).astype(BF16)
        bcum[d, ch] = _dot(jnp.concatenate([tri[d], tri[d]], axis=1), jnp.concatenate([la_hi, la_lo], axis=0))

    def scores(d, ch):
        qkv_ref = in_refs[d][0]
        w = GLA_WIDTH
        b = bcum[d, ch]
        btot = b[0:1] if d == 1 else b[c - 1:c]
        k = qkv_ref[rows(ch), w:2 * w].astype(F32)
        q_in = (qkv_ref[rows(ch), 0:w].astype(F32) * jnp.exp2(b)).astype(BF16)
        k_in = (k * jnp.exp2(-b)).astype(BF16)
        k_out = (k * jnp.exp2(btot - b)).astype(BF16)
        vb = qkv_ref[rows(ch), 2 * w:3 * w]
        att = [lax.dot_general(q_in[:, p], _pair_stack(k_in[:, p], lo), NT_DIMS, preferred_element_type=F32)
               for p in pairs]
        work[d, ch] = (btot, q_in, k_out, vb, att)

    def intra(d, ch):
        btot, q_in, k_out, vb, att = work[d, ch]
        o_intra[d, ch] = [_dot(jnp.where(att_mask[d], a, 0.0).astype(BF16), _pair_stack(vb[:, p], lo))
                          for a, p in zip(att, pairs)]
        ut[d, ch] = [lax.dot_general(k_out[:, p], vb[:, p], TN_DIMS, preferred_element_type=F32) for p in pairs]
        decay[d, ch] = [jnp.transpose(jnp.exp2(btot[:, p])) for p in pairs]

    def carry(d, ch):
        btot, q_in = work[d, ch][0:2]
        outs = []
        for j, p in enumerate(pairs):
            outs.append(o_intra[d, ch][j] + _dot(q_in[:, p], st[d][j].astype(BF16)))
            st[d][j] = st[d][j] * decay[d, ch][j] + jnp.where(bd_mask, ut[d, ch][j], 0.0)
        out_refs[d][rows(ch)] = jnp.concatenate(outs, axis=1).astype(out_refs[d].dtype)

    groups = [items[g:g + GLA_GROUP] for g in range(0, len(items), GLA_GROUP)]
    for it in groups[0]:
        gate_sums(*it)
    for it in groups[0]:
        scores(*it)
    for g, grp in enumerate(groups):
        for it in grp:
            intra(*it)
        nxt = groups[g + 1] if g + 1 < len(groups) else []
        for it in nxt:
            gate_sums(*it)
        for k_ in range(len(grp)):
            carry(*grp[k_])
            if k_ < len(nxt):
                scores(*nxt[k_])
    for d in range(2):
        st_refs[d][...] = jnp.concatenate(st[d], axis=0)

    @pl.when(i == pl.num_programs(1) - 1)
    def _():
        for d in range(2):
            sfin_ref[d] = jnp.concatenate(st[d], axis=0)


def _gla(g4, lf, s0, tb, cast=(), cast_layer=0):
    b, n, _ = g4.shape
    w = GLA_WIDTH
    nblk = n // tb
    fwd = lambda bi, i: (bi, i, 0)
    bwd = lambda bi, i: (bi, nblk - 1 - i, 0)
    bwd_la = lambda bi, i: (bi, nblk - 1 - i, 1)
    tok = lambda f: pl.BlockSpec((None, tb, w), f)
    qkv = lambda f: pl.BlockSpec((None, tb, 3 * w), f)
    state = pl.BlockSpec((None, 2, w, LANES), lambda bi, i: (bi, 0, 0, 0))
    cast_in, cast_out, cast_shapes = _slab_cast_specs(cast, cast_layer, (b, nblk))
    res = pl.pallas_call(
        functools.partial(_gla_kernel, ncast=len(cast)),
        out_shape=[jax.ShapeDtypeStruct((b, n, w), BF16), jax.ShapeDtypeStruct((b, n, w), BF16),
                   jax.ShapeDtypeStruct((b, 2, w, LANES), F32)] + cast_shapes,
        grid=(b, nblk),
        in_specs=[qkv(fwd), tok(fwd), qkv(bwd), tok(bwd_la), state] + cast_in,
        out_specs=[tok(fwd), tok(bwd), state] + cast_out,
        scratch_shapes=[pltpu.VMEM((w, LANES), F32), pltpu.VMEM((w, LANES), F32)],
        compiler_params=_params("arbitrary", "arbitrary"),
        name="gla_scan",
    )(g4, lf, g4, lf, s0, *cast)
    return res[0], res[1], res[2], res[3:]


def _tail_kernel(*refs, ncast, layer, mod_row):
    (att_ref, four_ref, of_ref, ob_ref, r_ref, x_ref, mod_ref, gg_ref, bd_ref, wmix_ref, g2_ref, wi_ref,
     wo_ref) = refs[0:13]
    o_ref, a_ref = refs[13 + ncast], refs[14 + 2 * ncast]
    for src, dst in zip(refs[13:13 + ncast], refs[14 + ncast:14 + 2 * ncast]):
        dst[...] = src[...].astype(dst.dtype)
    d = D_MODEL
    gla_g, g2 = gg_ref[layer:layer + 1, :], g2_ref[layer:layer + 1, :]
    mod_vec = mod_ref[pl.ds(pl.program_id(0) if mod_row is None else mod_row, 1), :]
    mod = lambda j: mod_vec[:, j * d:(j + 1) * d]
    a0, a1, a2 = ATT_WIDTH, ATT_WIDTH + FNET_WIDTH, ATT_WIDTH + FNET_WIDTH + GLA_WIDTH
    sub = min(x_ref.shape[0], TAIL_SUB)
    subs = [slice(sub * j, sub * (j + 1)) for j in range(x_ref.shape[0] // sub)]

    def mixed(rs):
        o = of_ref[rs].astype(F32) + ob_ref[rs].astype(F32)
        y = o * lax.rsqrt(_group_mean(o, bd_ref) + EPS) * gla_g
        y = y * _silu(r_ref[rs].astype(F32))
        mix = (_dot(att_ref[rs], wmix_ref[0:a0]) + _dot(four_ref[rs].astype(BF16), wmix_ref[a0:a1])
               + _dot(y.astype(BF16), wmix_ref[a1:a2]))
        x = x_ref[rs] + mod(2) * mix
        ms = jnp.mean(x * x, axis=-1, keepdims=True)
        return x, ((x * lax.rsqrt(ms + EPS) * g2) * (1.0 + mod(4)) + mod(3)).astype(BF16)

    pre = [mixed(rs) for rs in subs]
    for rs, (x, hb) in zip(subs, pre):
        for c0 in range(0, FFN_HIDDEN, FFN_CHUNK):
            g = _dot(hb, wi_ref[:, c0:c0 + FFN_CHUNK])
            u = _dot(hb, wi_ref[:, FFN_HIDDEN + c0:FFN_HIDDEN + c0 + FFN_CHUNK])
            a_ref[rs, c0:c0 + FFN_CHUNK] = (_silu(g) * u).astype(BF16)
        o_ref[rs] = x + mod(5) * _dot(a_ref[rs], wo_ref[...])


def _tail(att, four, of, ob, g4, x, mod3, mod_row, lw, tm, cast_next=()):
    b, n, d = x.shape
    steps = n // tm
    tok = lambda w: pl.BlockSpec((None, tm, w), lambda bi, i: (bi, i, 0))
    gate = pl.BlockSpec((None, tm, GLA_WIDTH), lambda bi, i: (bi, i, 3))
    vec = lambda name: _full(lw[name].shape)
    big = lambda shape: _layer_block(shape, lw["big_layer"])
    cast_in, cast_out, cast_shapes = _slab_cast_specs(cast_next, lw["layer"] + 1, (b, steps))
    res = pl.pallas_call(
        functools.partial(_tail_kernel, ncast=len(cast_next), layer=lw["layer"], mod_row=mod_row),
        out_shape=[jax.ShapeDtypeStruct((b, n, d), F32)] + cast_shapes,
        grid=(b, steps),
        in_specs=[tok(ATT_WIDTH), tok(FNET_WIDTH), tok(GLA_WIDTH), tok(GLA_WIDTH), gate, tok(d),
                  _layer_block((MOD_ROWS, 6 * d), lw["layer"]),
                  vec("gla_g"), _full((GLA_WIDTH, GLA_WIDTH)), big((d, d)),
                  vec("g2"), big((d, 2 * FFN_HIDDEN)), big((FFN_HIDDEN, d))] + cast_in,
        out_specs=[tok(d)] + cast_out,
        scratch_shapes=[pltpu.VMEM((tm, FFN_HIDDEN), BF16)],
        compiler_params=_params("arbitrary", "arbitrary", vmem_limit=TAIL_VMEM_LIMIT),
        name="mix_ffn",
    )(att, four, of, ob, g4, x, mod3, lw["gla_g"], lw["bd256"], lw["w_out"], lw["g2"], lw["w_ffn_in"],
      lw["w_ffn_out"], *cast_next)
    return res[0], res[1:]


def _rope_tables(n):
    axis_dim = HEAD_DIM // 2
    inv_freq = ROPE_BASE ** (-np.arange(0, axis_dim, 2, dtype=np.float64) / axis_dim)
    t = np.arange(n)
    ang_r = (t // GRID_W)[:, None] * inv_freq[None, :]
    ang_c = (t % GRID_W)[:, None] * inv_freq[None, :]
    cos = np.concatenate([np.cos(ang_r)] * 2 + [np.cos(ang_c)] * 2, axis=1)
    sin = np.concatenate([-np.sin(ang_r), np.sin(ang_r), -np.sin(ang_c), np.sin(ang_c)], axis=1)
    return jnp.asarray(np.tile(cos, (1, 2)), F32), jnp.asarray(np.tile(sin, (1, 2)), F32)


def _block_diag_mean(width):
    g = np.arange(width) // HEAD_DIM
    return jnp.asarray((g[:, None] == g[None, :]) / HEAD_DIM, BF16)


def _prepare_weights(w_in, g_norm1, q_norm_g, k_norm_g, w_fourier, wgf, bgf, wgb, bgb, gla_norm_g, g_norm2):
    depth = w_in.shape[0]
    r = GLA_GATE_RANK
    zr = jnp.zeros((depth, r, GLA_WIDTH), F32)
    w_gate = jnp.concatenate([jnp.concatenate([wgf, zr], axis=2), jnp.concatenate([zr, wgb], axis=2)], axis=1)
    eye = jnp.eye(FNET_GROUPS, dtype=F32)
    wf_bd = jnp.einsum("lgij,gh->lgihj", w_fourier, eye).reshape(depth, FNET_WIDTH, FNET_WIDTH)
    return {
        "g1": g_norm1,
        "w_main": w_in.astype(BF16),
        "w_z": jnp.pad(w_in[:, :, MAIN_WIDTH:], ((0, 0), (0, 0), (0, LANES - 2 * r))).astype(BF16),
        "w_gate": jnp.pad(w_gate, ((0, 0), (0, LANES - 2 * r), (0, 0))).astype(BF16),
        "b_gate": jnp.concatenate([bgf, bgb], axis=1),
        "q_g": jnp.tile(q_norm_g, (1, 2)),
        "k_g": jnp.tile(k_norm_g, (1, 2)),
        "bd256": _block_diag_mean(GLA_WIDTH),
        "wf_bd": wf_bd.astype(BF16),
        "gla_g": jnp.tile(gla_norm_g, (1, GLA_HEADS)),
        "g2": g_norm2,
    }


def kernel(x, c, ctx, c_ctx, w_mod, b_mod, g_norm1, w_in, q_norm_g, k_norm_g, attn_sink, w_fourier, gla_w_gate_f,
           gla_b_gate_f, gla_w_gate_b, gla_b_gate_b, gla_norm_g, w_out, g_norm2, w_ffn_in, w_ffn_out):
    b, n, d = x.shape
    lc = ctx.shape[1]
    depth = w_mod.shape[0]
    assert d == D_MODEL and b < MOD_ROWS and n % TOKEN_BLOCK == 0 and n % (DFT_N1 * DFT_STEP) == 0 and lc % GLA_CHUNK == 0

    cc = jnp.concatenate([c, c_ctx[None, :], jnp.zeros((MOD_ROWS - b - 1, d), F32)], axis=0)
    mod3 = _modulation(cc, w_mod, b_mod)
    rope_tabs = _rope_tables(n)
    dft = _dft_consts(n)
    dft_ctx = _dft_ctx_consts(lc)
    flat = lambda t: t.reshape(1, b * lc, t.shape[-1])
    unflat = lambda t: t.reshape(b, lc, t.shape[-1])
    xc = ctx
    weights = _prepare_weights(w_in, g_norm1, q_norm_g, k_norm_g, w_fourier, gla_w_gate_f, gla_b_gate_f,
                               gla_w_gate_b, gla_b_gate_b, gla_norm_g, g_norm2)
    s_zero = jnp.zeros((b, 2, GLA_WIDTH, LANES), F32)
    tail_names, tail_f32, tail_bf16 = ("w_out", "w_ffn_in", "w_ffn_out"), (w_out, w_ffn_in, w_ffn_out), ()
    for l in range(depth):
        need_ctx = l < depth - 1
        lw = dict(weights, layer=l, big_layer=0)
        sink = attn_sink[l]
        cq, ckv4, cg4, clf = map(unflat, _inproj(flat(xc), mod3, b, lw, None, b * lc))
        q, kv4, g4, lf = _inproj(x, mod3, None, lw, rope_tabs, TOKEN_BLOCK)
        att = _attention(q, kv4, ckv4, sink)
        four = _fourier(lf, lw["wf_bd"], l, dft)
        ocf, ocb, s_ctx, _ = _gla(cg4, clf, s_zero, lc)
        olf, olb, _, cast_now = _gla(g4, lf, s_ctx, TOKEN_BLOCK, () if tail_bf16 else tail_f32, l)
        lw.update(zip(tail_names, tail_bf16 or cast_now))
        x, tail_bf16 = _tail(att, four, olf, olb, g4, x, mod3, None, lw, TOKEN_BLOCK, tail_f32 if need_ctx else ())
        if need_ctx:
            att_c = _attention_ctx(cq, ckv4, sink)
            four_c = _fourier_ctx(clf, lw["wf_bd"], l, dft_ctx)
            xc = unflat(_tail(*map(flat, (att_c, four_c, ocf, ocb, cg4, xc)), mod3, b, lw, b * lc)[0])
    return x
```

```python
import functools

import numpy as np
import jax
import jax.numpy as jnp
from jax import lax
from jax.experimental import pallas as pl
from jax.experimental.pallas import tpu as pltpu

F32 = jnp.float32
BF16 = jnp.bfloat16

D_MODEL = 1024
HEAD_DIM = 64
GRID_W = 64
ROPE_BASE = 10000.0
ATT_HEADS = 8
ATT_KV_HEADS = 2
ATT_WIDTH = ATT_HEADS * HEAD_DIM
KV_WIDTH = ATT_KV_HEADS * HEAD_DIM
ATT_BLOCK = 128
ATT_QBLOCKS = 8
ATT_SCALE = HEAD_DIM ** -0.5
LOG2E = 1.4426950408889634
NEG_INF = -1e30
FNET_GROUPS = 4
FNET_WIDTH = FNET_GROUPS * HEAD_DIM
GLA_HEADS = 4
GLA_WIDTH = GLA_HEADS * HEAD_DIM
GLA_GATE_RANK = 16
GLA_TAU = 16.0
GLA_CHUNK = 64
GLA_SCALE = HEAD_DIM ** -0.5
MAIN_WIDTH = ATT_WIDTH + 2 * KV_WIDTH + FNET_WIDTH + 4 * GLA_WIDTH
FFN_HIDDEN = 2816
FFN_CHUNK = 256
INPROJ_SUB = 512
TAIL_SUB = 512
FU_BLOCK = 2
GLA_GROUP = 16
TOKEN_BLOCK = 1024
EPS = 1e-6
LANES = 128
MOD_ROWS = 8
DFT_N1 = 128
DFT_STEP = 32
VMEM_LIMIT = 56 * 1024 * 1024
TAIL_VMEM_LIMIT = 61 * 1024 * 1024

NT_DIMS = (((1,), (1,)), ((), ()))
TN_DIMS = (((0,), (0,)), ((), ()))


def _params(*sem, vmem_limit=VMEM_LIMIT):
    return pltpu.CompilerParams(dimension_semantics=sem, vmem_limit_bytes=vmem_limit)


def _dot(a, b):
    return jnp.dot(a, b, preferred_element_type=F32)


def _silu(x):
    return x / (1.0 + jnp.exp(-x))


def _layer_block(shape, layer):
    nd = len(shape)
    return pl.BlockSpec((None,) + tuple(shape), lambda *_: (layer,) + (0,) * nd, pipeline_mode=pl.Buffered(1))


def _slab_cast_specs(weights, layer, grid):
    steps = grid[0] * grid[1]
    assert all(w.shape[1] % (16 * steps) == 0 for w in weights)
    spec = lambda w, lyr: pl.BlockSpec((None, w.shape[1] // steps, w.shape[2]),
                                       lambda i, j: (lyr, i * grid[1] + j, 0))
    return ([spec(w, layer) for w in weights], [spec(w, 0) for w in weights],
            [jax.ShapeDtypeStruct((1,) + w.shape[1:], BF16) for w in weights])


def _full(shape):
    nd = len(shape)
    return pl.BlockSpec(shape, lambda *_: (0,) * nd)


def _group_mean(t, bd_ref):
    return _dot((t * t).astype(BF16), bd_ref[...])


def _mod_kernel(c_ref, w_ref, b_ref, o_ref):
    s = _silu(c_ref[...]).astype(BF16)
    o_ref[...] = _dot(s, w_ref[...].astype(BF16)) + b_ref[pl.ds(pl.program_id(0), 1), :]


def _modulation(cc, w_mod, b_mod):
    depth, d, width = w_mod.shape
    tn = 1536
    return pl.pallas_call(
        _mod_kernel,
        out_shape=jax.ShapeDtypeStruct((depth, MOD_ROWS, width), F32),
        grid=(depth, width // tn),
        in_specs=[
            _full((MOD_ROWS, d)),
            pl.BlockSpec((None, d, tn), lambda l, j: (l, 0, j)),
            pl.BlockSpec((depth, tn), lambda l, j: (0, j)),
        ],
        out_specs=pl.BlockSpec((None, MOD_ROWS, tn), lambda l, j: (l, 0, j)),
        compiler_params=_params("parallel", "parallel"),
        name="modulation",
    )(cc, w_mod, b_mod)


def _inproj_kernel(*refs, rope, layer, mod_row):
    if rope:
        (x_ref, mod_ref, g1_ref, wm_ref, wz_ref, wg_ref, bg_ref, qg_ref, kg_ref, bd_ref, cos_ref, sin_ref,
         q_ref, kv4_ref, g4_ref, lf_ref) = refs
    else:
        (x_ref, mod_ref, g1_ref, wm_ref, wz_ref, wg_ref, bg_ref, qg_ref, kg_ref, bd_ref,
         q_ref, kv4_ref, g4_ref, lf_ref) = refs
    d = D_MODEL
    pick = lambda ref: ref[layer:layer + 1, :]
    g1, bg, qg, kg = pick(g1_ref), pick(bg_ref), pick(qg_ref), pick(kg_ref)
    mod = mod_ref[pl.ds(pl.program_id(0) if mod_row is None else mod_row, 1), :]
    sh, sc = mod[:, 0:d], mod[:, d:2 * d]
    sub = min(x_ref.shape[0], INPROJ_SUB)
    subs = [slice(sub * j, sub * (j + 1)) for j in range(x_ref.shape[0] // sub)]

    def normed(rs):
        x = x_ref[rs]
        ms = jnp.mean(x * x, axis=-1, keepdims=True)
        return ((x * lax.rsqrt(ms + EPS) * g1) * (1.0 + sc) + sh).astype(BF16)

    lane = lax.broadcasted_iota(jnp.int32, (1, LANES), 1)
    lo = lane < HEAD_DIM
    second_half = (lane & 16) != 0

    def head_norm(t, g128):
        wdt = t.shape[1]
        ms = _dot((t * t).astype(BF16), bd_ref[0:wdt, 0:wdt])
        g = g128 if wdt == LANES else jnp.concatenate([g128] * (wdt // LANES), axis=1)
        return t * lax.rsqrt(ms + EPS) * g

    def rotary(t, rs):
        if not rope:
            return t
        partner = jnp.where(second_half, pltpu.roll(t, 16, 1), pltpu.roll(t, LANES - 16, 1))
        return t * cos_ref[rs] + partner * sin_ref[rs]

    def spread(t, c0, rs, idle):
        tr = pltpu.roll(t, HEAD_DIM, 1)
        fill = jnp.full_like(t, idle)
        tiles = (jnp.where(lo, t, fill), jnp.where(lo, fill, tr), jnp.where(lo, tr, fill), jnp.where(lo, fill, t))
        for j, tile in enumerate(tiles):
            kv4_ref[rs, c0 + LANES * j:c0 + LANES * (j + 1)] = tile.astype(kv4_ref.dtype)

    assert (ATT_WIDTH, 2 * KV_WIDTH + FNET_WIDTH, 2 * GLA_WIDTH) == (4 * LANES,) * 3

    def project(rs, hb):
        group = lambda g: _dot(hb, wm_ref[:, 4 * LANES * g:4 * LANES * (g + 1)])
        gz = _dot(hb, wz_ref[...])
        a_q = group(0)
        z = _dot(gz.astype(BF16), wg_ref[...]) + bg
        log_sig = jnp.minimum(z, 0.0) - jnp.log(1.0 + jnp.exp(-jnp.abs(z)))
        lf_ref[rs, 0:2 * GLA_WIDTH] = log_sig * (LOG2E / GLA_TAU)
        a_kvf = group(1)
        for j2 in range(ATT_WIDTH // (2 * LANES)):
            qn = head_norm(a_q[:, 2 * LANES * j2:2 * LANES * (j2 + 1)], qg)
            for j in range(2):
                t = rotary(qn[:, LANES * j:LANES * (j + 1)], rs) * (ATT_SCALE * LOG2E)
                q_ref[rs, LANES * (2 * j2 + j):LANES * (2 * j2 + j + 1)] = t.astype(q_ref.dtype)
        a_qk = group(2)
        spread(rotary(head_norm(a_kvf[:, 0:KV_WIDTH], kg), rs), 0, rs, 0.0)
        spread(a_kvf[:, KV_WIDTH:2 * KV_WIDTH], 4 * LANES, rs, 1.0)
        lf_ref[rs, 2 * GLA_WIDTH:] = a_kvf[:, 2 * KV_WIDTH:]
        a_vr = group(3)
        g4_ref[rs, 0:GLA_WIDTH] = (a_qk[:, 0:GLA_WIDTH] * GLA_SCALE).astype(g4_ref.dtype)
        g4_ref[rs, GLA_WIDTH:2 * GLA_WIDTH] = a_qk[:, GLA_WIDTH:].astype(g4_ref.dtype)
        g4_ref[rs, 2 * GLA_WIDTH:4 * GLA_WIDTH] = a_vr.astype(g4_ref.dtype)

    hbs = [normed(rs) for rs in subs]
    for rs, hb in zip(subs, hbs):
        project(rs, hb)


def _inproj(x, mod3, mod_row, lw, rope_tabs, tm):
    b, n, d = x.shape
    rope = rope_tabs is not None
    tok = lambda w: pl.BlockSpec((None, tm, w), lambda bi, i: (bi, i, 0))
    lb = lambda shape: _layer_block(shape, lw["layer"])
    vec = lambda name: _full(lw[name].shape)
    in_specs = [
        tok(d),
        lb((MOD_ROWS, 6 * d)),
        vec("g1"), lb((d, MAIN_WIDTH)), lb((d, LANES)), lb((LANES, 2 * GLA_WIDTH)),
        vec("b_gate"), vec("q_g"), vec("k_g"), _full((2 * LANES, 2 * LANES)),
    ]
    args = [x, mod3, lw["g1"], lw["w_main"], lw["w_z"], lw["w_gate"], lw["b_gate"], lw["q_g"], lw["k_g"], lw["bd256"]]
    if rope:
        in_specs += [pl.BlockSpec((tm, LANES), lambda bi, i: (i, 0))] * 2
        args += list(rope_tabs)
    sds = lambda w, dt: jax.ShapeDtypeStruct((b, n, w), dt)
    out_shape = [sds(ATT_WIDTH, BF16), sds(8 * LANES, BF16), sds(4 * GLA_WIDTH, BF16),
                 sds(2 * GLA_WIDTH + FNET_WIDTH, F32)]
    out_specs = [tok(s.shape[-1]) for s in out_shape]
    return pl.pallas_call(
        functools.partial(_inproj_kernel, rope=rope, layer=lw["layer"], mod_row=mod_row),
        out_shape=out_shape, grid=(b, n // tm), in_specs=in_specs, out_specs=out_specs,
        compiler_params=_params("parallel", "parallel"),
        name="inproj_rope" if rope else "inproj_ctx",
    )(*args)


def _scores(q2, key_tiles, masks):
    cols = []
    for k, tile_masks in zip(key_tiles, masks):
        s = lax.dot_general(q2, k, NT_DIMS, preferred_element_type=F32)
        for j, mk in enumerate(tile_masks):
            c = s[:, LANES * j:LANES * (j + 1)]
            cols.append(c if mk is None else jnp.where(mk, c, NEG_INF))
    return cols


def _softmax_pv(cols, val_tiles, sink_col):
    mx = cols[0]
    for c in cols[1:]:
        mx = jnp.maximum(mx, c)
    m = jnp.maximum(jnp.max(mx, axis=-1, keepdims=True), sink_col)
    p = jnp.concatenate([jnp.exp2(c - m).astype(BF16) for c in cols], axis=1)
    pv = _dot(p, jnp.concatenate(val_tiles, axis=0))
    den = pltpu.roll(pv, HEAD_DIM, 1) + jnp.exp2(sink_col - m)
    return pv / den


def _attn_kernel(*refs):
    nq = ATT_QBLOCKS
    sink_ref, q_ref = refs[0:2]
    kv_refs, kvc_ref, o_ref = refs[2:nq + 4], refs[nq + 4], refs[nq + 5]
    v_off = 4 * LANES
    i = pl.program_id(1)
    last = pl.num_programs(1) - 1
    blk = ATT_BLOCK
    group = ATT_HEADS // ATT_KV_HEADS
    row = lax.broadcasted_iota(jnp.int32, (2 * blk, blk), 0) & (blk - 1)
    col = lax.broadcasted_iota(jnp.int32, (2 * blk, blk), 1)
    band_l, band_r = col >= row, col <= row
    mask_l = [jnp.logical_and(band_l, i > 0) if s == 0 else band_l for s in range(nq)]
    mask_r = [jnp.logical_and(band_r, i < last) if s == nq - 1 else band_r for s in range(nq)]
    upper_rows = lax.broadcasted_iota(jnp.int32, (2 * blk, 1), 0) >= blk
    lo = lax.broadcasted_iota(jnp.int32, (1, LANES), 1) < HEAD_DIM

    def tiles(s, ks):
        l, m_, r = kv_refs[s:s + 3]
        return [jnp.concatenate([l[:, ks], m_[:, ks]], axis=0),
                jnp.concatenate([r[:, ks], kvc_ref[0:blk, ks]], axis=0), kvc_ref[blk:, ks]]

    slots = [(s, kvh, par) for s in range(nq) for kvh in range(ATT_KV_HEADS) for par in range(2)]
    lane_slice = lambda kvh, par, off=0: slice(off + 2 * LANES * kvh + LANES * par,
                                               off + 2 * LANES * kvh + LANES * (par + 1))
    cols = {}
    for s, kvh, par in slots:
        qs, rs = 2 * LANES * kvh, slice(blk * s, blk * (s + 1))
        q2 = jnp.concatenate([q_ref[rs, qs:qs + LANES], q_ref[rs, qs + LANES:qs + 2 * LANES]], axis=0)
        cols[s, kvh, par] = _scores(q2, tiles(s, lane_slice(kvh, par)),
                                    [(mask_l[s], None), (mask_r[s], None), (None,)])
    outs = {}
    for s, kvh, par in slots:
        base = group * kvh + par
        sink_col = jnp.where(upper_rows, sink_ref[base + 2], sink_ref[base]) * LOG2E
        outs[s, kvh, par] = _softmax_pv(cols[s, kvh, par], tiles(s, lane_slice(kvh, par, v_off)), sink_col)
    for s in range(nq):
        for kvh in range(ATT_KV_HEADS):
            qs, r0 = 2 * LANES * kvh, blk * s
            o = jnp.where(lo, outs[s, kvh, 0], outs[s, kvh, 1])
            o_ref[r0:r0 + blk, qs:qs + LANES] = o[0:blk].astype(o_ref.dtype)
            o_ref[r0:r0 + blk, qs + LANES:qs + 2 * LANES] = o[blk:2 * blk].astype(o_ref.dtype)


def _attention(q, kv4, ckv4, sink):
    b, n, _ = q.shape
    lc = ckv4.shape[1]
    nq = ATT_QBLOCKS
    nb = n // ATT_BLOCK
    assert lc == 2 * ATT_BLOCK and nb % nq == 0
    qspec = pl.BlockSpec((None, nq * ATT_BLOCK, 4 * LANES), lambda bi, i: (bi, i, 0))
    kv = [pl.BlockSpec((None, ATT_BLOCK, 8 * LANES),
                       functools.partial(lambda bi, i, j: (bi, jnp.clip(nq * i + j - 1, 0, nb - 1), 0), j=j))
          for j in range(nq + 2)]
    ctx = pl.BlockSpec((None, lc, 8 * LANES), lambda bi, i: (bi, 0, 0))
    return pl.pallas_call(
        _attn_kernel,
        out_shape=jax.ShapeDtypeStruct((b, n, ATT_WIDTH), BF16),
        grid=(b, nb // nq),
        in_specs=[pl.BlockSpec(memory_space=pltpu.SMEM), qspec] + kv + [ctx],
        out_specs=qspec,
        compiler_params=_params("parallel", "parallel"),
        name="window_attention",
    )(sink, q, *([kv4] * (nq + 2)), ckv4)


def _attn_ctx_kernel(sink_ref, q_ref, kc_ref, vc_ref, o_ref):
    kvh = pl.program_id(1)
    lc = q_ref.shape[0]
    q2 = jnp.concatenate([q_ref[:, 0:LANES], q_ref[:, LANES:2 * LANES]], axis=0)
    upper_rows = lax.broadcasted_iota(jnp.int32, (2 * lc, 1), 0) >= lc
    lo = lax.broadcasted_iota(jnp.int32, (1, LANES), 1) < HEAD_DIM
    outs = []
    for par in range(2):
        ks = slice(LANES * par, LANES * (par + 1))
        base = ATT_HEADS // ATT_KV_HEADS * kvh + par
        sink_col = jnp.where(upper_rows, sink_ref[base + 2], sink_ref[base]) * LOG2E
        cols = _scores(q2, [kc_ref[:, ks]], [(None,) * (lc // LANES)])
        outs.append(_softmax_pv(cols, [vc_ref[:, ks]], sink_col))
    o = jnp.where(lo, outs[0], outs[1])
    o_ref[:, 0:LANES] = o[0:lc].astype(o_ref.dtype)
    o_ref[:, LANES:2 * LANES] = o[lc:2 * lc].astype(o_ref.dtype)


def _attention_ctx(cq, ckv4, sink):
    b, lc, _ = cq.shape
    gw = 2 * LANES
    spec = pl.BlockSpec((None, lc, gw), lambda bi, h: (bi, 0, h))
    vspec = pl.BlockSpec((None, lc, gw), lambda bi, h: (bi, 0, ATT_KV_HEADS + h))
    return pl.pallas_call(
        _attn_ctx_kernel,
        out_shape=jax.ShapeDtypeStruct((b, lc, ATT_WIDTH), BF16),
        grid=(b, ATT_KV_HEADS),
        in_specs=[pl.BlockSpec(memory_space=pltpu.SMEM), spec, spec, vspec],
        out_specs=spec,
        compiler_params=_params("parallel", "parallel"),
        name="context_attention",
    )(sink, cq, ckv4, ckv4)


def _dft_consts(n):
    n1, n2 = DFT_N1, n // DFT_N1
    ang = lambda a, m: 2.0 * np.pi * np.outer(np.arange(a), np.arange(a)) / m
    c = HEAD_DIM
    cc, sc = np.cos(ang(c, c)) / np.sqrt(c), np.sin(ang(c, c)) / np.sqrt(c)
    eye = np.eye(FNET_GROUPS)
    w_chan = np.concatenate([np.kron(eye, cc), -np.kron(eye, sc)], axis=1)
    c1, s1 = np.cos(ang(n1, n1)) / np.sqrt(n1), np.sin(ang(n1, n1)) / np.sqrt(n1)
    m1 = np.block([[c1, s1], [-s1, c1]])
    kk = np.arange(n1)[:, None, None] + n1 * np.arange(n2)[None, :, None]
    ph = 2.0 * np.pi * (kk * np.arange(n2)[None, None, :] % n) / n
    m3 = np.concatenate([np.cos(ph), np.sin(ph)], axis=2) / np.sqrt(n2)
    return tuple(jnp.asarray(a, F32) for a in (w_chan, m1, m3))


def _dft_ctx_consts(lc):
    ang = lambda a, m: 2.0 * np.pi * np.outer(np.arange(a), np.arange(a)) / m
    c = HEAD_DIM
    cc, sc = np.cos(ang(c, c)) / np.sqrt(c), np.sin(ang(c, c)) / np.sqrt(c)
    eye = np.eye(FNET_GROUPS)
    w_chan = np.concatenate([np.kron(eye, cc), -np.kron(eye, sc)], axis=1)
    cl, sl = np.cos(ang(lc, lc)) / np.sqrt(lc), np.sin(ang(lc, lc)) / np.sqrt(lc)
    return jnp.asarray(w_chan, F32), jnp.asarray(np.concatenate([cl, sl], axis=1), F32)


def _regroup_rows(x, outer):
    r, c = x.shape
    return jnp.swapaxes(x.reshape(outer, r // outer, c), 0, 1).reshape(r, c)


def _fourier_kernel(u_ref, wc_ref, m1_ref, m3_ref, wf_ref, y_ref, p_ref):
    w, t = FNET_WIDTH, DFT_STEP
    s = pl.program_id(1)
    n_a = p_ref.shape[1]
    n2 = n_a * t

    @pl.when(s < n_a)
    def _():
        u = u_ref[...].reshape(DFT_N1 * t, w).astype(BF16)
        z = _regroup_rows(_dot(u, wc_ref[...]), DFT_N1)
        a_re, a_im = [], []
        for tt in range(t):
            zt = z[DFT_N1 * tt:DFT_N1 * (tt + 1)]
            zs = jnp.concatenate([zt[:, 0:w], zt[:, w:2 * w]], axis=0).astype(BF16)
            a = _dot(m1_ref[...], zs)
            a_re.append(a[0:DFT_N1])
            a_im.append(a[DFT_N1:2 * DFT_N1])
        for part, rows in enumerate((a_re, a_im)):
            a = _regroup_rows(jnp.concatenate(rows, axis=0), t)
            p_ref[part, s] = a.reshape(DFT_N1, t, w)

    @pl.when(s >= n_a)
    def _():
        k1_0 = (s - n_a) * t
        ys = []
        for kk in range(t):
            parts = [p_ref[part, :, k1_0 + kk].reshape(n2, w) for part in range(2)]
            rhs = jnp.concatenate(parts, axis=0).astype(BF16)
            ys.append(_dot(m3_ref[kk].astype(BF16), rhs).astype(BF16))
        out = _dot(jnp.concatenate(ys, axis=0), wf_ref[...])
        y_ref[...] = _regroup_rows(out, t).reshape(n2, t, w)


def _fourier(lf, wf_bd, layer, consts):
    b, n, wl = lf.shape
    w = FNET_WIDTH
    n1, n2, t = DFT_N1, n // DFT_N1, DFT_STEP
    n_a, n_b = n2 // t, n1 // t
    w_chan, m1, m3 = consts
    y = pl.pallas_call(
        _fourier_kernel,
        out_shape=jax.ShapeDtypeStruct((b, n2, n1, w), F32),
        grid=(b, n_a + n_b),
        in_specs=[pl.BlockSpec((None, n1, t, w), lambda bi, s: (bi, 0, jnp.minimum(s, n_a - 1), FU_BLOCK)),
                  _full((w, 2 * w)), _full((2 * n1, 2 * n1)),
                  pl.BlockSpec((t, n2, 2 * n2), lambda bi, s: (jnp.maximum(s - n_a, 0), 0, 0)),
                  _layer_block((w, w), layer)],
        out_specs=pl.BlockSpec((None, n2, t, w), lambda bi, s: (bi, 0, jnp.maximum(s - n_a, 0), 0)),
        scratch_shapes=[pltpu.VMEM((2, n_a, n1, t, w), F32)],
        compiler_params=_params("parallel", "arbitrary"),
        name="fourier_mix",
    )(lf.reshape(b, n1, n2, wl), w_chan.astype(BF16), m1.astype(BF16), m3, wf_bd)
    return y.reshape(b, n, w)


def _four_ctx_kernel(u_ref, wc_ref, m_ref, wf_ref, y_ref):
    w = FNET_WIDTH
    z = _dot(u_ref[...].astype(BF16), wc_ref[...])
    zs = jnp.concatenate([z[:, 0:w], z[:, w:2 * w]], axis=0).astype(BF16)
    y = _dot(m_ref[...], zs)
    y_ref[...] = _dot(y.astype(BF16), wf_ref[...]).astype(y_ref.dtype)


def _fourier_ctx(clf, wf_bd, layer, consts):
    b, lc, _ = clf.shape
    w = FNET_WIDTH
    w_chan, m = consts[0].astype(BF16), consts[1].astype(BF16)
    return pl.pallas_call(
        _four_ctx_kernel,
        out_shape=jax.ShapeDtypeStruct((b, lc, w), BF16),
        grid=(b,),
        in_specs=[pl.BlockSpec((None, lc, w), lambda bi: (bi, 0, FU_BLOCK)), _full((w, 2 * w)), _full((lc, 2 * lc)),
                  _layer_block((w, w), layer)],
        out_specs=pl.BlockSpec((None, lc, w), lambda bi: (bi, 0, 0)),
        compiler_params=_params("parallel"),
        name="fourier_context",
    )(clf, w_chan, m, wf_bd)


def _pair_stack(x, lo):
    zero = jnp.zeros_like(x)
    return jnp.concatenate([jnp.where(lo, x, zero), jnp.where(lo, zero, x)], axis=0)


def _gla_kernel(*refs, ncast):
    qkvf_ref, laf_ref, qkvb_ref, lab_ref, s0_ref = refs[0:5]
    of_ref, ob_ref, sfin_ref = refs[5 + ncast:8 + ncast]
    stf_ref, stb_ref = refs[8 + 2 * ncast:10 + 2 * ncast]
    for src, dst in zip(refs[5:5 + ncast], refs[8 + ncast:8 + 2 * ncast]):
        dst[...] = src[...].astype(dst.dtype)
    i = pl.program_id(1)
    c = GLA_CHUNK
    nchunk = qkvf_ref.shape[0] // c

    @pl.when(i == 0)
    def _():
        stf_ref[...] = s0_ref[0]
        stb_ref[...] = s0_ref[1]

    r64 = lax.broadcasted_iota(jnp.int32, (c, c), 0)
    c64 = lax.broadcasted_iota(jnp.int32, (c, c), 1)
    tri = ((c64 <= r64).astype(BF16), (c64 >= r64).astype(BF16))
    at = lax.broadcasted_iota(jnp.int32, (c, LANES), 0)
    as_ = lax.broadcasted_iota(jnp.int32, (c, LANES), 1) & (c - 1)
    att_mask = (as_ <= at, as_ >= at)
    lo = lax.broadcasted_iota(jnp.int32, (1, LANES), 1) < HEAD_DIM
    br = lax.broadcasted_iota(jnp.int32, (LANES, LANES), 0) < HEAD_DIM
    bc = lax.broadcasted_iota(jnp.int32, (LANES, LANES), 1) < HEAD_DIM
    bd_mask = br == bc
    pairs = [slice(LANES * p, LANES * (p + 1)) for p in range(GLA_WIDTH // LANES)]
    in_refs = ((qkvf_ref, laf_ref), (qkvb_ref, lab_ref))
    out_refs = (of_ref, ob_ref)
    items = [(d, step if d == 0 else nchunk - 1 - step) for step in range(nchunk) for d in range(2)]
    rows = lambda ch: slice(c * ch, c * (ch + 1))

    bcum, work, o_intra, ut, decay = {}, {}, {}, {}, {}
    st_refs = (stf_ref, stb_ref)
    st = [[st_refs[d][LANES * j:LANES * (j + 1)] for j in range(len(pairs))] for d in range(2)]

    def gate_sums(d, ch):
        la = in_refs[d][1][rows(ch)]
        la_hi = la.astype(BF16)
        la_lo = (la - la_hi.astype(F32)).astype(BF16)
        bcum[d, ch] = _dot(jnp.concatenate([tri[d], tri[d]], axis=1), jnp.concatenate([la_hi, la_lo], axis=0))

    def scores(d, ch):
        qkv_ref = in_refs[d][0]
        w = GLA_WIDTH
        b = bcum[d, ch]
        btot = b[0:1] if d == 1 else b[c - 1:c]
        k = qkv_ref[rows(ch), w:2 * w].astype(F32)
        q_in = (qkv_ref[rows(ch), 0:w].astype(F32) * jnp.exp2(b)).astype(BF16)
        k_in = (k * jnp.exp2(-b)).astype(BF16)
        k_out = (k * jnp.exp2(btot - b)).astype(BF16)
        vb = qkv_ref[rows(ch), 2 * w:3 * w]
        att = [lax.dot_general(q_in[:, p], _pair_stack(k_in[:, p], lo), NT_DIMS, preferred_element_type=F32)
               for p in pairs]
        work[d, ch] = (btot, q_in, k_out, vb, att)

    def intra(d, ch):
        btot, q_in, k_out, vb, att = work[d, ch]
        o_intra[d, ch] = [_dot(jnp.where(att_mask[d], a, 0.0).astype(BF16), _pair_stack(vb[:, p], lo))
                          for a, p in zip(att, pairs)]
        ut[d, ch] = [lax.dot_general(k_out[:, p], vb[:, p], TN_DIMS, preferred_element_type=F32) for p in pairs]
        decay[d, ch] = [jnp.transpose(jnp.exp2(btot[:, p])) for p in pairs]

    def carry(d, ch):
        btot, q_in = work[d, ch][0:2]
        outs = []
        for j, p in enumerate(pairs):
            outs.append(o_intra[d, ch][j] + _dot(q_in[:, p], st[d][j].astype(BF16)))
            st[d][j] = st[d][j] * decay[d, ch][j] + jnp.where(bd_mask, ut[d, ch][j], 0.0)
        out_refs[d][rows(ch)] = jnp.concatenate(outs, axis=1).astype(out_refs[d].dtype)

    groups = [items[g:g + GLA_GROUP] for g in range(0, len(items), GLA_GROUP)]
    for it in groups[0]:
        gate_sums(*it)
    for it in groups[0]:
        scores(*it)
    for g, grp in enumerate(groups):
        for it in grp:
            intra(*it)
        nxt = groups[g + 1] if g + 1 < len(groups) else []
        for it in nxt:
            gate_sums(*it)
        for k_ in range(len(grp)):
            carry(*grp[k_])
            if k_ < len(nxt):
                scores(*nxt[k_])
    for d in range(2):
        st_refs[d][...] = jnp.concatenate(st[d], axis=0)

    @pl.when(i == pl.num_programs(1) - 1)
    def _():
        for d in range(2):
            sfin_ref[d] = jnp.concatenate(st[d], axis=0)


def _gla(g4, lf, s0, tb, cast=(), cast_layer=0):
    b, n, _ = g4.shape
    w = GLA_WIDTH
    nblk = n // tb
    fwd = lambda bi, i: (bi, i, 0)
    bwd = lambda bi, i: (bi, nblk - 1 - i, 0)
    bwd_la = lambda bi, i: (bi, nblk - 1 - i, 1)
    tok = lambda f: pl.BlockSpec((None, tb, w), f)
    qkv = lambda f: pl.BlockSpec((None, tb, 3 * w), f)
    state = pl.BlockSpec((None, 2, w, LANES), lambda bi, i: (bi, 0, 0, 0))
    cast_in, cast_out, cast_shapes = _slab_cast_specs(cast, cast_layer, (b, nblk))
    res = pl.pallas_call(
        functools.partial(_gla_kernel, ncast=len(cast)),
        out_shape=[jax.ShapeDtypeStruct((b, n, w), BF16), jax.ShapeDtypeStruct((b, n, w), BF16),
                   jax.ShapeDtypeStruct((b, 2, w, LANES), F32)] + cast_shapes,
        grid=(b, nblk),
        in_specs=[qkv(fwd), tok(fwd), qkv(bwd), tok(bwd_la), state] + cast_in,
        out_specs=[tok(fwd), tok(bwd), state] + cast_out,
        scratch_shapes=[pltpu.VMEM((w, LANES), F32), pltpu.VMEM((w, LANES), F32)],
        compiler_params=_params("arbitrary", "arbitrary"),
        name="gla_scan",
    )(g4, lf, g4, lf, s0, *cast)
    return res[0], res[1], res[2], res[3:]


def _tail_kernel(*refs, ncast, layer, mod_row):
    (att_ref, four_ref, of_ref, ob_ref, r_ref, x_ref, mod_ref, gg_ref, bd_ref, wmix_ref, g2_ref, wi_ref,
     wo_ref) = refs[0:13]
    o_ref, a_ref = refs[13 + ncast], refs[14 + 2 * ncast]
    for src, dst in zip(refs[13:13 + ncast], refs[14 + ncast:14 + 2 * ncast]):
        dst[...] = src[...].astype(dst.dtype)
    d = D_MODEL
    gla_g, g2 = gg_ref[layer:layer + 1, :], g2_ref[layer:layer + 1, :]
    mod_vec = mod_ref[pl.ds(pl.program_id(0) if mod_row is None else mod_row, 1), :]
    mod = lambda j: mod_vec[:, j * d:(j + 1) * d]
    a0, a1, a2 = ATT_WIDTH, ATT_WIDTH + FNET_WIDTH, ATT_WIDTH + FNET_WIDTH + GLA_WIDTH
    sub = min(x_ref.shape[0], TAIL_SUB)
    subs = [slice(sub * j, sub * (j + 1)) for j in range(x_ref.shape[0] // sub)]

    def mixed(rs):
        o = of_ref[rs].astype(F32) + ob_ref[rs].astype(F32)
        y = o * lax.rsqrt(_group_mean(o, bd_ref) + EPS) * gla_g
        y = y * _silu(r_ref[rs].astype(F32))
        mix = (_dot(att_ref[rs], wmix_ref[0:a0]) + _dot(four_ref[rs].astype(BF16), wmix_ref[a0:a1])
               + _dot(y.astype(BF16), wmix_ref[a1:a2]))
        x = x_ref[rs] + mod(2) * mix
        ms = jnp.mean(x * x, axis=-1, keepdims=True)
        return x, ((x * lax.rsqrt(ms + EPS) * g2) * (1.0 + mod(4)) + mod(3)).astype(BF16)

    pre = [mixed(rs) for rs in subs]
    for rs, (x, hb) in zip(subs, pre):
        for c0 in range(0, FFN_HIDDEN, FFN_CHUNK):
            g = _dot(hb, wi_ref[:, c0:c0 + FFN_CHUNK])
            u = _dot(hb, wi_ref[:, FFN_HIDDEN + c0:FFN_HIDDEN + c0 + FFN_CHUNK])
            a_ref[rs, c0:c0 + FFN_CHUNK] = (_silu(g) * u).astype(BF16)
        o_ref[rs] = x + mod(5) * _dot(a_ref[rs], wo_ref[...])


def _tail(att, four, of, ob, g4, x, mod3, mod_row, lw, tm, cast_next=()):
    b, n, d = x.shape
    steps = n // tm
    tok = lambda w: pl.BlockSpec((None, tm, w), lambda bi, i: (bi, i, 0))
    gate = pl.BlockSpec((None, tm, GLA_WIDTH), lambda bi, i: (bi, i, 3))
    vec = lambda name: _full(lw[name].shape)
    big = lambda shape: _layer_block(shape, lw["big_layer"])
    cast_in, cast_out, cast_shapes = _slab_cast_specs(cast_next, lw["layer"] + 1, (b, steps))
    res = pl.pallas_call(
        functools.partial(_tail_kernel, ncast=len(cast_next), layer=lw["layer"], mod_row=mod_row),
        out_shape=[jax.ShapeDtypeStruct((b, n, d), F32)] + cast_shapes,
        grid=(b, steps),
        in_specs=[tok(ATT_WIDTH), tok(FNET_WIDTH), tok(GLA_WIDTH), tok(GLA_WIDTH), gate, tok(d),
                  _layer_block((MOD_ROWS, 6 * d), lw["layer"]),
                  vec("gla_g"), _full((GLA_WIDTH, GLA_WIDTH)), big((d, d)),
                  vec("g2"), big((d, 2 * FFN_HIDDEN)), big((FFN_HIDDEN, d))] + cast_in,
        out_specs=[tok(d)] + cast_out,
        scratch_shapes=[pltpu.VMEM((tm, FFN_HIDDEN), BF16)],
        compiler_params=_params("arbitrary", "arbitrary", vmem_limit=TAIL_VMEM_LIMIT),
        name="mix_ffn",
    )(att, four, of, ob, g4, x, mod3, lw["gla_g"], lw["bd256"], lw["w_out"], lw["g2"], lw["w_ffn_in"],
      lw["w_ffn_out"], *cast_next)
    return res[0], res[1:]


def _rope_tables(n):
    axis_dim = HEAD_DIM // 2
    inv_freq = ROPE_BASE ** (-np.arange(0, axis_dim, 2, dtype=np.float64) / axis_dim)
    t = np.arange(n)
    ang_r = (t // GRID_W)[:, None] * inv_freq[None, :]
    ang_c = (t % GRID_W)[:, None] * inv_freq[None, :]
    cos = np.concatenate([np.cos(ang_r)] * 2 + [np.cos(ang_c)] * 2, axis=1)
    sin = np.concatenate([-np.sin(ang_r), np.sin(ang_r), -np.sin(ang_c), np.sin(ang_c)], axis=1)
    return jnp.asarray(np.tile(cos, (1, 2)), F32), jnp.asarray(np.tile(sin, (1, 2)), F32)


def _block_diag_mean(width):
    g = np.arange(width) // HEAD_DIM
    return jnp.asarray((g[:, None] == g[None, :]) / HEAD_DIM, BF16)


def _prepare_weights(w_in, g_norm1, q_norm_g, k_norm_g, w_fourier, wgf, bgf, wgb, bgb, gla_norm_g, g_norm2):
    depth = w_in.shape[0]
    r = GLA_GATE_RANK
    zr = jnp.zeros((depth, r, GLA_WIDTH), F32)
    w_gate = jnp.concatenate([jnp.concatenate([wgf, zr], axis=2), jnp.concatenate([zr, wgb], axis=2)], axis=1)
    eye = jnp.eye(FNET_GROUPS, dtype=F32)
    wf_bd = jnp.einsum("lgij,gh->lgihj", w_fourier, eye).reshape(depth, FNET_WIDTH, FNET_WIDTH)
    return {
        "g1": g_norm1,
        "w_main": w_in.astype(BF16),
        "w_z": jnp.pad(w_in[:, :, MAIN_WIDTH:], ((0, 0), (0, 0), (0, LANES - 2 * r))).astype(BF16),
        "w_gate": jnp.pad(w_gate, ((0, 0), (0, LANES - 2 * r), (0, 0))).astype(BF16),
        "b_gate": jnp.concatenate([bgf, bgb], axis=1),
        "q_g": jnp.tile(q_norm_g, (1, 2)),
        "k_g": jnp.tile(k_norm_g, (1, 2)),
        "bd256": _block_diag_mean(GLA_WIDTH),
        "wf_bd": wf_bd.astype(BF16),
        "gla_g": jnp.tile(gla_norm_g, (1, GLA_HEADS)),
        "g2": g_norm2,
    }


def kernel(x, c, ctx, c_ctx, w_mod, b_mod, g_norm1, w_in, q_norm_g, k_norm_g, attn_sink, w_fourier, gla_w_gate_f,
           gla_b_gate_f, gla_w_gate_b, gla_b_gate_b, gla_norm_g, w_out, g_norm2, w_ffn_in, w_ffn_out):
    b, n, d = x.shape
    lc = ctx.shape[1]
    depth = w_mod.shape[0]
    assert d == D_MODEL and b < MOD_ROWS and n % TOKEN_BLOCK == 0 and n % (DFT_N1 * DFT_STEP) == 0 and lc % GLA_CHUNK == 0

    cc = jnp.concatenate([c, c_ctx[None, :], jnp.zeros((MOD_ROWS - b - 1, d), F32)], axis=0)
    mod3 = _modulation(cc, w_mod, b_mod)
    rope_tabs = _rope_tables(n)
    dft = _dft_consts(n)
    dft_ctx = _dft_ctx_consts(lc)
    flat = lambda t: t.reshape(1, b * lc, t.shape[-1])
    unflat = lambda t: t.reshape(b, lc, t.shape[-1])
    xc = ctx
    weights = _prepare_weights(w_in, g_norm1, q_norm_g, k_norm_g, w_fourier, gla_w_gate_f, gla_b_gate_f,
                               gla_w_gate_b, gla_b_gate_b, gla_norm_g, g_norm2)
    s_zero = jnp.zeros((b, 2, GLA_WIDTH, LANES), F32)
    tail_names, tail_f32, tail_bf16 = ("w_out", "w_ffn_in", "w_ffn_out"), (w_out, w_ffn_in, w_ffn_out), ()
    for l in range(depth):
        need_ctx = l < depth - 1
        lw = dict(weights, layer=l, big_layer=0)
        sink = attn_sink[l]
        cq, ckv4, cg4, clf = map(unflat, _inproj(flat(xc), mod3, b, lw, None, b * lc))
        q, kv4, g4, lf = _inproj(x, mod3, None, lw, rope_tabs, TOKEN_BLOCK)
        att = _attention(q, kv4, ckv4, sink)
        four = _fourier(lf, lw["wf_bd"], l, dft)
        ocf, ocb, s_ctx, _ = _gla(cg4, clf, s_zero, lc)
        olf, olb, _, cast_now = _gla(g4, lf, s_ctx, TOKEN_BLOCK, () if tail_bf16 else tail_f32, l)
        lw.update(zip(tail_names, tail_bf16 or cast_now))
        x, tail_bf16 = _tail(att, four, olf, olb, g4, x, mod3, None, lw, TOKEN_BLOCK, tail_f32 if need_ctx else ())
        if need_ctx:
            att_c = _attention_ctx(cq, ckv4, sink)
            four_c = _fourier_ctx(clf, lw["wf_bd"], l, dft_ctx)
            xc = unflat(_tail(*map(flat, (att_c, four_c, ocf, ocb, cg4, xc)), mod3, b, lw, b * lc)[0])
    return x
```

```python
import functools

import numpy as np
import jax
import jax.numpy as jnp
from jax import lax
from jax.experimental import pallas as pl
from jax.experimental.pallas import tpu as pltpu

F32 = jnp.float32
BF16 = jnp.bfloat16

D_MODEL = 1024
HEAD_DIM = 64
GRID_W = 64
ROPE_BASE = 10000.0
ATT_HEADS = 8
ATT_KV_HEADS = 2
ATT_WIDTH = ATT_HEADS * HEAD_DIM
KV_WIDTH = ATT_KV_HEADS * HEAD_DIM
ATT_BLOCK = 128
ATT_QBLOCKS = 8
ATT_SCALE = HEAD_DIM ** -0.5
LOG2E = 1.4426950408889634
NEG_INF = -1e30
FNET_GROUPS = 4
FNET_WIDTH = FNET_GROUPS * HEAD_DIM
GLA_HEADS = 4
GLA_WIDTH = GLA_HEADS * HEAD_DIM
GLA_GATE_RANK = 16
GLA_TAU = 16.0
GLA_CHUNK = 64
GLA_SCALE = HEAD_DIM ** -0.5
MAIN_WIDTH = ATT_WIDTH + 2 * KV_WIDTH + FNET_WIDTH + 4 * GLA_WIDTH
FFN_HIDDEN = 2816
FFN_CHUNK = 256
INPROJ_SUB = 512
TAIL_SUB = 512
FU_BLOCK = 2
GLA_GROUP = 16
TOKEN_BLOCK = 1024
GLA_BLOCK = 2048
MOD_TILE = 1536
EPS = 1e-6
LANES = 128
MOD_ROWS = 8
DFT_N1 = 128
DFT_STEP = 32
VMEM_LIMIT = 56 * 1024 * 1024
TAIL_VMEM_LIMIT = 61 * 1024 * 1024

NT_DIMS = (((1,), (1,)), ((), ()))
TN_DIMS = (((0,), (0,)), ((), ()))


def _params(*sem, vmem_limit=VMEM_LIMIT):
    return pltpu.CompilerParams(dimension_semantics=sem, vmem_limit_bytes=vmem_limit)


def _dot(a, b):
    return jnp.dot(a, b, preferred_element_type=F32)


def _silu(x):
    return x / (1.0 + jnp.exp(-x))


def _layer_block(shape, layer):
    nd = len(shape)
    return pl.BlockSpec((None,) + tuple(shape), lambda *_: (layer,) + (0,) * nd, pipeline_mode=pl.Buffered(1))


def _slab_cast_specs(weights, layer, grid):
    steps = grid[0] * grid[1]
    assert all(w.shape[1] % (16 * steps) == 0 for w in weights)
    spec = lambda w, lyr: pl.BlockSpec((None, w.shape[1] // steps, w.shape[2]),
                                       lambda i, j: (lyr, i * grid[1] + j, 0))
    return ([spec(w, layer) for w in weights], [spec(w, 0) for w in weights],
            [jax.ShapeDtypeStruct((1,) + w.shape[1:], BF16) for w in weights])


def _full(shape):
    nd = len(shape)
    return pl.BlockSpec(shape, lambda *_: (0,) * nd)


def _group_mean(t, bd_ref):
    return _dot((t * t).astype(BF16), bd_ref[...])


def _mod_kernel(c_ref, w_ref, b_ref, o_ref):
    s = _silu(c_ref[...]).astype(BF16)
    o_ref[...] = _dot(s, w_ref[...].astype(BF16)) + b_ref[pl.ds(pl.program_id(0), 1), :]


def _modulation(cc, w_mod, b_mod):
    depth, d, width = w_mod.shape
    tn = MOD_TILE
    return pl.pallas_call(
        _mod_kernel,
        out_shape=jax.ShapeDtypeStruct((depth, MOD_ROWS, width), F32),
        grid=(depth, width // tn),
        in_specs=[
            _full((MOD_ROWS, d)),
            pl.BlockSpec((None, d, tn), lambda l, j: (l, 0, j)),
            pl.BlockSpec((depth, tn), lambda l, j: (0, j)),
        ],
        out_specs=pl.BlockSpec((None, MOD_ROWS, tn), lambda l, j: (l, 0, j)),
        compiler_params=_params("parallel", "parallel"),
        name="modulation",
    )(cc, w_mod, b_mod)


def _inproj_kernel(*refs, rope, layer, mod_row):
    if rope:
        (x_ref, mod_ref, g1_ref, wm_ref, wz_ref, wg_ref, bg_ref, qg_ref, kg_ref, bd_ref, cos_ref, sin_ref,
         q_ref, kv4_ref, g4_ref, lf_ref) = refs
    else:
        (x_ref, mod_ref, g1_ref, wm_ref, wz_ref, wg_ref, bg_ref, qg_ref, kg_ref, bd_ref,
         q_ref, kv4_ref, g4_ref, lf_ref) = refs
    d = D_MODEL
    pick = lambda ref: ref[layer:layer + 1, :]
    g1, bg, qg, kg = pick(g1_ref), pick(bg_ref), pick(qg_ref), pick(kg_ref)
    mod = mod_ref[pl.ds(pl.program_id(0) if mod_row is None else mod_row, 1), :]
    sh, sc = mod[:, 0:d], mod[:, d:2 * d]
    sub = min(x_ref.shape[0], INPROJ_SUB)
    subs = [slice(sub * j, sub * (j + 1)) for j in range(x_ref.shape[0] // sub)]

    def normed(rs):
        x = x_ref[rs]
        ms = jnp.mean(x * x, axis=-1, keepdims=True)
        return ((x * lax.rsqrt(ms + EPS) * g1) * (1.0 + sc) + sh).astype(BF16)

    lane = lax.broadcasted_iota(jnp.int32, (1, LANES), 1)
    lo = lane < HEAD_DIM
    second_half = (lane & 16) != 0

    def head_norm(t, g128):
        wdt = t.shape[1]
        ms = _dot((t * t).astype(BF16), bd_ref[0:wdt, 0:wdt])
        g = g128 if wdt == LANES else jnp.concatenate([g128] * (wdt // LANES), axis=1)
        return t * lax.rsqrt(ms + EPS) * g

    def rotary(t, rs):
        if not rope:
            return t
        partner = jnp.where(second_half, pltpu.roll(t, 16, 1), pltpu.roll(t, LANES - 16, 1))
        return t * cos_ref[rs] + partner * sin_ref[rs]

    def spread(t, c0, rs, idle):
        tr = pltpu.roll(t, HEAD_DIM, 1)
        fill = jnp.full_like(t, idle)
        tiles = (jnp.where(lo, t, fill), jnp.where(lo, fill, tr), jnp.where(lo, tr, fill), jnp.where(lo, fill, t))
        for j, tile in enumerate(tiles):
            kv4_ref[rs, c0 + LANES * j:c0 + LANES * (j + 1)] = tile.astype(kv4_ref.dtype)

    assert (ATT_WIDTH, 2 * KV_WIDTH + FNET_WIDTH, 2 * GLA_WIDTH) == (4 * LANES,) * 3

    def project(rs, hb):
        group = lambda g: _dot(hb, wm_ref[:, 4 * LANES * g:4 * LANES * (g + 1)])
        gz = _dot(hb, wz_ref[...])
        a_q = group(0)
        z = _dot(gz.astype(BF16), wg_ref[...]) + bg
        log_sig = jnp.minimum(z, 0.0) - jnp.log(1.0 + jnp.exp(-jnp.abs(z)))
        lf_ref[rs, 0:2 * GLA_WIDTH] = log_sig * (LOG2E / GLA_TAU)
        a_kvf = group(1)
        for j2 in range(ATT_WIDTH // (2 * LANES)):
            qn = head_norm(a_q[:, 2 * LANES * j2:2 * LANES * (j2 + 1)], qg)
            for j in range(2):
                t = rotary(qn[:, LANES * j:LANES * (j + 1)], rs) * (ATT_SCALE * LOG2E)
                q_ref[rs, LANES * (2 * j2 + j):LANES * (2 * j2 + j + 1)] = t.astype(q_ref.dtype)
        a_qk = group(2)
        spread(rotary(head_norm(a_kvf[:, 0:KV_WIDTH], kg), rs), 0, rs, 0.0)
        spread(a_kvf[:, KV_WIDTH:2 * KV_WIDTH], 4 * LANES, rs, 1.0)
        lf_ref[rs, 2 * GLA_WIDTH:] = a_kvf[:, 2 * KV_WIDTH:]
        a_vr = group(3)
        g4_ref[rs, 0:GLA_WIDTH] = (a_qk[:, 0:GLA_WIDTH] * GLA_SCALE).astype(g4_ref.dtype)
        g4_ref[rs, GLA_WIDTH:2 * GLA_WIDTH] = a_qk[:, GLA_WIDTH:].astype(g4_ref.dtype)
        g4_ref[rs, 2 * GLA_WIDTH:4 * GLA_WIDTH] = a_vr.astype(g4_ref.dtype)

    hbs = [normed(rs) for rs in subs]
    for rs, hb in zip(subs, hbs):
        project(rs, hb)


def _inproj(x, mod3, mod_row, lw, rope_tabs, tm):
    b, n, d = x.shape
    rope = rope_tabs is not None
    tok = lambda w: pl.BlockSpec((None, tm, w), lambda bi, i: (bi, i, 0))
    lb = lambda shape: _layer_block(shape, lw["layer"])
    vec = lambda name: _full(lw[name].shape)
    in_specs = [
        tok(d),
        lb((MOD_ROWS, 6 * d)),
        vec("g1"), lb((d, MAIN_WIDTH)), lb((d, LANES)), lb((LANES, 2 * GLA_WIDTH)),
        vec("b_gate"), vec("q_g"), vec("k_g"), _full((2 * LANES, 2 * LANES)),
    ]
    args = [x, mod3, lw["g1"], lw["w_main"], lw["w_z"], lw["w_gate"], lw["b_gate"], lw["q_g"], lw["k_g"], lw["bd256"]]
    if rope:
        in_specs += [pl.BlockSpec((tm, LANES), lambda bi, i: (i, 0))] * 2
        args += list(rope_tabs)
    sds = lambda w, dt: jax.ShapeDtypeStruct((b, n, w), dt)
    out_shape = [sds(ATT_WIDTH, BF16), sds(8 * LANES, BF16), sds(4 * GLA_WIDTH, BF16),
                 sds(2 * GLA_WIDTH + FNET_WIDTH, F32)]
    out_specs = [tok(s.shape[-1]) for s in out_shape]
    return pl.pallas_call(
        functools.partial(_inproj_kernel, rope=rope, layer=lw["layer"], mod_row=mod_row),
        out_shape=out_shape, grid=(b, n // tm), in_specs=in_specs, out_specs=out_specs,
        compiler_params=_params("parallel", "parallel"),
        name="inproj_rope" if rope else "inproj_ctx",
    )(*args)


def _scores(q2, key_tiles, masks):
    cols = []
    for k, tile_masks in zip(key_tiles, masks):
        s = lax.dot_general(q2, k, NT_DIMS, preferred_element_type=F32)
        for j, mk in enumerate(tile_masks):
            c = s[:, LANES * j:LANES * (j + 1)]
            cols.append(c if mk is None else jnp.where(mk, c, NEG_INF))
    return cols


def _softmax_pv(cols, val_tiles, sink_col):
    mx = cols[0]
    for c in cols[1:]:
        mx = jnp.maximum(mx, c)
    m = jnp.maximum(jnp.max(mx, axis=-1, keepdims=True), sink_col)
    p = jnp.concatenate([jnp.exp2(c - m).astype(BF16) for c in cols], axis=1)
    pv = _dot(p, jnp.concatenate(val_tiles, axis=0))
    den = pltpu.roll(pv, HEAD_DIM, 1) + jnp.exp2(sink_col - m)
    return pv / den


def _attn_kernel(*refs):
    nq = ATT_QBLOCKS
    sink_ref, q_ref = refs[0:2]
    kv_refs, kvc_ref, o_ref = refs[2:nq + 4], refs[nq + 4], refs[nq + 5]
    v_off = 4 * LANES
    i = pl.program_id(1)
    last = pl.num_programs(1) - 1
    blk = ATT_BLOCK
    group = ATT_HEADS // ATT_KV_HEADS
    row = lax.broadcasted_iota(jnp.int32, (2 * blk, blk), 0) & (blk - 1)
    col = lax.broadcasted_iota(jnp.int32, (2 * blk, blk), 1)
    band_l, band_r = col >= row, col <= row
    mask_l = [jnp.logical_and(band_l, i > 0) if s == 0 else band_l for s in range(nq)]
    mask_r = [jnp.logical_and(band_r, i < last) if s == nq - 1 else band_r for s in range(nq)]
    upper_rows = lax.broadcasted_iota(jnp.int32, (2 * blk, 1), 0) >= blk
    lo = lax.broadcasted_iota(jnp.int32, (1, LANES), 1) < HEAD_DIM

    def tiles(s, ks):
        l, m_, r = kv_refs[s:s + 3]
        return [jnp.concatenate([l[:, ks], m_[:, ks]], axis=0),
                jnp.concatenate([r[:, ks], kvc_ref[0:blk, ks]], axis=0), kvc_ref[blk:, ks]]

    slots = [(s, kvh, par) for s in range(nq) for kvh in range(ATT_KV_HEADS) for par in range(2)]
    lane_slice = lambda kvh, par, off=0: slice(off + 2 * LANES * kvh + LANES * par,
                                               off + 2 * LANES * kvh + LANES * (par + 1))
    cols = {}
    for s, kvh, par in slots:
        qs, rs = 2 * LANES * kvh, slice(blk * s, blk * (s + 1))
        q2 = jnp.concatenate([q_ref[rs, qs:qs + LANES], q_ref[rs, qs + LANES:qs + 2 * LANES]], axis=0)
        cols[s, kvh, par] = _scores(q2, tiles(s, lane_slice(kvh, par)),
                                    [(mask_l[s], None), (mask_r[s], None), (None,)])
    outs = {}
    for s, kvh, par in slots:
        base = group * kvh + par
        sink_col = jnp.where(upper_rows, sink_ref[base + 2], sink_ref[base]) * LOG2E
        outs[s, kvh, par] = _softmax_pv(cols[s, kvh, par], tiles(s, lane_slice(kvh, par, v_off)), sink_col)
    for s in range(nq):
        for kvh in range(ATT_KV_HEADS):
            qs, r0 = 2 * LANES * kvh, blk * s
            o = jnp.where(lo, outs[s, kvh, 0], outs[s, kvh, 1])
            o_ref[r0:r0 + blk, qs:qs + LANES] = o[0:blk].astype(o_ref.dtype)
            o_ref[r0:r0 + blk, qs + LANES:qs + 2 * LANES] = o[blk:2 * blk].astype(o_ref.dtype)


def _attention(q, kv4, ckv4, sink):
    b, n, _ = q.shape
    lc = ckv4.shape[1]
    nq = ATT_QBLOCKS
    nb = n // ATT_BLOCK
    assert lc == 2 * ATT_BLOCK and nb % nq == 0
    qspec = pl.BlockSpec((None, nq * ATT_BLOCK, 4 * LANES), lambda bi, i: (bi, i, 0))
    kv = [pl.BlockSpec((None, ATT_BLOCK, 8 * LANES),
                       functools.partial(lambda bi, i, j: (bi, jnp.clip(nq * i + j - 1, 0, nb - 1), 0), j=j))
          for j in range(nq + 2)]
    ctx = pl.BlockSpec((None, lc, 8 * LANES), lambda bi, i: (bi, 0, 0))
    return pl.pallas_call(
        _attn_kernel,
        out_shape=jax.ShapeDtypeStruct((b, n, ATT_WIDTH), BF16),
        grid=(b, nb // nq),
        in_specs=[pl.BlockSpec(memory_space=pltpu.SMEM), qspec] + kv + [ctx],
        out_specs=qspec,
        compiler_params=_params("parallel", "parallel"),
        name="window_attention",
    )(sink, q, *([kv4] * (nq + 2)), ckv4)


def _attn_ctx_kernel(sink_ref, q_ref, kc_ref, vc_ref, o_ref):
    kvh = pl.program_id(1)
    lc = q_ref.shape[0]
    q2 = jnp.concatenate([q_ref[:, 0:LANES], q_ref[:, LANES:2 * LANES]], axis=0)
    upper_rows = lax.broadcasted_iota(jnp.int32, (2 * lc, 1), 0) >= lc
    lo = lax.broadcasted_iota(jnp.int32, (1, LANES), 1) < HEAD_DIM
    outs = []
    for par in range(2):
        ks = slice(LANES * par, LANES * (par + 1))
        base = ATT_HEADS // ATT_KV_HEADS * kvh + par
        sink_col = jnp.where(upper_rows, sink_ref[base + 2], sink_ref[base]) * LOG2E
        cols = _scores(q2, [kc_ref[:, ks]], [(None,) * (lc // LANES)])
        outs.append(_softmax_pv(cols, [vc_ref[:, ks]], sink_col))
    o = jnp.where(lo, outs[0], outs[1])
    o_ref[:, 0:LANES] = o[0:lc].astype(o_ref.dtype)
    o_ref[:, LANES:2 * LANES] = o[lc:2 * lc].astype(o_ref.dtype)


def _attention_ctx(cq, ckv4, sink):
    b, lc, _ = cq.shape
    gw = 2 * LANES
    spec = pl.BlockSpec((None, lc, gw), lambda bi, h: (bi, 0, h))
    vspec = pl.BlockSpec((None, lc, gw), lambda bi, h: (bi, 0, ATT_KV_HEADS + h))
    return pl.pallas_call(
        _attn_ctx_kernel,
        out_shape=jax.ShapeDtypeStruct((b, lc, ATT_WIDTH), BF16),
        grid=(b, ATT_KV_HEADS),
        in_specs=[pl.BlockSpec(memory_space=pltpu.SMEM), spec, spec, vspec],
        out_specs=spec,
        compiler_params=_params("parallel", "parallel"),
        name="context_attention",
    )(sink, cq, ckv4, ckv4)


def _dft_consts(n):
    n1, n2 = DFT_N1, n // DFT_N1
    ang = lambda a, m: 2.0 * np.pi * np.outer(np.arange(a), np.arange(a)) / m
    c = HEAD_DIM
    cc, sc = np.cos(ang(c, c)) / np.sqrt(c), np.sin(ang(c, c)) / np.sqrt(c)
    eye = np.eye(FNET_GROUPS)
    w_chan = np.concatenate([np.kron(eye, cc), -np.kron(eye, sc)], axis=1)
    c1, s1 = np.cos(ang(n1, n1)) / np.sqrt(n1), np.sin(ang(n1, n1)) / np.sqrt(n1)
    m1 = np.block([[c1, s1], [-s1, c1]])
    kk = np.arange(n1)[:, None, None] + n1 * np.arange(n2)[None, :, None]
    ph = 2.0 * np.pi * (kk * np.arange(n2)[None, None, :] % n) / n
    m3 = np.concatenate([np.cos(ph), np.sin(ph)], axis=2) / np.sqrt(n2)
    return tuple(jnp.asarray(a, F32) for a in (w_chan, m1, m3))


def _dft_ctx_consts(lc):
    ang = lambda a, m: 2.0 * np.pi * np.outer(np.arange(a), np.arange(a)) / m
    c = HEAD_DIM
    cc, sc = np.cos(ang(c, c)) / np.sqrt(c), np.sin(ang(c, c)) / np.sqrt(c)
    eye = np.eye(FNET_GROUPS)
    w_chan = np.concatenate([np.kron(eye, cc), -np.kron(eye, sc)], axis=1)
    cl, sl = np.cos(ang(lc, lc)) / np.sqrt(lc), np.sin(ang(lc, lc)) / np.sqrt(lc)
    return jnp.asarray(w_chan, F32), jnp.asarray(np.concatenate([cl, sl], axis=1), F32)


def _regroup_rows(x, outer):
    r, c = x.shape
    return jnp.swapaxes(x.reshape(outer, r // outer, c), 0, 1).reshape(r, c)


def _fourier_kernel(u_ref, wc_ref, m1_ref, m3_ref, wf_ref, y_ref, p_ref):
    w, t = FNET_WIDTH, DFT_STEP
    s = pl.program_id(1)
    n_a = p_ref.shape[1]
    n2 = n_a * t

    @pl.when(s < n_a)
    def _():
        u = u_ref[...].reshape(DFT_N1 * t, w).astype(BF16)
        z = _regroup_rows(_dot(u, wc_ref[...]), DFT_N1)
        a_re, a_im = [], []
        for tt in range(t):
            zt = z[DFT_N1 * tt:DFT_N1 * (tt + 1)]
            zs = jnp.concatenate([zt[:, 0:w], zt[:, w:2 * w]], axis=0).astype(BF16)
            a = _dot(m1_ref[...], zs)
            a_re.append(a[0:DFT_N1])
            a_im.append(a[DFT_N1:2 * DFT_N1])
        for part, rows in enumerate((a_re, a_im)):
            a = _regroup_rows(jnp.concatenate(rows, axis=0), t)
            p_ref[part, s] = a.reshape(DFT_N1, t, w)

    @pl.when(s >= n_a)
    def _():
        k1_0 = (s - n_a) * t
        ys = []
        for kk in range(t):
            parts = [p_ref[part, :, k1_0 + kk].reshape(n2, w) for part in range(2)]
            rhs = jnp.concatenate(parts, axis=0).astype(BF16)
            ys.append(_dot(m3_ref[kk].astype(BF16), rhs).astype(BF16))
        out = _dot(jnp.concatenate(ys, axis=0), wf_ref[...])
        y_ref[...] = _regroup_rows(out, t).reshape(n2, t, w)


def _fourier(lf, wf_bd, layer, consts):
    b, n, wl = lf.shape
    w = FNET_WIDTH
    n1, n2, t = DFT_N1, n // DFT_N1, DFT_STEP
    n_a, n_b = n2 // t, n1 // t
    w_chan, m1, m3 = consts
    y = pl.pallas_call(
        _fourier_kernel,
        out_shape=jax.ShapeDtypeStruct((b, n2, n1, w), F32),
        grid=(b, n_a + n_b),
        in_specs=[pl.BlockSpec((None, n1, t, w), lambda bi, s: (bi, 0, jnp.minimum(s, n_a - 1), FU_BLOCK)),
                  _full((w, 2 * w)), _full((2 * n1, 2 * n1)),
                  pl.BlockSpec((t, n2, 2 * n2), lambda bi, s: (jnp.maximum(s - n_a, 0), 0, 0)),
                  _layer_block((w, w), layer)],
        out_specs=pl.BlockSpec((None, n2, t, w), lambda bi, s: (bi, 0, jnp.maximum(s - n_a, 0), 0)),
        scratch_shapes=[pltpu.VMEM((2, n_a, n1, t, w), F32)],
        compiler_params=_params("parallel", "arbitrary"),
        name="fourier_mix",
    )(lf.reshape(b, n1, n2, wl), w_chan.astype(BF16), m1.astype(BF16), m3, wf_bd)
    return y.reshape(b, n, w)


def _four_ctx_kernel(u_ref, wc_ref, m_ref, wf_ref, y_ref):
    w = FNET_WIDTH
    z = _dot(u_ref[...].astype(BF16), wc_ref[...])
    zs = jnp.concatenate([z[:, 0:w], z[:, w:2 * w]], axis=0).astype(BF16)
    y = _dot(m_ref[...], zs)
    y_ref[...] = _dot(y.astype(BF16), wf_ref[...]).astype(y_ref.dtype)


def _fourier_ctx(clf, wf_bd, layer, consts):
    b, lc, _ = clf.shape
    w = FNET_WIDTH
    w_chan, m = consts[0].astype(BF16), consts[1].astype(BF16)
    return pl.pallas_call(
        _four_ctx_kernel,
        out_shape=jax.ShapeDtypeStruct((b, lc, w), BF16),
        grid=(b,),
        in_specs=[pl.BlockSpec((None, lc, w), lambda bi: (bi, 0, FU_BLOCK)), _full((w, 2 * w)), _full((lc, 2 * lc)),
                  _layer_block((w, w), layer)],
        out_specs=pl.BlockSpec((None, lc, w), lambda bi: (bi, 0, 0)),
        compiler_params=_params("parallel"),
        name="fourier_context",
    )(clf, w_chan, m, wf_bd)


def _pair_stack(x, lo):
    zero = jnp.zeros_like(x)
    return jnp.concatenate([jnp.where(lo, x, zero), jnp.where(lo, zero, x)], axis=0)


def _gla_kernel(*refs, ncast):
    qkvf_ref, laf_ref, qkvb_ref, lab_ref, s0_ref = refs[0:5]
    of_ref, ob_ref, sfin_ref = refs[5 + ncast:8 + ncast]
    stf_ref, stb_ref = refs[8 + 2 * ncast:10 + 2 * ncast]
    for src, dst in zip(refs[5:5 + ncast], refs[8 + ncast:8 + 2 * ncast]):
        dst[...] = src[...].astype(dst.dtype)
    i = pl.program_id(1)
    c = GLA_CHUNK
    nchunk = qkvf_ref.shape[0] // c

    @pl.when(i == 0)
    def _():
        stf_ref[...] = s0_ref[0]
        stb_ref[...] = s0_ref[1]

    r64 = lax.broadcasted_iota(jnp.int32, (c, c), 0)
    c64 = lax.broadcasted_iota(jnp.int32, (c, c), 1)
    tri = ((c64 <= r64).astype(BF16), (c64 >= r64).astype(BF16))
    at = lax.broadcasted_iota(jnp.int32, (c, LANES), 0)
    as_ = lax.broadcasted_iota(jnp.int32, (c, LANES), 1) & (c - 1)
    att_mask = (as_ <= at, as_ >= at)
    lo = lax.broadcasted_iota(jnp.int32, (1, LANES), 1) < HEAD_DIM
    br = lax.broadcasted_iota(jnp.int32, (LANES, LANES), 0) < HEAD_DIM
    bc = lax.broadcasted_iota(jnp.int32, (LANES, LANES), 1) < HEAD_DIM
    bd_mask = br == bc
    pairs = [slice(LANES * p, LANES * (p + 1)) for p in range(GLA_WIDTH // LANES)]
    in_refs = ((qkvf_ref, laf_ref), (qkvb_ref, lab_ref))
    out_refs = (of_ref, ob_ref)
    items = [(d, step if d == 0 else nchunk - 1 - step) for step in range(nchunk) for d in range(2)]
    rows = lambda ch: slice(c * ch, c * (ch + 1))

    bcum, work, o_intra, ut, decay = {}, {}, {}, {}, {}
    st_refs = (stf_ref, stb_ref)
    st = [[st_refs[d][LANES * j:LANES * (j + 1)] for j in range(len(pairs))] for d in range(2)]

    def gate_sums(d, ch):
        la = in_refs[d][1][rows(ch)]
        la_hi = la.astype(BF16)
        la_lo = (la - la_hi.astype(F32)).astype(BF16)
        bcum[d, ch] = _dot(jnp.concatenate([tri[d], tri[d]], axis=1), jnp.concatenate([la_hi, la_lo], axis=0))

    def scores(d, ch):
        qkv_ref = in_refs[d][0]
        w = GLA_WIDTH
        b = bcum[d, ch]
        btot = b[0:1] if d == 1 else b[c - 1:c]
        k = qkv_ref[rows(ch), w:2 * w].astype(F32)
        q_in = (qkv_ref[rows(ch), 0:w].astype(F32) * jnp.exp2(b)).astype(BF16)
        k_in = (k * jnp.exp2(-b)).astype(BF16)
        k_out = (k * jnp.exp2(btot - b)).astype(BF16)
        vb = qkv_ref[rows(ch), 2 * w:3 * w]
        att = [lax.dot_general(q_in[:, p], _pair_stack(k_in[:, p], lo), NT_DIMS, preferred_element_type=F32)
               for p in pairs]
        work[d, ch] = (btot, q_in, k_out, vb, att)

    def intra(d, ch):
        btot, q_in, k_out, vb, att = work[d, ch]
        o_intra[d, ch] = [_dot(jnp.where(att_mask[d], a, 0.0).astype(BF16), _pair_stack(vb[:, p], lo))
                          for a, p in zip(att, pairs)]
        ut[d, ch] = [lax.dot_general(k_out[:, p], vb[:, p], TN_DIMS, preferred_element_type=F32) for p in pairs]
        decay[d, ch] = [jnp.transpose(jnp.exp2(btot[:, p])) for p in pairs]

    def carry(d, ch):
        btot, q_in = work[d, ch][0:2]
        outs = []
        for j, p in enumerate(pairs):
            outs.append(o_intra[d, ch][j] + _dot(q_in[:, p], st[d][j].astype(BF16)))
            st[d][j] = st[d][j] * decay[d, ch][j] + jnp.where(bd_mask, ut[d, ch][j], 0.0)
        out_refs[d][rows(ch)] = jnp.concatenate(outs, axis=1).astype(out_refs[d].dtype)

    groups = [items[g:g + GLA_GROUP] for g in range(0, len(items), GLA_GROUP)]
    for it in groups[0]:
        gate_sums(*it)
    for it in groups[0]:
        scores(*it)
    for g, grp in enumerate(groups):
        for it in grp:
            intra(*it)
        nxt = groups[g + 1] if g + 1 < len(groups) else []
        for it in nxt:
            gate_sums(*it)
        for k_ in range(len(grp)):
            carry(*grp[k_])
            if k_ < len(nxt):
                scores(*nxt[k_])
    for d in range(2):
        st_refs[d][...] = jnp.concatenate(st[d], axis=0)

    @pl.when(i == pl.num_programs(1) - 1)
    def _():
        for d in range(2):
            sfin_ref[d] = jnp.concatenate(st[d], axis=0)


def _gla(g4, lf, s0, tb, cast=(), cast_layer=0):
    b, n, _ = g4.shape
    w = GLA_WIDTH
    nblk = n // tb
    fwd = lambda bi, i: (bi, i, 0)
    bwd = lambda bi, i: (bi, nblk - 1 - i, 0)
    bwd_la = lambda bi, i: (bi, nblk - 1 - i, 1)
    tok = lambda f: pl.BlockSpec((None, tb, w), f)
    qkv = lambda f: pl.BlockSpec((None, tb, 3 * w), f)
    state = pl.BlockSpec((None, 2, w, LANES), lambda bi, i: (bi, 0, 0, 0))
    cast_in, cast_out, cast_shapes = _slab_cast_specs(cast, cast_layer, (b, nblk))
    res = pl.pallas_call(
        functools.partial(_gla_kernel, ncast=len(cast)),
        out_shape=[jax.ShapeDtypeStruct((b, n, w), BF16), jax.ShapeDtypeStruct((b, n, w), BF16),
                   jax.ShapeDtypeStruct((b, 2, w, LANES), F32)] + cast_shapes,
        grid=(b, nblk),
        in_specs=[qkv(fwd), tok(fwd), qkv(bwd), tok(bwd_la), state] + cast_in,
        out_specs=[tok(fwd), tok(bwd), state] + cast_out,
        scratch_shapes=[pltpu.VMEM((w, LANES), F32), pltpu.VMEM((w, LANES), F32)],
        compiler_params=_params("arbitrary", "arbitrary"),
        name="gla_scan",
    )(g4, lf, g4, lf, s0, *cast)
    return res[0], res[1], res[2], res[3:]


def _tail_kernel(*refs, ncast, layer, mod_row):
    (att_ref, four_ref, of_ref, ob_ref, r_ref, x_ref, mod_ref, gg_ref, bd_ref, wmix_ref, g2_ref, wi_ref,
     wo_ref) = refs[0:13]
    o_ref, a_ref = refs[13 + ncast], refs[14 + 2 * ncast]
    for src, dst in zip(refs[13:13 + ncast], refs[14 + ncast:14 + 2 * ncast]):
        dst[...] = src[...].astype(dst.dtype)
    d = D_MODEL
    gla_g, g2 = gg_ref[layer:layer + 1, :], g2_ref[layer:layer + 1, :]
    mod_vec = mod_ref[pl.ds(pl.program_id(0) if mod_row is None else mod_row, 1), :]
    mod = lambda j: mod_vec[:, j * d:(j + 1) * d]
    a0, a1, a2 = ATT_WIDTH, ATT_WIDTH + FNET_WIDTH, ATT_WIDTH + FNET_WIDTH + GLA_WIDTH
    sub = min(x_ref.shape[0], TAIL_SUB)
    subs = [slice(sub * j, sub * (j + 1)) for j in range(x_ref.shape[0] // sub)]

    def mixed(rs):
        o = of_ref[rs].astype(F32) + ob_ref[rs].astype(F32)
        y = o * lax.rsqrt(_group_mean(o, bd_ref) + EPS) * gla_g
        y = y * _silu(r_ref[rs].astype(F32))
        mix = (_dot(att_ref[rs], wmix_ref[0:a0]) + _dot(four_ref[rs].astype(BF16), wmix_ref[a0:a1])
               + _dot(y.astype(BF16), wmix_ref[a1:a2]))
        x = x_ref[rs] + mod(2) * mix
        ms = jnp.mean(x * x, axis=-1, keepdims=True)
        return x, ((x * lax.rsqrt(ms + EPS) * g2) * (1.0 + mod(4)) + mod(3)).astype(BF16)

    pre = [mixed(rs) for rs in subs]
    for rs, (x, hb) in zip(subs, pre):
        for c0 in range(0, FFN_HIDDEN, FFN_CHUNK):
            g = _dot(hb, wi_ref[:, c0:c0 + FFN_CHUNK])
            u = _dot(hb, wi_ref[:, FFN_HIDDEN + c0:FFN_HIDDEN + c0 + FFN_CHUNK])
            a_ref[rs, c0:c0 + FFN_CHUNK] = (_silu(g) * u).astype(BF16)
        o_ref[rs] = x + mod(5) * _dot(a_ref[rs], wo_ref[...])


def _tail(att, four, of, ob, g4, x, mod3, mod_row, lw, tm, cast_next=()):
    b, n, d = x.shape
    steps = n // tm
    tok = lambda w: pl.BlockSpec((None, tm, w), lambda bi, i: (bi, i, 0))
    gate = pl.BlockSpec((None, tm, GLA_WIDTH), lambda bi, i: (bi, i, 3))
    vec = lambda name: _full(lw[name].shape)
    big = lambda shape: _layer_block(shape, lw["big_layer"])
    cast_in, cast_out, cast_shapes = _slab_cast_specs(cast_next, lw["layer"] + 1, (b, steps))
    res = pl.pallas_call(
        functools.partial(_tail_kernel, ncast=len(cast_next), layer=lw["layer"], mod_row=mod_row),
        out_shape=[jax.ShapeDtypeStruct((b, n, d), F32)] + cast_shapes,
        grid=(b, steps),
        in_specs=[tok(ATT_WIDTH), tok(FNET_WIDTH), tok(GLA_WIDTH), tok(GLA_WIDTH), gate, tok(d),
                  _layer_block((MOD_ROWS, 6 * d), lw["layer"]),
                  vec("gla_g"), _full((GLA_WIDTH, GLA_WIDTH)), big((d, d)),
                  vec("g2"), big((d, 2 * FFN_HIDDEN)), big((FFN_HIDDEN, d))] + cast_in,
        out_specs=[tok(d)] + cast_out,
        scratch_shapes=[pltpu.VMEM((tm, FFN_HIDDEN), BF16)],
        compiler_params=_params("arbitrary", "arbitrary", vmem_limit=TAIL_VMEM_LIMIT),
        name="mix_ffn",
    )(att, four, of, ob, g4, x, mod3, lw["gla_g"], lw["bd256"], lw["w_out"], lw["g2"], lw["w_ffn_in"],
      lw["w_ffn_out"], *cast_next)
    return res[0], res[1:]


def _rope_tables(n):
    axis_dim = HEAD_DIM // 2
    inv_freq = ROPE_BASE ** (-np.arange(0, axis_dim, 2, dtype=np.float64) / axis_dim)
    t = np.arange(n)
    ang_r = (t // GRID_W)[:, None] * inv_freq[None, :]
    ang_c = (t % GRID_W)[:, None] * inv_freq[None, :]
    cos = np.concatenate([np.cos(ang_r)] * 2 + [np.cos(ang_c)] * 2, axis=1)
    sin = np.concatenate([-np.sin(ang_r), np.sin(ang_r), -np.sin(ang_c), np.sin(ang_c)], axis=1)
    return jnp.asarray(np.tile(cos, (1, 2)), F32), jnp.asarray(np.tile(sin, (1, 2)), F32)


def _block_diag_mean(width):
    g = np.arange(width) // HEAD_DIM
    return jnp.asarray((g[:, None] == g[None, :]) / HEAD_DIM, BF16)


def _prepare_weights(w_in, g_norm1, q_norm_g, k_norm_g, w_fourier, wgf, bgf, wgb, bgb, gla_norm_g, g_norm2):
    depth = w_in.shape[0]
    r = GLA_GATE_RANK
    zr = jnp.zeros((depth, r, GLA_WIDTH), F32)
    w_gate = jnp.concatenate([jnp.concatenate([wgf, zr], axis=2), jnp.concatenate([zr, wgb], axis=2)], axis=1)
    eye = jnp.eye(FNET_GROUPS, dtype=F32)
    wf_bd = jnp.einsum("lgij,gh->lgihj", w_fourier, eye).reshape(depth, FNET_WIDTH, FNET_WIDTH)
    return {
        "g1": g_norm1,
        "w_main": w_in.astype(BF16),
        "w_z": jnp.pad(w_in[:, :, MAIN_WIDTH:], ((0, 0), (0, 0), (0, LANES - 2 * r))).astype(BF16),
        "w_gate": jnp.pad(w_gate, ((0, 0), (0, LANES - 2 * r), (0, 0))).astype(BF16),
        "b_gate": jnp.concatenate([bgf, bgb], axis=1),
        "q_g": jnp.tile(q_norm_g, (1, 2)),
        "k_g": jnp.tile(k_norm_g, (1, 2)),
        "bd256": _block_diag_mean(GLA_WIDTH),
        "wf_bd": wf_bd.astype(BF16),
        "gla_g": jnp.tile(gla_norm_g, (1, GLA_HEADS)),
        "g2": g_norm2,
    }


def kernel(x, c, ctx, c_ctx, w_mod, b_mod, g_norm1, w_in, q_norm_g, k_norm_g, attn_sink, w_fourier, gla_w_gate_f,
           gla_b_gate_f, gla_w_gate_b, gla_b_gate_b, gla_norm_g, w_out, g_norm2, w_ffn_in, w_ffn_out):
    b, n, d = x.shape
    lc = ctx.shape[1]
    depth = w_mod.shape[0]
    assert d == D_MODEL and b < MOD_ROWS and lc % GLA_CHUNK == 0
    assert n % TOKEN_BLOCK == 0 and n % GLA_BLOCK == 0 and n % (DFT_N1 * DFT_STEP) == 0

    cc = jnp.concatenate([c, c_ctx[None, :], jnp.zeros((MOD_ROWS - b - 1, d), F32)], axis=0)
    mod3 = _modulation(cc, w_mod, b_mod)
    rope_tabs = _rope_tables(n)
    dft = _dft_consts(n)
    dft_ctx = _dft_ctx_consts(lc)
    flat = lambda t: t.reshape(1, b * lc, t.shape[-1])
    unflat = lambda t: t.reshape(b, lc, t.shape[-1])
    xc = ctx
    weights = _prepare_weights(w_in, g_norm1, q_norm_g, k_norm_g, w_fourier, gla_w_gate_f, gla_b_gate_f,
                               gla_w_gate_b, gla_b_gate_b, gla_norm_g, g_norm2)
    s_zero = jnp.zeros((b, 2, GLA_WIDTH, LANES), F32)
    tail_names, tail_f32, tail_bf16 = ("w_out", "w_ffn_in", "w_ffn_out"), (w_out, w_ffn_in, w_ffn_out), ()
    for l in range(depth):
        need_ctx = l < depth - 1
        lw = dict(weights, layer=l, big_layer=0)
        sink = attn_sink[l]
        cq, ckv4, cg4, clf = map(unflat, _inproj(flat(xc), mod3, b, lw, None, b * lc))
        q, kv4, g4, lf = _inproj(x, mod3, None, lw, rope_tabs, TOKEN_BLOCK)
        att = _attention(q, kv4, ckv4, sink)
        four = _fourier(lf, lw["wf_bd"], l, dft)
        ocf, ocb, s_ctx, _ = _gla(cg4, clf, s_zero, lc)
        olf, olb, _, cast_now = _gla(g4, lf, s_ctx, GLA_BLOCK, () if tail_bf16 else tail_f32, l)
        lw.update(zip(tail_names, tail_bf16 or cast_now))
        x, tail_bf16 = _tail(att, four, olf, olb, g4, x, mod3, None, lw, TOKEN_BLOCK, tail_f32 if need_ctx else ())
        if need_ctx:
            att_c = _attention_ctx(cq, ckv4, sink)
            four_c = _fourier_ctx(clf, lw["wf_bd"], l, dft_ctx)
            xc = unflat(_tail(*map(flat, (att_c, four_c, ocf, ocb, cg4, xc)), mod3, b, lw, b * lc)[0])
    return x
```

```python
import functools

import numpy as np
import jax
import jax.numpy as jnp
from jax import lax
from jax.experimental import pallas as pl
from jax.experimental.pallas import tpu as pltpu

F32 = jnp.float32
BF16 = jnp.bfloat16

D_MODEL = 1024
HEAD_DIM = 64
GRID_W = 64
ROPE_BASE = 10000.0
ATT_HEADS = 8
ATT_KV_HEADS = 2
ATT_WIDTH = ATT_HEADS * HEAD_DIM
KV_WIDTH = ATT_KV_HEADS * HEAD_DIM
ATT_BLOCK = 128
ATT_QBLOCKS = 8
ATT_SCALE = HEAD_DIM ** -0.5
LOG2E = 1.4426950408889634
NEG_INF = -1e30
FNET_GROUPS = 4
FNET_WIDTH = FNET_GROUPS * HEAD_DIM
GLA_HEADS = 4
GLA_WIDTH = GLA_HEADS * HEAD_DIM
GLA_GATE_RANK = 16
GLA_TAU = 16.0
GLA_CHUNK = 64
GLA_SCALE = HEAD_DIM ** -0.5
MAIN_WIDTH = ATT_WIDTH + 2 * KV_WIDTH + FNET_WIDTH + 4 * GLA_WIDTH
FFN_HIDDEN = 2816
FFN_CHUNK = 256
INPROJ_SUB = 512
TAIL_SUB = 512
FU_BLOCK = 2
GLA_GROUP = 16
TOKEN_BLOCK = 1024
GLA_BLOCK = 2048
MOD_TILE = 1536
EPS = 1e-6
LANES = 128
MOD_ROWS = 8
DFT_N1 = 128
DFT_STEP = 32
VMEM_LIMIT = 56 * 1024 * 1024
TAIL_VMEM_LIMIT = 61 * 1024 * 1024

NT_DIMS = (((1,), (1,)), ((), ()))
TN_DIMS = (((0,), (0,)), ((), ()))


def _params(*sem, vmem_limit=VMEM_LIMIT):
    return pltpu.CompilerParams(dimension_semantics=sem, vmem_limit_bytes=vmem_limit)


def _dot(a, b):
    return jnp.dot(a, b, preferred_element_type=F32)


def _silu(x):
    return x / (1.0 + jnp.exp(-x))


def _layer_block(shape, layer):
    nd = len(shape)
    return pl.BlockSpec((None,) + tuple(shape), lambda *_: (layer,) + (0,) * nd, pipeline_mode=pl.Buffered(1))


def _slab_cast_specs(weights, layer, grid):
    steps = grid[0] * grid[1]
    assert all(w.shape[1] % (16 * steps) == 0 for w in weights)
    spec = lambda w, lyr: pl.BlockSpec((None, w.shape[1] // steps, w.shape[2]),
                                       lambda i, j: (lyr, i * grid[1] + j, 0))
    return ([spec(w, layer) for w in weights], [spec(w, 0) for w in weights],
            [jax.ShapeDtypeStruct((1,) + w.shape[1:], BF16) for w in weights])


def _full(shape):
    nd = len(shape)
    return pl.BlockSpec(shape, lambda *_: (0,) * nd)


def _group_mean(t, bd_ref):
    return _dot((t * t).astype(BF16), bd_ref[...])


def _mod_kernel(c_ref, w_ref, b_ref, o_ref):
    s = _silu(c_ref[...]).astype(BF16)
    o_ref[...] = _dot(s, w_ref[...].astype(BF16)) + b_ref[pl.ds(pl.program_id(0), 1), :]


def _modulation(cc, w_mod, b_mod):
    depth, d, width = w_mod.shape
    tn = MOD_TILE
    return pl.pallas_call(
        _mod_kernel,
        out_shape=jax.ShapeDtypeStruct((depth, MOD_ROWS, width), F32),
        grid=(depth, width // tn),
        in_specs=[
            _full((MOD_ROWS, d)),
            pl.BlockSpec((None, d, tn), lambda l, j: (l, 0, j)),
            pl.BlockSpec((depth, tn), lambda l, j: (0, j)),
        ],
        out_specs=pl.BlockSpec((None, MOD_ROWS, tn), lambda l, j: (l, 0, j)),
        compiler_params=_params("parallel", "parallel"),
        name="modulation",
    )(cc, w_mod, b_mod)


def _inproj_kernel(*refs, rope, layer, mod_row):
    if rope:
        (x_ref, mod_ref, g1_ref, wm_ref, wz_ref, wg_ref, bg_ref, qg_ref, kg_ref, bd_ref, cos_ref, sin_ref,
         q_ref, kv4_ref, g4_ref, lf_ref) = refs
    else:
        (x_ref, mod_ref, g1_ref, wm_ref, wz_ref, wg_ref, bg_ref, qg_ref, kg_ref, bd_ref,
         q_ref, kv4_ref, g4_ref, lf_ref) = refs
    d = D_MODEL
    pick = lambda ref: ref[layer:layer + 1, :]
    g1, bg, qg, kg = pick(g1_ref), pick(bg_ref), pick(qg_ref), pick(kg_ref)
    mod = mod_ref[pl.ds(pl.program_id(0) if mod_row is None else mod_row, 1), :]
    sh, sc = mod[:, 0:d], mod[:, d:2 * d]
    sub = min(x_ref.shape[0], INPROJ_SUB)
    subs = [slice(sub * j, sub * (j + 1)) for j in range(x_ref.shape[0] // sub)]

    def normed(rs):
        x = x_ref[rs]
        ms = jnp.mean(x * x, axis=-1, keepdims=True)
        return ((x * lax.rsqrt(ms + EPS) * g1) * (1.0 + sc) + sh).astype(BF16)

    lane = lax.broadcasted_iota(jnp.int32, (1, LANES), 1)
    lo = lane < HEAD_DIM
    second_half = (lane & 16) != 0

    def head_norm(t, g128):
        wdt = t.shape[1]
        ms = _dot((t * t).astype(BF16), bd_ref[0:wdt, 0:wdt])
        g = g128 if wdt == LANES else jnp.concatenate([g128] * (wdt // LANES), axis=1)
        return t * lax.rsqrt(ms + EPS) * g

    def rotary(t, rs):
        if not rope:
            return t
        partner = jnp.where(second_half, pltpu.roll(t, 16, 1), pltpu.roll(t, LANES - 16, 1))
        return t * cos_ref[rs] + partner * sin_ref[rs]

    def spread(t, c0, rs, idle):
        tr = pltpu.roll(t, HEAD_DIM, 1)
        fill = jnp.full_like(t, idle)
        tiles = (jnp.where(lo, t, fill), jnp.where(lo, fill, tr), jnp.where(lo, tr, fill), jnp.where(lo, fill, t))
        for j, tile in enumerate(tiles):
            kv4_ref[rs, c0 + LANES * j:c0 + LANES * (j + 1)] = tile.astype(kv4_ref.dtype)

    assert (ATT_WIDTH, 2 * KV_WIDTH + FNET_WIDTH, 2 * GLA_WIDTH) == (4 * LANES,) * 3

    def project(rs, hb):
        group = lambda g: _dot(hb, wm_ref[:, 4 * LANES * g:4 * LANES * (g + 1)])
        gz = _dot(hb, wz_ref[...])
        a_q = group(0)
        z = _dot(gz.astype(BF16), wg_ref[...]) + bg
        log_sig = jnp.minimum(z, 0.0) - jnp.log(1.0 + jnp.exp(-jnp.abs(z)))
        lf_ref[rs, 0:2 * GLA_WIDTH] = log_sig * (LOG2E / GLA_TAU)
        a_kvf = group(1)
        for j2 in range(ATT_WIDTH // (2 * LANES)):
            qn = head_norm(a_q[:, 2 * LANES * j2:2 * LANES * (j2 + 1)], qg)
            for j in range(2):
                t = rotary(qn[:, LANES * j:LANES * (j + 1)], rs) * (ATT_SCALE * LOG2E)
                q_ref[rs, LANES * (2 * j2 + j):LANES * (2 * j2 + j + 1)] = t.astype(q_ref.dtype)
        a_qk = group(2)
        spread(rotary(head_norm(a_kvf[:, 0:KV_WIDTH], kg), rs), 0, rs, 0.0)
        spread(a_kvf[:, KV_WIDTH:2 * KV_WIDTH], 4 * LANES, rs, 1.0)
        lf_ref[rs, 2 * GLA_WIDTH:] = a_kvf[:, 2 * KV_WIDTH:]
        a_vr = group(3)
        g4_ref[rs, 0:GLA_WIDTH] = (a_qk[:, 0:GLA_WIDTH] * GLA_SCALE).astype(g4_ref.dtype)
        g4_ref[rs, GLA_WIDTH:2 * GLA_WIDTH] = a_qk[:, GLA_WIDTH:].astype(g4_ref.dtype)
        g4_ref[rs, 2 * GLA_WIDTH:4 * GLA_WIDTH] = a_vr.astype(g4_ref.dtype)

    hbs = [normed(rs) for rs in subs]
    for rs, hb in zip(subs, hbs):
        project(rs, hb)


def _inproj(x, mod3, mod_row, lw, rope_tabs, tm):
    b, n, d = x.shape
    rope = rope_tabs is not None
    tok = lambda w: pl.BlockSpec((None, tm, w), lambda bi, i: (bi, i, 0))
    lb = lambda shape: _layer_block(shape, lw["layer"])
    vec = lambda name: _full(lw[name].shape)
    in_specs = [
        tok(d),
        lb((MOD_ROWS, 6 * d)),
        vec("g1"), lb((d, MAIN_WIDTH)), lb((d, LANES)), lb((LANES, 2 * GLA_WIDTH)),
        vec("b_gate"), vec("q_g"), vec("k_g"), _full((2 * LANES, 2 * LANES)),
    ]
    args = [x, mod3, lw["g1"], lw["w_main"], lw["w_z"], lw["w_gate"], lw["b_gate"], lw["q_g"], lw["k_g"], lw["bd256"]]
    if rope:
        in_specs += [pl.BlockSpec((tm, LANES), lambda bi, i: (i, 0))] * 2
        args += list(rope_tabs)
    sds = lambda w, dt: jax.ShapeDtypeStruct((b, n, w), dt)
    out_shape = [sds(ATT_WIDTH, BF16), sds(8 * LANES, BF16), sds(4 * GLA_WIDTH, BF16),
                 sds(2 * GLA_WIDTH + FNET_WIDTH, F32)]
    out_specs = [tok(s.shape[-1]) for s in out_shape]
    return pl.pallas_call(
        functools.partial(_inproj_kernel, rope=rope, layer=lw["layer"], mod_row=mod_row),
        out_shape=out_shape, grid=(b, n // tm), in_specs=in_specs, out_specs=out_specs,
        compiler_params=_params("parallel", "parallel"),
        name="inproj_rope" if rope else "inproj_ctx",
    )(*args)


def _scores(q2, key_tiles, masks):
    cols = []
    for k, tile_masks in zip(key_tiles, masks):
        s = lax.dot_general(q2, k, NT_DIMS, preferred_element_type=F32)
        for j, mk in enumerate(tile_masks):
            c = s[:, LANES * j:LANES * (j + 1)]
            cols.append(c if mk is None else jnp.where(mk, c, NEG_INF))
    return cols


def _softmax_pv(cols, val_tiles, sink_col):
    mx = cols[0]
    for c in cols[1:]:
        mx = jnp.maximum(mx, c)
    m = jnp.maximum(jnp.max(mx, axis=-1, keepdims=True), sink_col)
    p = jnp.concatenate([jnp.exp2(c - m).astype(BF16) for c in cols], axis=1)
    pv = _dot(p, jnp.concatenate(val_tiles, axis=0))
    den = pltpu.roll(pv, HEAD_DIM, 1) + jnp.exp2(sink_col - m)
    return pv / den


def _attn_kernel(*refs):
    nq = ATT_QBLOCKS
    sink_ref, q_ref = refs[0:2]
    kv_refs, kvc_ref, o_ref = refs[2:nq + 4], refs[nq + 4], refs[nq + 5]
    v_off = 4 * LANES
    i = pl.program_id(1)
    last = pl.num_programs(1) - 1
    blk = ATT_BLOCK
    group = ATT_HEADS // ATT_KV_HEADS
    row = lax.broadcasted_iota(jnp.int32, (2 * blk, blk), 0) & (blk - 1)
    col = lax.broadcasted_iota(jnp.int32, (2 * blk, blk), 1)
    band_l, band_r = col >= row, col <= row
    mask_l = [jnp.logical_and(band_l, i > 0) if s == 0 else band_l for s in range(nq)]
    mask_r = [jnp.logical_and(band_r, i < last) if s == nq - 1 else band_r for s in range(nq)]
    upper_rows = lax.broadcasted_iota(jnp.int32, (2 * blk, 1), 0) >= blk
    lo = lax.broadcasted_iota(jnp.int32, (1, LANES), 1) < HEAD_DIM

    def tiles(s, ks):
        l, m_, r = kv_refs[s:s + 3]
        return [jnp.concatenate([l[:, ks], m_[:, ks]], axis=0),
                jnp.concatenate([r[:, ks], kvc_ref[0:blk, ks]], axis=0), kvc_ref[blk:, ks]]

    slots = [(s, kvh, par) for s in range(nq) for kvh in range(ATT_KV_HEADS) for par in range(2)]
    lane_slice = lambda kvh, par, off=0: slice(off + 2 * LANES * kvh + LANES * par,
                                               off + 2 * LANES * kvh + LANES * (par + 1))
    cols = {}
    for s, kvh, par in slots:
        qs, rs = 2 * LANES * kvh, slice(blk * s, blk * (s + 1))
        q2 = jnp.concatenate([q_ref[rs, qs:qs + LANES], q_ref[rs, qs + LANES:qs + 2 * LANES]], axis=0)
        cols[s, kvh, par] = _scores(q2, tiles(s, lane_slice(kvh, par)),
                                    [(mask_l[s], None), (mask_r[s], None), (None,)])
    outs = {}
    for s, kvh, par in slots:
        base = group * kvh + par
        sink_col = jnp.where(upper_rows, sink_ref[base + 2], sink_ref[base]) * LOG2E
        outs[s, kvh, par] = _softmax_pv(cols[s, kvh, par], tiles(s, lane_slice(kvh, par, v_off)), sink_col)
    for s in range(nq):
        for kvh in range(ATT_KV_HEADS):
            qs, r0 = 2 * LANES * kvh, blk * s
            o = jnp.where(lo, outs[s, kvh, 0], outs[s, kvh, 1])
            o_ref[r0:r0 + blk, qs:qs + LANES] = o[0:blk].astype(o_ref.dtype)
            o_ref[r0:r0 + blk, qs + LANES:qs + 2 * LANES] = o[blk:2 * blk].astype(o_ref.dtype)


def _attention(q, kv4, ckv4, sink):
    b, n, _ = q.shape
    lc = ckv4.shape[1]
    nq = ATT_QBLOCKS
    nb = n // ATT_BLOCK
    assert lc == 2 * ATT_BLOCK and nb % nq == 0
    qspec = pl.BlockSpec((None, nq * ATT_BLOCK, 4 * LANES), lambda bi, i: (bi, i, 0))
    kv = [pl.BlockSpec((None, ATT_BLOCK, 8 * LANES),
                       functools.partial(lambda bi, i, j: (bi, jnp.clip(nq * i + j - 1, 0, nb - 1), 0), j=j))
          for j in range(nq + 2)]
    ctx = pl.BlockSpec((None, lc, 8 * LANES), lambda bi, i: (bi, 0, 0))
    return pl.pallas_call(
        _attn_kernel,
        out_shape=jax.ShapeDtypeStruct((b, n, ATT_WIDTH), BF16),
        grid=(b, nb // nq),
        in_specs=[pl.BlockSpec(memory_space=pltpu.SMEM), qspec] + kv + [ctx],
        out_specs=qspec,
        compiler_params=_params("parallel", "parallel"),
        name="window_attention",
    )(sink, q, *([kv4] * (nq + 2)), ckv4)


def _attn_ctx_kernel(sink_ref, q_ref, kc_ref, vc_ref, o_ref):
    kvh = pl.program_id(1)
    lc = q_ref.shape[0]
    q2 = jnp.concatenate([q_ref[:, 0:LANES], q_ref[:, LANES:2 * LANES]], axis=0)
    upper_rows = lax.broadcasted_iota(jnp.int32, (2 * lc, 1), 0) >= lc
    lo = lax.broadcasted_iota(jnp.int32, (1, LANES), 1) < HEAD_DIM
    outs = []
    for par in range(2):
        ks = slice(LANES * par, LANES * (par + 1))
        base = ATT_HEADS // ATT_KV_HEADS * kvh + par
        sink_col = jnp.where(upper_rows, sink_ref[base + 2], sink_ref[base]) * LOG2E
        cols = _scores(q2, [kc_ref[:, ks]], [(None,) * (lc // LANES)])
        outs.append(_softmax_pv(cols, [vc_ref[:, ks]], sink_col))
    o = jnp.where(lo, outs[0], outs[1])
    o_ref[:, 0:LANES] = o[0:lc].astype(o_ref.dtype)
    o_ref[:, LANES:2 * LANES] = o[lc:2 * lc].astype(o_ref.dtype)


def _attention_ctx(cq, ckv4, sink):
    b, lc, _ = cq.shape
    gw = 2 * LANES
    spec = pl.BlockSpec((None, lc, gw), lambda bi, h: (bi, 0, h))
    vspec = pl.BlockSpec((None, lc, gw), lambda bi, h: (bi, 0, ATT_KV_HEADS + h))
    return pl.pallas_call(
        _attn_ctx_kernel,
        out_shape=jax.ShapeDtypeStruct((b, lc, ATT_WIDTH), BF16),
        grid=(b, ATT_KV_HEADS),
        in_specs=[pl.BlockSpec(memory_space=pltpu.SMEM), spec, spec, vspec],
        out_specs=spec,
        compiler_params=_params("parallel", "parallel"),
        name="context_attention",
    )(sink, cq, ckv4, ckv4)


def _dft_consts(n):
    n1, n2 = DFT_N1, n // DFT_N1
    ang = lambda a, m: 2.0 * np.pi * np.outer(np.arange(a), np.arange(a)) / m
    c = HEAD_DIM
    cc, sc = np.cos(ang(c, c)) / np.sqrt(c), np.sin(ang(c, c)) / np.sqrt(c)
    eye = np.eye(FNET_GROUPS)
    w_chan = np.concatenate([np.kron(eye, cc), -np.kron(eye, sc)], axis=1)
    c1, s1 = np.cos(ang(n1, n1)) / np.sqrt(n1), np.sin(ang(n1, n1)) / np.sqrt(n1)
    m1 = np.block([[c1, s1], [-s1, c1]])
    kk = np.arange(n1)[:, None, None] + n1 * np.arange(n2)[None, :, None]
    ph = 2.0 * np.pi * (kk * np.arange(n2)[None, None, :] % n) / n
    m3 = np.concatenate([np.cos(ph), np.sin(ph)], axis=2) / np.sqrt(n2)
    return tuple(jnp.asarray(a, F32) for a in (w_chan, m1, m3))


def _dft_ctx_consts(lc):
    ang = lambda a, m: 2.0 * np.pi * np.outer(np.arange(a), np.arange(a)) / m
    c = HEAD_DIM
    cc, sc = np.cos(ang(c, c)) / np.sqrt(c), np.sin(ang(c, c)) / np.sqrt(c)
    eye = np.eye(FNET_GROUPS)
    w_chan = np.concatenate([np.kron(eye, cc), -np.kron(eye, sc)], axis=1)
    cl, sl = np.cos(ang(lc, lc)) / np.sqrt(lc), np.sin(ang(lc, lc)) / np.sqrt(lc)
    return jnp.asarray(w_chan, F32), jnp.asarray(np.concatenate([cl, sl], axis=1), F32)


def _regroup_rows(x, outer):
    r, c = x.shape
    return jnp.swapaxes(x.reshape(outer, r // outer, c), 0, 1).reshape(r, c)


def _fourier_kernel(u_ref, wc_ref, m1_ref, m3_ref, wf_ref, y_ref, p_ref):
    w, t = FNET_WIDTH, DFT_STEP
    s = pl.program_id(1)
    n_a = p_ref.shape[1]
    n2 = n_a * t

    @pl.when(s < n_a)
    def _():
        u = u_ref[...].reshape(DFT_N1 * t, w).astype(BF16)
        z = _regroup_rows(_dot(u, wc_ref[...]), DFT_N1)
        a_re, a_im = [], []
        for tt in range(t):
            zt = z[DFT_N1 * tt:DFT_N1 * (tt + 1)]
            zs = jnp.concatenate([zt[:, 0:w], zt[:, w:2 * w]], axis=0).astype(BF16)
            a = _dot(m1_ref[...], zs)
            a_re.append(a[0:DFT_N1])
            a_im.append(a[DFT_N1:2 * DFT_N1])
        for part, rows in enumerate((a_re, a_im)):
            a = _regroup_rows(jnp.concatenate(rows, axis=0), t)
            p_ref[part, s] = a.reshape(DFT_N1, t, w)

    @pl.when(s >= n_a)
    def _():
        k1_0 = (s - n_a) * t
        ys = []
        for kk in range(t):
            parts = [p_ref[part, :, k1_0 + kk].reshape(n2, w) for part in range(2)]
            rhs = jnp.concatenate(parts, axis=0).astype(BF16)
            ys.append(_dot(m3_ref[kk].astype(BF16), rhs).astype(BF16))
        out = _dot(jnp.concatenate(ys, axis=0), wf_ref[...])
        y_ref[...] = _regroup_rows(out, t).reshape(n2, t, w)


def _fourier(lf, wf_bd, layer, consts):
    b, n, wl = lf.shape
    w = FNET_WIDTH
    n1, n2, t = DFT_N1, n // DFT_N1, DFT_STEP
    n_a, n_b = n2 // t, n1 // t
    w_chan, m1, m3 = consts
    y = pl.pallas_call(
        _fourier_kernel,
        out_shape=jax.ShapeDtypeStruct((b, n2, n1, w), F32),
        grid=(b, n_a + n_b),
        in_specs=[pl.BlockSpec((None, n1, t, w), lambda bi, s: (bi, 0, jnp.minimum(s, n_a - 1), FU_BLOCK)),
                  _full((w, 2 * w)), _full((2 * n1, 2 * n1)),
                  pl.BlockSpec((t, n2, 2 * n2), lambda bi, s: (jnp.maximum(s - n_a, 0), 0, 0)),
                  _layer_block((w, w), layer)],
        out_specs=pl.BlockSpec((None, n2, t, w), lambda bi, s: (bi, 0, jnp.maximum(s - n_a, 0), 0)),
        scratch_shapes=[pltpu.VMEM((2, n_a, n1, t, w), F32)],
        compiler_params=_params("parallel", "arbitrary"),
        name="fourier_mix",
    )(lf.reshape(b, n1, n2, wl), w_chan.astype(BF16), m1.astype(BF16), m3, wf_bd)
    return y.reshape(b, n, w)


def _four_ctx_kernel(u_ref, wc_ref, m_ref, wf_ref, y_ref):
    w = FNET_WIDTH
    z = _dot(u_ref[...].astype(BF16), wc_ref[...])
    zs = jnp.concatenate([z[:, 0:w], z[:, w:2 * w]], axis=0).astype(BF16)
    y = _dot(m_ref[...], zs)
    y_ref[...] = _dot(y.astype(BF16), wf_ref[...]).astype(y_ref.dtype)


def _fourier_ctx(clf, wf_bd, layer, consts):
    b, lc, _ = clf.shape
    w = FNET_WIDTH
    w_chan, m = consts[0].astype(BF16), consts[1].astype(BF16)
    return pl.pallas_call(
        _four_ctx_kernel,
        out_shape=jax.ShapeDtypeStruct((b, lc, w), BF16),
        grid=(b,),
        in_specs=[pl.BlockSpec((None, lc, w), lambda bi: (bi, 0, FU_BLOCK)), _full((w, 2 * w)), _full((lc, 2 * lc)),
                  _layer_block((w, w), layer)],
        out_specs=pl.BlockSpec((None, lc, w), lambda bi: (bi, 0, 0)),
        compiler_params=_params("parallel"),
        name="fourier_context",
    )(clf, w_chan, m, wf_bd)


def _pair_stack(x, lo):
    zero = jnp.zeros_like(x)
    return jnp.concatenate([jnp.where(lo, x, zero), jnp.where(lo, zero, x)], axis=0)


def _gla_kernel(*refs, ncast):
    qkvf_ref, laf_ref, qkvb_ref, lab_ref, s0_ref = refs[0:5]
    of_ref, ob_ref, sfin_ref = refs[5 + ncast:8 + ncast]
    stf_ref, stb_ref = refs[8 + 2 * ncast:10 + 2 * ncast]
    for src, dst in zip(refs[5:5 + ncast], refs[8 + ncast:8 + 2 * ncast]):
        dst[...] = src[...].astype(dst.dtype)
    i = pl.program_id(1)
    c = GLA_CHUNK
    nchunk = qkvf_ref.shape[0] // c

    @pl.when(i == 0)
    def _():
        stf_ref[...] = s0_ref[0]
        stb_ref[...] = s0_ref[1]

    r64 = lax.broadcasted_iota(jnp.int32, (c, c), 0)
    c64 = lax.broadcasted_iota(jnp.int32, (c, c), 1)
    tri = ((c64 <= r64).astype(BF16), (c64 >= r64).astype(BF16))
    at = lax.broadcasted_iota(jnp.int32, (c, LANES), 0)
    as_ = lax.broadcasted_iota(jnp.int32, (c, LANES), 1) & (c - 1)
    att_mask = (as_ <= at, as_ >= at)
    lo = lax.broadcasted_iota(jnp.int32, (1, LANES), 1) < HEAD_DIM
    br = lax.broadcasted_iota(jnp.int32, (LANES, LANES), 0) < HEAD_DIM
    bc = lax.broadcasted_iota(jnp.int32, (LANES, LANES), 1) < HEAD_DIM
    bd_mask = br == bc
    pairs = [slice(LANES * p, LANES * (p + 1)) for p in range(GLA_WIDTH // LANES)]
    in_refs = ((qkvf_ref, laf_ref), (qkvb_ref, lab_ref))
    out_refs = (of_ref, ob_ref)
    items = [(d, step if d == 0 else nchunk - 1 - step) for step in range(nchunk) for d in range(2)]
    rows = lambda ch: slice(c * ch, c * (ch + 1))

    bcum, work, o_intra, ut, decay = {}, {}, {}, {}, {}
    st_refs = (stf_ref, stb_ref)
    st = [[st_refs[d][LANES * j:LANES * (j + 1)] for j in range(len(pairs))] for d in range(2)]

    def gate_sums(d, ch):
        la = in_refs[d][1][rows(ch)]
        la_hi = la.astype(BF16)
        la_lo = (la - la_hi.astype(F32)).astype(BF16)
        bcum[d, ch] = _dot(jnp.concatenate([tri[d], tri[d]], axis=1), jnp.concatenate([la_hi, la_lo], axis=0))

    def scores(d, ch):
        qkv_ref = in_refs[d][0]
        w = GLA_WIDTH
        b = bcum[d, ch]
        btot = b[0:1] if d == 1 else b[c - 1:c]
        k = qkv_ref[rows(ch), w:2 * w].astype(F32)
        q_in = (qkv_ref[rows(ch), 0:w].astype(F32) * jnp.exp2(b)).astype(BF16)
        k_in = (k * jnp.exp2(-b)).astype(BF16)
        k_out = (k * jnp.exp2(btot - b)).astype(BF16)
        vb = qkv_ref[rows(ch), 2 * w:3 * w]
        att = [lax.dot_general(q_in[:, p], _pair_stack(k_in[:, p], lo), NT_DIMS, preferred_element_type=F32)
               for p in pairs]
        work[d, ch] = (btot, q_in, k_out, vb, att)

    def intra(d, ch):
        btot, q_in, k_out, vb, att = work[d, ch]
        o_intra[d, ch] = [_dot(jnp.where(att_mask[d], a, 0.0).astype(BF16), _pair_stack(vb[:, p], lo))
                          for a, p in zip(att, pairs)]
        ut[d, ch] = [lax.dot_general(k_out[:, p], vb[:, p], TN_DIMS, preferred_element_type=F32) for p in pairs]
        decay[d, ch] = [jnp.transpose(jnp.exp2(btot[:, p])) for p in pairs]

    def carry(d, ch):
        btot, q_in = work[d, ch][0:2]
        outs = []
        for j, p in enumerate(pairs):
            outs.append(o_intra[d, ch][j] + _dot(q_in[:, p], st[d][j].astype(BF16)))
            st[d][j] = st[d][j] * decay[d, ch][j] + jnp.where(bd_mask, ut[d, ch][j], 0.0)
        out_refs[d][rows(ch)] = jnp.concatenate(outs, axis=1).astype(out_refs[d].dtype)

    groups = [items[g:g + GLA_GROUP] for g in range(0, len(items), GLA_GROUP)]
    for it in groups[0]:
        gate_sums(*it)
    for it in groups[0]:
        scores(*it)
    for g, grp in enumerate(groups):
        for it in grp:
            intra(*it)
        nxt = groups[g + 1] if g + 1 < len(groups) else []
        for it in nxt:
            gate_sums(*it)
        for k_ in range(len(grp)):
            carry(*grp[k_])
            if k_ < len(nxt):
                scores(*nxt[k_])
    for d in range(2):
        st_refs[d][...] = jnp.concatenate(st[d], axis=0)

    @pl.when(i == pl.num_programs(1) - 1)
    def _():
        for d in range(2):
            sfin_ref[d] = jnp.concatenate(st[d], axis=0)


def _gla(g4, lf, s0, tb, cast=(), cast_layer=0):
    b, n, _ = g4.shape
    w = GLA_WIDTH
    nblk = n // tb
    fwd = lambda bi, i: (bi, i, 0)
    bwd = lambda bi, i: (bi, nblk - 1 - i, 0)
    bwd_la = lambda bi, i: (bi, nblk - 1 - i, 1)
    tok = lambda f: pl.BlockSpec((None, tb, w), f)
    qkv = lambda f: pl.BlockSpec((None, tb, 3 * w), f)
    state = pl.BlockSpec((None, 2, w, LANES), lambda bi, i: (bi, 0, 0, 0))
    cast_in, cast_out, cast_shapes = _slab_cast_specs(cast, cast_layer, (b, nblk))
    res = pl.pallas_call(
        functools.partial(_gla_kernel, ncast=len(cast)),
        out_shape=[jax.ShapeDtypeStruct((b, n, w), BF16), jax.ShapeDtypeStruct((b, n, w), BF16),
                   jax.ShapeDtypeStruct((b, 2, w, LANES), F32)] + cast_shapes,
        grid=(b, nblk),
        in_specs=[qkv(fwd), tok(fwd), qkv(bwd), tok(bwd_la), state] + cast_in,
        out_specs=[tok(fwd), tok(bwd), state] + cast_out,
        scratch_shapes=[pltpu.VMEM((w, LANES), F32), pltpu.VMEM((w, LANES), F32)],
        compiler_params=_params("arbitrary", "arbitrary"),
        name="gla_scan",
    )(g4, lf, g4, lf, s0, *cast)
    return res[0], res[1], res[2], res[3:]


def _tail_kernel(*refs, ncast, layer, mod_row):
    (att_ref, four_ref, of_ref, ob_ref, r_ref, x_ref, mod_ref, gg_ref, bd_ref, wmix_ref, g2_ref, wi_ref,
     wo_ref) = refs[0:13]
    o_ref, a_ref = refs[13 + ncast], refs[14 + 2 * ncast]
    for src, dst in zip(refs[13:13 + ncast], refs[14 + ncast:14 + 2 * ncast]):
        dst[...] = src[...].astype(dst.dtype)
    d = D_MODEL
    gla_g, g2 = gg_ref[layer:layer + 1, :], g2_ref[layer:layer + 1, :]
    mod_vec = mod_ref[pl.ds(pl.program_id(0) if mod_row is None else mod_row, 1), :]
    mod = lambda j: mod_vec[:, j * d:(j + 1) * d]
    a0, a1, a2 = ATT_WIDTH, ATT_WIDTH + FNET_WIDTH, ATT_WIDTH + FNET_WIDTH + GLA_WIDTH
    sub = min(x_ref.shape[0], TAIL_SUB)
    subs = [slice(sub * j, sub * (j + 1)) for j in range(x_ref.shape[0] // sub)]

    def mixed(rs):
        o = of_ref[rs].astype(F32) + ob_ref[rs].astype(F32)
        y = o * lax.rsqrt(_group_mean(o, bd_ref) + EPS) * gla_g
        y = y * _silu(r_ref[rs].astype(F32))
        mixed_in = jnp.concatenate([att_ref[rs], four_ref[rs].astype(BF16), y.astype(BF16)], axis=1)
        mix = _dot(mixed_in, wmix_ref[0:a2])
        x = x_ref[rs] + mod(2) * mix
        ms = jnp.mean(x * x, axis=-1, keepdims=True)
        return x, ((x * lax.rsqrt(ms + EPS) * g2) * (1.0 + mod(4)) + mod(3)).astype(BF16)

    pre = [mixed(rs) for rs in subs]
    for rs, (x, hb) in zip(subs, pre):
        for c0 in range(0, FFN_HIDDEN, FFN_CHUNK):
            g = _dot(hb, wi_ref[:, c0:c0 + FFN_CHUNK])
            u = _dot(hb, wi_ref[:, FFN_HIDDEN + c0:FFN_HIDDEN + c0 + FFN_CHUNK])
            a_ref[rs, c0:c0 + FFN_CHUNK] = (_silu(g) * u).astype(BF16)
        o_ref[rs] = x + mod(5) * _dot(a_ref[rs], wo_ref[...])


def _tail(att, four, of, ob, g4, x, mod3, mod_row, lw, tm, cast_next=()):
    b, n, d = x.shape
    steps = n // tm
    tok = lambda w: pl.BlockSpec((None, tm, w), lambda bi, i: (bi, i, 0))
    gate = pl.BlockSpec((None, tm, GLA_WIDTH), lambda bi, i: (bi, i, 3))
    vec = lambda name: _full(lw[name].shape)
    big = lambda shape: _layer_block(shape, lw["big_layer"])
    cast_in, cast_out, cast_shapes = _slab_cast_specs(cast_next, lw["layer"] + 1, (b, steps))
    res = pl.pallas_call(
        functools.partial(_tail_kernel, ncast=len(cast_next), layer=lw["layer"], mod_row=mod_row),
        out_shape=[jax.ShapeDtypeStruct((b, n, d), F32)] + cast_shapes,
        grid=(b, steps),
        in_specs=[tok(ATT_WIDTH), tok(FNET_WIDTH), tok(GLA_WIDTH), tok(GLA_WIDTH), gate, tok(d),
                  _layer_block((MOD_ROWS, 6 * d), lw["layer"]),
                  vec("gla_g"), _full((GLA_WIDTH, GLA_WIDTH)), big((d, d)),
                  vec("g2"), big((d, 2 * FFN_HIDDEN)), big((FFN_HIDDEN, d))] + cast_in,
        out_specs=[tok(d)] + cast_out,
        scratch_shapes=[pltpu.VMEM((tm, FFN_HIDDEN), BF16)],
        compiler_params=_params("arbitrary", "arbitrary", vmem_limit=TAIL_VMEM_LIMIT),
        name="mix_ffn",
    )(att, four, of, ob, g4, x, mod3, lw["gla_g"], lw["bd256"], lw["w_out"], lw["g2"], lw["w_ffn_in"],
      lw["w_ffn_out"], *cast_next)
    return res[0], res[1:]


def _rope_tables(n):
    axis_dim = HEAD_DIM // 2
    inv_freq = ROPE_BASE ** (-np.arange(0, axis_dim, 2, dtype=np.float64) / axis_dim)
    t = np.arange(n)
    ang_r = (t // GRID_W)[:, None] * inv_freq[None, :]
    ang_c = (t % GRID_W)[:, None] * inv_freq[None, :]
    cos = np.concatenate([np.cos(ang_r)] * 2 + [np.cos(ang_c)] * 2, axis=1)
    sin = np.concatenate([-np.sin(ang_r), np.sin(ang_r), -np.sin(ang_c), np.sin(ang_c)], axis=1)
    return jnp.asarray(np.tile(cos, (1, 2)), F32), jnp.asarray(np.tile(sin, (1, 2)), F32)


def _block_diag_mean(width):
    g = np.arange(width) // HEAD_DIM
    return jnp.asarray((g[:, None] == g[None, :]) / HEAD_DIM, BF16)


def _prepare_weights(w_in, g_norm1, q_norm_g, k_norm_g, w_fourier, wgf, bgf, wgb, bgb, gla_norm_g, g_norm2):
    depth = w_in.shape[0]
    r = GLA_GATE_RANK
    zr = jnp.zeros((depth, r, GLA_WIDTH), F32)
    w_gate = jnp.concatenate([jnp.concatenate([wgf, zr], axis=2), jnp.concatenate([zr, wgb], axis=2)], axis=1)
    eye = jnp.eye(FNET_GROUPS, dtype=F32)
    wf_bd = jnp.einsum("lgij,gh->lgihj", w_fourier, eye).reshape(depth, FNET_WIDTH, FNET_WIDTH)
    return {
        "g1": g_norm1,
        "w_main": w_in.astype(BF16),
        "w_z": jnp.pad(w_in[:, :, MAIN_WIDTH:], ((0, 0), (0, 0), (0, LANES - 2 * r))).astype(BF16),
        "w_gate": jnp.pad(w_gate, ((0, 0), (0, LANES - 2 * r), (0, 0))).astype(BF16),
        "b_gate": jnp.concatenate([bgf, bgb], axis=1),
        "q_g": jnp.tile(q_norm_g, (1, 2)),
        "k_g": jnp.tile(k_norm_g, (1, 2)),
        "bd256": _block_diag_mean(GLA_WIDTH),
        "wf_bd": wf_bd.astype(BF16),
        "gla_g": jnp.tile(gla_norm_g, (1, GLA_HEADS)),
        "g2": g_norm2,
    }


def kernel(x, c, ctx, c_ctx, w_mod, b_mod, g_norm1, w_in, q_norm_g, k_norm_g, attn_sink, w_fourier, gla_w_gate_f,
           gla_b_gate_f, gla_w_gate_b, gla_b_gate_b, gla_norm_g, w_out, g_norm2, w_ffn_in, w_ffn_out):
    b, n, d = x.shape
    lc = ctx.shape[1]
    depth = w_mod.shape[0]
    assert d == D_MODEL and b < MOD_ROWS and lc % GLA_CHUNK == 0
    assert n % TOKEN_BLOCK == 0 and n % GLA_BLOCK == 0 and n % (DFT_N1 * DFT_STEP) == 0

    cc = jnp.concatenate([c, c_ctx[None, :], jnp.zeros((MOD_ROWS - b - 1, d), F32)], axis=0)
    mod3 = _modulation(cc, w_mod, b_mod)
    rope_tabs = _rope_tables(n)
    dft = _dft_consts(n)
    dft_ctx = _dft_ctx_consts(lc)
    flat = lambda t: t.reshape(1, b * lc, t.shape[-1])
    unflat = lambda t: t.reshape(b, lc, t.shape[-1])
    xc = ctx
    weights = _prepare_weights(w_in, g_norm1, q_norm_g, k_norm_g, w_fourier, gla_w_gate_f, gla_b_gate_f,
                               gla_w_gate_b, gla_b_gate_b, gla_norm_g, g_norm2)
    s_zero = jnp.zeros((b, 2, GLA_WIDTH, LANES), F32)
    tail_names, tail_f32, tail_bf16 = ("w_out", "w_ffn_in", "w_ffn_out"), (w_out, w_ffn_in, w_ffn_out), ()
    for l in range(depth):
        need_ctx = l < depth - 1
        lw = dict(weights, layer=l, big_layer=0)
        sink = attn_sink[l]
        cq, ckv4, cg4, clf = map(unflat, _inproj(flat(xc), mod3, b, lw, None, b * lc))
        q, kv4, g4, lf = _inproj(x, mod3, None, lw, rope_tabs, TOKEN_BLOCK)
        att = _attention(q, kv4, ckv4, sink)
        four = _fourier(lf, lw["wf_bd"], l, dft)
        ocf, ocb, s_ctx, _ = _gla(cg4, clf, s_zero, lc)
        olf, olb, _, cast_now = _gla(g4, lf, s_ctx, GLA_BLOCK, () if tail_bf16 else tail_f32, l)
        lw.update(zip(tail_names, tail_bf16 or cast_now))
        x, tail_bf16 = _tail(att, four, olf, olb, g4, x, mod3, None, lw, TOKEN_BLOCK, tail_f32 if need_ctx else ())
        if need_ctx:
            att_c = _attention_ctx(cq, ckv4, sink)
            four_c = _fourier_ctx(clf, lw["wf_bd"], l, dft_ctx)
            xc = unflat(_tail(*map(flat, (att_c, four_c, ocf, ocb, cg4, xc)), mod3, b, lw, b * lc)[0])
    return x
```
